```python
import jax, jax.numpy as jnp
from jax import lax
import numpy as np

D_MODEL = 1024
BATCH = 8
SEQ = 4096
DEPTH = 1
DEC_BATCH = 128
DEC_SEQ = 8
PAST_LEN = 16384
PAGE_SIZE = 128

HEAD_DIM = 64
SWA_HEADS = 8
SWA_KV_HEADS = 2
SWA_GROUP = SWA_HEADS // SWA_KV_HEADS
WINDOW = 128
RET_HEADS = 8
RET_DIM = 64
RET_CHUNK = 128
RET_THETA = 10000.0
MEM_LEN = 256
MEM_HEADS = 4
MEM_HEAD_DIM = 128
MEM_W = MEM_HEADS * MEM_HEAD_DIM
FFN_HIDDEN = ((8 * D_MODEL + 3 * 256 - 1) // (3 * 256)) * 256
ROPE_THETA = 10000.0
RMS_EPS = 1e-6
NEG_INF = -1e30
SWA_Q_W = SWA_HEADS * HEAD_DIM
SWA_KV_W = SWA_KV_HEADS * HEAD_DIM
RET_W = RET_HEADS * RET_DIM
IN_COLS = SWA_Q_W + 2 * SWA_KV_W + 4 * RET_W
MIX_WIDTH = SWA_Q_W + RET_W

kernel_name = 'hymba_swa_sink_retention_memxattn_step'


def rms_norm(x, g=None):
    xf = x.astype(jnp.float32)
    y = xf * lax.rsqrt(jnp.mean(xf * xf, axis=-1, keepdims=True) + RMS_EPS)
    if g is not None:
        y = y * g.astype(jnp.float32)
    return y.astype(x.dtype)


def rotary(x, pos):
    half = x.shape[-1] // 2
    inv = 1.0 / (ROPE_THETA ** (jnp.arange(half, dtype=jnp.float32) / half))
    ang = pos.astype(jnp.float32)[:, None] * inv[None, :]
    cos = jnp.cos(ang)[:, None, :]
    sin = jnp.sin(ang)[:, None, :]
    xf = x.astype(jnp.float32)
    x1, x2 = xf[..., :half], xf[..., half:]
    return jnp.concatenate([x1 * cos - x2 * sin, x2 * cos + x1 * sin], axis=-1).astype(x.dtype)


def retention_rotate(x, pos):
    half = x.shape[-1] // 2
    inv = RET_THETA ** (-jnp.linspace(0.0, 1.0, half, dtype=jnp.float32))
    ang = pos.astype(jnp.float32)[:, None] * inv[None, :]
    cos = jnp.cos(ang)[:, None, :]
    sin = jnp.sin(ang)[:, None, :]
    xf = x.astype(jnp.float32)
    xe, xo = xf[..., 0::2], xf[..., 1::2]
    out = jnp.stack([xe * cos - xo * sin, xo * cos + xe * sin], axis=-1)
    return out.reshape(x.shape).astype(x.dtype)


def window_mask(q_pos, k_pos):
    d = q_pos - k_pos
    return (k_pos >= 0) & (d >= 0) & (d < WINDOW)


def sink_attention(q, k, v, sinks, mask):
    s = jnp.einsum('...qhgd,...khd->...hgqk', q, k).astype(jnp.float32) * (HEAD_DIM ** -0.5)
    s = jnp.where(mask[..., None, None, :, :], s, NEG_INF)
    sink = sinks.astype(jnp.float32).reshape(SWA_KV_HEADS, SWA_GROUP, 1, 1)
    m = jnp.maximum(jnp.max(s, axis=-1, keepdims=True), sink)
    p = jnp.exp(s - m)
    denom = jnp.sum(p, axis=-1, keepdims=True) + jnp.exp(sink - m)
    w = (p / denom).astype(v.dtype)
    return jnp.einsum('...hgqk,...khd->...qhgd', w, v)


def swa_prompt(q, k, v, sinks):
    B, L, H, D = q.shape
    nb = L // WINDOW
    qb = q.reshape(B, nb, WINDOW, SWA_KV_HEADS, SWA_GROUP, D)
    kb = k.reshape(B, nb, WINDOW, SWA_KV_HEADS, D)
    vb = v.reshape(B, nb, WINDOW, SWA_KV_HEADS, D)
    prev = lambda t: jnp.concatenate([jnp.zeros_like(t[:, :1]), t[:, :-1]], axis=1)
    kk = jnp.concatenate([prev(kb), kb], axis=2)
    vv = jnp.concatenate([prev(vb), vb], axis=2)
    start = jnp.arange(nb, dtype=jnp.int32)[:, None] * WINDOW
    q_pos = start + jnp.arange(WINDOW, dtype=jnp.int32)[None, :]
    k_pos = start - WINDOW + jnp.arange(2 * WINDOW, dtype=jnp.int32)[None, :]
    mask = window_mask(q_pos[:, :, None], k_pos[:, None, :])
    o = sink_attention(qb, kk, vv, sinks, mask)
    return o.reshape(B, L, H, D), k[:, -WINDOW:], v[:, -WINDOW:]


def swa_sample(q, k, v, sinks, buf_k, buf_v):
    B, T, H, D = q.shape
    nbuf = buf_k.shape[1]
    kk = jnp.concatenate([buf_k.astype(k.dtype), k], axis=1)
    vv = jnp.concatenate([buf_v.astype(v.dtype), v], axis=1)
    q_pos = PAST_LEN + jnp.arange(T, dtype=jnp.int32)
    k_pos = PAST_LEN - nbuf + jnp.arange(nbuf + T, dtype=jnp.int32)
    mask = window_mask(q_pos[:, None], k_pos[None, :])
    o = sink_attention(q.reshape(B, T, SWA_KV_HEADS, SWA_GROUP, D), kk, vv, sinks, mask)
    return o.reshape(B, T, H, D), kk[:, -nbuf:], vv[:, -nbuf:]


def retention(q, k, v, s0):
    B, L, H, D = q.shape
    Dv = v.shape[-1]
    c = RET_CHUNK if L % RET_CHUNK == 0 else L
    n = L // c
    log_g = jnp.log(1.0 - jnp.exp2(-5.0 - jnp.arange(H, dtype=jnp.float32)))
    idx = jnp.arange(c, dtype=jnp.float32)
    diff = idx[:, None] - idx[None, :]
    dmat = jnp.where(diff >= 0, jnp.exp(jnp.maximum(diff, 0.0)[None] * log_g[:, None, None]), 0.0)
    q_decay = jnp.exp((idx + 1.0)[None, :] * log_g[:, None]).T[None, :, :, None]
    k_decay = jnp.exp((c - 1.0 - idx)[None, :] * log_g[:, None])
    chunk_decay = jnp.exp(c * log_g)[None, :, None, None]
    to_chunks = lambda t: t.astype(jnp.float32).reshape(B, n, c, H, t.shape[-1]).swapaxes(0, 1)

    def step(s, inp):
        qi, ki, vi = inp
        inner = jnp.einsum('bqhd,bkhd->bhqk', qi, ki) * dmat[None]
        o = jnp.einsum('bhqk,bkhe->bqhe', inner, vi) + jnp.einsum('bqhd,bhde->bqhe', qi, s) * q_decay
        s_new = s * chunk_decay + jnp.einsum('bkhd,bkhe,hk->bhde', ki, vi, k_decay)
        return s_new, o

    s, o = lax.scan(step, s0.astype(jnp.float32), (to_chunks(q), to_chunks(k), to_chunks(v)))
    return o.swapaxes(0, 1).reshape(B, L, H, Dv), s


def parallel_mixer(h, pos, swa_fn, s0, w_in, q_norm_a, k_norm_a, sinks, w_out):
    B, L, _ = h.shape
    proj = h @ w_in
    cuts = [SWA_Q_W, SWA_Q_W + SWA_KV_W, SWA_Q_W + 2 * SWA_KV_W]
    cuts = cuts + [cuts[-1] + RET_W, cuts[-1] + 2 * RET_W, cuts[-1] + 3 * RET_W]
    qa, ka, va, qr, kr, vr, g = jnp.split(proj, cuts, axis=-1)
    qa = rotary(rms_norm(qa.reshape(B, L, SWA_HEADS, HEAD_DIM), q_norm_a), pos)
    ka = rotary(rms_norm(ka.reshape(B, L, SWA_KV_HEADS, HEAD_DIM), k_norm_a), pos)
    va = va.reshape(B, L, SWA_KV_HEADS, HEAD_DIM)
    o_a, k_new, v_new = swa_fn(qa, ka, va, sinks)
    qr = retention_rotate(qr.reshape(B, L, RET_HEADS, RET_DIM), pos)
    kr = retention_rotate(kr.reshape(B, L, RET_HEADS, RET_DIM), pos) * (RET_DIM ** -0.5)
    vr = vr.reshape(B, L, RET_HEADS, RET_DIM)
    o_r, s_new = retention(qr, kr, vr, s0)
    o_r = rms_norm(o_r).astype(h.dtype).reshape(B, L, RET_W) * jax.nn.silu(g)
    y = jnp.concatenate([o_a.reshape(B, L, SWA_Q_W), o_r], axis=-1) @ w_out
    return y, k_new, v_new, s_new


def memory_kv(mem, norm_mem, w_mkv, k_norm_m):
    B, M, _ = mem.shape
    kv = rms_norm(mem, norm_mem) @ w_mkv
    k, v = jnp.split(kv, 2, axis=-1)
    k = rms_norm(k.reshape(B, M, MEM_HEADS, MEM_HEAD_DIM), k_norm_m)
    return k, v.reshape(B, M, MEM_HEADS, MEM_HEAD_DIM)


def memory_cross(h, mem_k, mem_v, w_mq, q_norm_m, w_mo):
    B, L, _ = h.shape
    q = rms_norm((h @ w_mq).reshape(B, L, MEM_HEADS, MEM_HEAD_DIM), q_norm_m)
    s = jnp.einsum('blhd,bmhd->bhlm', q, mem_k.astype(q.dtype)).astype(jnp.float32) * (MEM_HEAD_DIM ** -0.5)
    p = jax.nn.softmax(s, axis=-1).astype(h.dtype)
    o = jnp.einsum('bhlm,bmhd->blhd', p, mem_v.astype(h.dtype)).reshape(B, L, MEM_W)
    return o @ w_mo


def swiglu(h, w_gu, w_down):
    g, u = jnp.split(h @ w_gu, 2, axis=-1)
    return (jax.nn.silu(g) * u) @ w_down


def decoder_layer(x, pos, swa_fn, s0, mem_k, mem_v, norm_mix, w_in, q_norm_a, k_norm_a, sinks, w_out,
                  norm_cross, w_mq, q_norm_m, w_mo, norm_ffn, w_gu, w_down):
    y, k_new, v_new, s_new = parallel_mixer(rms_norm(x, norm_mix), pos, swa_fn, s0,
                                            w_in, q_norm_a, k_norm_a, sinks, w_out)
    x = x + y
    x = x + memory_cross(rms_norm(x, norm_cross), mem_k, mem_v, w_mq, q_norm_m, w_mo)
    x = x + swiglu(rms_norm(x, norm_ffn), w_gu, w_down)
    return x, k_new, v_new, s_new


def setup_inputs(seed: int = 0) -> dict:
    key = jax.random.key(seed)
    ks = jax.random.split(key, 24)
    f32 = jnp.float32
    nrm = lambda k, shape, scale: jax.random.normal(k, shape, f32) * scale
    gain = lambda k, n: 1.0 + 0.02 * jax.random.normal(k, (DEPTH, n), f32)
    buf = min(WINDOW, PAST_LEN)
    return {
        'x_prompt': nrm(ks[0], (BATCH, SEQ, D_MODEL), 1.0),
        'x_sample': nrm(ks[1], (DEC_BATCH, DEC_SEQ, D_MODEL), 1.0),
        'mem_prompt': nrm(ks[2], (BATCH, MEM_LEN, D_MODEL), 1.0),
        'cache_swa_k': nrm(ks[3], (DEPTH, DEC_BATCH, buf, SWA_KV_HEADS, HEAD_DIM), 1.0),
        'cache_swa_v': nrm(ks[4], (DEPTH, DEC_BATCH, buf, SWA_KV_HEADS, HEAD_DIM), 1.0),
        'state_ret': nrm(ks[5], (DEPTH, DEC_BATCH, RET_HEADS, RET_DIM, RET_DIM), 0.5),
        'cache_mem_k': nrm(ks[6], (DEPTH, DEC_BATCH, MEM_LEN, MEM_HEADS, MEM_HEAD_DIM), 1.0),
        'cache_mem_v': nrm(ks[7], (DEPTH, DEC_BATCH, MEM_LEN, MEM_HEADS, MEM_HEAD_DIM), 1.0),
        'norm_mix': gain(ks[8], D_MODEL),
        'w_in': nrm(ks[9], (DEPTH, D_MODEL, IN_COLS), D_MODEL ** -0.5),
        'q_norm_a': gain(ks[10], HEAD_DIM),
        'k_norm_a': gain(ks[11], HEAD_DIM),
        'sinks': nrm(ks[12], (DEPTH, SWA_HEADS), 1.0),
        'w_out': nrm(ks[13], (DEPTH, MIX_WIDTH, D_MODEL), MIX_WIDTH ** -0.5),
        'norm_cross': gain(ks[14], D_MODEL),
        'norm_mem': gain(ks[15], D_MODEL),
        'w_mq': nrm(ks[16], (DEPTH, D_MODEL, MEM_W), D_MODEL ** -0.5),
        'w_mkv': nrm(ks[17], (DEPTH, D_MODEL, 2 * MEM_W), D_MODEL ** -0.5),
        'q_norm_m': gain(ks[18], MEM_HEAD_DIM),
        'k_norm_m': gain(ks[19], MEM_HEAD_DIM),
        'w_mo': nrm(ks[20], (DEPTH, MEM_W, D_MODEL), MEM_W ** -0.5),
        'norm_ffn': gain(ks[21], D_MODEL),
        'w_gu': nrm(ks[22], (DEPTH, D_MODEL, 2 * FFN_HIDDEN), D_MODEL ** -0.5),
        'w_down': nrm(ks[23], (DEPTH, FFN_HIDDEN, D_MODEL), FFN_HIDDEN ** -0.5),
    }


def reference(x_prompt, x_sample, mem_prompt, cache_swa_k, cache_swa_v, state_ret, cache_mem_k, cache_mem_v,
              norm_mix, w_in, q_norm_a, k_norm_a, sinks, w_out, norm_cross, norm_mem, w_mq, w_mkv,
              q_norm_m, k_norm_m, w_mo, norm_ffn, w_gu, w_down):
    pos_p = jnp.arange(x_prompt.shape[1], dtype=jnp.int32)
    pos_s = PAST_LEN + jnp.arange(x_sample.shape[1], dtype=jnp.int32)
    xp, xs = x_prompt, x_sample
    pk, pv, ps, pmk, pmv, sk, sv, ss = [], [], [], [], [], [], [], []
    for l in range(DEPTH):
        lw = (norm_mix[l], w_in[l], q_norm_a[l], k_norm_a[l], sinks[l], w_out[l],
              norm_cross[l], w_mq[l], q_norm_m[l], w_mo[l], norm_ffn[l], w_gu[l], w_down[l])
        mk, mv = memory_kv(mem_prompt, norm_mem[l], w_mkv[l], k_norm_m[l])
        s0 = jnp.zeros((xp.shape[0], RET_HEADS, RET_DIM, RET_DIM), jnp.float32)
        xp, k_new, v_new, s_new = decoder_layer(xp, pos_p, swa_prompt, s0, mk, mv, *lw)
        pk.append(k_new); pv.append(v_new); ps.append(s_new); pmk.append(mk); pmv.append(mv)
        fn = lambda q, k, v, sk_, bk=cache_swa_k[l], bv=cache_swa_v[l]: swa_sample(q, k, v, sk_, bk, bv)
        xs, k_new, v_new, s_new = decoder_layer(xs, pos_s, fn, state_ret[l], cache_mem_k[l], cache_mem_v[l], *lw)
        sk.append(k_new); sv.append(v_new); ss.append(s_new)
    return (xp, xs, jnp.stack(pk), jnp.stack(pv), jnp.stack(ps), jnp.stack(pmk), jnp.stack(pmv),
            jnp.stack(sk), jnp.stack(sv), jnp.stack(ss))
```

```python
import functools

import jax
import jax.numpy as jnp
from jax import lax
from jax.experimental import pallas as pl
from jax.experimental.pallas import tpu as pltpu

F32 = jnp.float32
BF16 = jnp.bfloat16

LANES = 128
D_MODEL = 1024
HEAD_DIM = 64
SWA_HEADS = 8
SWA_KV_HEADS = 2
WINDOW = 128
RET_HEADS = 8
RET_DIM = 64
RET_CHUNK = 128
RET_THETA = 10000.0
ROPE_THETA = 10000.0
MEM_LEN = 256
MEM_HEADS = 4
MEM_HEAD_DIM = 128
MEM_W = MEM_HEADS * MEM_HEAD_DIM
FFN_HIDDEN = 2816
RMS_EPS = 1e-6
NEG_INF = -1e30
PAST_LEN = 16384

SWA_Q_W = SWA_HEADS * HEAD_DIM
SWA_KV_W = SWA_KV_HEADS * HEAD_DIM
RET_W = RET_HEADS * RET_DIM
IN_COLS = SWA_Q_W + 2 * SWA_KV_W + 4 * RET_W
C_QA, C_KA, C_VA = 0, SWA_Q_W, SWA_Q_W + SWA_KV_W
C_QR = SWA_Q_W + 2 * SWA_KV_W
C_KR, C_VR, C_G = C_QR + RET_W, C_QR + 2 * RET_W, C_QR + 3 * RET_W
N_QA_SLABS = SWA_Q_W // LANES
N_RET_SLABS = RET_W // LANES

VMEM_LIMIT = 56 * 1024 * 1024


def _dot(a, b):
    return jnp.dot(a.astype(BF16), b.astype(BF16), preferred_element_type=F32)


def _dot_nt(a, b):
    return lax.dot_general(a.astype(BF16), b.astype(BF16), (((1,), (1,)), ((), ())),
                           preferred_element_type=F32)


def _rms(x):
    return x * lax.rsqrt(jnp.mean(x * x, axis=-1, keepdims=True) + RMS_EPS)


def _lane_consts():
    lane = lax.broadcasted_iota(jnp.int32, (1, LANES), 1)
    m_left = (lane < HEAD_DIM).astype(F32)
    m_right = 1.0 - m_left
    first_half = (lane % HEAD_DIM) < (HEAD_DIM // 2)
    even = (lane % 2) == 0
    return m_left, m_right, first_half, even


def _head_norm(y, bd):
    ss = jnp.dot((y * y).astype(BF16), bd, preferred_element_type=F32) * (1.0 / HEAD_DIM)
    return y * lax.rsqrt(ss + RMS_EPS)


def _rot_half(y, cos, sin_signed, first_half):
    swapped = jnp.where(first_half, pltpu.roll(y, LANES - HEAD_DIM // 2, 1), pltpu.roll(y, HEAD_DIM // 2, 1))
    return y * cos + swapped * sin_signed


def _rot_pairs(y, cos, sin_signed, even):
    swapped = jnp.where(even, pltpu.roll(y, LANES - 1, 1), pltpu.roll(y, 1, 1))
    return y * cos + swapped * sin_signed


def _dup_head(a, g, m_left, m_right):
    one = a * (m_left if g == 0 else m_right)
    return one + pltpu.roll(one, HEAD_DIM, 1)


def _mixer_in(x, nmix, win_ref, qg, kg, bd, tabs, consts):
    ca, sa, cr, sr = tabs
    m_left, m_right, first_half, even = consts
    hb = (_rms(x) * nmix).astype(BF16)
    swa = jnp.dot(hb, win_ref[:, C_QA:C_QR], preferred_element_type=F32)
    qa = [_rot_half(_head_norm(swa[:, s * LANES:(s + 1) * LANES], bd) * qg, ca, sa, first_half)
          for s in range(N_QA_SLABS)]
    ka = _rot_half(_head_norm(swa[:, C_KA:C_VA], bd) * kg, ca, sa, first_half)
    va = swa[:, C_VA:C_QR]
    qr_all = jnp.dot(hb, win_ref[:, C_QR:C_KR], preferred_element_type=F32)
    kr_all = jnp.dot(hb, win_ref[:, C_KR:C_VR], preferred_element_type=F32)
    vr_all = jnp.dot(hb, win_ref[:, C_VR:C_G], preferred_element_type=F32)
    g_all = jnp.dot(hb, win_ref[:, C_G:IN_COLS], preferred_element_type=F32)
    sl = lambda a, p: a[:, p * LANES:(p + 1) * LANES]
    qr = [_rot_pairs(sl(qr_all, p), cr, sr, even) for p in range(N_RET_SLABS)]
    kr = [_rot_pairs(sl(kr_all, p), cr, sr, even) * (RET_DIM ** -0.5) for p in range(N_RET_SLABS)]
    vr = [sl(vr_all, p) for p in range(N_RET_SLABS)]
    gate = [sl(g_all, p) for p in range(N_RET_SLABS)]
    return qa, ka, va, qr, kr, vr, gate


def _mixer_out(x, oa, o_ret, gate, bd, wout_ref):
    outs = list(oa)
    for p in range(N_RET_SLABS):
        outs.append(_head_norm(o_ret[p], bd) * (gate[p] * jax.nn.sigmoid(gate[p])))
    mix = jnp.concatenate(outs, axis=1).astype(BF16)
    return x + jnp.dot(mix, wout_ref[...], preferred_element_type=F32)


def _sink_softmax_pv(s, valid, sink, vd):
    s = jnp.where(valid, s * (HEAD_DIM ** -0.5), NEG_INF)
    m = jnp.maximum(jnp.max(s, axis=-1, keepdims=True), sink)
    p = jnp.exp(s - m)
    denom = jnp.sum(p, axis=-1, keepdims=True) + jnp.exp(sink - m)
    return _dot(p, vd) / denom


def _stack_heads(slab_a, slab_b, m_left, m_right):
    return jnp.concatenate([slab_a * m_left, slab_a * m_right, slab_b * m_left, slab_b * m_right], axis=0)


def _prompt_mixer_kernel(x_ref, ca_ref, sa_ref, cr_ref, sr_ref, nmix_ref, win_ref, qg_ref, kg_ref,
                         sink_ref, wout_ref, dm_ref, qd_ref, kd_ref, cd_ref, bd_ref,
                         y_ref, kwin_ref, vwin_ref, sout_ref,
                         pk_ref, pv_ref, s2_ref, *, tile):
    t = pl.program_id(1)
    nblk = tile // WINDOW

    @pl.when(t == 0)
    def _():
        pk_ref[...] = jnp.zeros_like(pk_ref)
        pv_ref[...] = jnp.zeros_like(pv_ref)
        s2_ref[...] = jnp.zeros_like(s2_ref)

    consts = _lane_consts()
    m_left, m_right, _, _ = consts
    bd = bd_ref[...]
    x = x_ref[0]
    tabs = (ca_ref[...], sa_ref[...], cr_ref[...], sr_ref[...])
    qa, ka, va, qr, kr, vr, gate = _mixer_in(x, nmix_ref[...], win_ref, qg_ref[...], kg_ref[...], bd,
                                             tabs, consts)

    kfull = jnp.concatenate([pk_ref[...], ka], axis=0)
    vfull = jnp.concatenate([pv_ref[...], va], axis=0)
    row = lax.broadcasted_iota(jnp.int32, (4 * WINDOW, 2 * WINDOW), 0) % WINDOW
    col = lax.broadcasted_iota(jnp.int32, (4 * WINDOW, 2 * WINDOW), 1)
    band = (col > row) & (col <= row + WINDOW)
    oa_blocks = [[None] * nblk for _ in range(N_QA_SLABS)]
    for g in range(SWA_KV_HEADS):
        kd_g = _dup_head(kfull, g, m_left, m_right).astype(BF16)
        vd_g = _dup_head(vfull, g, m_left, m_right).astype(BF16)
        sink = sink_ref[g]
        for j in range(nblk):
            rows = slice(j * WINDOW, (j + 1) * WINDOW)
            keys = slice(j * WINDOW, (j + 2) * WINDOW)
            qst = _stack_heads(qa[2 * g][rows], qa[2 * g + 1][rows], m_left, m_right)
            s = _dot_nt(qst, kd_g[keys])
            valid = band & (col >= WINDOW * (1 - t)) if j == 0 else band
            o = _sink_softmax_pv(s, valid, sink, vd_g[keys])
            w = WINDOW
            oa_blocks[2 * g][j] = o[0:w] * m_left + o[w:2 * w] * m_right
            oa_blocks[2 * g + 1][j] = o[2 * w:3 * w] * m_left + o[3 * w:4 * w] * m_right
    oa = [jnp.concatenate(b, axis=0) for b in oa_blocks]
    pk_ref[...] = ka[tile - WINDOW:]
    pv_ref[...] = va[tile - WINDOW:]
    kwin_ref[0] = ka[tile - WINDOW:]
    vwin_ref[0] = va[tile - WINDOW:]

    o_ret = []
    for p in range(N_RET_SLABS):
        chunks = []
        for j in range(nblk):
            rows = slice(j * RET_CHUNK, (j + 1) * RET_CHUNK)
            qj, kj, vj = qr[p][rows], kr[p][rows], vr[p][rows]
            s2 = s2_ref[p]
            qst = jnp.concatenate([qj * m_left, qj * m_right], axis=0)
            inner = _dot_nt(qst, kj) * dm_ref[p]
            lhs = jnp.concatenate([inner, qst * qd_ref[p]], axis=1)
            rhs = jnp.concatenate([vj, s2], axis=0)
            r = _dot(lhs, rhs)
            chunks.append(r[0:RET_CHUNK] * m_left + r[RET_CHUNK:] * m_right)
            s2_ref[p] = s2 * cd_ref[p] + _dot((kj * kd_ref[p]).T, vj)
        o_ret.append(jnp.concatenate(chunks, axis=0))
    sout_ref[0] = s2_ref[...]

    y_ref[0] = _mixer_out(x, oa, o_ret, gate, bd, wout_ref)


def _sample_in_kernel(x_ref, ca_ref, sa_ref, cr_ref, sr_ref, nmix_ref, win_ref, qg_ref, kg_ref, bd_ref,
                      feat_ref):
    consts = _lane_consts()
    tabs = (ca_ref[...], sa_ref[...], cr_ref[...], sr_ref[...])
    qa, ka, va, qr, kr, vr, gate = _mixer_in(x_ref[...], nmix_ref[...], win_ref, qg_ref[...], kg_ref[...],
                                             bd_ref[...], tabs, consts)
    feat_ref[...] = jnp.concatenate(qa + [ka, va] + qr + kr + vr + gate, axis=1)


def _sample_seq_kernel(feat_ref, ck_ref, cv_ref, st_ref, sink_ref, dm_ref, qd_ref, kd_ref, cd_ref,
                       dup_ref, o_ref, kout_ref, vout_ref, sout_ref,
                       kbuf_ref, vbuf_ref, kpad_ref, vpad_ref, *, nseq, ntok):
    @pl.when(pl.program_id(0) == 0)
    def _():
        kbuf_ref[...] = jnp.zeros_like(kbuf_ref)
        vbuf_ref[...] = jnp.zeros_like(vbuf_ref)
        kpad_ref[...] = jnp.zeros_like(kpad_ref)
        vpad_ref[...] = jnp.zeros_like(vpad_ref)

    m_left, m_right, _, _ = _lane_consts()
    nkeys = 2 * WINDOW
    row_tok = lax.broadcasted_iota(jnp.int32, (4 * ntok, nkeys), 0) % ntok
    col = lax.broadcasted_iota(jnp.int32, (4 * ntok, nkeys), 1)
    valid = (col > row_tok) & (col <= row_tok + WINDOW)
    srow = lax.broadcasted_iota(jnp.int32, (LANES, LANES), 0)
    scol = lax.broadcasted_iota(jnp.int32, (LANES, LANES), 1)
    bd_mask = ((srow < RET_DIM) == (scol < RET_DIM)).astype(F32)
    top_rows = lax.broadcasted_iota(jnp.int32, (LANES, RET_DIM), 0) < RET_DIM

    def per_seq(i, carry):
        r0 = pl.multiple_of(i * ntok, ntok)
        rows = pl.ds(r0, ntok)
        slab = lambda c: feat_ref[rows, c:c + LANES]
        k_new, v_new = slab(C_KA), slab(C_VA)
        kbuf_ref[0:WINDOW] = ck_ref[i]
        vbuf_ref[0:WINDOW] = cv_ref[i]
        kbuf_ref[WINDOW:WINDOW + ntok] = k_new
        vbuf_ref[WINDOW:WINDOW + ntok] = v_new
        kout_ref[i, 0:WINDOW - ntok] = ck_ref[i, ntok:WINDOW]
        vout_ref[i, 0:WINDOW - ntok] = cv_ref[i, ntok:WINDOW]
        kout_ref[i, WINDOW - ntok:WINDOW] = k_new
        vout_ref[i, WINDOW - ntok:WINDOW] = v_new
        kfull, vfull = kbuf_ref[...], vbuf_ref[...]
        for g in range(SWA_KV_HEADS):
            kd_g = _dup_head(kfull, g, m_left, m_right)
            vd_g = _dup_head(vfull, g, m_left, m_right)
            qst = _stack_heads(slab(C_QA + 2 * g * LANES), slab(C_QA + (2 * g + 1) * LANES), m_left, m_right)
            o = _sink_softmax_pv(_dot_nt(qst, kd_g), valid, sink_ref[g], vd_g)
            n = ntok
            o_ref[rows, 2 * g * LANES:(2 * g + 1) * LANES] = o[0:n] * m_left + o[n:2 * n] * m_right
            o_ref[rows, (2 * g + 1) * LANES:(2 * g + 2) * LANES] = o[2 * n:3 * n] * m_left + o[3 * n:] * m_right

        for p in range(N_RET_SLABS):
            q2, k2, v2 = slab(C_QR + p * LANES), slab(C_KR + p * LANES), slab(C_VR + p * LANES)
            kpad_ref[0:ntok] = k2
            vpad_ref[0:ntok] = v2
            kpad, vpad = kpad_ref[...], vpad_ref[...]
            s_stack = st_ref[i, p]
            s_bd = jnp.dot(s_stack.astype(BF16), dup_ref[...], preferred_element_type=F32) * bd_mask
            qst = jnp.concatenate([q2 * m_left, q2 * m_right], axis=0)
            inner = _dot_nt(qst, kpad) * dm_ref[p]
            lhs = jnp.concatenate([inner, qst * qd_ref[p]], axis=1)
            rhs = jnp.concatenate([vpad, s_bd], axis=0)
            r = _dot(lhs, rhs)
            o_ref[rows, SWA_Q_W + p * LANES:SWA_Q_W + (p + 1) * LANES] = r[0:ntok] * m_left + r[ntok:] * m_right
            kdt = (kpad * kd_ref[p]).T
            u_same = _dot(kdt, vpad)
            u_swap = _dot(kdt, pltpu.roll(vpad, RET_DIM, 1))
            upd = jnp.where(top_rows, u_same[:, 0:RET_DIM], u_swap[:, 0:RET_DIM])
            sout_ref[i, p] = s_stack * cd_ref[p] + upd
        return carry

    lax.fori_loop(0, nseq, per_seq, 0)


def _sample_out_kernel(x_ref, o_ref, feat_ref, bd_ref, wout_ref, y_ref):
    o = o_ref[...]
    oa = [o[:, s * LANES:(s + 1) * LANES] for s in range(N_QA_SLABS)]
    o_ret = [o[:, SWA_Q_W + p * LANES:SWA_Q_W + (p + 1) * LANES] for p in range(N_RET_SLABS)]
    gate = [feat_ref[:, C_G + p * LANES:C_G + (p + 1) * LANES] for p in range(N_RET_SLABS)]
    y_ref[...] = _mixer_out(x_ref[...], oa, o_ret, gate, bd_ref[...], wout_ref)


def _mem_kv_kernel(m_ref, nmem_ref, wkv_ref, kg_ref, k_ref, v_ref):
    hb = (_rms(m_ref[...]) * nmem_ref[...]).astype(BF16)
    kv = jnp.dot(hb, wkv_ref[...], preferred_element_type=F32)
    for h in range(MEM_HEADS):
        kh = kv[:, h * LANES:(h + 1) * LANES]
        k_ref[:, h * LANES:(h + 1) * LANES] = _rms(kh) * kg_ref[...]
    v_ref[...] = kv[:, MEM_W:]


def _cross_attn_kernel(x_ref, mk_ref, mv_ref, ncross_ref, wq_ref, qg_ref, o_ref, *, nseq, tq):
    hb = (_rms(x_ref[...]) * ncross_ref[...]).astype(BF16)
    q = jnp.dot(hb, wq_ref[...], preferred_element_type=F32)
    for i in range(nseq):
        for h in range(MEM_HEADS):
            cols = slice(h * LANES, (h + 1) * LANES)
            qh = _rms(q[i * tq:(i + 1) * tq, cols]) * qg_ref[...]
            s = _dot_nt(qh, mk_ref[i, :, cols]) * (MEM_HEAD_DIM ** -0.5)
            m = jnp.max(s, axis=-1, keepdims=True)
            p = jnp.exp(s - m)
            denom = jnp.sum(p, axis=-1, keepdims=True)
            o_ref[i * tq:(i + 1) * tq, cols] = _dot(p, mv_ref[i, :, cols]) / denom


def _cross_out_ffn_kernel(x_ref, o_ref, wo_ref, nffn_ref, wgu_ref, wdown_ref, y_ref):
    x = x_ref[...] + jnp.dot(o_ref[...].astype(BF16), wo_ref[...], preferred_element_type=F32)
    hb = (_rms(x) * nffn_ref[...]).astype(BF16)
    g = jnp.dot(hb, wgu_ref[:, 0:FFN_HIDDEN], preferred_element_type=F32)
    u = jnp.dot(hb, wgu_ref[:, FFN_HIDDEN:], preferred_element_type=F32)
    act = (g * jax.nn.sigmoid(g) * u).astype(BF16)
    y_ref[...] = x + jnp.dot(act, wdown_ref[...], preferred_element_type=F32)


def _rope_tables(pos):
    half = HEAD_DIM // 2
    inv = 1.0 / (ROPE_THETA ** (jnp.arange(half, dtype=F32) / half))
    ang = pos.astype(F32)[:, None] * inv[None, :]
    cos, sin = jnp.cos(ang), jnp.sin(ang)
    c64 = jnp.concatenate([cos, cos], axis=-1)
    s64 = jnp.concatenate([-sin, sin], axis=-1)
    return jnp.tile(c64, (1, 2)), jnp.tile(s64, (1, 2))


def _ret_tables(pos):
    half = RET_DIM // 2
    inv = RET_THETA ** (-jnp.linspace(0.0, 1.0, half, dtype=F32))
    ang = pos.astype(F32)[:, None] * inv[None, :]
    cos, sin = jnp.cos(ang), jnp.sin(ang)
    c64 = jnp.repeat(cos, 2, axis=-1)
    s64 = jnp.stack([-sin, sin], axis=-1).reshape(pos.shape[0], RET_DIM)
    return jnp.tile(c64, (1, 2)), jnp.tile(s64, (1, 2))


def _decay_consts(c, pad_rows, pad_cols):
    log_g = jnp.log(1.0 - jnp.exp2(-5.0 - jnp.arange(RET_HEADS, dtype=F32)))
    idx = jnp.arange(c, dtype=F32)
    diff = idx[:, None] - idx[None, :]
    dmat = jnp.where(diff >= 0, jnp.exp(jnp.maximum(diff, 0.0)[None] * log_g[:, None, None]), 0.0)
    dm = jnp.pad(dmat.reshape(N_RET_SLABS, 2 * c, c), ((0, 0), (0, 0), (0, pad_cols - c)))
    qd = jnp.exp((idx + 1.0)[None, :] * log_g[:, None]).reshape(N_RET_SLABS, 2 * c, 1)
    kdec = jnp.exp((c - 1.0 - idx)[None, :] * log_g[:, None])
    kd = jnp.repeat(kdec.reshape(N_RET_SLABS, 2, c).transpose(0, 2, 1), RET_DIM, axis=-1)
    kd = jnp.pad(kd, ((0, 0), (0, pad_rows - c), (0, 0)))
    cd = jnp.repeat(jnp.exp(c * log_g).reshape(N_RET_SLABS, 2), RET_DIM, axis=-1)[..., None]
    return dm, qd, kd, cd


def _full(shape):
    nd = len(shape)
    return pl.BlockSpec(shape, lambda *_: (0,) * nd)


def _params(sem):
    return pltpu.CompilerParams(dimension_semantics=sem, vmem_limit_bytes=VMEM_LIMIT)


def _prompt_mixer(x, w, tile):
    b, l, d = x.shape
    pos = jnp.arange(l, dtype=jnp.int32)
    ca, sa = _rope_tables(pos)
    cr, sr = _ret_tables(pos)
    dm, qd, kd, cd = _decay_consts(RET_CHUNK, RET_CHUNK, RET_CHUNK)
    sink = jnp.repeat(w['sinks'].reshape(SWA_KV_HEADS, -1), WINDOW, axis=-1)[..., None]
    tab = pl.BlockSpec((tile, LANES), lambda i, t: (t, 0))
    xspec = pl.BlockSpec((1, tile, d), lambda i, t: (i, t, 0))
    win_spec = pl.BlockSpec((1, WINDOW, LANES), lambda i, t: (i, 0, 0))
    return pl.pallas_call(
        functools.partial(_prompt_mixer_kernel, tile=tile),
        grid=(b, l // tile),
        in_specs=[xspec, tab, tab, tab, tab, _full((1, d)), _full((d, IN_COLS)), _full((1, LANES)),
                  _full((1, LANES)), _full(sink.shape), _full((d, d)), _full(dm.shape), _full(qd.shape),
                  _full(kd.shape), _full(cd.shape), _full((LANES, LANES))],
        out_specs=[xspec, win_spec, win_spec,
                   pl.BlockSpec((1, N_RET_SLABS, LANES, LANES), lambda i, t: (i, 0, 0, 0))],
        out_shape=[jax.ShapeDtypeStruct(x.shape, F32),
                   jax.ShapeDtypeStruct((b, WINDOW, LANES), F32),
                   jax.ShapeDtypeStruct((b, WINDOW, LANES), F32),
                   jax.ShapeDtypeStruct((b, N_RET_SLABS, LANES, LANES), F32)],
        scratch_shapes=[pltpu.VMEM((WINDOW, LANES), F32), pltpu.VMEM((WINDOW, LANES), F32),
                        pltpu.VMEM((N_RET_SLABS, LANES, LANES), F32)],
        compiler_params=_params(("arbitrary", "arbitrary")),
        name="prompt_mixer",
    )(x, ca, sa, cr, sr, w['norm_mix'], w['w_in'], w['qg'], w['kg'], sink, w['w_out'], dm, qd, kd, cd,
      w['bd'])


def _sample_mixer(x, cache_k, cache_v, state, w, tile, nseq):
    b, ntok, d = x.shape
    rows = b * ntok
    xf = x.reshape(rows, d)
    pos = jnp.tile(PAST_LEN + jnp.arange(ntok, dtype=jnp.int32), b)
    ca, sa = _rope_tables(pos)
    cr, sr = _ret_tables(pos)
    tab = pl.BlockSpec((tile, LANES), lambda i: (i, 0))
    row_spec = lambda width: pl.BlockSpec((tile, width), lambda i: (i, 0))
    feat = pl.pallas_call(
        _sample_in_kernel,
        grid=(rows // tile,),
        in_specs=[row_spec(d), tab, tab, tab, tab, _full((1, d)), _full((d, IN_COLS)), _full((1, LANES)),
                  _full((1, LANES)), _full((LANES, LANES))],
        out_specs=row_spec(IN_COLS),
        out_shape=jax.ShapeDtypeStruct((rows, IN_COLS), F32),
        compiler_params=_params(("arbitrary",)),
        name="sample_in",
    )(xf, ca, sa, cr, sr, w['norm_mix'], w['w_in'], w['qg'], w['kg'], w['bd'])

    dm, qd, kd, cd = _decay_consts(ntok, LANES, LANES)
    sink = jnp.repeat(w['sinks'].reshape(SWA_KV_HEADS, -1), ntok, axis=-1)[..., None]
    eye = jnp.eye(RET_DIM, dtype=BF16)
    dup = jnp.concatenate([eye, eye], axis=1)
    st = state.reshape(b, N_RET_SLABS, 2 * RET_DIM, RET_DIM)
    seq_rows = nseq * ntok
    cache_spec = pl.BlockSpec((nseq, WINDOW, LANES), lambda i: (i, 0, 0))
    st_spec = pl.BlockSpec((nseq, N_RET_SLABS, 2 * RET_DIM, RET_DIM), lambda i: (i, 0, 0, 0))
    o, k_out, v_out, s_out = pl.pallas_call(
        functools.partial(_sample_seq_kernel, nseq=nseq, ntok=ntok),
        grid=(b // nseq,),
        in_specs=[pl.BlockSpec((seq_rows, IN_COLS), lambda i: (i, 0)), cache_spec, cache_spec, st_spec,
                  _full(sink.shape), _full(dm.shape), _full(qd.shape), _full(kd.shape), _full(cd.shape),
                  _full(dup.shape)],
        out_specs=[pl.BlockSpec((seq_rows, d), lambda i: (i, 0)), cache_spec, cache_spec, st_spec],
        out_shape=[jax.ShapeDtypeStruct((rows, d), F32),
                   jax.ShapeDtypeStruct(cache_k.shape, F32),
                   jax.ShapeDtypeStruct(cache_v.shape, F32),
                   jax.ShapeDtypeStruct(st.shape, F32)],
        scratch_shapes=[pltpu.VMEM((2 * WINDOW, LANES), F32), pltpu.VMEM((2 * WINDOW, LANES), F32),
                        pltpu.VMEM((LANES, LANES), F32), pltpu.VMEM((LANES, LANES), F32)],
        compiler_params=_params(("arbitrary",)),
        name="sample_seq",
    )(feat, cache_k, cache_v, st, sink, dm, qd, kd, cd, dup)

    y = pl.pallas_call(
        _sample_out_kernel,
        grid=(rows // tile,),
        in_specs=[row_spec(d), row_spec(d), row_spec(IN_COLS), _full((LANES, LANES)), _full((d, d))],
        out_specs=row_spec(d),
        out_shape=jax.ShapeDtypeStruct((rows, d), F32),
        compiler_params=_params(("arbitrary",)),
        name="sample_out",
    )(xf, o, feat, w['bd'], w['w_out'])
    return y, k_out, v_out, s_out


def _mem_kv(mem, w, tile):
    rows, d = mem.shape
    row_spec = lambda width: pl.BlockSpec((tile, width), lambda i: (i, 0))
    return pl.pallas_call(
        _mem_kv_kernel,
        grid=(rows // tile,),
        in_specs=[row_spec(d), _full((1, d)), _full((d, 2 * MEM_W)), _full((1, LANES))],
        out_specs=[row_spec(MEM_W), row_spec(MEM_W)],
        out_shape=[jax.ShapeDtypeStruct((rows, MEM_W), F32)] * 2,
        compiler_params=_params(("arbitrary",)),
        name="mem_kv",
    )(mem, w['norm_mem'], w['w_mkv'], w['kgm'])


def _cross_attn(x, mk, mv, w, nseq, tq, name):
    rows, d = x.shape
    rows_per_mem = rows // mk.shape[0]
    steps_per_mem = rows_per_mem // tq if nseq == 1 else 1
    blk = nseq * tq
    if nseq == 1:
        mem_spec = pl.BlockSpec((1, MEM_LEN, MEM_W), lambda i: (i // steps_per_mem, 0, 0))
    else:
        mem_spec = pl.BlockSpec((nseq, MEM_LEN, MEM_W), lambda i: (i, 0, 0))
    return pl.pallas_call(
        functools.partial(_cross_attn_kernel, nseq=nseq, tq=tq),
        grid=(rows // blk,),
        in_specs=[pl.BlockSpec((blk, d), lambda i: (i, 0)), mem_spec, mem_spec, _full((1, d)),
                  _full((d, MEM_W)), _full((1, LANES))],
        out_specs=pl.BlockSpec((blk, MEM_W), lambda i: (i, 0)),
        out_shape=jax.ShapeDtypeStruct((rows, MEM_W), F32),
        compiler_params=_params(("arbitrary",)),
        name=name,
    )(x, mk, mv, w['norm_cross'], w['w_mq'], w['qgm'])


def _cross_out_ffn(x, o, w, tile, name):
    rows, d = x.shape
    row_spec = lambda width: pl.BlockSpec((tile, width), lambda i: (i, 0))
    return pl.pallas_call(
        _cross_out_ffn_kernel,
        grid=(rows // tile,),
        in_specs=[row_spec(d), row_spec(MEM_W), _full((MEM_W, d)), _full((1, d)),
                  _full((d, 2 * FFN_HIDDEN)), _full((FFN_HIDDEN, d))],
        out_specs=row_spec(d),
        out_shape=jax.ShapeDtypeStruct((rows, d), F32),
        compiler_params=_params(("arbitrary",)),
        name=name,
    )(x, o, w['w_mo'], w['norm_ffn'], w['w_gu'], w['w_down'])


def kernel(x_prompt, x_sample, mem_prompt, cache_swa_k, cache_swa_v, state_ret, cache_mem_k, cache_mem_v,
           norm_mix, w_in, q_norm_a, k_norm_a, sinks, w_out, norm_cross, norm_mem, w_mq, w_mkv,
           q_norm_m, k_norm_m, w_mo, norm_ffn, w_gu, w_down):
    assert norm_mix.shape[0] == 1, "single-layer kernel"
    b, l, d = x_prompt.shape
    sb, st, _ = x_sample.shape
    half = (jnp.arange(LANES) // HEAD_DIM)[:, None] == (jnp.arange(LANES) // HEAD_DIM)[None, :]
    w = {
        'norm_mix': norm_mix, 'norm_cross': norm_cross, 'norm_mem': norm_mem, 'norm_ffn': norm_ffn,
        'w_in': w_in[0].astype(BF16), 'w_out': w_out[0].astype(BF16), 'w_mq': w_mq[0].astype(BF16),
        'w_mkv': w_mkv[0].astype(BF16), 'w_mo': w_mo[0].astype(BF16), 'w_gu': w_gu[0].astype(BF16),
        'w_down': w_down[0].astype(BF16),
        'qg': jnp.tile(q_norm_a, (1, 2)), 'kg': jnp.tile(k_norm_a, (1, 2)),
        'qgm': q_norm_m, 'kgm': k_norm_m, 'sinks': sinks[0],
        'bd': half.astype(BF16),
    }

    mk, mv = _mem_kv(mem_prompt.reshape(b * MEM_LEN, d), w, tile=256)
    xp, kwin, vwin, s_pairs = _prompt_mixer(x_prompt, w, tile=256)
    xp = xp.reshape(b * l, d)
    o = _cross_attn(xp, mk.reshape(b, MEM_LEN, MEM_W), mv.reshape(b, MEM_LEN, MEM_W), w, nseq=1, tq=256,
                    name="prompt_cross_attn")
    yp = _cross_out_ffn(xp, o, w, tile=256, name="prompt_ffn").reshape(b, l, d)
    ret_p = jnp.stack([s_pairs[:, :, :RET_DIM, :RET_DIM], s_pairs[:, :, RET_DIM:, RET_DIM:]], axis=2)
    ret_p = ret_p.reshape(1, b, RET_HEADS, RET_DIM, RET_DIM)

    nbuf = cache_swa_k.shape[2]
    xs, k_s, v_s, s_s = _sample_mixer(x_sample, cache_swa_k.reshape(sb, nbuf, LANES),
                                      cache_swa_v.reshape(sb, nbuf, LANES), state_ret[0], w,
                                      tile=256, nseq=8)
    os_ = _cross_attn(xs, cache_mem_k.reshape(sb, MEM_LEN, MEM_W), cache_mem_v.reshape(sb, MEM_LEN, MEM_W),
                      w, nseq=8, tq=st, name="sample_cross_attn")
    ys = _cross_out_ffn(xs, os_, w, tile=256, name="sample_ffn").reshape(sb, st, d)

    kv_shape = (1, b, WINDOW, SWA_KV_HEADS, HEAD_DIM)
    mem_shape = (1, b, MEM_LEN, MEM_HEADS, MEM_HEAD_DIM)
    return (yp, ys, kwin.reshape(kv_shape), vwin.reshape(kv_shape), ret_p,
            mk.reshape(mem_shape), mv.reshape(mem_shape),
            k_s.reshape(cache_swa_k.shape), v_s.reshape(cache_swa_v.shape),
            s_s.reshape(state_ret.shape))
```

```python
import functools

import jax
import jax.numpy as jnp
from jax import lax
from jax.experimental import pallas as pl
from jax.experimental.pallas import tpu as pltpu

F32 = jnp.float32
BF16 = jnp.bfloat16

LANES = 128
D_MODEL = 1024
HEAD_DIM = 64
SWA_HEADS = 8
SWA_KV_HEADS = 2
WINDOW = 128
RET_HEADS = 8
RET_DIM = 64
RET_CHUNK = 128
RET_THETA = 10000.0
ROPE_THETA = 10000.0
MEM_LEN = 256
MEM_HEADS = 4
MEM_HEAD_DIM = 128
MEM_W = MEM_HEADS * MEM_HEAD_DIM
FFN_HIDDEN = 2816
RMS_EPS = 1e-6
NEG_INF = -1e30
PAST_LEN = 16384

SWA_Q_W = SWA_HEADS * HEAD_DIM
SWA_KV_W = SWA_KV_HEADS * HEAD_DIM
RET_W = RET_HEADS * RET_DIM
IN_COLS = SWA_Q_W + 2 * SWA_KV_W + 4 * RET_W
C_QA, C_KA, C_VA = 0, SWA_Q_W, SWA_Q_W + SWA_KV_W
C_QR = SWA_Q_W + 2 * SWA_KV_W
C_KR, C_VR, C_G = C_QR + RET_W, C_QR + 2 * RET_W, C_QR + 3 * RET_W
N_QA_SLABS = SWA_Q_W // LANES
N_RET_SLABS = RET_W // LANES

VMEM_LIMIT = 56 * 1024 * 1024


def _dot(a, b):
    return jnp.dot(a.astype(BF16), b.astype(BF16), preferred_element_type=F32)


def _dot_nt(a, b):
    return lax.dot_general(a.astype(BF16), b.astype(BF16), (((1,), (1,)), ((), ())),
                           preferred_element_type=F32)


def _rms(x):
    return x * lax.rsqrt(jnp.mean(x * x, axis=-1, keepdims=True) + RMS_EPS)


def _lane_consts():
    lane = lax.broadcasted_iota(jnp.int32, (1, LANES), 1)
    m_left = (lane < HEAD_DIM).astype(F32)
    m_right = 1.0 - m_left
    first_half = (lane % HEAD_DIM) < (HEAD_DIM // 2)
    even = (lane % 2) == 0
    return m_left, m_right, first_half, even


def _head_norm(y, bd):
    ss = jnp.dot((y * y).astype(BF16), bd, preferred_element_type=F32) * (1.0 / HEAD_DIM)
    return y * lax.rsqrt(ss + RMS_EPS)


def _rot_half(y, cos, sin_signed, first_half):
    swapped = jnp.where(first_half, pltpu.roll(y, LANES - HEAD_DIM // 2, 1), pltpu.roll(y, HEAD_DIM // 2, 1))
    return y * cos + swapped * sin_signed


def _rot_pairs(y, cos, sin_signed, even):
    swapped = jnp.where(even, pltpu.roll(y, LANES - 1, 1), pltpu.roll(y, 1, 1))
    return y * cos + swapped * sin_signed


def _dup_head(a, g, m_left, m_right):
    one = a * (m_left if g == 0 else m_right)
    return one + pltpu.roll(one, HEAD_DIM, 1)


def _mixer_in(x, nmix, win_ref, qg, kg, bd, tabs, consts):
    ca, sa, cr, sr = tabs
    m_left, m_right, first_half, even = consts
    hb = (_rms(x) * nmix).astype(BF16)
    swa = jnp.dot(hb, win_ref[:, C_QA:C_QR], preferred_element_type=F32)
    qa = [_rot_half(_head_norm(swa[:, s * LANES:(s + 1) * LANES], bd) * qg, ca, sa, first_half)
          for s in range(N_QA_SLABS)]
    ka = _rot_half(_head_norm(swa[:, C_KA:C_VA], bd) * kg, ca, sa, first_half)
    va = swa[:, C_VA:C_QR]
    qr_all = jnp.dot(hb, win_ref[:, C_QR:C_KR], preferred_element_type=F32)
    kr_all = jnp.dot(hb, win_ref[:, C_KR:C_VR], preferred_element_type=F32)
    vr_all = jnp.dot(hb, win_ref[:, C_VR:C_G], preferred_element_type=F32)
    g_all = jnp.dot(hb, win_ref[:, C_G:IN_COLS], preferred_element_type=F32)
    sl = lambda a, p: a[:, p * LANES:(p + 1) * LANES]
    qr = [_rot_pairs(sl(qr_all, p), cr, sr, even) for p in range(N_RET_SLABS)]
    kr = [_rot_pairs(sl(kr_all, p), cr, sr, even) * (RET_DIM ** -0.5) for p in range(N_RET_SLABS)]
    vr = [sl(vr_all, p) for p in range(N_RET_SLABS)]
    gate = [sl(g_all, p) for p in range(N_RET_SLABS)]
    return qa, ka, va, qr, kr, vr, gate


def _mixer_out(x, oa, o_ret, gate, bd, wout_ref):
    outs = list(oa)
    for p in range(N_RET_SLABS):
        outs.append(_head_norm(o_ret[p], bd) * (gate[p] * jax.nn.sigmoid(gate[p])))
    mix = jnp.concatenate(outs, axis=1).astype(BF16)
    return x + jnp.dot(mix, wout_ref[...], preferred_element_type=F32)


def _sink_softmax_pv(s, valid, sink, vd):
    s = jnp.where(valid, s * (HEAD_DIM ** -0.5), NEG_INF)
    m = jnp.maximum(jnp.max(s, axis=-1, keepdims=True), sink)
    p = jnp.exp(s - m)
    denom = jnp.sum(p, axis=-1, keepdims=True) + jnp.exp(sink - m)
    return _dot(p, vd) / denom


def _stack_heads(slab_a, slab_b, m_left, m_right):
    return jnp.concatenate([slab_a * m_left, slab_a * m_right, slab_b * m_left, slab_b * m_right], axis=0)


def _dot_tn(a, b):
    return lax.dot_general(a.astype(BF16), b.astype(BF16), (((0,), (0,)), ((), ())),
                           preferred_element_type=F32)


def _head_norm_t(y):
    return y * lax.rsqrt(jnp.mean(y * y, axis=0, keepdims=True) + RMS_EPS)


def _rot_half_t(y, cos, sin):
    half = HEAD_DIM // 2
    y1, y2 = y[0:half], y[half:]
    return jnp.concatenate([y1 * cos - y2 * sin, y2 * cos + y1 * sin], axis=0)


def _rot_pairs_t(y, cos, sin_signed, even_row):
    n = y.shape[0]
    swapped = jnp.where(even_row, pltpu.roll(y, n - 1, 0), pltpu.roll(y, 1, 0))
    return y * cos + swapped * sin_signed


def _prompt_mixer_kernel(x_ref, ca_ref, sa_ref, cr_ref, sr_ref, nmix_ref, wint_ref, qg_ref, kg_ref,
                         sink_ref, wout_ref, dm_ref, qd_ref, kd_ref, cd_ref,
                         y_ref, kwin_ref, vwin_ref, sout_ref,
                         pk_ref, pv_ref, s_ref, *, tile):
    t = pl.program_id(1)
    nblk = tile // WINDOW
    hd = HEAD_DIM

    @pl.when(t == 0)
    def _():
        pk_ref[...] = jnp.zeros_like(pk_ref)
        pv_ref[...] = jnp.zeros_like(pv_ref)
        s_ref[...] = jnp.zeros_like(s_ref)

    x = x_ref[0]
    hb = (_rms(x) * nmix_ref[...]).astype(BF16)
    proj = lambda lo, hi: lax.dot_general(wint_ref[lo:hi, :], hb, (((1,), (1,)), ((), ())),
                                          preferred_element_type=F32)
    ca, sa = ca_ref[...], sa_ref[...]
    cr, sr = cr_ref[...], sr_ref[...]
    qg, kg = qg_ref[...], kg_ref[...]

    swa_t = proj(C_QA, C_QR)
    head = lambda a, h: a[h * hd:(h + 1) * hd]
    qa = [_rot_half_t(_head_norm_t(head(swa_t, h)) * qg, ca, sa) for h in range(SWA_HEADS)]
    ka = jnp.concatenate([_rot_half_t(_head_norm_t(head(swa_t, SWA_HEADS + g)) * kg, ca, sa)
                          for g in range(SWA_KV_HEADS)], axis=0)
    va = swa_t[C_VA:C_QR]
    kfull = jnp.concatenate([pk_ref[...], ka], axis=1).astype(BF16)
    vfull = jnp.concatenate([pv_ref[...], va], axis=1).astype(BF16)
    pk_ref[...] = ka[:, tile - WINDOW:]
    pv_ref[...] = va[:, tile - WINDOW:]
    kwin_ref[0] = ka[:, tile - WINDOW:].T
    vwin_ref[0] = va[:, tile - WINDOW:].T

    group = SWA_HEADS // SWA_KV_HEADS
    key = lax.broadcasted_iota(jnp.int32, (2 * WINDOW, group * WINDOW), 0)
    qry = lax.broadcasted_iota(jnp.int32, (2 * WINDOW, group * WINDOW), 1) % WINDOW
    band = (key > qry) & (key <= qry + WINDOW)
    oa_heads = [[None] * nblk for _ in range(SWA_HEADS)]
    units = [(g, j) for j in range(nblk) for g in range(SWA_KV_HEADS)]
    keys_of = lambda j: slice(j * WINDOW, (j + 2) * WINDOW)
    scores = []
    for g, j in units:
        toks = slice(j * WINDOW, (j + 1) * WINDOW)
        q4 = jnp.concatenate([qa[group * g + u][:, toks] for u in range(group)], axis=1)
        scores.append(_dot_tn(head(kfull, g)[:, keys_of(j)], q4))
    probs, denoms = [], []
    for (g, j), s in zip(units, scores):
        sink = sink_ref[g]
        valid = band & (key >= WINDOW * (1 - t)) if j == 0 else band
        s = jnp.where(valid, s * (hd ** -0.5), NEG_INF)
        m = jnp.maximum(jnp.max(s, axis=0, keepdims=True), sink)
        p = jnp.exp(s - m)
        denoms.append(jnp.sum(p, axis=0, keepdims=True) + jnp.exp(sink - m))
        probs.append(p.astype(BF16))
    for (g, j), p, denom in zip(units, probs, denoms):
        o = jnp.dot(head(vfull, g)[:, keys_of(j)], p, preferred_element_type=F32) / denom
        for u in range(group):
            oa_heads[group * g + u][j] = o[:, u * WINDOW:(u + 1) * WINDOW]

    ret_t = proj(C_QR, C_VR)
    even_row = (lax.broadcasted_iota(jnp.int32, (RET_W, tile), 0) % 2) == 0
    cr8 = jnp.concatenate([cr] * RET_HEADS, axis=0)
    sr8 = jnp.concatenate([sr] * RET_HEADS, axis=0)
    qr = _rot_pairs_t(ret_t[0:RET_W], cr8, sr8, even_row)
    kr = _rot_pairs_t(ret_t[RET_W:], cr8, sr8, even_row) * (RET_DIM ** -0.5)
    vr = proj(C_VR, C_G)
    zeros = jnp.zeros((hd, RET_CHUNK), F32)
    or_heads = [[None] * nblk for _ in range(RET_HEADS)]
    chunk = lambda a, h, j: a[h * hd:(h + 1) * hd, j * RET_CHUNK:(j + 1) * RET_CHUNK]
    inner = [[None] * nblk for _ in range(N_RET_SLABS)]
    for j in range(nblk):
        for pr in range(N_RET_SLABS):
            q_bd = jnp.concatenate([jnp.concatenate([chunk(qr, 2 * pr, j), zeros], axis=1),
                                    jnp.concatenate([zeros, chunk(qr, 2 * pr + 1, j)], axis=1)], axis=0)
            k2 = kr[pr * LANES:(pr + 1) * LANES, j * RET_CHUNK:(j + 1) * RET_CHUNK]
            inner[pr][j] = _dot_tn(k2, q_bd) * dm_ref[pr]
    incr = [[_dot_nt(chunk(vr, h, j), chunk(kr, h, j) * kd_ref[h]) for j in range(nblk)]
            for h in range(RET_HEADS)]
    state = [[s_ref[h]] for h in range(RET_HEADS)]
    for h in range(RET_HEADS):
        for j in range(nblk):
            state[h].append(state[h][j] * cd_ref[h] + incr[h][j])
        s_ref[h] = state[h][nblk]
    for j in range(nblk):
        for h in range(RET_HEADS):
            u = h % 2
            lhs = jnp.concatenate([chunk(vr, h, j), state[h][j]], axis=1)
            rhs = jnp.concatenate([inner[h // 2][j][:, u * RET_CHUNK:(u + 1) * RET_CHUNK],
                                   chunk(qr, h, j) * qd_ref[h]], axis=0)
            or_heads[h][j] = _dot(lhs, rhs)

    @pl.when(t == pl.num_programs(1) - 1)
    def _():
        for h in range(RET_HEADS):
            sout_ref[0, h] = state[h][nblk].T

    gate_t = proj(C_G, IN_COLS)
    mix = [jnp.concatenate(blocks, axis=1) for blocks in oa_heads]
    for h in range(RET_HEADS):
        o_h = _head_norm_t(jnp.concatenate(or_heads[h], axis=1))
        g_h = head(gate_t, h)
        mix.append(o_h * (g_h * jax.nn.sigmoid(g_h)))
    mix_t = jnp.concatenate(mix, axis=0).astype(BF16)
    y_ref[0] = x + lax.dot_general(mix_t, wout_ref[...], (((0,), (0,)), ((), ())),
                                   preferred_element_type=F32)


def _sample_in_kernel(x_ref, ca_ref, sa_ref, cr_ref, sr_ref, nmix_ref, win_ref, qg_ref, kg_ref, bd_ref,
                      feat_ref):
    consts = _lane_consts()
    tabs = (ca_ref[...], sa_ref[...], cr_ref[...], sr_ref[...])
    qa, ka, va, qr, kr, vr, gate = _mixer_in(x_ref[...], nmix_ref[...], win_ref, qg_ref[...], kg_ref[...],
                                             bd_ref[...], tabs, consts)
    feat_ref[...] = jnp.concatenate(qa + [ka, va] + qr + kr + vr + gate, axis=1)


def _sample_seq_kernel(feat_ref, ck_ref, cv_ref, st_ref, sink_ref, dm_ref, qd_ref, kd_ref, cd_ref,
                       dup_ref, o_ref, kout_ref, vout_ref, sout_ref,
                       kbuf_ref, vbuf_ref, kpad_ref, vpad_ref, *, nseq, ntok):
    @pl.when(pl.program_id(0) == 0)
    def _():
        kbuf_ref[...] = jnp.zeros_like(kbuf_ref)
        vbuf_ref[...] = jnp.zeros_like(vbuf_ref)
        kpad_ref[...] = jnp.zeros_like(kpad_ref)
        vpad_ref[...] = jnp.zeros_like(vpad_ref)

    m_left, m_right, _, _ = _lane_consts()
    nkeys = 2 * WINDOW
    row_tok = lax.broadcasted_iota(jnp.int32, (4 * ntok, nkeys), 0) % ntok
    col = lax.broadcasted_iota(jnp.int32, (4 * ntok, nkeys), 1)
    valid = (col > row_tok) & (col <= row_tok + WINDOW)
    srow = lax.broadcasted_iota(jnp.int32, (LANES, LANES), 0)
    scol = lax.broadcasted_iota(jnp.int32, (LANES, LANES), 1)
    bd_mask = ((srow < RET_DIM) == (scol < RET_DIM)).astype(F32)
    top_rows = lax.broadcasted_iota(jnp.int32, (LANES, RET_DIM), 0) < RET_DIM

    def per_seq(i, carry):
        r0 = pl.multiple_of(i * ntok, ntok)
        rows = pl.ds(r0, ntok)
        slab = lambda c: feat_ref[rows, c:c + LANES]
        k_new, v_new = slab(C_KA), slab(C_VA)
        kbuf_ref[0:WINDOW] = ck_ref[i]
        vbuf_ref[0:WINDOW] = cv_ref[i]
        kbuf_ref[WINDOW:WINDOW + ntok] = k_new
        vbuf_ref[WINDOW:WINDOW + ntok] = v_new
        kout_ref[i, 0:WINDOW - ntok] = ck_ref[i, ntok:WINDOW]
        vout_ref[i, 0:WINDOW - ntok] = cv_ref[i, ntok:WINDOW]
        kout_ref[i, WINDOW - ntok:WINDOW] = k_new
        vout_ref[i, WINDOW - ntok:WINDOW] = v_new
        kfull, vfull = kbuf_ref[...], vbuf_ref[...]
        for g in range(SWA_KV_HEADS):
            kd_g = _dup_head(kfull, g, m_left, m_right)
            vd_g = _dup_head(vfull, g, m_left, m_right)
            qst = _stack_heads(slab(C_QA + 2 * g * LANES), slab(C_QA + (2 * g + 1) * LANES), m_left, m_right)
            o = _sink_softmax_pv(_dot_nt(qst, kd_g), valid, sink_ref[g], vd_g)
            n = ntok
            o_ref[rows, 2 * g * LANES:(2 * g + 1) * LANES] = o[0:n] * m_left + o[n:2 * n] * m_right
            o_ref[rows, (2 * g + 1) * LANES:(2 * g + 2) * LANES] = o[2 * n:3 * n] * m_left + o[3 * n:] * m_right

        for p in range(N_RET_SLABS):
            q2, k2, v2 = slab(C_QR + p * LANES), slab(C_KR + p * LANES), slab(C_VR + p * LANES)
            kpad_ref[0:ntok] = k2
            vpad_ref[0:ntok] = v2
            kpad, vpad = kpad_ref[...], vpad_ref[...]
            s_stack = st_ref[i, p]
            s_bd = jnp.dot(s_stack.astype(BF16), dup_ref[...], preferred_element_type=F32) * bd_mask
            qst = jnp.concatenate([q2 * m_left, q2 * m_right], axis=0)
            inner = _dot_nt(qst, kpad) * dm_ref[p]
            lhs = jnp.concatenate([inner, qst * qd_ref[p]], axis=1)
            rhs = jnp.concatenate([vpad, s_bd], axis=0)
            r = _dot(lhs, rhs)
            o_ref[rows, SWA_Q_W + p * LANES:SWA_Q_W + (p + 1) * LANES] = r[0:ntok] * m_left + r[ntok:] * m_right
            kdt = (kpad * kd_ref[p]).T
            u_same = _dot(kdt, vpad)
            u_swap = _dot(kdt, pltpu.roll(vpad, RET_DIM, 1))
            upd = jnp.where(top_rows, u_same[:, 0:RET_DIM], u_swap[:, 0:RET_DIM])
            sout_ref[i, p] = s_stack * cd_ref[p] + upd
        return carry

    lax.fori_loop(0, nseq, per_seq, 0)


def _sample_out_kernel(x_ref, o_ref, feat_ref, bd_ref, wout_ref, y_ref):
    o = o_ref[...]
    oa = [o[:, s * LANES:(s + 1) * LANES] for s in range(N_QA_SLABS)]
    o_ret = [o[:, SWA_Q_W + p * LANES:SWA_Q_W + (p + 1) * LANES] for p in range(N_RET_SLABS)]
    gate = [feat_ref[:, C_G + p * LANES:C_G + (p + 1) * LANES] for p in range(N_RET_SLABS)]
    y_ref[...] = _mixer_out(x_ref[...], oa, o_ret, gate, bd_ref[...], wout_ref)


def _mem_kv_kernel(m_ref, nmem_ref, wkv_ref, kg_ref, k_ref, v_ref):
    hb = (_rms(m_ref[...]) * nmem_ref[...]).astype(BF16)
    kv = jnp.dot(hb, wkv_ref[...], preferred_element_type=F32)
    for h in range(MEM_HEADS):
        kh = kv[:, h * LANES:(h + 1) * LANES]
        k_ref[:, h * LANES:(h + 1) * LANES] = _rms(kh) * kg_ref[...]
    v_ref[...] = kv[:, MEM_W:]


def _cross_attn_kernel(x_ref, mk_ref, mv_ref, ncross_ref, wq_ref, qg_ref, o_ref, *, nseq, tq):
    hb = (_rms(x_ref[...]) * ncross_ref[...]).astype(BF16)
    q = jnp.dot(hb, wq_ref[...], preferred_element_type=F32)
    for i in range(nseq):
        for h in range(MEM_HEADS):
            cols = slice(h * LANES, (h + 1) * LANES)
            qh = _rms(q[i * tq:(i + 1) * tq, cols]) * qg_ref[...]
            s = _dot_nt(qh, mk_ref[i, :, cols]) * (MEM_HEAD_DIM ** -0.5)
            m = jnp.max(s, axis=-1, keepdims=True)
            p = jnp.exp(s - m)
            denom = jnp.sum(p, axis=-1, keepdims=True)
            o_ref[i * tq:(i + 1) * tq, cols] = _dot(p, mv_ref[i, :, cols]) / denom


def _cross_out_ffn_kernel(x_ref, o_ref, wo_ref, nffn_ref, wgu_ref, wdown_ref, y_ref):
    x = x_ref[...] + jnp.dot(o_ref[...].astype(BF16), wo_ref[...], preferred_element_type=F32)
    hb = (_rms(x) * nffn_ref[...]).astype(BF16)
    g = jnp.dot(hb, wgu_ref[:, 0:FFN_HIDDEN], preferred_element_type=F32)
    u = jnp.dot(hb, wgu_ref[:, FFN_HIDDEN:], preferred_element_type=F32)
    act = (g * jax.nn.sigmoid(g) * u).astype(BF16)
    y_ref[...] = x + jnp.dot(act, wdown_ref[...], preferred_element_type=F32)


def _rope_tables(pos):
    half = HEAD_DIM // 2
    inv = 1.0 / (ROPE_THETA ** (jnp.arange(half, dtype=F32) / half))
    ang = pos.astype(F32)[:, None] * inv[None, :]
    cos, sin = jnp.cos(ang), jnp.sin(ang)
    c64 = jnp.concatenate([cos, cos], axis=-1)
    s64 = jnp.concatenate([-sin, sin], axis=-1)
    return jnp.tile(c64, (1, 2)), jnp.tile(s64, (1, 2))


def _ret_tables(pos):
    half = RET_DIM // 2
    inv = RET_THETA ** (-jnp.linspace(0.0, 1.0, half, dtype=F32))
    ang = pos.astype(F32)[:, None] * inv[None, :]
    cos, sin = jnp.cos(ang), jnp.sin(ang)
    c64 = jnp.repeat(cos, 2, axis=-1)
    s64 = jnp.stack([-sin, sin], axis=-1).reshape(pos.shape[0], RET_DIM)
    return jnp.tile(c64, (1, 2)), jnp.tile(s64, (1, 2))


def _decay_consts(c, pad_rows, pad_cols):
    log_g = jnp.log(1.0 - jnp.exp2(-5.0 - jnp.arange(RET_HEADS, dtype=F32)))
    idx = jnp.arange(c, dtype=F32)
    diff = idx[:, None] - idx[None, :]
    dmat = jnp.where(diff >= 0, jnp.exp(jnp.maximum(diff, 0.0)[None] * log_g[:, None, None]), 0.0)
    dm = jnp.pad(dmat.reshape(N_RET_SLABS, 2 * c, c), ((0, 0), (0, 0), (0, pad_cols - c)))
    qd = jnp.exp((idx + 1.0)[None, :] * log_g[:, None]).reshape(N_RET_SLABS, 2 * c, 1)
    kdec = jnp.exp((c - 1.0 - idx)[None, :] * log_g[:, None])
    kd = jnp.repeat(kdec.reshape(N_RET_SLABS, 2, c).transpose(0, 2, 1), RET_DIM, axis=-1)
    kd = jnp.pad(kd, ((0, 0), (0, pad_rows - c), (0, 0)))
    cd = jnp.repeat(jnp.exp(c * log_g).reshape(N_RET_SLABS, 2), RET_DIM, axis=-1)[..., None]
    return dm, qd, kd, cd


def _full(shape):
    nd = len(shape)
    return pl.BlockSpec(shape, lambda *_: (0,) * nd)


def _params(sem):
    return pltpu.CompilerParams(dimension_semantics=sem, vmem_limit_bytes=VMEM_LIMIT)


def _prompt_tables_t(pos):
    half = HEAD_DIM // 2
    inv = 1.0 / (ROPE_THETA ** (jnp.arange(half, dtype=F32) / half))
    ang = inv[:, None] * pos.astype(F32)[None, :]
    inv_r = RET_THETA ** (-jnp.linspace(0.0, 1.0, RET_DIM // 2, dtype=F32))
    ang_r = inv_r[:, None] * pos.astype(F32)[None, :]
    cos_r, sin_r = jnp.cos(ang_r), jnp.sin(ang_r)
    cr = jnp.repeat(cos_r, 2, axis=0)
    sr = jnp.stack([-sin_r, sin_r], axis=1).reshape(RET_DIM, pos.shape[0])
    return jnp.cos(ang), jnp.sin(ang), cr, sr


def _decay_consts_t(c):
    log_g = jnp.log(1.0 - jnp.exp2(-5.0 - jnp.arange(RET_HEADS, dtype=F32)))
    idx = jnp.arange(c, dtype=F32)
    diff = idx[:, None] - idx[None, :]
    dmat = jnp.where(diff >= 0, jnp.exp(jnp.maximum(diff, 0.0)[None] * log_g[:, None, None]), 0.0)
    dm = dmat.transpose(0, 2, 1).reshape(N_RET_SLABS, 2, c, c).transpose(0, 2, 1, 3).reshape(N_RET_SLABS, c, 2 * c)
    qd = jnp.exp((idx + 1.0)[None, :] * log_g[:, None])[:, None, :]
    kd = jnp.exp((c - 1.0 - idx)[None, :] * log_g[:, None])[:, None, :]
    cd = jnp.broadcast_to(jnp.exp(c * log_g)[:, None, None], (RET_HEADS, 1, RET_DIM))
    return dm, qd, kd, cd


def _prompt_mixer(x, w, tile):
    b, l, d = x.shape
    ca, sa, cr, sr = _prompt_tables_t(jnp.arange(l, dtype=jnp.int32))
    dm, qd, kd, cd = _decay_consts_t(RET_CHUNK)
    sink = jnp.repeat(w['sinks'].reshape(SWA_KV_HEADS, 1, -1), WINDOW, axis=-1)
    tab = lambda rows: pl.BlockSpec((rows, tile), lambda i, t: (0, t))
    xspec = pl.BlockSpec((1, tile, d), lambda i, t: (i, t, 0))
    win_spec = pl.BlockSpec((1, WINDOW, LANES), lambda i, t: (i, 0, 0))
    st_spec = pl.BlockSpec((1, RET_HEADS, RET_DIM, RET_DIM), lambda i, t: (i, 0, 0, 0))
    return pl.pallas_call(
        functools.partial(_prompt_mixer_kernel, tile=tile),
        grid=(b, l // tile),
        in_specs=[xspec, tab(HEAD_DIM // 2), tab(HEAD_DIM // 2), tab(RET_DIM), tab(RET_DIM), _full((1, d)),
                  _full((IN_COLS, d)), _full((HEAD_DIM, 1)), _full((HEAD_DIM, 1)), _full(sink.shape),
                  _full((d, d)), _full(dm.shape), _full(qd.shape), _full(kd.shape), _full(cd.shape)],
        out_specs=[xspec, win_spec, win_spec, st_spec],
        out_shape=[jax.ShapeDtypeStruct(x.shape, F32),
                   jax.ShapeDtypeStruct((b, WINDOW, LANES), F32),
                   jax.ShapeDtypeStruct((b, WINDOW, LANES), F32),
                   jax.ShapeDtypeStruct((b, RET_HEADS, RET_DIM, RET_DIM), F32)],
        scratch_shapes=[pltpu.VMEM((SWA_KV_W, WINDOW), F32), pltpu.VMEM((SWA_KV_W, WINDOW), F32),
                        pltpu.VMEM((RET_HEADS, RET_DIM, RET_DIM), F32)],
        compiler_params=_params(("arbitrary", "arbitrary")),
        name="prompt_mixer",
    )(x, ca, sa, cr, sr, w['norm_mix'], w['w_in_t'], w['qg_col'], w['kg_col'], sink, w['w_out'], dm, qd, kd,
      cd)


def _sample_mixer(x, cache_k, cache_v, state, w, tile, nseq):
    b, ntok, d = x.shape
    rows = b * ntok
    xf = x.reshape(rows, d)
    pos = jnp.tile(PAST_LEN + jnp.arange(ntok, dtype=jnp.int32), b)
    ca, sa = _rope_tables(pos)
    cr, sr = _ret_tables(pos)
    tab = pl.BlockSpec((tile, LANES), lambda i: (i, 0))
    row_spec = lambda width: pl.BlockSpec((tile, width), lambda i: (i, 0))
    feat = pl.pallas_call(
        _sample_in_kernel,
        grid=(rows // tile,),
        in_specs=[row_spec(d), tab, tab, tab, tab, _full((1, d)), _full((d, IN_COLS)), _full((1, LANES)),
                  _full((1, LANES)), _full((LANES, LANES))],
        out_specs=row_spec(IN_COLS),
        out_shape=jax.ShapeDtypeStruct((rows, IN_COLS), F32),
        compiler_params=_params(("arbitrary",)),
        name="sample_in",
    )(xf, ca, sa, cr, sr, w['norm_mix'], w['w_in'], w['qg'], w['kg'], w['bd'])

    dm, qd, kd, cd = _decay_consts(ntok, LANES, LANES)
    sink = jnp.repeat(w['sinks'].reshape(SWA_KV_HEADS, -1), ntok, axis=-1)[..., None]
    eye = jnp.eye(RET_DIM, dtype=BF16)
    dup = jnp.concatenate([eye, eye], axis=1)
    st = state.reshape(b, N_RET_SLABS, 2 * RET_DIM, RET_DIM)
    seq_rows = nseq * ntok
    cache_spec = pl.BlockSpec((nseq, WINDOW, LANES), lambda i: (i, 0, 0))
    st_spec = pl.BlockSpec((nseq, N_RET_SLABS, 2 * RET_DIM, RET_DIM), lambda i: (i, 0, 0, 0))
    o, k_out, v_out, s_out = pl.pallas_call(
        functools.partial(_sample_seq_kernel, nseq=nseq, ntok=ntok),
        grid=(b // nseq,),
        in_specs=[pl.BlockSpec((seq_rows, IN_COLS), lambda i: (i, 0)), cache_spec, cache_spec, st_spec,
                  _full(sink.shape), _full(dm.shape), _full(qd.shape), _full(kd.shape), _full(cd.shape),
                  _full(dup.shape)],
        out_specs=[pl.BlockSpec((seq_rows, d), lambda i: (i, 0)), cache_spec, cache_spec, st_spec],
        out_shape=[jax.ShapeDtypeStruct((rows, d), F32),
                   jax.ShapeDtypeStruct(cache_k.shape, F32),
                   jax.ShapeDtypeStruct(cache_v.shape, F32),
                   jax.ShapeDtypeStruct(st.shape, F32)],
        scratch_shapes=[pltpu.VMEM((2 * WINDOW, LANES), F32), pltpu.VMEM((2 * WINDOW, LANES), F32),
                        pltpu.VMEM((LANES, LANES), F32), pltpu.VMEM((LANES, LANES), F32)],
        compiler_params=_params(("arbitrary",)),
        name="sample_seq",
    )(feat, cache_k, cache_v, st, sink, dm, qd, kd, cd, dup)

    y = pl.pallas_call(
        _sample_out_kernel,
        grid=(rows // tile,),
        in_specs=[row_spec(d), row_spec(d), row_spec(IN_COLS), _full((LANES, LANES)), _full((d, d))],
        out_specs=row_spec(d),
        out_shape=jax.ShapeDtypeStruct((rows, d), F32),
        compiler_params=_params(("arbitrary",)),
        name="sample_out",
    )(xf, o, feat, w['bd'], w['w_out'])
    return y, k_out, v_out, s_out


def _mem_kv(mem, w, tile):
    rows, d = mem.shape
    row_spec = lambda width: pl.BlockSpec((tile, width), lambda i: (i, 0))
    return pl.pallas_call(
        _mem_kv_kernel,
        grid=(rows // tile,),
        in_specs=[row_spec(d), _full((1, d)), _full((d, 2 * MEM_W)), _full((1, LANES))],
        out_specs=[row_spec(MEM_W), row_spec(MEM_W)],
        out_shape=[jax.ShapeDtypeStruct((rows, MEM_W), F32)] * 2,
        compiler_params=_params(("arbitrary",)),
        name="mem_kv",
    )(mem, w['norm_mem'], w['w_mkv'], w['kgm'])


def _cross_attn(x, mk, mv, w, nseq, tq, name):
    rows, d = x.shape
    rows_per_mem = rows // mk.shape[0]
    steps_per_mem = rows_per_mem // tq if nseq == 1 else 1
    blk = nseq * tq
    if nseq == 1:
        mem_spec = pl.BlockSpec((1, MEM_LEN, MEM_W), lambda i: (i // steps_per_mem, 0, 0))
    else:
        mem_spec = pl.BlockSpec((nseq, MEM_LEN, MEM_W), lambda i: (i, 0, 0))
    return pl.pallas_call(
        functools.partial(_cross_attn_kernel, nseq=nseq, tq=tq),
        grid=(rows // blk,),
        in_specs=[pl.BlockSpec((blk, d), lambda i: (i, 0)), mem_spec, mem_spec, _full((1, d)),
                  _full((d, MEM_W)), _full((1, LANES))],
        out_specs=pl.BlockSpec((blk, MEM_W), lambda i: (i, 0)),
        out_shape=jax.ShapeDtypeStruct((rows, MEM_W), F32),
        compiler_params=_params(("arbitrary",)),
        name=name,
    )(x, mk, mv, w['norm_cross'], w['w_mq'], w['qgm'])


def _cross_out_ffn(x, o, w, tile, name):
    rows, d = x.shape
    row_spec = lambda width: pl.BlockSpec((tile, width), lambda i: (i, 0))
    return pl.pallas_call(
        _cross_out_ffn_kernel,
        grid=(rows // tile,),
        in_specs=[row_spec(d), row_spec(MEM_W), _full((MEM_W, d)), _full((1, d)),
                  _full((d, 2 * FFN_HIDDEN)), _full((FFN_HIDDEN, d))],
        out_specs=row_spec(d),
        out_shape=jax.ShapeDtypeStruct((rows, d), F32),
        compiler_params=_params(("arbitrary",)),
        name=name,
    )(x, o, w['w_mo'], w['norm_ffn'], w['w_gu'], w['w_down'])


def kernel(x_prompt, x_sample, mem_prompt, cache_swa_k, cache_swa_v, state_ret, cache_mem_k, cache_mem_v,
           norm_mix, w_in, q_norm_a, k_norm_a, sinks, w_out, norm_cross, norm_mem, w_mq, w_mkv,
           q_norm_m, k_norm_m, w_mo, norm_ffn, w_gu, w_down):
    assert norm_mix.shape[0] == 1, "single-layer kernel"
    b, l, d = x_prompt.shape
    sb, st, _ = x_sample.shape
    half = (jnp.arange(LANES) // HEAD_DIM)[:, None] == (jnp.arange(LANES) // HEAD_DIM)[None, :]
    w = {
        'norm_mix': norm_mix, 'norm_cross': norm_cross, 'norm_mem': norm_mem, 'norm_ffn': norm_ffn,
        'w_in': w_in[0].astype(BF16), 'w_out': w_out[0].astype(BF16), 'w_mq': w_mq[0].astype(BF16),
        'w_mkv': w_mkv[0].astype(BF16), 'w_mo': w_mo[0].astype(BF16), 'w_gu': w_gu[0].astype(BF16),
        'w_down': w_down[0].astype(BF16),
        'w_in_t': w_in[0].T.astype(BF16),
        'qg': jnp.tile(q_norm_a, (1, 2)), 'kg': jnp.tile(k_norm_a, (1, 2)),
        'qg_col': q_norm_a.reshape(HEAD_DIM, 1), 'kg_col': k_norm_a.reshape(HEAD_DIM, 1),
        'qgm': q_norm_m, 'kgm': k_norm_m, 'sinks': sinks[0],
        'bd': half.astype(BF16),
    }

    mk, mv = _mem_kv(mem_prompt.reshape(b * MEM_LEN, d), w, tile=256)
    xp, kwin, vwin, ret_p = _prompt_mixer(x_prompt, w, tile=512)
    xp = xp.reshape(b * l, d)
    o = _cross_attn(xp, mk.reshape(b, MEM_LEN, MEM_W), mv.reshape(b, MEM_LEN, MEM_W), w, nseq=1, tq=256,
                    name="prompt_cross_attn")
    yp = _cross_out_ffn(xp, o, w, tile=256, name="prompt_ffn").reshape(b, l, d)
    ret_p = ret_p.reshape(1, b, RET_HEADS, RET_DIM, RET_DIM)

    nbuf = cache_swa_k.shape[2]
    xs, k_s, v_s, s_s = _sample_mixer(x_sample, cache_swa_k.reshape(sb, nbuf, LANES),
                                      cache_swa_v.reshape(sb, nbuf, LANES), state_ret[0], w,
                                      tile=256, nseq=8)
    os_ = _cross_attn(xs, cache_mem_k.reshape(sb, MEM_LEN, MEM_W), cache_mem_v.reshape(sb, MEM_LEN, MEM_W),
                      w, nseq=8, tq=st, name="sample_cross_attn")
    ys = _cross_out_ffn(xs, os_, w, tile=256, name="sample_ffn").reshape(sb, st, d)

    kv_shape = (1, b, WINDOW, SWA_KV_HEADS, HEAD_DIM)
    mem_shape = (1, b, MEM_LEN, MEM_HEADS, MEM_HEAD_DIM)
    return (yp, ys, kwin.reshape(kv_shape), vwin.reshape(kv_shape), ret_p,
            mk.reshape(mem_shape), mv.reshape(mem_shape),
            k_s.reshape(cache_swa_k.shape), v_s.reshape(cache_swa_v.shape),
            s_s.reshape(state_ret.shape))
```

```python
import functools

import jax
import jax.numpy as jnp
from jax import lax
from jax.experimental import pallas as pl
from jax.experimental.pallas import tpu as pltpu

F32 = jnp.float32
BF16 = jnp.bfloat16

LANES = 128
D_MODEL = 1024
HEAD_DIM = 64
SWA_HEADS = 8
SWA_KV_HEADS = 2
WINDOW = 128
RET_HEADS = 8
RET_DIM = 64
RET_CHUNK = 128
RET_THETA = 10000.0
ROPE_THETA = 10000.0
MEM_LEN = 256
MEM_HEADS = 4
MEM_HEAD_DIM = 128
MEM_W = MEM_HEADS * MEM_HEAD_DIM
FFN_HIDDEN = 2816
RMS_EPS = 1e-6
NEG_INF = -1e30
PAST_LEN = 16384

SWA_Q_W = SWA_HEADS * HEAD_DIM
SWA_KV_W = SWA_KV_HEADS * HEAD_DIM
RET_W = RET_HEADS * RET_DIM
IN_COLS = SWA_Q_W + 2 * SWA_KV_W + 4 * RET_W
C_QA, C_KA, C_VA = 0, SWA_Q_W, SWA_Q_W + SWA_KV_W
C_QR = SWA_Q_W + 2 * SWA_KV_W
C_KR, C_VR, C_G = C_QR + RET_W, C_QR + 2 * RET_W, C_QR + 3 * RET_W
N_QA_SLABS = SWA_Q_W // LANES
N_RET_SLABS = RET_W // LANES

VMEM_LIMIT = 56 * 1024 * 1024


def _dot(a, b):
    return jnp.dot(a.astype(BF16), b.astype(BF16), preferred_element_type=F32)


def _dot_nt(a, b):
    return lax.dot_general(a.astype(BF16), b.astype(BF16), (((1,), (1,)), ((), ())),
                           preferred_element_type=F32)


def _rms(x):
    return x * lax.rsqrt(jnp.mean(x * x, axis=-1, keepdims=True) + RMS_EPS)


def _lane_consts():
    lane = lax.broadcasted_iota(jnp.int32, (1, LANES), 1)
    m_left = (lane < HEAD_DIM).astype(F32)
    m_right = 1.0 - m_left
    first_half = (lane % HEAD_DIM) < (HEAD_DIM // 2)
    even = (lane % 2) == 0
    return m_left, m_right, first_half, even


def _head_norm(y, bd):
    ss = jnp.dot((y * y).astype(BF16), bd, preferred_element_type=F32) * (1.0 / HEAD_DIM)
    return y * lax.rsqrt(ss + RMS_EPS)


def _rot_half(y, cos, sin_signed, first_half):
    swapped = jnp.where(first_half, pltpu.roll(y, LANES - HEAD_DIM // 2, 1), pltpu.roll(y, HEAD_DIM // 2, 1))
    return y * cos + swapped * sin_signed


def _rot_pairs(y, cos, sin_signed, even):
    swapped = jnp.where(even, pltpu.roll(y, LANES - 1, 1), pltpu.roll(y, 1, 1))
    return y * cos + swapped * sin_signed


def _dup_head(a, g, m_left, m_right):
    one = a * (m_left if g == 0 else m_right)
    return one + pltpu.roll(one, HEAD_DIM, 1)


def _mixer_in(x, nmix, win_ref, qg, kg, bd, tabs, consts):
    ca, sa, cr, sr = tabs
    m_left, m_right, first_half, even = consts
    hb = (_rms(x) * nmix).astype(BF16)
    swa = jnp.dot(hb, win_ref[:, C_QA:C_QR], preferred_element_type=F32)
    qa = [_rot_half(_head_norm(swa[:, s * LANES:(s + 1) * LANES], bd) * qg, ca, sa, first_half)
          for s in range(N_QA_SLABS)]
    ka = _rot_half(_head_norm(swa[:, C_KA:C_VA], bd) * kg, ca, sa, first_half)
    va = swa[:, C_VA:C_QR]
    qr_all = jnp.dot(hb, win_ref[:, C_QR:C_KR], preferred_element_type=F32)
    kr_all = jnp.dot(hb, win_ref[:, C_KR:C_VR], preferred_element_type=F32)
    vr_all = jnp.dot(hb, win_ref[:, C_VR:C_G], preferred_element_type=F32)
    g_all = jnp.dot(hb, win_ref[:, C_G:IN_COLS], preferred_element_type=F32)
    sl = lambda a, p: a[:, p * LANES:(p + 1) * LANES]
    qr = [_rot_pairs(sl(qr_all, p), cr, sr, even) for p in range(N_RET_SLABS)]
    kr = [_rot_pairs(sl(kr_all, p), cr, sr, even) * (RET_DIM ** -0.5) for p in range(N_RET_SLABS)]
    vr = [sl(vr_all, p) for p in range(N_RET_SLABS)]
    gate = [sl(g_all, p) for p in range(N_RET_SLABS)]
    return qa, ka, va, qr, kr, vr, gate


def _mixer_out(x, oa, o_ret, gate, bd, wout_ref):
    outs = list(oa)
    for p in range(N_RET_SLABS):
        outs.append(_head_norm(o_ret[p], bd) * (gate[p] * jax.nn.sigmoid(gate[p])))
    mix = jnp.concatenate(outs, axis=1).astype(BF16)
    return x + jnp.dot(mix, wout_ref[...], preferred_element_type=F32)


def _sink_softmax_pv(s, valid, sink, vd):
    s = jnp.where(valid, s * (HEAD_DIM ** -0.5), NEG_INF)
    m = jnp.maximum(jnp.max(s, axis=-1, keepdims=True), sink)
    p = jnp.exp(s - m)
    denom = jnp.sum(p, axis=-1, keepdims=True) + jnp.exp(sink - m)
    return _dot(p, vd) / denom


def _stack_heads(slab_a, slab_b, m_left, m_right):
    return jnp.concatenate([slab_a * m_left, slab_a * m_right, slab_b * m_left, slab_b * m_right], axis=0)


def _dot_tn(a, b):
    return lax.dot_general(a.astype(BF16), b.astype(BF16), (((0,), (0,)), ((), ())),
                           preferred_element_type=F32)


def _head_norm_t(y):
    return y * lax.rsqrt(jnp.mean(y * y, axis=0, keepdims=True) + RMS_EPS)


def _rot_half_t(y, cos, sin):
    half = HEAD_DIM // 2
    y1, y2 = y[0:half], y[half:]
    return jnp.concatenate([y1 * cos - y2 * sin, y2 * cos + y1 * sin], axis=0)


def _rot_pairs_t(y, cos, sin_signed, even_row):
    n = y.shape[0]
    swapped = jnp.where(even_row, pltpu.roll(y, n - 1, 0), pltpu.roll(y, 1, 0))
    return y * cos + swapped * sin_signed


def _prompt_mixer_kernel(x_ref, ca_ref, sa_ref, cr_ref, sr_ref, nmix_ref, wint_ref, qg_ref, kg_ref,
                         sink_ref, wout_ref, dm_ref, qd_ref, kd_ref, cd_ref,
                         y_ref, kwin_ref, vwin_ref, sout_ref,
                         pk_ref, pv_ref, s_ref, *, tile):
    t = pl.program_id(1)
    nblk = tile // WINDOW
    hd = HEAD_DIM

    @pl.when(t == 0)
    def _():
        pk_ref[...] = jnp.zeros_like(pk_ref)
        pv_ref[...] = jnp.zeros_like(pv_ref)
        s_ref[...] = jnp.zeros_like(s_ref)

    x = x_ref[0]
    hb = (_rms(x) * nmix_ref[...]).astype(BF16)
    proj = lambda lo, hi: lax.dot_general(wint_ref[lo:hi, :], hb, (((1,), (1,)), ((), ())),
                                          preferred_element_type=F32)
    ca, sa = ca_ref[...], sa_ref[...]
    cr, sr = cr_ref[...], sr_ref[...]
    qg, kg = qg_ref[...], kg_ref[...]

    swa_t = proj(C_QA, C_QR)
    head = lambda a, h: a[h * hd:(h + 1) * hd]
    qa = [_rot_half_t(_head_norm_t(head(swa_t, h)) * qg, ca, sa) for h in range(SWA_HEADS)]
    ka = jnp.concatenate([_rot_half_t(_head_norm_t(head(swa_t, SWA_HEADS + g)) * kg, ca, sa)
                          for g in range(SWA_KV_HEADS)], axis=0)
    va = swa_t[C_VA:C_QR]
    kfull = jnp.concatenate([pk_ref[...], ka], axis=1).astype(BF16)
    vfull = jnp.concatenate([pv_ref[...], va], axis=1).astype(BF16)
    pk_ref[...] = ka[:, tile - WINDOW:]
    pv_ref[...] = va[:, tile - WINDOW:]
    kwin_ref[0] = ka[:, tile - WINDOW:].T
    vwin_ref[0] = va[:, tile - WINDOW:].T

    group = SWA_HEADS // SWA_KV_HEADS
    key = lax.broadcasted_iota(jnp.int32, (2 * WINDOW, group * WINDOW), 0)
    qry = lax.broadcasted_iota(jnp.int32, (2 * WINDOW, group * WINDOW), 1) % WINDOW
    band = (key > qry) & (key <= qry + WINDOW)
    oa_heads = [[None] * nblk for _ in range(SWA_HEADS)]
    units = [(g, j) for j in range(nblk) for g in range(SWA_KV_HEADS)]
    keys_of = lambda j: slice(j * WINDOW, (j + 2) * WINDOW)
    scores = []
    for g, j in units:
        toks = slice(j * WINDOW, (j + 1) * WINDOW)
        q4 = jnp.concatenate([qa[group * g + u][:, toks] for u in range(group)], axis=1)
        scores.append(_dot_tn(head(kfull, g)[:, keys_of(j)], q4))
    probs, denoms = [], []
    for (g, j), s in zip(units, scores):
        sink = sink_ref[g]
        valid = band & (key >= WINDOW * (1 - t)) if j == 0 else band
        s = jnp.where(valid, s * (hd ** -0.5), NEG_INF)
        m = jnp.maximum(jnp.max(s, axis=0, keepdims=True), sink)
        p = jnp.exp(s - m)
        denoms.append(jnp.sum(p, axis=0, keepdims=True) + jnp.exp(sink - m))
        probs.append(p.astype(BF16))
    for (g, j), p, denom in zip(units, probs, denoms):
        o = jnp.dot(head(vfull, g)[:, keys_of(j)], p, preferred_element_type=F32) / denom
        for u in range(group):
            oa_heads[group * g + u][j] = o[:, u * WINDOW:(u + 1) * WINDOW]

    ret_t = proj(C_QR, C_VR)
    even_row = (lax.broadcasted_iota(jnp.int32, (RET_W, tile), 0) % 2) == 0
    cr8 = jnp.concatenate([cr] * RET_HEADS, axis=0)
    sr8 = jnp.concatenate([sr] * RET_HEADS, axis=0)
    qr = _rot_pairs_t(ret_t[0:RET_W], cr8, sr8, even_row)
    kr = _rot_pairs_t(ret_t[RET_W:], cr8, sr8, even_row) * (RET_DIM ** -0.5)
    vr = proj(C_VR, C_G)
    zeros = jnp.zeros((hd, RET_CHUNK), F32)
    or_heads = [[None] * nblk for _ in range(RET_HEADS)]
    chunk = lambda a, h, j: a[h * hd:(h + 1) * hd, j * RET_CHUNK:(j + 1) * RET_CHUNK]
    inner = [[None] * nblk for _ in range(N_RET_SLABS)]
    for j in range(nblk):
        for pr in range(N_RET_SLABS):
            q_bd = jnp.concatenate([jnp.concatenate([chunk(qr, 2 * pr, j), zeros], axis=1),
                                    jnp.concatenate([zeros, chunk(qr, 2 * pr + 1, j)], axis=1)], axis=0)
            k2 = kr[pr * LANES:(pr + 1) * LANES, j * RET_CHUNK:(j + 1) * RET_CHUNK]
            inner[pr][j] = _dot_tn(k2, q_bd) * dm_ref[pr]
    incr = [[_dot_nt(chunk(vr, h, j), chunk(kr, h, j) * kd_ref[h]) for j in range(nblk)]
            for h in range(RET_HEADS)]
    state = [[s_ref[h]] for h in range(RET_HEADS)]
    for h in range(RET_HEADS):
        for j in range(nblk):
            state[h].append(state[h][j] * cd_ref[h] + incr[h][j])
        s_ref[h] = state[h][nblk]
    for j in range(nblk):
        for h in range(RET_HEADS):
            u = h % 2
            lhs = jnp.concatenate([chunk(vr, h, j), state[h][j]], axis=1)
            rhs = jnp.concatenate([inner[h // 2][j][:, u * RET_CHUNK:(u + 1) * RET_CHUNK],
                                   chunk(qr, h, j) * qd_ref[h]], axis=0)
            or_heads[h][j] = _dot(lhs, rhs)

    @pl.when(t == pl.num_programs(1) - 1)
    def _():
        for h in range(RET_HEADS):
            sout_ref[0, h] = state[h][nblk].T

    gate_t = proj(C_G, IN_COLS)
    mix = [jnp.concatenate(blocks, axis=1) for blocks in oa_heads]
    for h in range(RET_HEADS):
        o_h = _head_norm_t(jnp.concatenate(or_heads[h], axis=1))
        g_h = head(gate_t, h)
        mix.append(o_h * (g_h * jax.nn.sigmoid(g_h)))
    mix_t = jnp.concatenate(mix, axis=0).astype(BF16)
    y_ref[0] = x + lax.dot_general(mix_t, wout_ref[...], (((0,), (0,)), ((), ())),
                                   preferred_element_type=F32)


def _sample_in_kernel(x_ref, ca_ref, sa_ref, cr_ref, sr_ref, nmix_ref, win_ref, qg_ref, kg_ref, bd_ref,
                      feat_ref):
    consts = _lane_consts()
    tabs = (ca_ref[...], sa_ref[...], cr_ref[...], sr_ref[...])
    qa, ka, va, qr, kr, vr, gate = _mixer_in(x_ref[...], nmix_ref[...], win_ref, qg_ref[...], kg_ref[...],
                                             bd_ref[...], tabs, consts)
    feat_ref[...] = jnp.concatenate(qa + [ka, va] + qr + kr + vr + gate, axis=1)


def _sample_seq_kernel(feat_ref, ck_ref, cv_ref, st_ref, sink_ref, dm_ref, qd_ref, kd_ref, cd_ref,
                       dup_ref, o_ref, kout_ref, vout_ref, sout_ref,
                       kbuf_ref, vbuf_ref, kpad_ref, vpad_ref, *, nseq, ntok):
    @pl.when(pl.program_id(0) == 0)
    def _():
        kbuf_ref[...] = jnp.zeros_like(kbuf_ref)
        vbuf_ref[...] = jnp.zeros_like(vbuf_ref)
        kpad_ref[...] = jnp.zeros_like(kpad_ref)
        vpad_ref[...] = jnp.zeros_like(vpad_ref)

    m_left, m_right, _, _ = _lane_consts()
    nkeys = 2 * WINDOW
    row_tok = lax.broadcasted_iota(jnp.int32, (4 * ntok, nkeys), 0) % ntok
    col = lax.broadcasted_iota(jnp.int32, (4 * ntok, nkeys), 1)
    valid = (col > row_tok) & (col <= row_tok + WINDOW)
    srow = lax.broadcasted_iota(jnp.int32, (LANES, LANES), 0)
    scol = lax.broadcasted_iota(jnp.int32, (LANES, LANES), 1)
    bd_mask = ((srow < RET_DIM) == (scol < RET_DIM)).astype(F32)
    top_rows = lax.broadcasted_iota(jnp.int32, (LANES, RET_DIM), 0) < RET_DIM

    def per_seq(i, carry):
        r0 = pl.multiple_of(i * ntok, ntok)
        rows = pl.ds(r0, ntok)
        slab = lambda c: feat_ref[rows, c:c + LANES]
        k_new, v_new = slab(C_KA), slab(C_VA)
        kbuf_ref[0:WINDOW] = ck_ref[i]
        vbuf_ref[0:WINDOW] = cv_ref[i]
        kbuf_ref[WINDOW:WINDOW + ntok] = k_new
        vbuf_ref[WINDOW:WINDOW + ntok] = v_new
        kout_ref[i, 0:WINDOW - ntok] = ck_ref[i, ntok:WINDOW]
        vout_ref[i, 0:WINDOW - ntok] = cv_ref[i, ntok:WINDOW]
        kout_ref[i, WINDOW - ntok:WINDOW] = k_new
        vout_ref[i, WINDOW - ntok:WINDOW] = v_new
        kfull, vfull = kbuf_ref[...], vbuf_ref[...]
        for g in range(SWA_KV_HEADS):
            kd_g = _dup_head(kfull, g, m_left, m_right)
            vd_g = _dup_head(vfull, g, m_left, m_right)
            qst = _stack_heads(slab(C_QA + 2 * g * LANES), slab(C_QA + (2 * g + 1) * LANES), m_left, m_right)
            o = _sink_softmax_pv(_dot_nt(qst, kd_g), valid, sink_ref[g], vd_g)
            n = ntok
            o_ref[rows, 2 * g * LANES:(2 * g + 1) * LANES] = o[0:n] * m_left + o[n:2 * n] * m_right
            o_ref[rows, (2 * g + 1) * LANES:(2 * g + 2) * LANES] = o[2 * n:3 * n] * m_left + o[3 * n:] * m_right

        for p in range(N_RET_SLABS):
            q2, k2, v2 = slab(C_QR + p * LANES), slab(C_KR + p * LANES), slab(C_VR + p * LANES)
            kpad_ref[0:ntok] = k2
            vpad_ref[0:ntok] = v2
            kpad, vpad = kpad_ref[...], vpad_ref[...]
            s_stack = st_ref[i, p]
            s_bd = jnp.dot(s_stack.astype(BF16), dup_ref[...], preferred_element_type=F32) * bd_mask
            qst = jnp.concatenate([q2 * m_left, q2 * m_right], axis=0)
            inner = _dot_nt(qst, kpad) * dm_ref[p]
            lhs = jnp.concatenate([inner, qst * qd_ref[p]], axis=1)
            rhs = jnp.concatenate([vpad, s_bd], axis=0)
            r = _dot(lhs, rhs)
            o_ref[rows, SWA_Q_W + p * LANES:SWA_Q_W + (p + 1) * LANES] = r[0:ntok] * m_left + r[ntok:] * m_right
            kdt = (kpad * kd_ref[p]).T
            u_same = _dot(kdt, vpad)
            u_swap = _dot(kdt, pltpu.roll(vpad, RET_DIM, 1))
            upd = jnp.where(top_rows, u_same[:, 0:RET_DIM], u_swap[:, 0:RET_DIM])
            sout_ref[i, p] = s_stack * cd_ref[p] + upd
        return carry

    lax.fori_loop(0, nseq, per_seq, 0)


def _sample_out_kernel(x_ref, o_ref, feat_ref, bd_ref, wout_ref, y_ref):
    o = o_ref[...]
    oa = [o[:, s * LANES:(s + 1) * LANES] for s in range(N_QA_SLABS)]
    o_ret = [o[:, SWA_Q_W + p * LANES:SWA_Q_W + (p + 1) * LANES] for p in range(N_RET_SLABS)]
    gate = [feat_ref[:, C_G + p * LANES:C_G + (p + 1) * LANES] for p in range(N_RET_SLABS)]
    y_ref[...] = _mixer_out(x_ref[...], oa, o_ret, gate, bd_ref[...], wout_ref)


def _mem_kv_kernel(m_ref, nmem_ref, wkv_ref, kg_ref, k_ref, v_ref):
    hb = (_rms(m_ref[...]) * nmem_ref[...]).astype(BF16)
    kv = jnp.dot(hb, wkv_ref[...], preferred_element_type=F32)
    for h in range(MEM_HEADS):
        kh = kv[:, h * LANES:(h + 1) * LANES]
        k_ref[:, h * LANES:(h + 1) * LANES] = _rms(kh) * kg_ref[...]
    v_ref[...] = kv[:, MEM_W:]


def _mem_queries(x, ncross, wq_ref, qg):
    hb = (_rms(x) * ncross).astype(BF16)
    q = jnp.dot(hb, wq_ref[...], preferred_element_type=F32)
    return [_rms(q[:, h * LANES:(h + 1) * LANES]) * qg for h in range(MEM_HEADS)]


def _prompt_cross_attn_kernel(x_ref, mk_ref, mv_ref, ncross_ref, wq_ref, qg_ref, ot_ref):
    qn = _mem_queries(x_ref[...], ncross_ref[...], wq_ref, qg_ref[...])
    head = lambda ref, h: ref[0, :, h * LANES:(h + 1) * LANES].astype(BF16)
    scores = [_dot_nt(head(mk_ref, h), qn[h]) for h in range(MEM_HEADS)]
    probs, denoms = [], []
    for s in scores:
        s = s * (MEM_HEAD_DIM ** -0.5)
        p = jnp.exp(s - jnp.max(s, axis=0, keepdims=True))
        denoms.append(jnp.sum(p, axis=0, keepdims=True))
        probs.append(p.astype(BF16))
    for h in range(MEM_HEADS):
        ot_ref[0, h * LANES:(h + 1) * LANES, :] = _dot_tn(head(mv_ref, h), probs[h]) / denoms[h]


def _sample_cross_attn_kernel(x_ref, mk_ref, mv_ref, ncross_ref, wq_ref, qg_ref, o_ref, *, nseq, tq):
    qn = _mem_queries(x_ref[...], ncross_ref[...], wq_ref, qg_ref[...])
    nrow = MEM_HEADS * tq
    nmem = MEM_LEN * MEM_HEADS
    own = (lax.broadcasted_iota(jnp.int32, (nrow, nmem), 0) // tq
           == lax.broadcasted_iota(jnp.int32, (nrow, nmem), 1) % MEM_HEADS)
    scores = []
    for i in range(nseq):
        q_stack = jnp.concatenate([qn[h][i * tq:(i + 1) * tq] for h in range(MEM_HEADS)], axis=0)
        scores.append(_dot_nt(q_stack, mk_ref[i]))
    probs, denoms = [], []
    for s in scores:
        s = jnp.where(own, s * (MEM_HEAD_DIM ** -0.5), NEG_INF)
        p = jnp.exp(s - jnp.max(s, axis=-1, keepdims=True))
        denoms.append(jnp.sum(p, axis=-1, keepdims=True))
        probs.append(p.astype(BF16))
    for i in range(nseq):
        o = jnp.dot(probs[i], mv_ref[i].astype(BF16), preferred_element_type=F32) / denoms[i]
        for h in range(MEM_HEADS):
            o_ref[i * tq:(i + 1) * tq, h * LANES:(h + 1) * LANES] = o[h * tq:(h + 1) * tq]


def _cross_out_ffn_kernel(x_ref, o_ref, wo_ref, nffn_ref, wgu_ref, wdown_ref, y_ref, *, o_feature_major):
    if o_feature_major:
        attn = lax.dot_general(o_ref[0].astype(BF16), wo_ref[...], (((0,), (0,)), ((), ())),
                               preferred_element_type=F32)
    else:
        attn = jnp.dot(o_ref[...].astype(BF16), wo_ref[...], preferred_element_type=F32)
    x = x_ref[...] + attn
    hb = (_rms(x) * nffn_ref[...]).astype(BF16)
    g = jnp.dot(hb, wgu_ref[:, 0:FFN_HIDDEN], preferred_element_type=F32)
    u = jnp.dot(hb, wgu_ref[:, FFN_HIDDEN:], preferred_element_type=F32)
    act = (g * jax.nn.sigmoid(g) * u).astype(BF16)
    y_ref[...] = x + jnp.dot(act, wdown_ref[...], preferred_element_type=F32)


def _rope_tables(pos):
    half = HEAD_DIM // 2
    inv = 1.0 / (ROPE_THETA ** (jnp.arange(half, dtype=F32) / half))
    ang = pos.astype(F32)[:, None] * inv[None, :]
    cos, sin = jnp.cos(ang), jnp.sin(ang)
    c64 = jnp.concatenate([cos, cos], axis=-1)
    s64 = jnp.concatenate([-sin, sin], axis=-1)
    return jnp.tile(c64, (1, 2)), jnp.tile(s64, (1, 2))


def _ret_tables(pos):
    half = RET_DIM // 2
    inv = RET_THETA ** (-jnp.linspace(0.0, 1.0, half, dtype=F32))
    ang = pos.astype(F32)[:, None] * inv[None, :]
    cos, sin = jnp.cos(ang), jnp.sin(ang)
    c64 = jnp.repeat(cos, 2, axis=-1)
    s64 = jnp.stack([-sin, sin], axis=-1).reshape(pos.shape[0], RET_DIM)
    return jnp.tile(c64, (1, 2)), jnp.tile(s64, (1, 2))


def _decay_consts(c, pad_rows, pad_cols):
    log_g = jnp.log(1.0 - jnp.exp2(-5.0 - jnp.arange(RET_HEADS, dtype=F32)))
    idx = jnp.arange(c, dtype=F32)
    diff = idx[:, None] - idx[None, :]
    dmat = jnp.where(diff >= 0, jnp.exp(jnp.maximum(diff, 0.0)[None] * log_g[:, None, None]), 0.0)
    dm = jnp.pad(dmat.reshape(N_RET_SLABS, 2 * c, c), ((0, 0), (0, 0), (0, pad_cols - c)))
    qd = jnp.exp((idx + 1.0)[None, :] * log_g[:, None]).reshape(N_RET_SLABS, 2 * c, 1)
    kdec = jnp.exp((c - 1.0 - idx)[None, :] * log_g[:, None])
    kd = jnp.repeat(kdec.reshape(N_RET_SLABS, 2, c).transpose(0, 2, 1), RET_DIM, axis=-1)
    kd = jnp.pad(kd, ((0, 0), (0, pad_rows - c), (0, 0)))
    cd = jnp.repeat(jnp.exp(c * log_g).reshape(N_RET_SLABS, 2), RET_DIM, axis=-1)[..., None]
    return dm, qd, kd, cd


def _full(shape):
    nd = len(shape)
    return pl.BlockSpec(shape, lambda *_: (0,) * nd)


def _params(sem):
    return pltpu.CompilerParams(dimension_semantics=sem, vmem_limit_bytes=VMEM_LIMIT)


def _prompt_tables_t(pos):
    half = HEAD_DIM // 2
    inv = 1.0 / (ROPE_THETA ** (jnp.arange(half, dtype=F32) / half))
    ang = inv[:, None] * pos.astype(F32)[None, :]
    inv_r = RET_THETA ** (-jnp.linspace(0.0, 1.0, RET_DIM // 2, dtype=F32))
    ang_r = inv_r[:, None] * pos.astype(F32)[None, :]
    cos_r, sin_r = jnp.cos(ang_r), jnp.sin(ang_r)
    cr = jnp.repeat(cos_r, 2, axis=0)
    sr = jnp.stack([-sin_r, sin_r], axis=1).reshape(RET_DIM, pos.shape[0])
    return jnp.cos(ang), jnp.sin(ang), cr, sr


def _decay_consts_t(c):
    log_g = jnp.log(1.0 - jnp.exp2(-5.0 - jnp.arange(RET_HEADS, dtype=F32)))
    idx = jnp.arange(c, dtype=F32)
    diff = idx[:, None] - idx[None, :]
    dmat = jnp.where(diff >= 0, jnp.exp(jnp.maximum(diff, 0.0)[None] * log_g[:, None, None]), 0.0)
    dm = dmat.transpose(0, 2, 1).reshape(N_RET_SLABS, 2, c, c).transpose(0, 2, 1, 3).reshape(N_RET_SLABS, c, 2 * c)
    qd = jnp.exp((idx + 1.0)[None, :] * log_g[:, None])[:, None, :]
    kd = jnp.exp((c - 1.0 - idx)[None, :] * log_g[:, None])[:, None, :]
    cd = jnp.broadcast_to(jnp.exp(c * log_g)[:, None, None], (RET_HEADS, 1, RET_DIM))
    return dm, qd, kd, cd


def _prompt_mixer(x, w, tile):
    b, l, d = x.shape
    ca, sa, cr, sr = _prompt_tables_t(jnp.arange(l, dtype=jnp.int32))
    dm, qd, kd, cd = _decay_consts_t(RET_CHUNK)
    sink = jnp.repeat(w['sinks'].reshape(SWA_KV_HEADS, 1, -1), WINDOW, axis=-1)
    tab = lambda rows: pl.BlockSpec((rows, tile), lambda i, t: (0, t))
    xspec = pl.BlockSpec((1, tile, d), lambda i, t: (i, t, 0))
    win_spec = pl.BlockSpec((1, WINDOW, LANES), lambda i, t: (i, 0, 0))
    st_spec = pl.BlockSpec((1, RET_HEADS, RET_DIM, RET_DIM), lambda i, t: (i, 0, 0, 0))
    return pl.pallas_call(
        functools.partial(_prompt_mixer_kernel, tile=tile),
        grid=(b, l // tile),
        in_specs=[xspec, tab(HEAD_DIM // 2), tab(HEAD_DIM // 2), tab(RET_DIM), tab(RET_DIM), _full((1, d)),
                  _full((IN_COLS, d)), _full((HEAD_DIM, 1)), _full((HEAD_DIM, 1)), _full(sink.shape),
                  _full((d, d)), _full(dm.shape), _full(qd.shape), _full(kd.shape), _full(cd.shape)],
        out_specs=[xspec, win_spec, win_spec, st_spec],
        out_shape=[jax.ShapeDtypeStruct(x.shape, F32),
                   jax.ShapeDtypeStruct((b, WINDOW, LANES), F32),
                   jax.ShapeDtypeStruct((b, WINDOW, LANES), F32),
                   jax.ShapeDtypeStruct((b, RET_HEADS, RET_DIM, RET_DIM), F32)],
        scratch_shapes=[pltpu.VMEM((SWA_KV_W, WINDOW), F32), pltpu.VMEM((SWA_KV_W, WINDOW), F32),
                        pltpu.VMEM((RET_HEADS, RET_DIM, RET_DIM), F32)],
        compiler_params=_params(("arbitrary", "arbitrary")),
        name="prompt_mixer",
    )(x, ca, sa, cr, sr, w['norm_mix'], w['w_in_t'], w['qg_col'], w['kg_col'], sink, w['w_out'], dm, qd, kd,
      cd)


def _sample_mixer(x, cache_k, cache_v, state, w, tile, nseq):
    b, ntok, d = x.shape
    rows = b * ntok
    xf = x.reshape(rows, d)
    pos = jnp.tile(PAST_LEN + jnp.arange(ntok, dtype=jnp.int32), b)
    ca, sa = _rope_tables(pos)
    cr, sr = _ret_tables(pos)
    tab = pl.BlockSpec((tile, LANES), lambda i: (i, 0))
    row_spec = lambda width: pl.BlockSpec((tile, width), lambda i: (i, 0))
    feat = pl.pallas_call(
        _sample_in_kernel,
        grid=(rows // tile,),
        in_specs=[row_spec(d), tab, tab, tab, tab, _full((1, d)), _full((d, IN_COLS)), _full((1, LANES)),
                  _full((1, LANES)), _full((LANES, LANES))],
        out_specs=row_spec(IN_COLS),
        out_shape=jax.ShapeDtypeStruct((rows, IN_COLS), F32),
        compiler_params=_params(("arbitrary",)),
        name="sample_in",
    )(xf, ca, sa, cr, sr, w['norm_mix'], w['w_in'], w['qg'], w['kg'], w['bd'])

    dm, qd, kd, cd = _decay_consts(ntok, LANES, LANES)
    sink = jnp.repeat(w['sinks'].reshape(SWA_KV_HEADS, -1), ntok, axis=-1)[..., None]
    eye = jnp.eye(RET_DIM, dtype=BF16)
    dup = jnp.concatenate([eye, eye], axis=1)
    st = state.reshape(b, N_RET_SLABS, 2 * RET_DIM, RET_DIM)
    seq_rows = nseq * ntok
    cache_spec = pl.BlockSpec((nseq, WINDOW, LANES), lambda i: (i, 0, 0))
    st_spec = pl.BlockSpec((nseq, N_RET_SLABS, 2 * RET_DIM, RET_DIM), lambda i: (i, 0, 0, 0))
    o, k_out, v_out, s_out = pl.pallas_call(
        functools.partial(_sample_seq_kernel, nseq=nseq, ntok=ntok),
        grid=(b // nseq,),
        in_specs=[pl.BlockSpec((seq_rows, IN_COLS), lambda i: (i, 0)), cache_spec, cache_spec, st_spec,
                  _full(sink.shape), _full(dm.shape), _full(qd.shape), _full(kd.shape), _full(cd.shape),
                  _full(dup.shape)],
        out_specs=[pl.BlockSpec((seq_rows, d), lambda i: (i, 0)), cache_spec, cache_spec, st_spec],
        out_shape=[jax.ShapeDtypeStruct((rows, d), F32),
                   jax.ShapeDtypeStruct(cache_k.shape, F32),
                   jax.ShapeDtypeStruct(cache_v.shape, F32),
                   jax.ShapeDtypeStruct(st.shape, F32)],
        scratch_shapes=[pltpu.VMEM((2 * WINDOW, LANES), F32), pltpu.VMEM((2 * WINDOW, LANES), F32),
                        pltpu.VMEM((LANES, LANES), F32), pltpu.VMEM((LANES, LANES), F32)],
        compiler_params=_params(("arbitrary",)),
        name="sample_seq",
    )(feat, cache_k, cache_v, st, sink, dm, qd, kd, cd, dup)

    y = pl.pallas_call(
        _sample_out_kernel,
        grid=(rows // tile,),
        in_specs=[row_spec(d), row_spec(d), row_spec(IN_COLS), _full((LANES, LANES)), _full((d, d))],
        out_specs=row_spec(d),
        out_shape=jax.ShapeDtypeStruct((rows, d), F32),
        compiler_params=_params(("arbitrary",)),
        name="sample_out",
    )(xf, o, feat, w['bd'], w['w_out'])
    return y, k_out, v_out, s_out


def _mem_kv(mem, w, tile):
    rows, d = mem.shape
    row_spec = lambda width: pl.BlockSpec((tile, width), lambda i: (i, 0))
    return pl.pallas_call(
        _mem_kv_kernel,
        grid=(rows // tile,),
        in_specs=[row_spec(d), _full((1, d)), _full((d, 2 * MEM_W)), _full((1, LANES))],
        out_specs=[row_spec(MEM_W), row_spec(MEM_W)],
        out_shape=[jax.ShapeDtypeStruct((rows, MEM_W), F32)] * 2,
        compiler_params=_params(("arbitrary",)),
        name="mem_kv",
    )(mem, w['norm_mem'], w['w_mkv'], w['kgm'])


def _prompt_cross_attn(x, mk, mv, w, tile):
    rows, d = x.shape
    b = mk.shape[0]
    steps = rows // b // tile
    mem_spec = pl.BlockSpec((1, MEM_LEN, MEM_W), lambda i, t: (i, 0, 0))
    return pl.pallas_call(
        _prompt_cross_attn_kernel,
        grid=(b, steps),
        in_specs=[pl.BlockSpec((tile, d), lambda i, t: (i * steps + t, 0)), mem_spec, mem_spec,
                  _full((1, d)), _full((d, MEM_W)), _full((1, LANES))],
        out_specs=pl.BlockSpec((1, MEM_W, tile), lambda i, t: (i, 0, t)),
        out_shape=jax.ShapeDtypeStruct((b, MEM_W, rows // b), F32),
        compiler_params=_params(("arbitrary", "arbitrary")),
        name="prompt_cross_attn",
    )(x, mk, mv, w['norm_cross'], w['w_mq'], w['qgm'])


def _sample_cross_attn(x, mk, mv, w, nseq, tq):
    rows, d = x.shape
    blk = nseq * tq
    mem_spec = pl.BlockSpec((nseq, MEM_LEN * MEM_HEADS, MEM_HEAD_DIM), lambda i: (i, 0, 0))
    return pl.pallas_call(
        functools.partial(_sample_cross_attn_kernel, nseq=nseq, tq=tq),
        grid=(rows // blk,),
        in_specs=[pl.BlockSpec((blk, d), lambda i: (i, 0)), mem_spec, mem_spec, _full((1, d)),
                  _full((d, MEM_W)), _full((1, LANES))],
        out_specs=pl.BlockSpec((blk, MEM_W), lambda i: (i, 0)),
        out_shape=jax.ShapeDtypeStruct((rows, MEM_W), F32),
        compiler_params=_params(("arbitrary",)),
        name="sample_cross_attn",
    )(x, mk, mv, w['norm_cross'], w['w_mq'], w['qgm'])


def _cross_out_ffn(x, o, w, tile, name):
    rows, d = x.shape
    o_feature_major = o.ndim == 3
    row_spec = lambda width: pl.BlockSpec((tile, width), lambda i: (i, 0))
    single = lambda shape: pl.BlockSpec(shape, lambda i: (0,) * len(shape), pipeline_mode=pl.Buffered(1))
    if o_feature_major:
        steps = o.shape[2] // tile
        o_spec = pl.BlockSpec((1, MEM_W, tile), lambda i: (i // steps, 0, i % steps))
    else:
        o_spec = row_spec(MEM_W)
    return pl.pallas_call(
        functools.partial(_cross_out_ffn_kernel, o_feature_major=o_feature_major),
        grid=(rows // tile,),
        in_specs=[row_spec(d), o_spec, single((MEM_W, d)), _full((1, d)),
                  single((d, 2 * FFN_HIDDEN)), single((FFN_HIDDEN, d))],
        out_specs=row_spec(d),
        out_shape=jax.ShapeDtypeStruct((rows, d), F32),
        compiler_params=_params(("arbitrary",)),
        name=name,
    )(x, o, w['w_mo'], w['norm_ffn'], w['w_gu'], w['w_down'])


def kernel(x_prompt, x_sample, mem_prompt, cache_swa_k, cache_swa_v, state_ret, cache_mem_k, cache_mem_v,
           norm_mix, w_in, q_norm_a, k_norm_a, sinks, w_out, norm_cross, norm_mem, w_mq, w_mkv,
           q_norm_m, k_norm_m, w_mo, norm_ffn, w_gu, w_down):
    assert norm_mix.shape[0] == 1, "single-layer kernel"
    b, l, d = x_prompt.shape
    sb, st, _ = x_sample.shape
    half = (jnp.arange(LANES) // HEAD_DIM)[:, None] == (jnp.arange(LANES) // HEAD_DIM)[None, :]
    w = {
        'norm_mix': norm_mix, 'norm_cross': norm_cross, 'norm_mem': norm_mem, 'norm_ffn': norm_ffn,
        'w_in': w_in[0].astype(BF16), 'w_out': w_out[0].astype(BF16), 'w_mq': w_mq[0].astype(BF16),
        'w_mkv': w_mkv[0].astype(BF16), 'w_mo': w_mo[0].astype(BF16), 'w_gu': w_gu[0].astype(BF16),
        'w_down': w_down[0].astype(BF16),
        'w_in_t': w_in[0].T.astype(BF16),
        'qg': jnp.tile(q_norm_a, (1, 2)), 'kg': jnp.tile(k_norm_a, (1, 2)),
        'qg_col': q_norm_a.reshape(HEAD_DIM, 1), 'kg_col': k_norm_a.reshape(HEAD_DIM, 1),
        'qgm': q_norm_m, 'kgm': k_norm_m, 'sinks': sinks[0],
        'bd': half.astype(BF16),
    }

    mk, mv = _mem_kv(mem_prompt.reshape(b * MEM_LEN, d), w, tile=256)
    xp, kwin, vwin, ret_p = _prompt_mixer(x_prompt, w, tile=512)
    xp = xp.reshape(b * l, d)
    o_t = _prompt_cross_attn(xp, mk.reshape(b, MEM_LEN, MEM_W), mv.reshape(b, MEM_LEN, MEM_W), w, tile=512)
    yp = _cross_out_ffn(xp, o_t, w, tile=512, name="prompt_ffn").reshape(b, l, d)
    ret_p = ret_p.reshape(1, b, RET_HEADS, RET_DIM, RET_DIM)

    nbuf = cache_swa_k.shape[2]
    xs, k_s, v_s, s_s = _sample_mixer(x_sample, cache_swa_k.reshape(sb, nbuf, LANES),
                                      cache_swa_v.reshape(sb, nbuf, LANES), state_ret[0], w,
                                      tile=256, nseq=8)
    mem_rows = (sb, MEM_LEN * MEM_HEADS, MEM_HEAD_DIM)
    os_ = _sample_cross_attn(xs, cache_mem_k.reshape(mem_rows), cache_mem_v.reshape(mem_rows), w, nseq=8, tq=st)
    ys = _cross_out_ffn(xs, os_, w, tile=512, name="sample_ffn").reshape(sb, st, d)

    kv_shape = (1, b, WINDOW, SWA_KV_HEADS, HEAD_DIM)
    mem_shape = (1, b, MEM_LEN, MEM_HEADS, MEM_HEAD_DIM)
    return (yp, ys, kwin.reshape(kv_shape), vwin.reshape(kv_shape), ret_p,
            mk.reshape(mem_shape), mv.reshape(mem_shape),
            k_s.reshape(cache_swa_k.shape), v_s.reshape(cache_swa_v.shape),
            s_s.reshape(state_ret.shape))
```

```python
import functools

import jax
import jax.numpy as jnp
from jax import lax
from jax.experimental import pallas as pl
from jax.experimental.pallas import tpu as pltpu

F32 = jnp.float32
BF16 = jnp.bfloat16

LANES = 128
D_MODEL = 1024
HEAD_DIM = 64
SWA_HEADS = 8
SWA_KV_HEADS = 2
WINDOW = 128
RET_HEADS = 8
RET_DIM = 64
RET_CHUNK = 128
RET_THETA = 10000.0
ROPE_THETA = 10000.0
MEM_LEN = 256
MEM_HEADS = 4
MEM_HEAD_DIM = 128
MEM_W = MEM_HEADS * MEM_HEAD_DIM
FFN_HIDDEN = 2816
RMS_EPS = 1e-6
NEG_INF = -1e30
PAST_LEN = 16384

SWA_Q_W = SWA_HEADS * HEAD_DIM
SWA_KV_W = SWA_KV_HEADS * HEAD_DIM
RET_W = RET_HEADS * RET_DIM
IN_COLS = SWA_Q_W + 2 * SWA_KV_W + 4 * RET_W
C_QA, C_KA, C_VA = 0, SWA_Q_W, SWA_Q_W + SWA_KV_W
C_QR = SWA_Q_W + 2 * SWA_KV_W
C_KR, C_VR, C_G = C_QR + RET_W, C_QR + 2 * RET_W, C_QR + 3 * RET_W
N_QA_SLABS = SWA_Q_W // LANES
N_RET_SLABS = RET_W // LANES

VMEM_LIMIT = 56 * 1024 * 1024


def _dot(a, b):
    return jnp.dot(a.astype(BF16), b.astype(BF16), preferred_element_type=F32)


def _dot_nt(a, b):
    return lax.dot_general(a.astype(BF16), b.astype(BF16), (((1,), (1,)), ((), ())),
                           preferred_element_type=F32)


def _rms(x):
    return x * lax.rsqrt(jnp.mean(x * x, axis=-1, keepdims=True) + RMS_EPS)


def _lane_consts():
    lane = lax.broadcasted_iota(jnp.int32, (1, LANES), 1)
    m_left = (lane < HEAD_DIM).astype(F32)
    m_right = 1.0 - m_left
    first_half = (lane % HEAD_DIM) < (HEAD_DIM // 2)
    even = (lane % 2) == 0
    return m_left, m_right, first_half, even


def _head_norm(y, bd):
    ss = jnp.dot((y * y).astype(BF16), bd, preferred_element_type=F32) * (1.0 / HEAD_DIM)
    return y * lax.rsqrt(ss + RMS_EPS)


def _rot_half(y, cos, sin_signed, first_half):
    swapped = jnp.where(first_half, pltpu.roll(y, LANES - HEAD_DIM // 2, 1), pltpu.roll(y, HEAD_DIM // 2, 1))
    return y * cos + swapped * sin_signed


def _rot_pairs(y, cos, sin_signed, even):
    swapped = jnp.where(even, pltpu.roll(y, LANES - 1, 1), pltpu.roll(y, 1, 1))
    return y * cos + swapped * sin_signed


def _dup_head(a, g, m_left, m_right):
    one = a * (m_left if g == 0 else m_right)
    return one + pltpu.roll(one, HEAD_DIM, 1)


def _mixer_in(x, nmix, win_ref, qg, kg, bd, tabs, consts):
    ca, sa, cr, sr = tabs
    m_left, m_right, first_half, even = consts
    hb = (_rms(x) * nmix).astype(BF16)
    swa = jnp.dot(hb, win_ref[:, C_QA:C_QR], preferred_element_type=F32)
    qa = [_rot_half(_head_norm(swa[:, s * LANES:(s + 1) * LANES], bd) * qg, ca, sa, first_half)
          for s in range(N_QA_SLABS)]
    ka = _rot_half(_head_norm(swa[:, C_KA:C_VA], bd) * kg, ca, sa, first_half)
    va = swa[:, C_VA:C_QR]
    qr_all = jnp.dot(hb, win_ref[:, C_QR:C_KR], preferred_element_type=F32)
    kr_all = jnp.dot(hb, win_ref[:, C_KR:C_VR], preferred_element_type=F32)
    vr_all = jnp.dot(hb, win_ref[:, C_VR:C_G], preferred_element_type=F32)
    g_all = jnp.dot(hb, win_ref[:, C_G:IN_COLS], preferred_element_type=F32)
    sl = lambda a, p: a[:, p * LANES:(p + 1) * LANES]
    qr = [_rot_pairs(sl(qr_all, p), cr, sr, even) for p in range(N_RET_SLABS)]
    kr = [_rot_pairs(sl(kr_all, p), cr, sr, even) * (RET_DIM ** -0.5) for p in range(N_RET_SLABS)]
    vr = [sl(vr_all, p) for p in range(N_RET_SLABS)]
    gate = [sl(g_all, p) for p in range(N_RET_SLABS)]
    return qa, ka, va, qr, kr, vr, gate


def _mixer_out(x, oa, o_ret, gate, bd, wout_ref):
    outs = list(oa)
    for p in range(N_RET_SLABS):
        outs.append(_head_norm(o_ret[p], bd) * (gate[p] * jax.nn.sigmoid(gate[p])))
    mix = jnp.concatenate(outs, axis=1).astype(BF16)
    return x + jnp.dot(mix, wout_ref[...], preferred_element_type=F32)


def _sink_softmax_pv(s, valid, sink, vd):
    s = jnp.where(valid, s * (HEAD_DIM ** -0.5), NEG_INF)
    m = jnp.maximum(jnp.max(s, axis=-1, keepdims=True), sink)
    p = jnp.exp(s - m)
    denom = jnp.sum(p, axis=-1, keepdims=True) + jnp.exp(sink - m)
    return _dot(p, vd) / denom


def _stack_heads(slab_a, slab_b, m_left, m_right):
    return jnp.concatenate([slab_a * m_left, slab_a * m_right, slab_b * m_left, slab_b * m_right], axis=0)


def _dot_tn(a, b):
    return lax.dot_general(a.astype(BF16), b.astype(BF16), (((0,), (0,)), ((), ())),
                           preferred_element_type=F32)


def _head_norm_t(y):
    return y * lax.rsqrt(jnp.mean(y * y, axis=0, keepdims=True) + RMS_EPS)


def _rot_half_t(y, cos, sin):
    half = HEAD_DIM // 2
    y1, y2 = y[0:half], y[half:]
    return jnp.concatenate([y1 * cos - y2 * sin, y2 * cos + y1 * sin], axis=0)


def _rot_pairs_t(y, cos, sin_signed, even_row):
    n = y.shape[0]
    swapped = jnp.where(even_row, pltpu.roll(y, n - 1, 0), pltpu.roll(y, 1, 0))
    return y * cos + swapped * sin_signed


def _prompt_mixer_kernel(x_ref, ca_ref, sa_ref, cr_ref, sr_ref, nmix_ref, wint_ref, qg_ref, kg_ref,
                         sink_ref, wout_ref, dm_ref, qd_ref, kd_ref, cd_ref,
                         y_ref, kwin_ref, vwin_ref, sout_ref,
                         pk_ref, pv_ref, s_ref, *, tile):
    t = pl.program_id(1)
    nblk = tile // WINDOW
    hd = HEAD_DIM

    @pl.when(t == 0)
    def _():
        pk_ref[...] = jnp.zeros_like(pk_ref)
        pv_ref[...] = jnp.zeros_like(pv_ref)
        s_ref[...] = jnp.zeros_like(s_ref)

    x = x_ref[0]
    hb = (_rms(x) * nmix_ref[...]).astype(BF16)
    proj = lambda lo, hi: lax.dot_general(wint_ref[lo:hi, :], hb, (((1,), (1,)), ((), ())),
                                          preferred_element_type=F32)
    ca, sa = ca_ref[...], sa_ref[...]
    cr, sr = cr_ref[...], sr_ref[...]
    qg, kg = qg_ref[...], kg_ref[...]

    swa_t = proj(C_QA, C_QR)
    ret_t = proj(C_QR, C_VR)
    vr = proj(C_VR, C_G)
    gate_t = proj(C_G, IN_COLS)
    head = lambda a, h: a[h * hd:(h + 1) * hd]

    qa = [_rot_half_t(_head_norm_t(head(swa_t, h)) * qg, ca, sa) for h in range(SWA_HEADS)]
    ka = jnp.concatenate([_rot_half_t(_head_norm_t(head(swa_t, SWA_HEADS + g)) * kg, ca, sa)
                          for g in range(SWA_KV_HEADS)], axis=0)
    va = swa_t[C_VA:C_QR]
    kfull = jnp.concatenate([pk_ref[...], ka], axis=1).astype(BF16)
    vfull = jnp.concatenate([pv_ref[...], va], axis=1).astype(BF16)
    pk_ref[...] = ka[:, tile - WINDOW:]
    pv_ref[...] = va[:, tile - WINDOW:]
    kwin_ref[0] = ka[:, tile - WINDOW:].T
    vwin_ref[0] = va[:, tile - WINDOW:].T

    group = SWA_HEADS // SWA_KV_HEADS
    key = lax.broadcasted_iota(jnp.int32, (2 * WINDOW, group * WINDOW), 0)
    qry = lax.broadcasted_iota(jnp.int32, (2 * WINDOW, group * WINDOW), 1) % WINDOW
    band = (key > qry) & (key <= qry + WINDOW)
    oa_heads = [[None] * nblk for _ in range(SWA_HEADS)]
    units = [(g, j) for j in range(nblk) for g in range(SWA_KV_HEADS)]
    keys_of = lambda j: slice(j * WINDOW, (j + 2) * WINDOW)
    scores = []
    for g, j in units:
        toks = slice(j * WINDOW, (j + 1) * WINDOW)
        q4 = jnp.concatenate([qa[group * g + u][:, toks] for u in range(group)], axis=1)
        scores.append(_dot_tn(head(kfull, g)[:, keys_of(j)], q4))

    even_row = (lax.broadcasted_iota(jnp.int32, (RET_W, tile), 0) % 2) == 0
    cr8 = jnp.concatenate([cr] * RET_HEADS, axis=0)
    sr8 = jnp.concatenate([sr] * RET_HEADS, axis=0)
    qr = _rot_pairs_t(ret_t[0:RET_W], cr8, sr8, even_row)
    kr = _rot_pairs_t(ret_t[RET_W:], cr8, sr8, even_row) * (RET_DIM ** -0.5)
    zeros = jnp.zeros((hd, RET_CHUNK), F32)
    or_heads = [[None] * nblk for _ in range(RET_HEADS)]
    chunk = lambda a, h, j: a[h * hd:(h + 1) * hd, j * RET_CHUNK:(j + 1) * RET_CHUNK]
    inner = [[None] * nblk for _ in range(N_RET_SLABS)]
    for j in range(nblk):
        for pr in range(N_RET_SLABS):
            q_bd = jnp.concatenate([jnp.concatenate([chunk(qr, 2 * pr, j), zeros], axis=1),
                                    jnp.concatenate([zeros, chunk(qr, 2 * pr + 1, j)], axis=1)], axis=0)
            k2 = kr[pr * LANES:(pr + 1) * LANES, j * RET_CHUNK:(j + 1) * RET_CHUNK]
            inner[pr][j] = _dot_tn(k2, q_bd) * dm_ref[pr]
    incr = [[_dot_nt(chunk(vr, h, j), chunk(kr, h, j) * kd_ref[h]) for j in range(nblk)]
            for h in range(RET_HEADS)]

    probs, denoms = [], []
    for (g, j), s in zip(units, scores):
        sink = sink_ref[g]
        valid = band & (key >= WINDOW * (1 - t)) if j == 0 else band
        s = jnp.where(valid, s * (hd ** -0.5), NEG_INF)
        m = jnp.maximum(jnp.max(s, axis=0, keepdims=True), sink)
        p = jnp.exp(s - m)
        denoms.append(jnp.sum(p, axis=0, keepdims=True) + jnp.exp(sink - m))
        probs.append(p.astype(BF16))
    for (g, j), p, denom in zip(units, probs, denoms):
        o = jnp.dot(head(vfull, g)[:, keys_of(j)], p, preferred_element_type=F32) / denom
        for u in range(group):
            oa_heads[group * g + u][j] = o[:, u * WINDOW:(u + 1) * WINDOW]

    state = [[s_ref[h]] for h in range(RET_HEADS)]
    for h in range(RET_HEADS):
        for j in range(nblk):
            state[h].append(state[h][j] * cd_ref[h] + incr[h][j])
        s_ref[h] = state[h][nblk]
    for j in range(nblk):
        for h in range(RET_HEADS):
            u = h % 2
            lhs = jnp.concatenate([chunk(vr, h, j), state[h][j]], axis=1)
            rhs = jnp.concatenate([inner[h // 2][j][:, u * RET_CHUNK:(u + 1) * RET_CHUNK],
                                   chunk(qr, h, j) * qd_ref[h]], axis=0)
            or_heads[h][j] = _dot(lhs, rhs)

    @pl.when(t == pl.num_programs(1) - 1)
    def _():
        for h in range(RET_HEADS):
            sout_ref[0, h] = state[h][nblk].T

    mix =[jnp.concatenate(blocks, axis=1) for blocks in oa_heads]
    for h in range(RET_HEADS):
        o_h = _head_norm_t(jnp.concatenate(or_heads[h], axis=1))
        g_h = head(gate_t, h)
        mix.append(o_h * (g_h * jax.nn.sigmoid(g_h)))
    mix_t = jnp.concatenate(mix, axis=0).astype(BF16)
    y_ref[0] = x + lax.dot_general(mix_t, wout_ref[...], (((0,), (0,)), ((), ())),
                                   preferred_element_type=F32)


def _sample_in_kernel(x_ref, ca_ref, sa_ref, cr_ref, sr_ref, nmix_ref, win_ref, qg_ref, kg_ref, bd_ref,
                      feat_ref):
    consts = _lane_consts()
    tabs = (ca_ref[...], sa_ref[...], cr_ref[...], sr_ref[...])
    qa, ka, va, qr, kr, vr, gate = _mixer_in(x_ref[...], nmix_ref[...], win_ref, qg_ref[...], kg_ref[...],
                                             bd_ref[...], tabs, consts)
    feat_ref[...] = jnp.concatenate(qa + [ka, va] + qr + kr + vr + gate, axis=1)


def _sample_swa_kernel(feat_ref, ck_ref, cv_ref, sink_ref, o_ref, kout_ref, vout_ref,
                       kbuf_ref, vbuf_ref, *, nseq, ntok):
    @pl.when(pl.program_id(0) == 0)
    def _():
        kbuf_ref[...] = jnp.zeros_like(kbuf_ref)
        vbuf_ref[...] = jnp.zeros_like(vbuf_ref)

    m_left, m_right, _, _ = _lane_consts()
    nkeys = 2 * WINDOW
    row_tok = lax.broadcasted_iota(jnp.int32, (4 * ntok, nkeys), 0) % ntok
    col = lax.broadcasted_iota(jnp.int32, (4 * ntok, nkeys), 1)
    valid = (col > row_tok) & (col <= row_tok + WINDOW)

    def per_seq(i, carry):
        r0 = pl.multiple_of(i * ntok, ntok)
        rows = pl.ds(r0, ntok)
        slab = lambda c: feat_ref[rows, c:c + LANES]
        k_new, v_new = slab(C_KA), slab(C_VA)
        kbuf_ref[0:WINDOW] = ck_ref[i]
        vbuf_ref[0:WINDOW] = cv_ref[i]
        kbuf_ref[WINDOW:WINDOW + ntok] = k_new
        vbuf_ref[WINDOW:WINDOW + ntok] = v_new
        kout_ref[i, 0:WINDOW - ntok] = ck_ref[i, ntok:WINDOW]
        vout_ref[i, 0:WINDOW - ntok] = cv_ref[i, ntok:WINDOW]
        kout_ref[i, WINDOW - ntok:WINDOW] = k_new
        vout_ref[i, WINDOW - ntok:WINDOW] = v_new
        kfull, vfull = kbuf_ref[...], vbuf_ref[...]
        for g in range(SWA_KV_HEADS):
            kd_g = _dup_head(kfull, g, m_left, m_right)
            vd_g = _dup_head(vfull, g, m_left, m_right)
            qst = _stack_heads(slab(C_QA + 2 * g * LANES), slab(C_QA + (2 * g + 1) * LANES), m_left, m_right)
            o = _sink_softmax_pv(_dot_nt(qst, kd_g), valid, sink_ref[g], vd_g)
            n = ntok
            o_ref[rows, 2 * g * LANES:(2 * g + 1) * LANES] = o[0:n] * m_left + o[n:2 * n] * m_right
            o_ref[rows, (2 * g + 1) * LANES:(2 * g + 2) * LANES] = o[2 * n:3 * n] * m_left + o[3 * n:] * m_right
        return carry

    lax.fori_loop(0, nseq, per_seq, 0)


def _sample_ret_in_kernel(x_ref, cr_ref, sr_ref, nmix_ref, wint_ref, qkv_ref):
    hb = (_rms(x_ref[...]) * nmix_ref[...]).astype(BF16)
    qkv = lax.dot_general(wint_ref[...], hb, (((1,), (1,)), ((), ())), preferred_element_type=F32)
    n = qkv.shape[1]
    even_row = (lax.broadcasted_iota(jnp.int32, (RET_W, n), 0) % 2) == 0
    cr8 = jnp.concatenate([cr_ref[...]] * RET_HEADS, axis=0)
    sr8 = jnp.concatenate([sr_ref[...]] * RET_HEADS, axis=0)
    qkv_ref[0:RET_W] = _rot_pairs_t(qkv[0:RET_W], cr8, sr8, even_row)
    qkv_ref[RET_W:2 * RET_W] = _rot_pairs_t(qkv[RET_W:2 * RET_W], cr8, sr8, even_row) * (RET_DIM ** -0.5)
    qkv_ref[2 * RET_W:] = qkv[2 * RET_W:]


def _sample_ret_kernel(q_ref, k_ref, v_ref, s_ref, c_ref, o_ref, so_ref, ks_ref, *, ntok):
    hd, nseq = RET_DIM, LANES
    tok = lambda t: slice(t * nseq, (t + 1) * nseq)
    row_qd, row_kd, row_cd = ntok * ntok, ntok * ntok + ntok, ntok * ntok + 2 * ntok
    group = 8
    pair_outs = []
    for u in range(2):
        feats = slice(u * hd, (u + 1) * hd)
        const = lambda r: c_ref[u, r:r + 1, :]
        for t in range(ntok):
            ks_ref[t] = k_ref[feats, tok(t)] * const(row_kd + t)
        outs = []
        for t in range(ntok):
            q_t = q_ref[feats, tok(t)]
            acc = jnp.zeros((hd, nseq), F32)
            for k in range(t + 1):
                w_tk = jnp.sum(q_t * k_ref[feats, tok(k)], axis=0, keepdims=True) * const(t * ntok + k)
                acc = acc + w_tk * v_ref[feats, tok(k)]

            def cross(j, carry, t=t):
                d0 = pl.multiple_of(j * group, group)
                q_rows = q_ref[pl.ds(u * hd + d0, group), tok(t)]
                for i in range(group):
                    carry = carry + q_rows[i:i + 1, :] * s_ref[u, d0 + i]
                return carry
            carried = lax.fori_loop(0, hd // group, cross, jnp.zeros((hd, nseq), F32))
            outs.append(acc + carried * const(row_qd + t))

        def update(j, carry):
            d0 = pl.multiple_of(j * group, group)
            k_rows = [ks_ref[t, pl.ds(d0, group), :] for t in range(ntok)]
            for i in range(group):
                new = s_ref[u, d0 + i] * const(row_cd)
                for t in range(ntok):
                    new = new + k_rows[t][i:i + 1, :] * v_ref[feats, tok(t)]
                so_ref[u, d0 + i] = new
            return carry
        lax.fori_loop(0, hd // group, update, 0)
        pair_outs.append(outs)
    for t in range(ntok):
        o_ref[t] = jnp.concatenate([pair_outs[0][t], pair_outs[1][t]], axis=0).T


def _sample_out_kernel(x_ref, oa_ref, or_ref, feat_ref, bd_ref, wout_ref, y_ref):
    oa = [oa_ref[:, s * LANES:(s + 1) * LANES] for s in range(N_QA_SLABS)]
    o_ret = [or_ref[:, p * LANES:(p + 1) * LANES] for p in range(N_RET_SLABS)]
    gate = [feat_ref[:, C_G + p * LANES:C_G + (p + 1) * LANES] for p in range(N_RET_SLABS)]
    y_ref[...] = _mixer_out(x_ref[...], oa, o_ret, gate, bd_ref[...], wout_ref)


def _mem_kv_kernel(m_ref, nmem_ref, wkv_ref, kg_ref, k_ref, v_ref):
    hb = (_rms(m_ref[...]) * nmem_ref[...]).astype(BF16)
    kv = jnp.dot(hb, wkv_ref[...], preferred_element_type=F32)
    for h in range(MEM_HEADS):
        kh = kv[:, h * LANES:(h + 1) * LANES]
        k_ref[:, h * LANES:(h + 1) * LANES] = _rms(kh) * kg_ref[...]
    v_ref[...] = kv[:, MEM_W:]


def _mem_queries(x, ncross, wq_ref, qg):
    hb = (_rms(x) * ncross).astype(BF16)
    q = jnp.dot(hb, wq_ref[...], preferred_element_type=F32)
    return [_rms(q[:, h * LANES:(h + 1) * LANES]) * qg for h in range(MEM_HEADS)]


def _prompt_cross_attn_kernel(x_ref, mk_ref, mv_ref, ncross_ref, wq_ref, qg_ref, ot_ref):
    qn = _mem_queries(x_ref[...], ncross_ref[...], wq_ref, qg_ref[...])
    head = lambda ref, h: ref[0, :, h * LANES:(h + 1) * LANES].astype(BF16)
    scores = [_dot_nt(head(mk_ref, h), qn[h]) for h in range(MEM_HEADS)]
    probs, denoms = [], []
    for s in scores:
        s = s * (MEM_HEAD_DIM ** -0.5)
        p = jnp.exp(s - jnp.max(s, axis=0, keepdims=True))
        denoms.append(jnp.sum(p, axis=0, keepdims=True))
        probs.append(p.astype(BF16))
    for h in range(MEM_HEADS):
        ot_ref[0, h * LANES:(h + 1) * LANES, :] = _dot_tn(head(mv_ref, h), probs[h]) / denoms[h]


def _sample_cross_attn_kernel(x_ref, mk_ref, mv_ref, ncross_ref, wq_ref, qg_ref, o_ref, *, nseq, tq):
    qn = _mem_queries(x_ref[...], ncross_ref[...], wq_ref, qg_ref[...])
    nrow = MEM_HEADS * tq
    nmem = MEM_LEN * MEM_HEADS
    own = (lax.broadcasted_iota(jnp.int32, (nrow, nmem), 0) // tq
           == lax.broadcasted_iota(jnp.int32, (nrow, nmem), 1) % MEM_HEADS)
    scores = []
    for i in range(nseq):
        q_stack = jnp.concatenate([qn[h][i * tq:(i + 1) * tq] for h in range(MEM_HEADS)], axis=0)
        scores.append(_dot_nt(q_stack, mk_ref[i]))
    probs, denoms = [], []
    for s in scores:
        s = jnp.where(own, s * (MEM_HEAD_DIM ** -0.5), NEG_INF)
        p = jnp.exp(s - jnp.max(s, axis=-1, keepdims=True))
        denoms.append(jnp.sum(p, axis=-1, keepdims=True))
        probs.append(p.astype(BF16))
    for i in range(nseq):
        o = jnp.dot(probs[i], mv_ref[i].astype(BF16), preferred_element_type=F32) / denoms[i]
        for h in range(MEM_HEADS):
            o_ref[i * tq:(i + 1) * tq, h * LANES:(h + 1) * LANES] = o[h * tq:(h + 1) * tq]


def _cross_out_ffn_kernel(x_ref, o_ref, wo_ref, nffn_ref, wgu_ref, wdown_ref, y_ref, *, o_feature_major):
    if o_feature_major:
        attn = lax.dot_general(o_ref[0].astype(BF16), wo_ref[...], (((0,), (0,)), ((), ())),
                               preferred_element_type=F32)
    else:
        attn = jnp.dot(o_ref[...].astype(BF16), wo_ref[...], preferred_element_type=F32)
    x = x_ref[...] + attn
    hb = (_rms(x) * nffn_ref[...]).astype(BF16)
    g = jnp.dot(hb, wgu_ref[:, 0:FFN_HIDDEN], preferred_element_type=F32)
    u = jnp.dot(hb, wgu_ref[:, FFN_HIDDEN:], preferred_element_type=F32)
    act = (g * jax.nn.sigmoid(g) * u).astype(BF16)
    y_ref[...] = x + jnp.dot(act, wdown_ref[...], preferred_element_type=F32)


def _rope_tables(pos):
    half = HEAD_DIM // 2
    inv = 1.0 / (ROPE_THETA ** (jnp.arange(half, dtype=F32) / half))
    ang = pos.astype(F32)[:, None] * inv[None, :]
    cos, sin = jnp.cos(ang), jnp.sin(ang)
    c64 = jnp.concatenate([cos, cos], axis=-1)
    s64 = jnp.concatenate([-sin, sin], axis=-1)
    return jnp.tile(c64, (1, 2)), jnp.tile(s64, (1, 2))


def _ret_tables(pos):
    half = RET_DIM // 2
    inv = RET_THETA ** (-jnp.linspace(0.0, 1.0, half, dtype=F32))
    ang = pos.astype(F32)[:, None] * inv[None, :]
    cos, sin = jnp.cos(ang), jnp.sin(ang)
    c64 = jnp.repeat(cos, 2, axis=-1)
    s64 = jnp.stack([-sin, sin], axis=-1).reshape(pos.shape[0], RET_DIM)
    return jnp.tile(c64, (1, 2)), jnp.tile(s64, (1, 2))


def _decay_consts(c, pad_rows, pad_cols):
    log_g = jnp.log(1.0 - jnp.exp2(-5.0 - jnp.arange(RET_HEADS, dtype=F32)))
    idx = jnp.arange(c, dtype=F32)
    diff = idx[:, None] - idx[None, :]
    dmat = jnp.where(diff >= 0, jnp.exp(jnp.maximum(diff, 0.0)[None] * log_g[:, None, None]), 0.0)
    dm = jnp.pad(dmat.reshape(N_RET_SLABS, 2 * c, c), ((0, 0), (0, 0), (0, pad_cols - c)))
    qd = jnp.exp((idx + 1.0)[None, :] * log_g[:, None]).reshape(N_RET_SLABS, 2 * c, 1)
    kdec = jnp.exp((c - 1.0 - idx)[None, :] * log_g[:, None])
    kd = jnp.repeat(kdec.reshape(N_RET_SLABS, 2, c).transpose(0, 2, 1), RET_DIM, axis=-1)
    kd = jnp.pad(kd, ((0, 0), (0, pad_rows - c), (0, 0)))
    cd = jnp.repeat(jnp.exp(c * log_g).reshape(N_RET_SLABS, 2), RET_DIM, axis=-1)[..., None]
    return dm, qd, kd, cd


def _sample_decay_rows(c):
    log_g = jnp.log(1.0 - jnp.exp2(-5.0 - jnp.arange(RET_HEADS, dtype=F32)))
    idx = jnp.arange(c, dtype=F32)
    diff = idx[:, None] - idx[None, :]
    dmat = jnp.where(diff >= 0, jnp.exp(jnp.maximum(diff, 0.0)[None] * log_g[:, None, None]), 0.0)
    qd = jnp.exp((idx + 1.0)[None, :] * log_g[:, None])
    kd = jnp.exp((c - 1.0 - idx)[None, :] * log_g[:, None])
    cd = jnp.exp(c * log_g)[:, None]
    rows = jnp.concatenate([dmat.reshape(RET_HEADS, c * c), qd, kd, cd], axis=1)
    rows = jnp.pad(rows, ((0, 0), (0, -rows.shape[1] % 8)))
    return jnp.broadcast_to(rows[:, :, None], rows.shape + (LANES,))


def _full(shape):
    nd = len(shape)
    return pl.BlockSpec(shape, lambda *_: (0,) * nd)


def _params(sem):
    return pltpu.CompilerParams(dimension_semantics=sem, vmem_limit_bytes=VMEM_LIMIT)


def _prompt_tables_t(pos):
    half = HEAD_DIM // 2
    inv = 1.0 / (ROPE_THETA ** (jnp.arange(half, dtype=F32) / half))
    ang = inv[:, None] * pos.astype(F32)[None, :]
    inv_r = RET_THETA ** (-jnp.linspace(0.0, 1.0, RET_DIM // 2, dtype=F32))
    ang_r = inv_r[:, None] * pos.astype(F32)[None, :]
    cos_r, sin_r = jnp.cos(ang_r), jnp.sin(ang_r)
    cr = jnp.repeat(cos_r, 2, axis=0)
    sr = jnp.stack([-sin_r, sin_r], axis=1).reshape(RET_DIM, pos.shape[0])
    return jnp.cos(ang), jnp.sin(ang), cr, sr


def _decay_consts_t(c):
    log_g = jnp.log(1.0 - jnp.exp2(-5.0 - jnp.arange(RET_HEADS, dtype=F32)))
    idx = jnp.arange(c, dtype=F32)
    diff = idx[:, None] - idx[None, :]
    dmat = jnp.where(diff >= 0, jnp.exp(jnp.maximum(diff, 0.0)[None] * log_g[:, None, None]), 0.0)
    dm = dmat.transpose(0, 2, 1).reshape(N_RET_SLABS, 2, c, c).transpose(0, 2, 1, 3).reshape(N_RET_SLABS, c, 2 * c)
    qd = jnp.exp((idx + 1.0)[None, :] * log_g[:, None])[:, None, :]
    kd = jnp.exp((c - 1.0 - idx)[None, :] * log_g[:, None])[:, None, :]
    cd = jnp.broadcast_to(jnp.exp(c * log_g)[:, None, None], (RET_HEADS, 1, RET_DIM))
    return dm, qd, kd, cd


def _prompt_mixer(x, w, tile):
    b, l, d = x.shape
    ca, sa, cr, sr = _prompt_tables_t(jnp.arange(l, dtype=jnp.int32))
    dm, qd, kd, cd = _decay_consts_t(RET_CHUNK)
    sink = jnp.repeat(w['sinks'].reshape(SWA_KV_HEADS, 1, -1), WINDOW, axis=-1)
    tab = lambda rows: pl.BlockSpec((rows, tile), lambda i, t: (0, t))
    xspec = pl.BlockSpec((1, tile, d), lambda i, t: (i, t, 0))
    win_spec = pl.BlockSpec((1, WINDOW, LANES), lambda i, t: (i, 0, 0))
    st_spec = pl.BlockSpec((1, RET_HEADS, RET_DIM, RET_DIM), lambda i, t: (i, 0, 0, 0))
    return pl.pallas_call(
        functools.partial(_prompt_mixer_kernel, tile=tile),
        grid=(b, l // tile),
        in_specs=[xspec, tab(HEAD_DIM // 2), tab(HEAD_DIM // 2), tab(RET_DIM), tab(RET_DIM), _full((1, d)),
                  _full((IN_COLS, d)), _full((HEAD_DIM, 1)), _full((HEAD_DIM, 1)), _full(sink.shape),
                  _full((d, d)), _full(dm.shape), _full(qd.shape), _full(kd.shape), _full(cd.shape)],
        out_specs=[xspec, win_spec, win_spec, st_spec],
        out_shape=[jax.ShapeDtypeStruct(x.shape, F32),
                   jax.ShapeDtypeStruct((b, WINDOW, LANES), F32),
                   jax.ShapeDtypeStruct((b, WINDOW, LANES), F32),
                   jax.ShapeDtypeStruct((b, RET_HEADS, RET_DIM, RET_DIM), F32)],
        scratch_shapes=[pltpu.VMEM((SWA_KV_W, WINDOW), F32), pltpu.VMEM((SWA_KV_W, WINDOW), F32),
                        pltpu.VMEM((RET_HEADS, RET_DIM, RET_DIM), F32)],
        compiler_params=_params(("arbitrary", "arbitrary")),
        name="prompt_mixer",
    )(x, ca, sa, cr, sr, w['norm_mix'], w['w_in_t'], w['qg_col'], w['kg_col'], sink, w['w_out'], dm, qd, kd,
      cd)


def _sample_mixer(x, cache_k, cache_v, state, w, tile, nseq):
    b, ntok, d = x.shape
    rows = b * ntok
    xf = x.reshape(rows, d)
    pos = jnp.tile(PAST_LEN + jnp.arange(ntok, dtype=jnp.int32), b)
    ca, sa = _rope_tables(pos)
    cr, sr = _ret_tables(pos)
    tab = pl.BlockSpec((tile, LANES), lambda i: (i, 0))
    row_spec = lambda width: pl.BlockSpec((tile, width), lambda i: (i, 0))
    feat = pl.pallas_call(
        _sample_in_kernel,
        grid=(rows // tile,),
        in_specs=[row_spec(d), tab, tab, tab, tab, _full((1, d)), _full((d, IN_COLS)), _full((1, LANES)),
                  _full((1, LANES)), _full((LANES, LANES))],
        out_specs=row_spec(IN_COLS),
        out_shape=jax.ShapeDtypeStruct((rows, IN_COLS), F32),
        compiler_params=_params(("arbitrary",)),
        name="sample_in",
    )(xf, ca, sa, cr, sr, w['norm_mix'], w['w_in'], w['qg'], w['kg'], w['bd'])

    sink = jnp.repeat(w['sinks'].reshape(SWA_KV_HEADS, -1), ntok, axis=-1)[..., None]
    seq_rows = nseq * ntok
    cache_spec = pl.BlockSpec((nseq, WINDOW, LANES), lambda i: (i, 0, 0))
    o_a, k_out, v_out = pl.pallas_call(
        functools.partial(_sample_swa_kernel, nseq=nseq, ntok=ntok),
        grid=(b // nseq,),
        in_specs=[pl.BlockSpec((seq_rows, IN_COLS), lambda i: (i, 0)), cache_spec, cache_spec,
                  _full(sink.shape)],
        out_specs=[pl.BlockSpec((seq_rows, SWA_Q_W), lambda i: (i, 0)), cache_spec, cache_spec],
        out_shape=[jax.ShapeDtypeStruct((rows, SWA_Q_W), F32),
                   jax.ShapeDtypeStruct(cache_k.shape, F32),
                   jax.ShapeDtypeStruct(cache_v.shape, F32)],
        scratch_shapes=[pltpu.VMEM((2 * WINDOW, LANES), F32), pltpu.VMEM((2 * WINDOW, LANES), F32)],
        compiler_params=_params(("arbitrary",)),
        name="sample_swa",
    )(feat, cache_k, cache_v, sink)

    assert b == LANES, "the retention step puts one sequence per lane"
    xt = jnp.transpose(x, (1, 0, 2)).reshape(rows, d)
    pos_t = jnp.repeat(PAST_LEN + jnp.arange(ntok, dtype=jnp.int32), b)
    _, _, cr_t, sr_t = _prompt_tables_t(pos_t)
    half = rows // 2
    tab_t = pl.BlockSpec((RET_DIM, half), lambda i: (0, i))
    qkv = pl.pallas_call(
        _sample_ret_in_kernel,
        grid=(2,),
        in_specs=[pl.BlockSpec((half, d), lambda i: (i, 0)), tab_t, tab_t, _full((1, d)),
                  _full((3 * RET_W, d))],
        out_specs=pl.BlockSpec((3 * RET_W, half), lambda i: (0, i)),
        out_shape=jax.ShapeDtypeStruct((3 * RET_W, rows), F32),
        compiler_params=_params(("arbitrary",)),
        name="sample_ret_in",
    )(xt, cr_t, sr_t, w['norm_mix'], w['w_in_t'][C_QR:C_G])
    consts = _sample_decay_rows(ntok)
    pair_rows = lambda off: pl.BlockSpec((LANES, rows), lambda p: (off + p, 0))
    st_spec = pl.BlockSpec((2, RET_DIM, RET_DIM, b), lambda p: (p, 0, 0, 0))
    o_r, s_out = pl.pallas_call(
        functools.partial(_sample_ret_kernel, ntok=ntok),
        grid=(N_RET_SLABS,),
        in_specs=[pair_rows(0), pair_rows(N_RET_SLABS), pair_rows(2 * N_RET_SLABS), st_spec,
                  pl.BlockSpec((2,) + consts.shape[1:], lambda p: (p, 0, 0))],
        out_specs=[pl.BlockSpec((ntok, b, LANES), lambda p: (0, 0, p)), st_spec],
        out_shape=[jax.ShapeDtypeStruct((ntok, b, RET_W), F32), jax.ShapeDtypeStruct(state.shape, F32)],
        scratch_shapes=[pltpu.VMEM((ntok, RET_DIM, b), F32)],
        compiler_params=_params(("arbitrary",)),
        name="sample_ret",
    )(qkv, qkv, qkv, state, consts)
    o_r = jnp.transpose(o_r, (1, 0, 2)).reshape(rows, RET_W)

    y = pl.pallas_call(
        _sample_out_kernel,
        grid=(rows // tile,),
        in_specs=[row_spec(d), row_spec(SWA_Q_W), row_spec(RET_W), row_spec(IN_COLS), _full((LANES, LANES)),
                  _full((d, d))],
        out_specs=row_spec(d),
        out_shape=jax.ShapeDtypeStruct((rows, d), F32),
        compiler_params=_params(("arbitrary",)),
        name="sample_out",
    )(xf, o_a, o_r, feat, w['bd'], w['w_out'])
    return y, k_out, v_out, s_out


def _mem_kv(mem, w, tile):
    rows, d = mem.shape
    row_spec = lambda width: pl.BlockSpec((tile, width), lambda i: (i, 0))
    return pl.pallas_call(
        _mem_kv_kernel,
        grid=(rows // tile,),
        in_specs=[row_spec(d), _full((1, d)), _full((d, 2 * MEM_W)), _full((1, LANES))],
        out_specs=[row_spec(MEM_W), row_spec(MEM_W)],
        out_shape=[jax.ShapeDtypeStruct((rows, MEM_W), F32)] * 2,
        compiler_params=_params(("arbitrary",)),
        name="mem_kv",
    )(mem, w['norm_mem'], w['w_mkv'], w['kgm'])


def _prompt_cross_attn(x, mk, mv, w, tile):
    rows, d = x.shape
    b = mk.shape[0]
    steps = rows // b // tile
    mem_spec = pl.BlockSpec((1, MEM_LEN, MEM_W), lambda i, t: (i, 0, 0))
    return pl.pallas_call(
        _prompt_cross_attn_kernel,
        grid=(b, steps),
        in_specs=[pl.BlockSpec((tile, d), lambda i, t: (i * steps + t, 0)), mem_spec, mem_spec,
                  _full((1, d)), _full((d, MEM_W)), _full((1, LANES))],
        out_specs=pl.BlockSpec((1, MEM_W, tile), lambda i, t: (i, 0, t)),
        out_shape=jax.ShapeDtypeStruct((b, MEM_W, rows // b), F32),
        compiler_params=_params(("arbitrary", "arbitrary")),
        name="prompt_cross_attn",
    )(x, mk, mv, w['norm_cross'], w['w_mq'], w['qgm'])


def _sample_cross_attn(x, mk, mv, w, nseq, tq):
    rows, d = x.shape
    blk = nseq * tq
    mem_spec = pl.BlockSpec((nseq, MEM_LEN * MEM_HEADS, MEM_HEAD_DIM), lambda i: (i, 0, 0))
    return pl.pallas_call(
        functools.partial(_sample_cross_attn_kernel, nseq=nseq, tq=tq),
        grid=(rows // blk,),
        in_specs=[pl.BlockSpec((blk, d), lambda i: (i, 0)), mem_spec, mem_spec, _full((1, d)),
                  _full((d, MEM_W)), _full((1, LANES))],
        out_specs=pl.BlockSpec((blk, MEM_W), lambda i: (i, 0)),
        out_shape=jax.ShapeDtypeStruct((rows, MEM_W), F32),
        compiler_params=_params(("arbitrary",)),
        name="sample_cross_attn",
    )(x, mk, mv, w['norm_cross'], w['w_mq'], w['qgm'])


def _cross_out_ffn(x, o, w, tile, name):
    rows, d = x.shape
    o_feature_major = o.ndim == 3
    row_spec = lambda width: pl.BlockSpec((tile, width), lambda i: (i, 0))
    single = lambda shape: pl.BlockSpec(shape, lambda i: (0,) * len(shape), pipeline_mode=pl.Buffered(1))
    if o_feature_major:
        steps = o.shape[2] // tile
        o_spec = pl.BlockSpec((1, MEM_W, tile), lambda i: (i // steps, 0, i % steps))
    else:
        o_spec = row_spec(MEM_W)
    return pl.pallas_call(
        functools.partial(_cross_out_ffn_kernel, o_feature_major=o_feature_major),
        grid=(rows // tile,),
        in_specs=[row_spec(d), o_spec, single((MEM_W, d)), _full((1, d)),
                  single((d, 2 * FFN_HIDDEN)), single((FFN_HIDDEN, d))],
        out_specs=row_spec(d),
        out_shape=jax.ShapeDtypeStruct((rows, d), F32),
        compiler_params=_params(("arbitrary",)),
        name=name,
    )(x, o, w['w_mo'], w['norm_ffn'], w['w_gu'], w['w_down'])


def kernel(x_prompt, x_sample, mem_prompt, cache_swa_k, cache_swa_v, state_ret, cache_mem_k, cache_mem_v,
           norm_mix, w_in, q_norm_a, k_norm_a, sinks, w_out, norm_cross, norm_mem, w_mq, w_mkv,
           q_norm_m, k_norm_m, w_mo, norm_ffn, w_gu, w_down):
    assert norm_mix.shape[0] == 1, "single-layer kernel"
    b, l, d = x_prompt.shape
    sb, st, _ = x_sample.shape
    half = (jnp.arange(LANES) // HEAD_DIM)[:, None] == (jnp.arange(LANES) // HEAD_DIM)[None, :]
    w = {
        'norm_mix': norm_mix, 'norm_cross': norm_cross, 'norm_mem': norm_mem, 'norm_ffn': norm_ffn,
        'w_in': w_in[0].astype(BF16), 'w_out': w_out[0].astype(BF16), 'w_mq': w_mq[0].astype(BF16),
        'w_mkv': w_mkv[0].astype(BF16), 'w_mo': w_mo[0].astype(BF16), 'w_gu': w_gu[0].astype(BF16),
        'w_down': w_down[0].astype(BF16),
        'w_in_t': w_in[0].T.astype(BF16),
        'qg': jnp.tile(q_norm_a, (1, 2)), 'kg': jnp.tile(k_norm_a, (1, 2)),
        'qg_col': q_norm_a.reshape(HEAD_DIM, 1), 'kg_col': k_norm_a.reshape(HEAD_DIM, 1),
        'qgm': q_norm_m, 'kgm': k_norm_m, 'sinks': sinks[0],
        'bd': half.astype(BF16),
    }

    mk, mv = _mem_kv(mem_prompt.reshape(b * MEM_LEN, d), w, tile=256)
    xp, kwin, vwin, ret_p = _prompt_mixer(x_prompt, w, tile=1024)
    xp = xp.reshape(b * l, d)
    o_t = _prompt_cross_attn(xp, mk.reshape(b, MEM_LEN, MEM_W), mv.reshape(b, MEM_LEN, MEM_W), w, tile=512)
    yp = _cross_out_ffn(xp, o_t, w, tile=512, name="prompt_ffn").reshape(b, l, d)
    ret_p = ret_p.reshape(1, b, RET_HEADS, RET_DIM, RET_DIM)

    nbuf = cache_swa_k.shape[2]
    xs, k_s, v_s, s_s = _sample_mixer(x_sample, cache_swa_k.reshape(sb, nbuf, LANES),
                                      cache_swa_v.reshape(sb, nbuf, LANES),
                                      jnp.transpose(state_ret[0], (1, 2, 3, 0)), w, tile=256, nseq=8)
    mem_rows = (sb, MEM_LEN * MEM_HEADS, MEM_HEAD_DIM)
    os_ = _sample_cross_attn(xs, cache_mem_k.reshape(mem_rows), cache_mem_v.reshape(mem_rows), w, nseq=8, tq=st)
    ys = _cross_out_ffn(xs, os_, w, tile=512, name="sample_ffn").reshape(sb, st, d)

    kv_shape = (1, b, WINDOW, SWA_KV_HEADS, HEAD_DIM)
    mem_shape = (1, b, MEM_LEN, MEM_HEADS, MEM_HEAD_DIM)
    return (yp, ys, kwin.reshape(kv_shape), vwin.reshape(kv_shape), ret_p,
            mk.reshape(mem_shape), mv.reshape(mem_shape),
            k_s.reshape(cache_swa_k.shape), v_s.reshape(cache_swa_v.shape),
            jnp.transpose(s_s, (3, 0, 1, 2))[None])
```

```python
import functools

import jax
import jax.numpy as jnp
from jax import lax
from jax.experimental import pallas as pl
from jax.experimental.pallas import tpu as pltpu

F32 = jnp.float32
BF16 = jnp.bfloat16

LANES = 128
D_MODEL = 1024
HEAD_DIM = 64
SWA_HEADS = 8
SWA_KV_HEADS = 2
WINDOW = 128
RET_HEADS = 8
RET_DIM = 64
RET_CHUNK = 128
RET_THETA = 10000.0
ROPE_THETA = 10000.0
MEM_LEN = 256
MEM_HEADS = 4
MEM_HEAD_DIM = 128
MEM_W = MEM_HEADS * MEM_HEAD_DIM
FFN_HIDDEN = 2816
RMS_EPS = 1e-6
NEG_INF = -1e30
PAST_LEN = 16384

SWA_Q_W = SWA_HEADS * HEAD_DIM
SWA_KV_W = SWA_KV_HEADS * HEAD_DIM
RET_W = RET_HEADS * RET_DIM
IN_COLS = SWA_Q_W + 2 * SWA_KV_W + 4 * RET_W
C_QA, C_KA, C_VA = 0, SWA_Q_W, SWA_Q_W + SWA_KV_W
C_QR = SWA_Q_W + 2 * SWA_KV_W
C_KR, C_VR, C_G = C_QR + RET_W, C_QR + 2 * RET_W, C_QR + 3 * RET_W
N_QA_SLABS = SWA_Q_W // LANES
N_RET_SLABS = RET_W // LANES

VMEM_LIMIT = 56 * 1024 * 1024


def _dot(a, b):
    return jnp.dot(a.astype(BF16), b.astype(BF16), preferred_element_type=F32)


def _dot_nt(a, b):
    return lax.dot_general(a.astype(BF16), b.astype(BF16), (((1,), (1,)), ((), ())),
                           preferred_element_type=F32)


def _rms(x):
    return x * lax.rsqrt(jnp.mean(x * x, axis=-1, keepdims=True) + RMS_EPS)


def _lane_consts():
    lane = lax.broadcasted_iota(jnp.int32, (1, LANES), 1)
    m_left = (lane < HEAD_DIM).astype(F32)
    m_right = 1.0 - m_left
    first_half = (lane % HEAD_DIM) < (HEAD_DIM // 2)
    even = (lane % 2) == 0
    return m_left, m_right, first_half, even


def _head_norm(y, bd):
    ss = jnp.dot((y * y).astype(BF16), bd, preferred_element_type=F32) * (1.0 / HEAD_DIM)
    return y * lax.rsqrt(ss + RMS_EPS)


def _rot_half(y, cos, sin_signed, first_half):
    swapped = jnp.where(first_half, pltpu.roll(y, LANES - HEAD_DIM // 2, 1), pltpu.roll(y, HEAD_DIM // 2, 1))
    return y * cos + swapped * sin_signed


def _rot_pairs(y, cos, sin_signed, even):
    swapped = jnp.where(even, pltpu.roll(y, LANES - 1, 1), pltpu.roll(y, 1, 1))
    return y * cos + swapped * sin_signed


def _dup_head(a, g, m_left, m_right):
    one = a * (m_left if g == 0 else m_right)
    return one + pltpu.roll(one, HEAD_DIM, 1)


def _mixer_in(x, nmix, win_ref, qg, kg, bd, tabs, consts):
    ca, sa, cr, sr = tabs
    m_left, m_right, first_half, even = consts
    hb = (_rms(x) * nmix).astype(BF16)
    swa = jnp.dot(hb, win_ref[:, C_QA:C_QR], preferred_element_type=F32)
    qa = [_rot_half(_head_norm(swa[:, s * LANES:(s + 1) * LANES], bd) * qg, ca, sa, first_half)
          for s in range(N_QA_SLABS)]
    ka = _rot_half(_head_norm(swa[:, C_KA:C_VA], bd) * kg, ca, sa, first_half)
    va = swa[:, C_VA:C_QR]
    qr_all = jnp.dot(hb, win_ref[:, C_QR:C_KR], preferred_element_type=F32)
    kr_all = jnp.dot(hb, win_ref[:, C_KR:C_VR], preferred_element_type=F32)
    vr_all = jnp.dot(hb, win_ref[:, C_VR:C_G], preferred_element_type=F32)
    g_all = jnp.dot(hb, win_ref[:, C_G:IN_COLS], preferred_element_type=F32)
    sl = lambda a, p: a[:, p * LANES:(p + 1) * LANES]
    qr = [_rot_pairs(sl(qr_all, p), cr, sr, even) for p in range(N_RET_SLABS)]
    kr = [_rot_pairs(sl(kr_all, p), cr, sr, even) * (RET_DIM ** -0.5) for p in range(N_RET_SLABS)]
    vr = [sl(vr_all, p) for p in range(N_RET_SLABS)]
    gate = [sl(g_all, p) for p in range(N_RET_SLABS)]
    return qa, ka, va, qr, kr, vr, gate


def _mixer_out(x, oa, o_ret, gate, bd, wout_ref):
    outs = list(oa)
    for p in range(N_RET_SLABS):
        outs.append(_head_norm(o_ret[p], bd) * (gate[p] * jax.nn.sigmoid(gate[p])))
    mix = jnp.concatenate(outs, axis=1).astype(BF16)
    return x + jnp.dot(mix, wout_ref[...], preferred_element_type=F32)


def _sink_softmax_pv(s, valid, sink, vd):
    s = jnp.where(valid, s * (HEAD_DIM ** -0.5), NEG_INF)
    m = jnp.maximum(jnp.max(s, axis=-1, keepdims=True), sink)
    p = jnp.exp(s - m)
    denom = jnp.sum(p, axis=-1, keepdims=True) + jnp.exp(sink - m)
    return _dot(p, vd) / denom


def _stack_heads(slab_a, slab_b, m_left, m_right):
    return jnp.concatenate([slab_a * m_left, slab_a * m_right, slab_b * m_left, slab_b * m_right], axis=0)


def _dot_tn(a, b):
    return lax.dot_general(a.astype(BF16), b.astype(BF16), (((0,), (0,)), ((), ())),
                           preferred_element_type=F32)


def _head_norm_t(y):
    return y * lax.rsqrt(jnp.mean(y * y, axis=0, keepdims=True) + RMS_EPS)


def _rot_half_t(y, cos, sin):
    half = HEAD_DIM // 2
    y1, y2 = y[0:half], y[half:]
    return jnp.concatenate([y1 * cos - y2 * sin, y2 * cos + y1 * sin], axis=0)


def _rot_pairs_t(y, cos, sin_signed, even_row):
    n = y.shape[0]
    swapped = jnp.where(even_row, pltpu.roll(y, n - 1, 0), pltpu.roll(y, 1, 0))
    return y * cos + swapped * sin_signed


def _prompt_mixer_kernel(x_ref, ca_ref, sa_ref, cr_ref, sr_ref, nmix_ref, wint_ref, qg_ref, kg_ref,
                         sink_ref, wout_ref, dm_ref, qd_ref, kd_ref, cd_ref,
                         y_ref, kwin_ref, vwin_ref, sout_ref,
                         pk_ref, pv_ref, s_ref, *, tile):
    t = pl.program_id(1)
    nblk = tile // WINDOW
    hd = HEAD_DIM

    @pl.when(t == 0)
    def _():
        pk_ref[...] = jnp.zeros_like(pk_ref)
        pv_ref[...] = jnp.zeros_like(pv_ref)
        s_ref[...] = jnp.zeros_like(s_ref)

    x = x_ref[0]
    hb = (_rms(x) * nmix_ref[...]).astype(BF16)
    proj = lambda lo, hi: lax.dot_general(wint_ref[lo:hi, :], hb, (((1,), (1,)), ((), ())),
                                          preferred_element_type=F32)
    ca, sa = ca_ref[...], sa_ref[...]
    cr, sr = cr_ref[...], sr_ref[...]
    qg, kg = qg_ref[...], kg_ref[...]

    swa_t = proj(C_QA, C_QR)
    ret_t = proj(C_QR, C_VR)
    vr = proj(C_VR, C_G)
    gate_t = proj(C_G, IN_COLS)
    head = lambda a, h: a[h * hd:(h + 1) * hd]

    qa = [_rot_half_t(_head_norm_t(head(swa_t, h)) * qg, ca, sa) for h in range(SWA_HEADS)]
    ka = jnp.concatenate([_rot_half_t(_head_norm_t(head(swa_t, SWA_HEADS + g)) * kg, ca, sa)
                          for g in range(SWA_KV_HEADS)], axis=0)
    va = swa_t[C_VA:C_QR]
    kfull = jnp.concatenate([pk_ref[...], ka], axis=1).astype(BF16)
    vfull = jnp.concatenate([pv_ref[...], va], axis=1).astype(BF16)
    pk_ref[...] = ka[:, tile - WINDOW:]
    pv_ref[...] = va[:, tile - WINDOW:]
    kwin_ref[0] = ka[:, tile - WINDOW:].T
    vwin_ref[0] = va[:, tile - WINDOW:].T

    group = SWA_HEADS // SWA_KV_HEADS
    key = lax.broadcasted_iota(jnp.int32, (2 * WINDOW, group * WINDOW), 0)
    qry = lax.broadcasted_iota(jnp.int32, (2 * WINDOW, group * WINDOW), 1) % WINDOW
    band = (key > qry) & (key <= qry + WINDOW)
    oa_heads = [[None] * nblk for _ in range(SWA_HEADS)]
    units = [(g, j) for j in range(nblk) for g in range(SWA_KV_HEADS)]
    keys_of = lambda j: slice(j * WINDOW, (j + 2) * WINDOW)
    scores = []
    for g, j in units:
        toks = slice(j * WINDOW, (j + 1) * WINDOW)
        q4 = jnp.concatenate([qa[group * g + u][:, toks] for u in range(group)], axis=1)
        scores.append(_dot_tn(head(kfull, g)[:, keys_of(j)], q4))

    even_row = (lax.broadcasted_iota(jnp.int32, (RET_W, tile), 0) % 2) == 0
    cr8 = jnp.concatenate([cr] * RET_HEADS, axis=0)
    sr8 = jnp.concatenate([sr] * RET_HEADS, axis=0)
    qr = _rot_pairs_t(ret_t[0:RET_W], cr8, sr8, even_row)
    kr = _rot_pairs_t(ret_t[RET_W:], cr8, sr8, even_row) * (RET_DIM ** -0.5)
    zeros = jnp.zeros((hd, RET_CHUNK), F32)
    or_heads = [[None] * nblk for _ in range(RET_HEADS)]
    chunk = lambda a, h, j: a[h * hd:(h + 1) * hd, j * RET_CHUNK:(j + 1) * RET_CHUNK]
    inner = [[None] * nblk for _ in range(N_RET_SLABS)]
    for j in range(nblk):
        for pr in range(N_RET_SLABS):
            q_bd = jnp.concatenate([jnp.concatenate([chunk(qr, 2 * pr, j), zeros], axis=1),
                                    jnp.concatenate([zeros, chunk(qr, 2 * pr + 1, j)], axis=1)], axis=0)
            k2 = kr[pr * LANES:(pr + 1) * LANES, j * RET_CHUNK:(j + 1) * RET_CHUNK]
            inner[pr][j] = _dot_tn(k2, q_bd) * dm_ref[pr]
    incr = [[_dot_nt(chunk(vr, h, j), chunk(kr, h, j) * kd_ref[h]) for j in range(nblk)]
            for h in range(RET_HEADS)]

    probs, denoms = [], []
    for (g, j), s in zip(units, scores):
        sink = sink_ref[g]
        valid = band & (key >= WINDOW * (1 - t)) if j == 0 else band
        s = jnp.where(valid, s * (hd ** -0.5), NEG_INF)
        m = jnp.maximum(jnp.max(s, axis=0, keepdims=True), sink)
        p = jnp.exp(s - m)
        denoms.append(jnp.sum(p, axis=0, keepdims=True) + jnp.exp(sink - m))
        probs.append(p.astype(BF16))
    for (g, j), p, denom in zip(units, probs, denoms):
        o = jnp.dot(head(vfull, g)[:, keys_of(j)], p, preferred_element_type=F32) / denom
        for u in range(group):
            oa_heads[group * g + u][j] = o[:, u * WINDOW:(u + 1) * WINDOW]

    state = [[s_ref[h]] for h in range(RET_HEADS)]
    for h in range(RET_HEADS):
        for j in range(nblk):
            state[h].append(state[h][j] * cd_ref[h] + incr[h][j])
        s_ref[h] = state[h][nblk]
    for j in range(nblk):
        for h in range(RET_HEADS):
            u = h % 2
            lhs = jnp.concatenate([chunk(vr, h, j), state[h][j]], axis=1)
            rhs = jnp.concatenate([inner[h // 2][j][:, u * RET_CHUNK:(u + 1) * RET_CHUNK],
                                   chunk(qr, h, j) * qd_ref[h]], axis=0)
            or_heads[h][j] = _dot(lhs, rhs)

    @pl.when(t == pl.num_programs(1) - 1)
    def _():
        for h in range(RET_HEADS):
            sout_ref[0, h] = state[h][nblk].T

    mix =[jnp.concatenate(blocks, axis=1) for blocks in oa_heads]
    for h in range(RET_HEADS):
        o_h = _head_norm_t(jnp.concatenate(or_heads[h], axis=1))
        g_h = head(gate_t, h)
        mix.append(o_h * (g_h * jax.nn.sigmoid(g_h)))
    mix_t = jnp.concatenate(mix, axis=0).astype(BF16)
    y_ref[0] = x + lax.dot_general(mix_t, wout_ref[...], (((0,), (0,)), ((), ())),
                                   preferred_element_type=F32)


def _sample_in_kernel(x_ref, ca_ref, sa_ref, cr_ref, sr_ref, nmix_ref, win_ref, qg_ref, kg_ref, bd_ref,
                      feat_ref):
    consts = _lane_consts()
    tabs = (ca_ref[...], sa_ref[...], cr_ref[...], sr_ref[...])
    qa, ka, va, qr, kr, vr, gate = _mixer_in(x_ref[...], nmix_ref[...], win_ref, qg_ref[...], kg_ref[...],
                                             bd_ref[...], tabs, consts)
    feat_ref[...] = jnp.concatenate(qa + [ka, va] + qr + kr + vr + gate, axis=1)


def _sample_swa_kernel(feat_ref, ckt_ref, cvt_ref, sink_ref, o_ref, kout_ref, vout_ref,
                       kn_ref, vn_ref, *, nseq, ntok, unroll):
    @pl.when(pl.program_id(0) == 0)
    def _():
        kn_ref[...] = jnp.zeros_like(kn_ref)
        vn_ref[...] = jnp.zeros_like(vn_ref)

    m_left, m_right, _, _ = _lane_consts()
    tail = WINDOW - ntok
    nrow = SWA_HEADS * ntok
    row_tok = lax.broadcasted_iota(jnp.int32, (nrow, WINDOW), 0) % ntok
    col = lax.broadcasted_iota(jnp.int32, (nrow, WINDOW), 1)
    valid_cache = col > row_tok
    valid_new = (col >= tail) & (col - tail <= row_tok)
    in_tail = lax.broadcasted_iota(jnp.int32, (LANES, WINDOW), 1) >= tail
    sink = sink_ref[...]
    shift = lambda a: pltpu.roll(a, HEAD_DIM, 1)

    def body(step, carry):
        seqs = [step * unroll + u for u in range(unroll)]
        rows = [pl.ds(pl.multiple_of(i * ntok, ntok), ntok) for i in seqs]
        q_rows, caches = [], []
        for u, i in enumerate(seqs):
            slab = lambda c, u=u: feat_ref[rows[u], c:c + LANES]
            kn_ref[u, tail:WINDOW] = slab(C_KA)
            vn_ref[u, tail:WINDOW] = slab(C_VA)
            s0, s1, s2, s3 = [slab(C_QA + s * LANES) for s in range(N_QA_SLABS)]
            q_rows.append(jnp.concatenate(
                [s0 * m_left, shift(s0 * m_right), s1 * m_left, shift(s1 * m_right),
                 shift(s2 * m_left), s2 * m_right, shift(s3 * m_left), s3 * m_right], axis=0))
            caches.append((ckt_ref[i], cvt_ref[i], kn_ref[u], vn_ref[u]))
        scores = [(_dot(q, kt), _dot_nt(q, kn)) for q, (kt, _, kn, _) in zip(q_rows, caches)]
        probs = []
        for s_c, s_n in scores:
            s_c = jnp.where(valid_cache, s_c * (HEAD_DIM ** -0.5), NEG_INF)
            s_n = jnp.where(valid_new, s_n * (HEAD_DIM ** -0.5), NEG_INF)
            m = jnp.maximum(jnp.maximum(jnp.max(s_c, axis=-1, keepdims=True),
                                        jnp.max(s_n, axis=-1, keepdims=True)), sink)
            p_c, p_n = jnp.exp(s_c - m), jnp.exp(s_n - m)
            denom = (jnp.sum(p_c, axis=-1, keepdims=True) + jnp.sum(p_n, axis=-1, keepdims=True)
                     + jnp.exp(sink - m))
            probs.append((p_c, p_n, denom))
        for u, i in enumerate(seqs):
            kt, vt, kn, vn = caches[u]
            p_c, p_n, denom = probs[u]
            o = (_dot_nt(p_c, vt) + _dot(p_n, vn)) / denom
            n = ntok
            left = lambda h: o[h * n:(h + 1) * n] * m_left
            right = lambda h: o[h * n:(h + 1) * n] * m_right
            o_ref[rows[u], 0:LANES] = left(0) + shift(left(1))
            o_ref[rows[u], LANES:2 * LANES] = left(2) + shift(left(3))
            o_ref[rows[u], 2 * LANES:3 * LANES] = shift(right(4)) + right(5)
            o_ref[rows[u], 3 * LANES:4 * LANES] = shift(right(6)) + right(7)
            kout_ref[i] = jnp.where(in_tail, kn.T, pltpu.roll(kt, tail, 1))
            vout_ref[i] = jnp.where(in_tail, vn.T, pltpu.roll(vt, tail, 1))
        return carry

    lax.fori_loop(0, nseq // unroll, body, 0)


def _sample_ret_in_kernel(x_ref, cr_ref, sr_ref, nmix_ref, wint_ref, qkv_ref):
    hb = (_rms(x_ref[...]) * nmix_ref[...]).astype(BF16)
    qkv = lax.dot_general(wint_ref[...], hb, (((1,), (1,)), ((), ())), preferred_element_type=F32)
    n = qkv.shape[1]
    even_row = (lax.broadcasted_iota(jnp.int32, (RET_W, n), 0) % 2) == 0
    cr8 = jnp.concatenate([cr_ref[...]] * RET_HEADS, axis=0)
    sr8 = jnp.concatenate([sr_ref[...]] * RET_HEADS, axis=0)
    qkv_ref[0:RET_W] = _rot_pairs_t(qkv[0:RET_W], cr8, sr8, even_row)
    qkv_ref[RET_W:2 * RET_W] = _rot_pairs_t(qkv[RET_W:2 * RET_W], cr8, sr8, even_row) * (RET_DIM ** -0.5)
    qkv_ref[2 * RET_W:] = qkv[2 * RET_W:]


def _sample_ret_kernel(q_ref, k_ref, v_ref, s_ref, c_ref, o_ref, so_ref, ks_ref, *, ntok):
    hd, nseq = RET_DIM, LANES
    tok = lambda t: slice(t * nseq, (t + 1) * nseq)
    row_qd, row_kd, row_cd = ntok * ntok, ntok * ntok + ntok, ntok * ntok + 2 * ntok
    group = 8
    pair_outs = []
    for u in range(2):
        feats = slice(u * hd, (u + 1) * hd)
        const = lambda r: c_ref[u, r:r + 1, :]
        for t in range(ntok):
            ks_ref[t] = k_ref[feats, tok(t)] * const(row_kd + t)
        outs = []
        for t in range(ntok):
            q_t = q_ref[feats, tok(t)]
            acc = jnp.zeros((hd, nseq), F32)
            for k in range(t + 1):
                w_tk = jnp.sum(q_t * k_ref[feats, tok(k)], axis=0, keepdims=True) * const(t * ntok + k)
                acc = acc + w_tk * v_ref[feats, tok(k)]

            def cross(j, carry, t=t):
                d0 = pl.multiple_of(j * group, group)
                q_rows = q_ref[pl.ds(u * hd + d0, group), tok(t)]
                for i in range(group):
                    carry = carry + q_rows[i:i + 1, :] * s_ref[u, d0 + i]
                return carry
            carried = lax.fori_loop(0, hd // group, cross, jnp.zeros((hd, nseq), F32))
            outs.append(acc + carried * const(row_qd + t))

        def update(j, carry):
            d0 = pl.multiple_of(j * group, group)
            k_rows = [ks_ref[t, pl.ds(d0, group), :] for t in range(ntok)]
            for i in range(group):
                new = s_ref[u, d0 + i] * const(row_cd)
                for t in range(ntok):
                    new = new + k_rows[t][i:i + 1, :] * v_ref[feats, tok(t)]
                so_ref[u, d0 + i] = new
            return carry
        lax.fori_loop(0, hd // group, update, 0)
        pair_outs.append(outs)
    for t in range(ntok):
        o_ref[t] = jnp.concatenate([pair_outs[0][t], pair_outs[1][t]], axis=0).T


def _sample_out_kernel(x_ref, oa_ref, or_ref, feat_ref, bd_ref, wout_ref, y_ref):
    oa = [oa_ref[:, s * LANES:(s + 1) * LANES] for s in range(N_QA_SLABS)]
    o_ret = [or_ref[:, p * LANES:(p + 1) * LANES] for p in range(N_RET_SLABS)]
    gate = [feat_ref[:, C_G + p * LANES:C_G + (p + 1) * LANES] for p in range(N_RET_SLABS)]
    y_ref[...] = _mixer_out(x_ref[...], oa, o_ret, gate, bd_ref[...], wout_ref)


def _mem_kv_kernel(m_ref, nmem_ref, wkv_ref, kg_ref, k_ref, v_ref):
    hb = (_rms(m_ref[...]) * nmem_ref[...]).astype(BF16)
    kv = jnp.dot(hb, wkv_ref[...], preferred_element_type=F32)
    for h in range(MEM_HEADS):
        kh = kv[:, h * LANES:(h + 1) * LANES]
        k_ref[:, h * LANES:(h + 1) * LANES] = _rms(kh) * kg_ref[...]
    v_ref[...] = kv[:, MEM_W:]


def _mem_queries(x, ncross, wq_ref, qg):
    hb = (_rms(x) * ncross).astype(BF16)
    q = jnp.dot(hb, wq_ref[...], preferred_element_type=F32)
    return [_rms(q[:, h * LANES:(h + 1) * LANES]) * qg for h in range(MEM_HEADS)]


def _prompt_cross_attn_kernel(x_ref, mk_ref, mv_ref, ncross_ref, wq_ref, qg_ref, ot_ref):
    qn = _mem_queries(x_ref[...], ncross_ref[...], wq_ref, qg_ref[...])
    head = lambda ref, h: ref[0, :, h * LANES:(h + 1) * LANES].astype(BF16)
    scores = [_dot_nt(head(mk_ref, h), qn[h]) for h in range(MEM_HEADS)]
    probs, denoms = [], []
    for s in scores:
        s = s * (MEM_HEAD_DIM ** -0.5)
        p = jnp.exp(s - jnp.max(s, axis=0, keepdims=True))
        denoms.append(jnp.sum(p, axis=0, keepdims=True))
        probs.append(p.astype(BF16))
    for h in range(MEM_HEADS):
        ot_ref[0, h * LANES:(h + 1) * LANES, :] = _dot_tn(head(mv_ref, h), probs[h]) / denoms[h]


def _prompt_cross_ffn_kernel(x_ref, mk_ref, mv_ref, ncross_ref, wq_ref, qg_ref, wo_ref, nffn_ref, wgu_ref,
                             wdown_ref, y_ref, xa_ref):
    @pl.when(pl.program_id(0) == 0)
    def _():
        xa_ref[...] = jnp.zeros_like(xa_ref)

    xa = xa_ref[...]
    hb = (_rms(xa) * nffn_ref[...]).astype(BF16)
    x = x_ref[...]
    qn = _mem_queries(x, ncross_ref[...], wq_ref, qg_ref[...])
    g = jnp.dot(hb, wgu_ref[:, 0:FFN_HIDDEN], preferred_element_type=F32)
    head = lambda ref, h: ref[0, :, h * LANES:(h + 1) * LANES].astype(BF16)
    scores = [_dot_nt(head(mk_ref, h), qn[h]) for h in range(MEM_HEADS)]
    u = jnp.dot(hb, wgu_ref[:, FFN_HIDDEN:], preferred_element_type=F32)
    probs, denoms = [], []
    for s in scores:
        s = s * (MEM_HEAD_DIM ** -0.5)
        p = jnp.exp(s - jnp.max(s, axis=0, keepdims=True))
        denoms.append(jnp.sum(p, axis=0, keepdims=True))
        probs.append(p.astype(BF16))
    act = (g * jax.nn.sigmoid(g) * u).astype(BF16)
    o_t = jnp.concatenate([_dot_tn(head(mv_ref, h), probs[h]) / denoms[h] for h in range(MEM_HEADS)], axis=0)
    y_ref[...] = xa + jnp.dot(act, wdown_ref[...], preferred_element_type=F32)
    xa_ref[...] = x + lax.dot_general(o_t.astype(BF16), wo_ref[...], (((0,), (0,)), ((), ())),
                                      preferred_element_type=F32)


def _sample_cross_attn_kernel(x_ref, mk_ref, mv_ref, ncross_ref, wq_ref, qg_ref, o_ref, *, nseq, tq):
    qn = _mem_queries(x_ref[...], ncross_ref[...], wq_ref, qg_ref[...])
    nrow = MEM_HEADS * tq
    nmem = MEM_LEN * MEM_HEADS
    own = (lax.broadcasted_iota(jnp.int32, (nrow, nmem), 0) // tq
           == lax.broadcasted_iota(jnp.int32, (nrow, nmem), 1) % MEM_HEADS)
    scores = []
    for i in range(nseq):
        q_stack = jnp.concatenate([qn[h][i * tq:(i + 1) * tq] for h in range(MEM_HEADS)], axis=0)
        scores.append(_dot_nt(q_stack, mk_ref[i]))
    probs, denoms = [], []
    for s in scores:
        s = jnp.where(own, s * (MEM_HEAD_DIM ** -0.5), NEG_INF)
        p = jnp.exp(s - jnp.max(s, axis=-1, keepdims=True))
        denoms.append(jnp.sum(p, axis=-1, keepdims=True))
        probs.append(p.astype(BF16))
    for i in range(nseq):
        o = jnp.dot(probs[i], mv_ref[i].astype(BF16), preferred_element_type=F32) / denoms[i]
        for h in range(MEM_HEADS):
            o_ref[i * tq:(i + 1) * tq, h * LANES:(h + 1) * LANES] = o[h * tq:(h + 1) * tq]


def _cross_out_ffn_kernel(x_ref, o_ref, wo_ref, nffn_ref, wgu_ref, wdown_ref, y_ref, *, o_feature_major):
    if o_feature_major:
        attn = lax.dot_general(o_ref[0].astype(BF16), wo_ref[...], (((0,), (0,)), ((), ())),
                               preferred_element_type=F32)
    else:
        attn = jnp.dot(o_ref[...].astype(BF16), wo_ref[...], preferred_element_type=F32)
    x = x_ref[...] + attn
    hb = (_rms(x) * nffn_ref[...]).astype(BF16)
    g = jnp.dot(hb, wgu_ref[:, 0:FFN_HIDDEN], preferred_element_type=F32)
    u = jnp.dot(hb, wgu_ref[:, FFN_HIDDEN:], preferred_element_type=F32)
    act = (g * jax.nn.sigmoid(g) * u).astype(BF16)
    y_ref[...] = x + jnp.dot(act, wdown_ref[...], preferred_element_type=F32)


def _rope_tables(pos):
    half = HEAD_DIM // 2
    inv = 1.0 / (ROPE_THETA ** (jnp.arange(half, dtype=F32) / half))
    ang = pos.astype(F32)[:, None] * inv[None, :]
    cos, sin = jnp.cos(ang), jnp.sin(ang)
    c64 = jnp.concatenate([cos, cos], axis=-1)
    s64 = jnp.concatenate([-sin, sin], axis=-1)
    return jnp.tile(c64, (1, 2)), jnp.tile(s64, (1, 2))


def _ret_tables(pos):
    half = RET_DIM // 2
    inv = RET_THETA ** (-jnp.linspace(0.0, 1.0, half, dtype=F32))
    ang = pos.astype(F32)[:, None] * inv[None, :]
    cos, sin = jnp.cos(ang), jnp.sin(ang)
    c64 = jnp.repeat(cos, 2, axis=-1)
    s64 = jnp.stack([-sin, sin], axis=-1).reshape(pos.shape[0], RET_DIM)
    return jnp.tile(c64, (1, 2)), jnp.tile(s64, (1, 2))


def _decay_consts(c, pad_rows, pad_cols):
    log_g = jnp.log(1.0 - jnp.exp2(-5.0 - jnp.arange(RET_HEADS, dtype=F32)))
    idx = jnp.arange(c, dtype=F32)
    diff = idx[:, None] - idx[None, :]
    dmat = jnp.where(diff >= 0, jnp.exp(jnp.maximum(diff, 0.0)[None] * log_g[:, None, None]), 0.0)
    dm = jnp.pad(dmat.reshape(N_RET_SLABS, 2 * c, c), ((0, 0), (0, 0), (0, pad_cols - c)))
    qd = jnp.exp((idx + 1.0)[None, :] * log_g[:, None]).reshape(N_RET_SLABS, 2 * c, 1)
    kdec = jnp.exp((c - 1.0 - idx)[None, :] * log_g[:, None])
    kd = jnp.repeat(kdec.reshape(N_RET_SLABS, 2, c).transpose(0, 2, 1), RET_DIM, axis=-1)
    kd = jnp.pad(kd, ((0, 0), (0, pad_rows - c), (0, 0)))
    cd = jnp.repeat(jnp.exp(c * log_g).reshape(N_RET_SLABS, 2), RET_DIM, axis=-1)[..., None]
    return dm, qd, kd, cd


def _sample_decay_rows(c):
    log_g = jnp.log(1.0 - jnp.exp2(-5.0 - jnp.arange(RET_HEADS, dtype=F32)))
    idx = jnp.arange(c, dtype=F32)
    diff = idx[:, None] - idx[None, :]
    dmat = jnp.where(diff >= 0, jnp.exp(jnp.maximum(diff, 0.0)[None] * log_g[:, None, None]), 0.0)
    qd = jnp.exp((idx + 1.0)[None, :] * log_g[:, None])
    kd = jnp.exp((c - 1.0 - idx)[None, :] * log_g[:, None])
    cd = jnp.exp(c * log_g)[:, None]
    rows = jnp.concatenate([dmat.reshape(RET_HEADS, c * c), qd, kd, cd], axis=1)
    rows = jnp.pad(rows, ((0, 0), (0, -rows.shape[1] % 8)))
    return jnp.broadcast_to(rows[:, :, None], rows.shape + (LANES,))


def _full(shape):
    nd = len(shape)
    return pl.BlockSpec(shape, lambda *_: (0,) * nd)


def _params(sem):
    return pltpu.CompilerParams(dimension_semantics=sem, vmem_limit_bytes=VMEM_LIMIT)


def _prompt_tables_t(pos):
    half = HEAD_DIM // 2
    inv = 1.0 / (ROPE_THETA ** (jnp.arange(half, dtype=F32) / half))
    ang = inv[:, None] * pos.astype(F32)[None, :]
    inv_r = RET_THETA ** (-jnp.linspace(0.0, 1.0, RET_DIM // 2, dtype=F32))
    ang_r = inv_r[:, None] * pos.astype(F32)[None, :]
    cos_r, sin_r = jnp.cos(ang_r), jnp.sin(ang_r)
    cr = jnp.repeat(cos_r, 2, axis=0)
    sr = jnp.stack([-sin_r, sin_r], axis=1).reshape(RET_DIM, pos.shape[0])
    return jnp.cos(ang), jnp.sin(ang), cr, sr


def _decay_consts_t(c):
    log_g = jnp.log(1.0 - jnp.exp2(-5.0 - jnp.arange(RET_HEADS, dtype=F32)))
    idx = jnp.arange(c, dtype=F32)
    diff = idx[:, None] - idx[None, :]
    dmat = jnp.where(diff >= 0, jnp.exp(jnp.maximum(diff, 0.0)[None] * log_g[:, None, None]), 0.0)
    dm = dmat.transpose(0, 2, 1).reshape(N_RET_SLABS, 2, c, c).transpose(0, 2, 1, 3).reshape(N_RET_SLABS, c, 2 * c)
    qd = jnp.exp((idx + 1.0)[None, :] * log_g[:, None])[:, None, :]
    kd = jnp.exp((c - 1.0 - idx)[None, :] * log_g[:, None])[:, None, :]
    cd = jnp.broadcast_to(jnp.exp(c * log_g)[:, None, None], (RET_HEADS, 1, RET_DIM))
    return dm, qd, kd, cd


def _prompt_mixer(x, w, tile):
    b, l, d = x.shape
    ca, sa, cr, sr = _prompt_tables_t(jnp.arange(l, dtype=jnp.int32))
    dm, qd, kd, cd = _decay_consts_t(RET_CHUNK)
    sink = jnp.repeat(w['sinks'].reshape(SWA_KV_HEADS, 1, -1), WINDOW, axis=-1)
    tab = lambda rows: pl.BlockSpec((rows, tile), lambda i, t: (0, t))
    xspec = pl.BlockSpec((1, tile, d), lambda i, t: (i, t, 0))
    win_spec = pl.BlockSpec((1, WINDOW, LANES), lambda i, t: (i, 0, 0))
    st_spec = pl.BlockSpec((1, RET_HEADS, RET_DIM, RET_DIM), lambda i, t: (i, 0, 0, 0))
    return pl.pallas_call(
        functools.partial(_prompt_mixer_kernel, tile=tile),
        grid=(b, l // tile),
        in_specs=[xspec, tab(HEAD_DIM // 2), tab(HEAD_DIM // 2), tab(RET_DIM), tab(RET_DIM), _full((1, d)),
                  _full((IN_COLS, d)), _full((HEAD_DIM, 1)), _full((HEAD_DIM, 1)), _full(sink.shape),
                  _full((d, d)), _full(dm.shape), _full(qd.shape), _full(kd.shape), _full(cd.shape)],
        out_specs=[xspec, win_spec, win_spec, st_spec],
        out_shape=[jax.ShapeDtypeStruct(x.shape, F32),
                   jax.ShapeDtypeStruct((b, WINDOW, LANES), F32),
                   jax.ShapeDtypeStruct((b, WINDOW, LANES), F32),
                   jax.ShapeDtypeStruct((b, RET_HEADS, RET_DIM, RET_DIM), F32)],
        scratch_shapes=[pltpu.VMEM((SWA_KV_W, WINDOW), F32), pltpu.VMEM((SWA_KV_W, WINDOW), F32),
                        pltpu.VMEM((RET_HEADS, RET_DIM, RET_DIM), F32)],
        compiler_params=_params(("arbitrary", "arbitrary")),
        name="prompt_mixer",
    )(x, ca, sa, cr, sr, w['norm_mix'], w['w_in_t'], w['qg_col'], w['kg_col'], sink, w['w_out'], dm, qd, kd,
      cd)


def _sample_mixer(x, cache_k, cache_v, state, w, tile, nseq):
    b, ntok, d = x.shape
    rows = b * ntok
    xf = x.reshape(rows, d)
    pos = jnp.tile(PAST_LEN + jnp.arange(ntok, dtype=jnp.int32), b)
    ca, sa = _rope_tables(pos)
    cr, sr = _ret_tables(pos)
    tab = pl.BlockSpec((tile, LANES), lambda i: (i, 0))
    row_spec = lambda width: pl.BlockSpec((tile, width), lambda i: (i, 0))
    feat = pl.pallas_call(
        _sample_in_kernel,
        grid=(rows // tile,),
        in_specs=[row_spec(d), tab, tab, tab, tab, _full((1, d)), _full((d, IN_COLS)), _full((1, LANES)),
                  _full((1, LANES)), _full((LANES, LANES))],
        out_specs=row_spec(IN_COLS),
        out_shape=jax.ShapeDtypeStruct((rows, IN_COLS), F32),
        compiler_params=_params(("arbitrary",)),
        name="sample_in",
    )(xf, ca, sa, cr, sr, w['norm_mix'], w['w_in'], w['qg'], w['kg'], w['bd'])

    sink = jnp.repeat(w['sinks'], ntok)[:, None]
    seq_rows = nseq * ntok
    unroll = 4
    cache_spec = pl.BlockSpec((nseq, LANES, WINDOW), lambda i: (i, 0, 0))
    o_a, k_out, v_out = pl.pallas_call(
        functools.partial(_sample_swa_kernel, nseq=nseq, ntok=ntok, unroll=unroll),
        grid=(b // nseq,),
        in_specs=[pl.BlockSpec((seq_rows, IN_COLS), lambda i: (i, 0)), cache_spec, cache_spec,
                  _full(sink.shape)],
        out_specs=[pl.BlockSpec((seq_rows, SWA_Q_W), lambda i: (i, 0)), cache_spec, cache_spec],
        out_shape=[jax.ShapeDtypeStruct((rows, SWA_Q_W), F32),
                   jax.ShapeDtypeStruct(cache_k.shape, F32),
                   jax.ShapeDtypeStruct(cache_v.shape, F32)],
        scratch_shapes=[pltpu.VMEM((unroll, WINDOW, LANES), F32), pltpu.VMEM((unroll, WINDOW, LANES), F32)],
        compiler_params=_params(("arbitrary",)),
        name="sample_swa",
    )(feat, cache_k, cache_v, sink)

    assert b == LANES, "the retention step puts one sequence per lane"
    xt = jnp.transpose(x, (1, 0, 2)).reshape(rows, d)
    pos_t = jnp.repeat(PAST_LEN + jnp.arange(ntok, dtype=jnp.int32), b)
    _, _, cr_t, sr_t = _prompt_tables_t(pos_t)
    half = rows // 2
    tab_t = pl.BlockSpec((RET_DIM, half), lambda i: (0, i))
    qkv = pl.pallas_call(
        _sample_ret_in_kernel,
        grid=(2,),
        in_specs=[pl.BlockSpec((half, d), lambda i: (i, 0)), tab_t, tab_t, _full((1, d)),
                  _full((3 * RET_W, d))],
        out_specs=pl.BlockSpec((3 * RET_W, half), lambda i: (0, i)),
        out_shape=jax.ShapeDtypeStruct((3 * RET_W, rows), F32),
        compiler_params=_params(("arbitrary",)),
        name="sample_ret_in",
    )(xt, cr_t, sr_t, w['norm_mix'], w['w_in_t'][C_QR:C_G])
    consts = _sample_decay_rows(ntok)
    pair_rows = lambda off: pl.BlockSpec((LANES, rows), lambda p: (off + p, 0))
    st_spec = pl.BlockSpec((2, RET_DIM, RET_DIM, b), lambda p: (p, 0, 0, 0))
    o_r, s_out = pl.pallas_call(
        functools.partial(_sample_ret_kernel, ntok=ntok),
        grid=(N_RET_SLABS,),
        in_specs=[pair_rows(0), pair_rows(N_RET_SLABS), pair_rows(2 * N_RET_SLABS), st_spec,
                  pl.BlockSpec((2,) + consts.shape[1:], lambda p: (p, 0, 0))],
        out_specs=[pl.BlockSpec((ntok, b, LANES), lambda p: (0, 0, p)), st_spec],
        out_shape=[jax.ShapeDtypeStruct((ntok, b, RET_W), F32), jax.ShapeDtypeStruct(state.shape, F32)],
        scratch_shapes=[pltpu.VMEM((ntok, RET_DIM, b), F32)],
        compiler_params=_params(("arbitrary",)),
        name="sample_ret",
    )(qkv, qkv, qkv, state, consts)
    o_r = jnp.transpose(o_r, (1, 0, 2)).reshape(rows, RET_W)

    y = pl.pallas_call(
        _sample_out_kernel,
        grid=(rows // tile,),
        in_specs=[row_spec(d), row_spec(SWA_Q_W), row_spec(RET_W), row_spec(IN_COLS), _full((LANES, LANES)),
                  _full((d, d))],
        out_specs=row_spec(d),
        out_shape=jax.ShapeDtypeStruct((rows, d), F32),
        compiler_params=_params(("arbitrary",)),
        name="sample_out",
    )(xf, o_a, o_r, feat, w['bd'], w['w_out'])
    return y, k_out, v_out, s_out


def _mem_kv(mem, w, tile):
    rows, d = mem.shape
    row_spec = lambda width: pl.BlockSpec((tile, width), lambda i: (i, 0))
    return pl.pallas_call(
        _mem_kv_kernel,
        grid=(rows // tile,),
        in_specs=[row_spec(d), _full((1, d)), _full((d, 2 * MEM_W)), _full((1, LANES))],
        out_specs=[row_spec(MEM_W), row_spec(MEM_W)],
        out_shape=[jax.ShapeDtypeStruct((rows, MEM_W), F32)] * 2,
        compiler_params=_params(("arbitrary",)),
        name="mem_kv",
    )(mem, w['norm_mem'], w['w_mkv'], w['kgm'])


def _prompt_cross_attn(x, mk, mv, w, tile):
    rows, d = x.shape
    b = mk.shape[0]
    steps = rows // b // tile
    mem_spec = pl.BlockSpec((1, MEM_LEN, MEM_W), lambda i, t: (i, 0, 0))
    return pl.pallas_call(
        _prompt_cross_attn_kernel,
        grid=(b, steps),
        in_specs=[pl.BlockSpec((tile, d), lambda i, t: (i * steps + t, 0)), mem_spec, mem_spec,
                  _full((1, d)), _full((d, MEM_W)), _full((1, LANES))],
        out_specs=pl.BlockSpec((1, MEM_W, tile), lambda i, t: (i, 0, t)),
        out_shape=jax.ShapeDtypeStruct((b, MEM_W, rows // b), F32),
        compiler_params=_params(("arbitrary", "arbitrary")),
        name="prompt_cross_attn",
    )(x, mk, mv, w['norm_cross'], w['w_mq'], w['qgm'])


def _prompt_cross_ffn(x, mk, mv, w, tile):
    rows, d = x.shape
    n = rows // tile
    per_mem = n // mk.shape[0]
    cur = lambda i: jnp.minimum(i, n - 1)
    single = lambda shape: pl.BlockSpec(shape, lambda i: (0,) * len(shape), pipeline_mode=pl.Buffered(1))
    mem_spec = pl.BlockSpec((1, MEM_LEN, MEM_W), lambda i: (cur(i) // per_mem, 0, 0))
    return pl.pallas_call(
        _prompt_cross_ffn_kernel,
        grid=(n + 1,),
        in_specs=[pl.BlockSpec((tile, d), lambda i: (cur(i), 0)), mem_spec, mem_spec, _full((1, d)),
                  single((d, MEM_W)), _full((1, LANES)), single((MEM_W, d)), _full((1, d)),
                  single((d, 2 * FFN_HIDDEN)), single((FFN_HIDDEN, d))],
        out_specs=pl.BlockSpec((tile, d), lambda i: (jnp.maximum(i - 1, 0), 0)),
        out_shape=jax.ShapeDtypeStruct((rows, d), F32),
        scratch_shapes=[pltpu.VMEM((tile, d), F32)],
        compiler_params=_params(("arbitrary",)),
        name="prompt_cross_ffn",
    )(x, mk, mv, w['norm_cross'], w['w_mq'], w['qgm'], w['w_mo'], w['norm_ffn'], w['w_gu'], w['w_down'])


def _sample_cross_attn(x, mk, mv, w, nseq, tq):
    rows, d = x.shape
    blk = nseq * tq
    mem_spec = pl.BlockSpec((nseq, MEM_LEN * MEM_HEADS, MEM_HEAD_DIM), lambda i: (i, 0, 0))
    return pl.pallas_call(
        functools.partial(_sample_cross_attn_kernel, nseq=nseq, tq=tq),
        grid=(rows // blk,),
        in_specs=[pl.BlockSpec((blk, d), lambda i: (i, 0)), mem_spec, mem_spec, _full((1, d)),
                  _full((d, MEM_W)), _full((1, LANES))],
        out_specs=pl.BlockSpec((blk, MEM_W), lambda i: (i, 0)),
        out_shape=jax.ShapeDtypeStruct((rows, MEM_W), F32),
        compiler_params=_params(("arbitrary",)),
        name="sample_cross_attn",
    )(x, mk, mv, w['norm_cross'], w['w_mq'], w['qgm'])


def _cross_out_ffn(x, o, w, tile, name):
    rows, d = x.shape
    o_feature_major = o.ndim == 3
    row_spec = lambda width: pl.BlockSpec((tile, width), lambda i: (i, 0))
    single = lambda shape: pl.BlockSpec(shape, lambda i: (0,) * len(shape), pipeline_mode=pl.Buffered(1))
    if o_feature_major:
        steps = o.shape[2] // tile
        o_spec = pl.BlockSpec((1, MEM_W, tile), lambda i: (i // steps, 0, i % steps))
    else:
        o_spec = row_spec(MEM_W)
    return pl.pallas_call(
        functools.partial(_cross_out_ffn_kernel, o_feature_major=o_feature_major),
        grid=(rows // tile,),
        in_specs=[row_spec(d), o_spec, single((MEM_W, d)), _full((1, d)),
                  single((d, 2 * FFN_HIDDEN)), single((FFN_HIDDEN, d))],
        out_specs=row_spec(d),
        out_shape=jax.ShapeDtypeStruct((rows, d), F32),
        compiler_params=_params(("arbitrary",)),
        name=name,
    )(x, o, w['w_mo'], w['norm_ffn'], w['w_gu'], w['w_down'])


def kernel(x_prompt, x_sample, mem_prompt, cache_swa_k, cache_swa_v, state_ret, cache_mem_k, cache_mem_v,
           norm_mix, w_in, q_norm_a, k_norm_a, sinks, w_out, norm_cross, norm_mem, w_mq, w_mkv,
           q_norm_m, k_norm_m, w_mo, norm_ffn, w_gu, w_down):
    assert norm_mix.shape[0] == 1, "single-layer kernel"
    b, l, d = x_prompt.shape
    sb, st, _ = x_sample.shape
    half = (jnp.arange(LANES) // HEAD_DIM)[:, None] == (jnp.arange(LANES) // HEAD_DIM)[None, :]
    w = {
        'norm_mix': norm_mix, 'norm_cross': norm_cross, 'norm_mem': norm_mem, 'norm_ffn': norm_ffn,
        'w_in': w_in[0].astype(BF16), 'w_out': w_out[0].astype(BF16), 'w_mq': w_mq[0].astype(BF16),
        'w_mkv': w_mkv[0].astype(BF16), 'w_mo': w_mo[0].astype(BF16), 'w_gu': w_gu[0].astype(BF16),
        'w_down': w_down[0].astype(BF16),
        'w_in_t': w_in[0].T.astype(BF16),
        'qg': jnp.tile(q_norm_a, (1, 2)), 'kg': jnp.tile(k_norm_a, (1, 2)),
        'qg_col': q_norm_a.reshape(HEAD_DIM, 1), 'kg_col': k_norm_a.reshape(HEAD_DIM, 1),
        'qgm': q_norm_m, 'kgm': k_norm_m, 'sinks': sinks[0],
        'bd': half.astype(BF16),
    }

    mk, mv = _mem_kv(mem_prompt.reshape(b * MEM_LEN, d), w, tile=256)
    xp, kwin, vwin, ret_p = _prompt_mixer(x_prompt, w, tile=1024)
    xp = xp.reshape(b * l, d)
    yp = _prompt_cross_ffn(xp, mk.reshape(b, MEM_LEN, MEM_W), mv.reshape(b, MEM_LEN, MEM_W), w, tile=512)
    yp = yp.reshape(b, l, d)
    ret_p = ret_p.reshape(1, b, RET_HEADS, RET_DIM, RET_DIM)

    nbuf = cache_swa_k.shape[2]
    assert nbuf == WINDOW
    to_feature_major = lambda c: jnp.transpose(c[0], (0, 2, 3, 1)).reshape(sb, SWA_KV_W, nbuf)
    from_feature_major = lambda c: jnp.transpose(c.reshape(sb, SWA_KV_HEADS, HEAD_DIM, nbuf), (0, 3, 1, 2))[None]
    xs, k_s, v_s, s_s = _sample_mixer(x_sample, to_feature_major(cache_swa_k), to_feature_major(cache_swa_v),
                                      jnp.transpose(state_ret[0], (1, 2, 3, 0)), w, tile=256, nseq=8)
    mem_rows = (sb, MEM_LEN * MEM_HEADS, MEM_HEAD_DIM)
    os_ = _sample_cross_attn(xs, cache_mem_k.reshape(mem_rows), cache_mem_v.reshape(mem_rows), w, nseq=8, tq=st)
    ys = _cross_out_ffn(xs, os_, w, tile=512, name="sample_ffn").reshape(sb, st, d)

    kv_shape = (1, b, WINDOW, SWA_KV_HEADS, HEAD_DIM)
    mem_shape = (1, b, MEM_LEN, MEM_HEADS, MEM_HEAD_DIM)
    return (yp, ys, kwin.reshape(kv_shape), vwin.reshape(kv_shape), ret_p,
            mk.reshape(mem_shape), mv.reshape(mem_shape),
            from_feature_major(k_s), from_feature_major(v_s),
            jnp.transpose(s_s, (3, 0, 1, 2))[None])
```

```python
import functools

import jax
import jax.numpy as jnp
from jax import lax
from jax.experimental import pallas as pl
from jax.experimental.pallas import tpu as pltpu

F32 = jnp.float32
BF16 = jnp.bfloat16

LANES = 128
D_MODEL = 1024
HEAD_DIM = 64
SWA_HEADS = 8
SWA_KV_HEADS = 2
WINDOW = 128
RET_HEADS = 8
RET_DIM = 64
RET_CHUNK = 128
RET_THETA = 10000.0
ROPE_THETA = 10000.0
MEM_LEN = 256
MEM_HEADS = 4
MEM_HEAD_DIM = 128
MEM_W = MEM_HEADS * MEM_HEAD_DIM
FFN_HIDDEN = 2816
RMS_EPS = 1e-6
NEG_INF = -1e30
LOG2E = 1.4426950408889634
PAST_LEN = 16384

SWA_Q_W = SWA_HEADS * HEAD_DIM
SWA_KV_W = SWA_KV_HEADS * HEAD_DIM
RET_W = RET_HEADS * RET_DIM
IN_COLS = SWA_Q_W + 2 * SWA_KV_W + 4 * RET_W
C_QA, C_KA, C_VA = 0, SWA_Q_W, SWA_Q_W + SWA_KV_W
C_QR = SWA_Q_W + 2 * SWA_KV_W
C_KR, C_VR, C_G = C_QR + RET_W, C_QR + 2 * RET_W, C_QR + 3 * RET_W
C_GATE_S = C_QR
SAMPLE_FEAT = C_QR + RET_W
N_QA_SLABS = SWA_Q_W // LANES
N_RET_SLABS = RET_W // LANES

VMEM_LIMIT = 56 * 1024 * 1024


def _dot(a, b):
    return jnp.dot(a.astype(BF16), b.astype(BF16), preferred_element_type=F32)


def _dot_nt(a, b):
    return lax.dot_general(a.astype(BF16), b.astype(BF16), (((1,), (1,)), ((), ())),
                           preferred_element_type=F32)


def _rms(x):
    return x * lax.rsqrt(jnp.mean(x * x, axis=-1, keepdims=True) + RMS_EPS)


def _lane_consts():
    lane = lax.broadcasted_iota(jnp.int32, (1, LANES), 1)
    m_left = (lane < HEAD_DIM).astype(F32)
    m_right = 1.0 - m_left
    first_half = (lane % HEAD_DIM) < (HEAD_DIM // 2)
    even = (lane % 2) == 0
    return m_left, m_right, first_half, even


def _head_norm(y, bd):
    ss = jnp.dot((y * y).astype(BF16), bd, preferred_element_type=F32) * (1.0 / HEAD_DIM)
    return y * lax.rsqrt(ss + RMS_EPS)


def _rot_half(y, cos, sin_signed, first_half):
    swapped = jnp.where(first_half, pltpu.roll(y, LANES - HEAD_DIM // 2, 1), pltpu.roll(y, HEAD_DIM // 2, 1))
    return y * cos + swapped * sin_signed


def _dot_tn(a, b):
    return lax.dot_general(a.astype(BF16), b.astype(BF16), (((0,), (0,)), ((), ())),
                           preferred_element_type=F32)


def _head_norm_t(y):
    return y * lax.rsqrt(jnp.mean(y * y, axis=0, keepdims=True) + RMS_EPS)


def _rot_half_t(y, cos, sin):
    half = HEAD_DIM // 2
    y1, y2 = y[0:half], y[half:]
    return jnp.concatenate([y1 * cos - y2 * sin, y2 * cos + y1 * sin], axis=0)


def _rot_pairs_t(y, cos, sin_signed, even_row):
    n = y.shape[0]
    swapped = jnp.where(even_row, pltpu.roll(y, n - 1, 0), pltpu.roll(y, 1, 0))
    return y * cos + swapped * sin_signed


def _prompt_mixer_kernel(x_ref, ca_ref, sa_ref, cr_ref, sr_ref, nmix_ref, wint_ref, qg_ref, kg_ref,
                         sink_ref, wout_ref, dm_ref, qd_ref, kd_ref, cd_ref,
                         y_ref, kwin_ref, vwin_ref, sout_ref,
                         pk_ref, pv_ref, s_ref, *, tile):
    t = pl.program_id(1)
    nblk = tile // WINDOW
    hd = HEAD_DIM

    @pl.when(t == 0)
    def _():
        pk_ref[...] = jnp.zeros_like(pk_ref)
        pv_ref[...] = jnp.zeros_like(pv_ref)
        s_ref[...] = jnp.zeros_like(s_ref)

    x = x_ref[0]
    hb = (_rms(x) * nmix_ref[...]).astype(BF16)
    proj = lambda lo, hi: lax.dot_general(wint_ref[lo:hi, :], hb, (((1,), (1,)), ((), ())),
                                          preferred_element_type=F32)
    ca, sa = ca_ref[...], sa_ref[...]
    cr, sr = cr_ref[...], sr_ref[...]
    qg, kg = qg_ref[...], kg_ref[...]

    swa_t = proj(C_QA, C_QR)
    ret_t = proj(C_QR, C_VR)
    vr = proj(C_VR, C_G)
    gate_t = proj(C_G, IN_COLS)
    head = lambda a, h: a[h * hd:(h + 1) * hd]

    qs = qg * (hd ** -0.5 * LOG2E)
    qa = [_rot_half_t(_head_norm_t(head(swa_t, h)) * qs, ca, sa) for h in range(SWA_HEADS)]
    ka = jnp.concatenate([_rot_half_t(_head_norm_t(head(swa_t, SWA_HEADS + g)) * kg, ca, sa)
                          for g in range(SWA_KV_HEADS)], axis=0)
    va = swa_t[C_VA:C_QR]
    kfull = jnp.concatenate([pk_ref[...], ka], axis=1).astype(BF16)
    vfull = jnp.concatenate([pv_ref[...], va], axis=1).astype(BF16)
    pk_ref[...] = ka[:, tile - WINDOW:]
    pv_ref[...] = va[:, tile - WINDOW:]
    kwin_ref[0] = ka[:, tile - WINDOW:].T
    vwin_ref[0] = va[:, tile - WINDOW:].T

    group = SWA_HEADS // SWA_KV_HEADS
    key = lax.broadcasted_iota(jnp.int32, (2 * WINDOW, group * WINDOW), 0)
    qry = lax.broadcasted_iota(jnp.int32, (2 * WINDOW, group * WINDOW), 1) % WINDOW
    band = (key > qry) & (key <= qry + WINDOW)
    oa_heads = [[None] * nblk for _ in range(SWA_HEADS)]
    units = [(g, j) for j in range(nblk) for g in range(SWA_KV_HEADS)]
    keys_of = lambda j: slice(j * WINDOW, (j + 2) * WINDOW)
    scores = []
    for g, j in units:
        toks = slice(j * WINDOW, (j + 1) * WINDOW)
        q4 = jnp.concatenate([qa[group * g + u][:, toks] for u in range(group)], axis=1)
        scores.append(_dot_tn(head(kfull, g)[:, keys_of(j)], q4))

    even_row = (lax.broadcasted_iota(jnp.int32, (RET_W, tile), 0) % 2) == 0
    cr8 = jnp.concatenate([cr] * RET_HEADS, axis=0)
    sr8 = jnp.concatenate([sr] * RET_HEADS, axis=0)
    qr = _rot_pairs_t(ret_t[0:RET_W], cr8, sr8, even_row)
    kr = _rot_pairs_t(ret_t[RET_W:], cr8, sr8, even_row) * (RET_DIM ** -0.5)
    zeros = jnp.zeros((hd, RET_CHUNK), F32)
    or_heads = [[None] * nblk for _ in range(RET_HEADS)]
    chunk = lambda a, h, j: a[h * hd:(h + 1) * hd, j * RET_CHUNK:(j + 1) * RET_CHUNK]
    inner = [[None] * nblk for _ in range(N_RET_SLABS)]
    for j in range(nblk):
        for pr in range(N_RET_SLABS):
            q_bd = jnp.concatenate([jnp.concatenate([chunk(qr, 2 * pr, j), zeros], axis=1),
                                    jnp.concatenate([zeros, chunk(qr, 2 * pr + 1, j)], axis=1)], axis=0)
            k2 = kr[pr * LANES:(pr + 1) * LANES, j * RET_CHUNK:(j + 1) * RET_CHUNK]
            inner[pr][j] = _dot_tn(k2, q_bd) * dm_ref[pr]
    incr = [[_dot_nt(chunk(vr, h, j), chunk(kr, h, j) * kd_ref[h]) for j in range(nblk)]
            for h in range(RET_HEADS)]

    probs, denoms = [], []
    for (g, j), s in zip(units, scores):
        sink = sink_ref[g] * LOG2E
        valid = band & (key >= WINDOW * (1 - t)) if j == 0 else band
        s = jnp.where(valid, s, NEG_INF)
        m = jnp.maximum(jnp.max(s, axis=0, keepdims=True), sink)
        p = jnp.exp2(s - m)
        denoms.append(jnp.sum(p, axis=0, keepdims=True) + jnp.exp2(sink - m))
        probs.append(p.astype(BF16))
    for (g, j), p, denom in zip(units, probs, denoms):
        o = jnp.dot(head(vfull, g)[:, keys_of(j)], p, preferred_element_type=F32) / denom
        for u in range(group):
            oa_heads[group * g + u][j] = o[:, u * WINDOW:(u + 1) * WINDOW]

    state = [[s_ref[h]] for h in range(RET_HEADS)]
    for h in range(RET_HEADS):
        for j in range(nblk):
            state[h].append(state[h][j] * cd_ref[h] + incr[h][j])
        s_ref[h] = state[h][nblk]
    for j in range(nblk):
        for h in range(RET_HEADS):
            u = h % 2
            lhs = jnp.concatenate([chunk(vr, h, j), state[h][j]], axis=1)
            rhs = jnp.concatenate([inner[h // 2][j][:, u * RET_CHUNK:(u + 1) * RET_CHUNK],
                                   chunk(qr, h, j) * qd_ref[h]], axis=0)
            or_heads[h][j] = _dot(lhs, rhs)

    @pl.when(t == pl.num_programs(1) - 1)
    def _():
        for h in range(RET_HEADS):
            sout_ref[0, h] = state[h][nblk].T

    mix = [jnp.concatenate(blocks, axis=1) for blocks in oa_heads]
    for h in range(RET_HEADS):
        o_h = _head_norm_t(jnp.concatenate(or_heads[h], axis=1))
        g_h = head(gate_t, h)
        mix.append(o_h * (g_h * jax.nn.sigmoid(g_h)))
    mix_t = jnp.concatenate(mix, axis=0).astype(BF16)
    y_ref[0] = x + lax.dot_general(mix_t, wout_ref[...], (((0,), (0,)), ((), ())),
                                   preferred_element_type=F32)


def _sample_in_kernel(x_ref, ca_ref, sa_ref, nmix_ref, wint_ref, qg_ref, kg_ref, bd_ref, feat_ref):
    _, _, first_half, _ = _lane_consts()
    ca, sa, bd = ca_ref[...], sa_ref[...], bd_ref[...]
    hb = (_rms(x_ref[...]) * nmix_ref[...]).astype(BF16)
    proj = lambda lo, hi: lax.dot_general(hb, wint_ref[lo:hi, :], (((1,), (1,)), ((), ())),
                                          preferred_element_type=F32)
    swa = proj(C_QA, C_QR)
    slabs = [_rot_half(_head_norm(swa[:, s * LANES:(s + 1) * LANES], bd) * qg_ref[...], ca, sa, first_half)
             for s in range(N_QA_SLABS)]
    slabs.append(_rot_half(_head_norm(swa[:, C_KA:C_VA], bd) * kg_ref[...], ca, sa, first_half))
    feat_ref[...] = jnp.concatenate(slabs + [swa[:, C_VA:C_QR], proj(C_G, IN_COLS)], axis=1)


def _sample_swa_kernel(feat_ref, ckt_ref, cvt_ref, sink_ref, o_ref, kout_ref, vout_ref,
                       kn_ref, vn_ref, *, nseq, ntok, unroll):
    @pl.when(pl.program_id(0) == 0)
    def _():
        kn_ref[...] = jnp.zeros_like(kn_ref)
        vn_ref[...] = jnp.zeros_like(vn_ref)

    m_left, m_right, _, _ = _lane_consts()
    tail = WINDOW - ntok
    nrow = SWA_HEADS * ntok
    row_tok = lax.broadcasted_iota(jnp.int32, (nrow, WINDOW), 0) % ntok
    col = lax.broadcasted_iota(jnp.int32, (nrow, WINDOW), 1)
    valid_cache = col > row_tok
    valid_new = (col >= tail) & (col - tail <= row_tok)
    in_tail = lax.broadcasted_iota(jnp.int32, (LANES, WINDOW), 1) >= tail
    sink = sink_ref[...]
    shift = lambda a: pltpu.roll(a, HEAD_DIM, 1)

    def body(step, carry):
        seqs = [step * unroll + u for u in range(unroll)]
        rows = [pl.ds(pl.multiple_of(i * ntok, ntok), ntok) for i in seqs]
        q_rows, caches = [], []
        for u, i in enumerate(seqs):
            slab = lambda c, u=u: feat_ref[rows[u], c:c + LANES]
            kn_ref[u, tail:WINDOW] = slab(C_KA)
            vn_ref[u, tail:WINDOW] = slab(C_VA)
            s0, s1, s2, s3 = [slab(C_QA + s * LANES) for s in range(N_QA_SLABS)]
            q_rows.append(jnp.concatenate(
                [s0 * m_left, shift(s0 * m_right), s1 * m_left, shift(s1 * m_right),
                 shift(s2 * m_left), s2 * m_right, shift(s3 * m_left), s3 * m_right], axis=0))
            caches.append((ckt_ref[i], cvt_ref[i], kn_ref[u], vn_ref[u]))
        scores = [(_dot(q, kt), _dot_nt(q, kn)) for q, (kt, _, kn, _) in zip(q_rows, caches)]
        probs = []
        for s_c, s_n in scores:
            s_c = jnp.where(valid_cache, s_c * (HEAD_DIM ** -0.5), NEG_INF)
            s_n = jnp.where(valid_new, s_n * (HEAD_DIM ** -0.5), NEG_INF)
            m = jnp.maximum(jnp.maximum(jnp.max(s_c, axis=-1, keepdims=True),
                                        jnp.max(s_n, axis=-1, keepdims=True)), sink)
            p_c, p_n = jnp.exp(s_c - m), jnp.exp(s_n - m)
            denom = (jnp.sum(p_c, axis=-1, keepdims=True) + jnp.sum(p_n, axis=-1, keepdims=True)
                     + jnp.exp(sink - m))
            probs.append((p_c, p_n, denom))
        for u, i in enumerate(seqs):
            kt, vt, kn, vn = caches[u]
            p_c, p_n, denom = probs[u]
            o = (_dot_nt(p_c, vt) + _dot(p_n, vn)) / denom
            n = ntok
            left = lambda h: o[h * n:(h + 1) * n] * m_left
            right = lambda h: o[h * n:(h + 1) * n] * m_right
            o_ref[rows[u], 0:LANES] = left(0) + shift(left(1))
            o_ref[rows[u], LANES:2 * LANES] = left(2) + shift(left(3))
            o_ref[rows[u], 2 * LANES:3 * LANES] = shift(right(4)) + right(5)
            o_ref[rows[u], 3 * LANES:4 * LANES] = shift(right(6)) + right(7)
            kout_ref[i] = jnp.where(in_tail, kn.T, pltpu.roll(kt, tail, 1))
            vout_ref[i] = jnp.where(in_tail, vn.T, pltpu.roll(vt, tail, 1))
        return carry

    lax.fori_loop(0, nseq // unroll, body, 0)


def _sample_ret_in_kernel(x_ref, cr_ref, sr_ref, nmix_ref, wint_ref, qkv_ref):
    hb = (_rms(x_ref[...]) * nmix_ref[...]).astype(BF16)
    qkv = lax.dot_general(wint_ref[...], hb, (((1,), (1,)), ((), ())), preferred_element_type=F32)
    n = qkv.shape[1]
    even_row = (lax.broadcasted_iota(jnp.int32, (RET_W, n), 0) % 2) == 0
    cr8 = jnp.concatenate([cr_ref[...]] * RET_HEADS, axis=0)
    sr8 = jnp.concatenate([sr_ref[...]] * RET_HEADS, axis=0)
    qkv_ref[0:RET_W] = _rot_pairs_t(qkv[0:RET_W], cr8, sr8, even_row)
    qkv_ref[RET_W:2 * RET_W] = _rot_pairs_t(qkv[RET_W:2 * RET_W], cr8, sr8, even_row) * (RET_DIM ** -0.5)
    qkv_ref[2 * RET_W:] = qkv[2 * RET_W:]


def _sample_ret_kernel(q_ref, k_ref, v_ref, s_ref, c_ref, o_ref, so_ref, ks_ref, *, ntok):
    hd, nseq = RET_DIM, LANES
    tok = lambda t: slice(t * nseq, (t + 1) * nseq)
    row_qd, row_kd, row_cd = ntok * ntok, ntok * ntok + ntok, ntok * ntok + 2 * ntok
    group = 8
    pair_outs = []
    for u in range(2):
        feats = slice(u * hd, (u + 1) * hd)
        const = lambda r: c_ref[u, r:r + 1, :]
        for t in range(ntok):
            ks_ref[t] = k_ref[feats, tok(t)] * const(row_kd + t)
        outs = []
        for t in range(ntok):
            q_t = q_ref[feats, tok(t)]
            acc = jnp.zeros((hd, nseq), F32)
            for k in range(t + 1):
                w_tk = jnp.sum(q_t * k_ref[feats, tok(k)], axis=0, keepdims=True) * const(t * ntok + k)
                acc = acc + w_tk * v_ref[feats, tok(k)]

            def cross(j, carry, t=t):
                d0 = pl.multiple_of(j * group, group)
                q_rows = q_ref[pl.ds(u * hd + d0, group), tok(t)]
                for i in range(group):
                    carry = carry + q_rows[i:i + 1, :] * s_ref[u, d0 + i]
                return carry
            carried = lax.fori_loop(0, hd // group, cross, jnp.zeros((hd, nseq), F32))
            outs.append(acc + carried * const(row_qd + t))

        def update(j, carry):
            d0 = pl.multiple_of(j * group, group)
            k_rows = [ks_ref[t, pl.ds(d0, group), :] for t in range(ntok)]
            for i in range(group):
                new = s_ref[u, d0 + i] * const(row_cd)
                for t in range(ntok):
                    new = new + k_rows[t][i:i + 1, :] * v_ref[feats, tok(t)]
                so_ref[u, d0 + i] = new
            return carry
        lax.fori_loop(0, hd // group, update, 0)
        pair_outs.append(outs)
    for t in range(ntok):
        o_ref[t] = jnp.concatenate([pair_outs[0][t], pair_outs[1][t]], axis=0).T


def _sample_out_kernel(x_ref, oa_ref, or_ref, feat_ref, bd_ref, wout_ref, y_ref):
    outs = [oa_ref[...]]
    for p in range(N_RET_SLABS):
        gate = feat_ref[:, C_GATE_S + p * LANES:C_GATE_S + (p + 1) * LANES]
        o_ret = or_ref[:, p * LANES:(p + 1) * LANES]
        outs.append(_head_norm(o_ret, bd_ref[...]) * (gate * jax.nn.sigmoid(gate)))
    mix = jnp.concatenate(outs, axis=1).astype(BF16)
    y_ref[...] = x_ref[...] + jnp.dot(mix, wout_ref[...], preferred_element_type=F32)


def _mem_kv_kernel(m_ref, nmem_ref, wkv_ref, kg_ref, k_ref, v_ref):
    hb = (_rms(m_ref[...]) * nmem_ref[...]).astype(BF16)
    kv = jnp.dot(hb, wkv_ref[...], preferred_element_type=F32)
    for h in range(MEM_HEADS):
        kh = kv[:, h * LANES:(h + 1) * LANES]
        k_ref[:, h * LANES:(h + 1) * LANES] = _rms(kh) * kg_ref[...]
    v_ref[...] = kv[:, MEM_W:]


def _mem_queries(x, ncross, wq_ref, qg):
    hb = (_rms(x) * ncross).astype(BF16)
    q = jnp.dot(hb, wq_ref[...], preferred_element_type=F32)
    qs = qg * (MEM_HEAD_DIM ** -0.5 * LOG2E)
    return [_rms(q[:, h * LANES:(h + 1) * LANES]) * qs for h in range(MEM_HEADS)]


def _prompt_cross_ffn_kernel(x_ref, mk_ref, mv_ref, ncross_ref, wq_ref, qg_ref, wo_ref, nffn_ref, wgu_ref,
                             wdown_ref, y_ref, xa_ref):
    @pl.when(pl.program_id(0) == 0)
    def _():
        xa_ref[...] = jnp.zeros_like(xa_ref)

    xa = xa_ref[...]
    hb = (_rms(xa) * nffn_ref[...]).astype(BF16)
    x = x_ref[...]
    qn = _mem_queries(x, ncross_ref[...], wq_ref, qg_ref[...])
    g = jnp.dot(hb, wgu_ref[:, 0:FFN_HIDDEN], preferred_element_type=F32)
    head = lambda ref, h: ref[0, :, h * LANES:(h + 1) * LANES].astype(BF16)
    scores = [_dot_nt(head(mk_ref, h), qn[h]) for h in range(MEM_HEADS)]
    u = jnp.dot(hb, wgu_ref[:, FFN_HIDDEN:], preferred_element_type=F32)
    probs, denoms = [], []
    for s in scores:
        p = jnp.exp2(s - jnp.max(s, axis=0, keepdims=True))
        denoms.append(jnp.sum(p, axis=0, keepdims=True))
        probs.append(p.astype(BF16))
    act = (g * jax.nn.sigmoid(g) * u).astype(BF16)
    o_t = jnp.concatenate([_dot_tn(head(mv_ref, h), probs[h]) / denoms[h] for h in range(MEM_HEADS)], axis=0)
    y_ref[...] = xa + jnp.dot(act, wdown_ref[...], preferred_element_type=F32)
    xa_ref[...] = x + lax.dot_general(o_t.astype(BF16), wo_ref[...], (((0,), (0,)), ((), ())),
                                      preferred_element_type=F32)


def _sample_cross_attn_kernel(x_ref, mk_ref, mv_ref, ncross_ref, wq_ref, qg_ref, o_ref, *, nseq, tq):
    qn = _mem_queries(x_ref[...], ncross_ref[...], wq_ref, qg_ref[...])
    nrow = MEM_HEADS * tq
    nmem = MEM_LEN * MEM_HEADS
    own = (lax.broadcasted_iota(jnp.int32, (nrow, nmem), 0) // tq
           == lax.broadcasted_iota(jnp.int32, (nrow, nmem), 1) % MEM_HEADS)
    scores = []
    for i in range(nseq):
        q_stack = jnp.concatenate([qn[h][i * tq:(i + 1) * tq] for h in range(MEM_HEADS)], axis=0)
        scores.append(_dot_nt(q_stack, mk_ref[i]))
    probs, denoms = [], []
    for s in scores:
        s = jnp.where(own, s, NEG_INF)
        p = jnp.exp2(s - jnp.max(s, axis=-1, keepdims=True))
        denoms.append(jnp.sum(p, axis=-1, keepdims=True))
        probs.append(p.astype(BF16))
    for i in range(nseq):
        o = jnp.dot(probs[i], mv_ref[i].astype(BF16), preferred_element_type=F32) / denoms[i]
        for h in range(MEM_HEADS):
            o_ref[i * tq:(i + 1) * tq, h * LANES:(h + 1) * LANES] = o[h * tq:(h + 1) * tq]


def _cross_out_ffn_kernel(x_ref, o_ref, wo_ref, nffn_ref, wgu_ref, wdown_ref, y_ref):
    x = x_ref[...] + jnp.dot(o_ref[...].astype(BF16), wo_ref[...], preferred_element_type=F32)
    hb = (_rms(x) * nffn_ref[...]).astype(BF16)
    g = jnp.dot(hb, wgu_ref[:, 0:FFN_HIDDEN], preferred_element_type=F32)
    u = jnp.dot(hb, wgu_ref[:, FFN_HIDDEN:], preferred_element_type=F32)
    act = (g * jax.nn.sigmoid(g) * u).astype(BF16)
    y_ref[...] = x + jnp.dot(act, wdown_ref[...], preferred_element_type=F32)


def _rope_tables(pos):
    half = HEAD_DIM // 2
    inv = 1.0 / (ROPE_THETA ** (jnp.arange(half, dtype=F32) / half))
    ang = pos.astype(F32)[:, None] * inv[None, :]
    cos, sin = jnp.cos(ang), jnp.sin(ang)
    c64 = jnp.concatenate([cos, cos], axis=-1)
    s64 = jnp.concatenate([-sin, sin], axis=-1)
    return jnp.tile(c64, (1, 2)), jnp.tile(s64, (1, 2))


def _sample_decay_rows(c):
    log_g = jnp.log(1.0 - jnp.exp2(-5.0 - jnp.arange(RET_HEADS, dtype=F32)))
    idx = jnp.arange(c, dtype=F32)
    diff = idx[:, None] - idx[None, :]
    dmat = jnp.where(diff >= 0, jnp.exp(jnp.maximum(diff, 0.0)[None] * log_g[:, None, None]), 0.0)
    qd = jnp.exp((idx + 1.0)[None, :] * log_g[:, None])
    kd = jnp.exp((c - 1.0 - idx)[None, :] * log_g[:, None])
    cd = jnp.exp(c * log_g)[:, None]
    rows = jnp.concatenate([dmat.reshape(RET_HEADS, c * c), qd, kd, cd], axis=1)
    rows = jnp.pad(rows, ((0, 0), (0, -rows.shape[1] % 8)))
    return jnp.broadcast_to(rows[:, :, None], rows.shape + (LANES,))


def _full(shape):
    nd = len(shape)
    return pl.BlockSpec(shape, lambda *_: (0,) * nd)


def _params(sem):
    return pltpu.CompilerParams(dimension_semantics=sem, vmem_limit_bytes=VMEM_LIMIT)


def _prompt_tables_t(pos):
    half = HEAD_DIM // 2
    inv = 1.0 / (ROPE_THETA ** (jnp.arange(half, dtype=F32) / half))
    ang = inv[:, None] * pos.astype(F32)[None, :]
    inv_r = RET_THETA ** (-jnp.linspace(0.0, 1.0, RET_DIM // 2, dtype=F32))
    ang_r = inv_r[:, None] * pos.astype(F32)[None, :]
    cos_r, sin_r = jnp.cos(ang_r), jnp.sin(ang_r)
    cr = jnp.repeat(cos_r, 2, axis=0)
    sr = jnp.stack([-sin_r, sin_r], axis=1).reshape(RET_DIM, pos.shape[0])
    return jnp.cos(ang), jnp.sin(ang), cr, sr


def _decay_consts_t(c):
    log_g = jnp.log(1.0 - jnp.exp2(-5.0 - jnp.arange(RET_HEADS, dtype=F32)))
    idx = jnp.arange(c, dtype=F32)
    diff = idx[:, None] - idx[None, :]
    dmat = jnp.where(diff >= 0, jnp.exp(jnp.maximum(diff, 0.0)[None] * log_g[:, None, None]), 0.0)
    dm = dmat.transpose(0, 2, 1).reshape(N_RET_SLABS, 2, c, c).transpose(0, 2, 1, 3).reshape(N_RET_SLABS, c, 2 * c)
    qd = jnp.exp((idx + 1.0)[None, :] * log_g[:, None])[:, None, :]
    kd = jnp.exp((c - 1.0 - idx)[None, :] * log_g[:, None])[:, None, :]
    cd = jnp.broadcast_to(jnp.exp(c * log_g)[:, None, None], (RET_HEADS, 1, RET_DIM))
    return dm, qd, kd, cd


def _prompt_mixer(x, w, tile):
    b, l, d = x.shape
    ca, sa, cr, sr = _prompt_tables_t(jnp.arange(l, dtype=jnp.int32))
    dm, qd, kd, cd = _decay_consts_t(RET_CHUNK)
    sink = jnp.repeat(w['sinks'].reshape(SWA_KV_HEADS, 1, -1), WINDOW, axis=-1)
    tab = lambda rows: pl.BlockSpec((rows, tile), lambda i, t: (0, t))
    xspec = pl.BlockSpec((1, tile, d), lambda i, t: (i, t, 0))
    win_spec = pl.BlockSpec((1, WINDOW, LANES), lambda i, t: (i, 0, 0))
    st_spec = pl.BlockSpec((1, RET_HEADS, RET_DIM, RET_DIM), lambda i, t: (i, 0, 0, 0))
    return pl.pallas_call(
        functools.partial(_prompt_mixer_kernel, tile=tile),
        grid=(b, l // tile),
        in_specs=[xspec, tab(HEAD_DIM // 2), tab(HEAD_DIM // 2), tab(RET_DIM), tab(RET_DIM), _full((1, d)),
                  _full((IN_COLS, d)), _full((HEAD_DIM, 1)), _full((HEAD_DIM, 1)), _full(sink.shape),
                  _full((d, d)), _full(dm.shape), _full(qd.shape), _full(kd.shape), _full(cd.shape)],
        out_specs=[xspec, win_spec, win_spec, st_spec],
        out_shape=[jax.ShapeDtypeStruct(x.shape, F32),
                   jax.ShapeDtypeStruct((b, WINDOW, LANES), F32),
                   jax.ShapeDtypeStruct((b, WINDOW, LANES), F32),
                   jax.ShapeDtypeStruct((b, RET_HEADS, RET_DIM, RET_DIM), F32)],
        scratch_shapes=[pltpu.VMEM((SWA_KV_W, WINDOW), F32), pltpu.VMEM((SWA_KV_W, WINDOW), F32),
                        pltpu.VMEM((RET_HEADS, RET_DIM, RET_DIM), F32)],
        compiler_params=_params(("arbitrary", "arbitrary")),
        name="prompt_mixer",
    )(x, ca, sa, cr, sr, w['norm_mix'], w['w_in_t'], w['qg_col'], w['kg_col'], sink, w['w_out'], dm, qd, kd,
      cd)


def _sample_mixer(x, cache_k, cache_v, state, w, tile, nseq):
    b, ntok, d = x.shape
    rows = b * ntok
    xf = x.reshape(rows, d)
    pos = jnp.tile(PAST_LEN + jnp.arange(ntok, dtype=jnp.int32), b)
    ca, sa = _rope_tables(pos)
    tab = pl.BlockSpec((tile, LANES), lambda i: (i, 0))
    row_spec = lambda width: pl.BlockSpec((tile, width), lambda i: (i, 0))
    feat = pl.pallas_call(
        _sample_in_kernel,
        grid=(rows // tile,),
        in_specs=[row_spec(d), tab, tab, _full((1, d)), _full((IN_COLS, d)), _full((1, LANES)),
                  _full((1, LANES)), _full((LANES, LANES))],
        out_specs=row_spec(SAMPLE_FEAT),
        out_shape=jax.ShapeDtypeStruct((rows, SAMPLE_FEAT), F32),
        compiler_params=_params(("arbitrary",)),
        name="sample_in",
    )(xf, ca, sa, w['norm_mix'], w['w_in_t'], w['qg'], w['kg'], w['bd'])

    sink = jnp.repeat(w['sinks'], ntok)[:, None]
    seq_rows = nseq * ntok
    unroll = 4
    cache_spec = pl.BlockSpec((nseq, LANES, WINDOW), lambda i: (i, 0, 0))
    o_a, k_out, v_out = pl.pallas_call(
        functools.partial(_sample_swa_kernel, nseq=nseq, ntok=ntok, unroll=unroll),
        grid=(b // nseq,),
        in_specs=[pl.BlockSpec((seq_rows, SAMPLE_FEAT), lambda i: (i, 0)), cache_spec, cache_spec,
                  _full(sink.shape)],
        out_specs=[pl.BlockSpec((seq_rows, SWA_Q_W), lambda i: (i, 0)), cache_spec, cache_spec],
        out_shape=[jax.ShapeDtypeStruct((rows, SWA_Q_W), F32),
                   jax.ShapeDtypeStruct(cache_k.shape, F32),
                   jax.ShapeDtypeStruct(cache_v.shape, F32)],
        scratch_shapes=[pltpu.VMEM((unroll, WINDOW, LANES), F32), pltpu.VMEM((unroll, WINDOW, LANES), F32)],
        compiler_params=_params(("arbitrary",)),
        name="sample_swa",
    )(feat, cache_k, cache_v, sink)

    assert b == LANES, "the retention step puts one sequence per lane"
    xt = jnp.transpose(x, (1, 0, 2)).reshape(rows, d)
    pos_t = jnp.repeat(PAST_LEN + jnp.arange(ntok, dtype=jnp.int32), b)
    _, _, cr_t, sr_t = _prompt_tables_t(pos_t)
    half = rows // 2
    tab_t = pl.BlockSpec((RET_DIM, half), lambda i: (0, i))
    qkv = pl.pallas_call(
        _sample_ret_in_kernel,
        grid=(2,),
        in_specs=[pl.BlockSpec((half, d), lambda i: (i, 0)), tab_t, tab_t, _full((1, d)),
                  _full((3 * RET_W, d))],
        out_specs=pl.BlockSpec((3 * RET_W, half), lambda i: (0, i)),
        out_shape=jax.ShapeDtypeStruct((3 * RET_W, rows), F32),
        compiler_params=_params(("arbitrary",)),
        name="sample_ret_in",
    )(xt, cr_t, sr_t, w['norm_mix'], w['w_in_t'][C_QR:C_G])
    consts = _sample_decay_rows(ntok)
    pair_rows = lambda off: pl.BlockSpec((LANES, rows), lambda p: (off + p, 0))
    st_spec = pl.BlockSpec((2, RET_DIM, RET_DIM, b), lambda p: (p, 0, 0, 0))
    o_r, s_out = pl.pallas_call(
        functools.partial(_sample_ret_kernel, ntok=ntok),
        grid=(N_RET_SLABS,),
        in_specs=[pair_rows(0), pair_rows(N_RET_SLABS), pair_rows(2 * N_RET_SLABS), st_spec,
                  pl.BlockSpec((2,) + consts.shape[1:], lambda p: (p, 0, 0))],
        out_specs=[pl.BlockSpec((ntok, b, LANES), lambda p: (0, 0, p)), st_spec],
        out_shape=[jax.ShapeDtypeStruct((ntok, b, RET_W), F32), jax.ShapeDtypeStruct(state.shape, F32)],
        scratch_shapes=[pltpu.VMEM((ntok, RET_DIM, b), F32)],
        compiler_params=_params(("arbitrary",)),
        name="sample_ret",
    )(qkv, qkv, qkv, state, consts)
    o_r = jnp.transpose(o_r, (1, 0, 2)).reshape(rows, RET_W)

    y = pl.pallas_call(
        _sample_out_kernel,
        grid=(rows // tile,),
        in_specs=[row_spec(d), row_spec(SWA_Q_W), row_spec(RET_W), row_spec(SAMPLE_FEAT), _full((LANES, LANES)),
                  _full((d, d))],
        out_specs=row_spec(d),
        out_shape=jax.ShapeDtypeStruct((rows, d), F32),
        compiler_params=_params(("arbitrary",)),
        name="sample_out",
    )(xf, o_a, o_r, feat, w['bd'], w['w_out'])
    return y, k_out, v_out, s_out


def _mem_kv(mem, w, tile):
    rows, d = mem.shape
    row_spec = lambda width: pl.BlockSpec((tile, width), lambda i: (i, 0))
    return pl.pallas_call(
        _mem_kv_kernel,
        grid=(rows // tile,),
        in_specs=[row_spec(d), _full((1, d)), _full((d, 2 * MEM_W)), _full((1, LANES))],
        out_specs=[row_spec(MEM_W), row_spec(MEM_W)],
        out_shape=[jax.ShapeDtypeStruct((rows, MEM_W), F32)] * 2,
        compiler_params=_params(("arbitrary",)),
        name="mem_kv",
    )(mem, w['norm_mem'], w['w_mkv'], w['kgm'])


def _prompt_cross_ffn(x, mk, mv, w, tile):
    rows, d = x.shape
    n = rows // tile
    per_mem = n // mk.shape[0]
    cur = lambda i: jnp.minimum(i, n - 1)
    single = lambda shape: pl.BlockSpec(shape, lambda i: (0,) * len(shape), pipeline_mode=pl.Buffered(1))
    mem_spec = pl.BlockSpec((1, MEM_LEN, MEM_W), lambda i: (cur(i) // per_mem, 0, 0))
    return pl.pallas_call(
        _prompt_cross_ffn_kernel,
        grid=(n + 1,),
        in_specs=[pl.BlockSpec((tile, d), lambda i: (cur(i), 0)), mem_spec, mem_spec, _full((1, d)),
                  single((d, MEM_W)), _full((1, LANES)), single((MEM_W, d)), _full((1, d)),
                  single((d, 2 * FFN_HIDDEN)), single((FFN_HIDDEN, d))],
        out_specs=pl.BlockSpec((tile, d), lambda i: (jnp.maximum(i - 1, 0), 0)),
        out_shape=jax.ShapeDtypeStruct((rows, d), F32),
        scratch_shapes=[pltpu.VMEM((tile, d), F32)],
        compiler_params=_params(("arbitrary",)),
        name="prompt_cross_ffn",
    )(x, mk, mv, w['norm_cross'], w['w_mq'], w['qgm'], w['w_mo'], w['norm_ffn'], w['w_gu'], w['w_down'])


def _sample_cross_attn(x, mk, mv, w, nseq, tq):
    rows, d = x.shape
    blk = nseq * tq
    mem_spec = pl.BlockSpec((nseq, MEM_LEN * MEM_HEADS, MEM_HEAD_DIM), lambda i: (i, 0, 0))
    return pl.pallas_call(
        functools.partial(_sample_cross_attn_kernel, nseq=nseq, tq=tq),
        grid=(rows // blk,),
        in_specs=[pl.BlockSpec((blk, d), lambda i: (i, 0)), mem_spec, mem_spec, _full((1, d)),
                  _full((d, MEM_W)), _full((1, LANES))],
        out_specs=pl.BlockSpec((blk, MEM_W), lambda i: (i, 0)),
        out_shape=jax.ShapeDtypeStruct((rows, MEM_W), F32),
        compiler_params=_params(("arbitrary",)),
        name="sample_cross_attn",
    )(x, mk, mv, w['norm_cross'], w['w_mq'], w['qgm'])


def _sample_cross_out_ffn(x, o, w, tile):
    rows, d = x.shape
    row_spec = lambda width: pl.BlockSpec((tile, width), lambda i: (i, 0))
    single = lambda shape: pl.BlockSpec(shape, lambda i: (0,) * len(shape), pipeline_mode=pl.Buffered(1))
    return pl.pallas_call(
        _cross_out_ffn_kernel,
        grid=(rows // tile,),
        in_specs=[row_spec(d), row_spec(MEM_W), single((MEM_W, d)), _full((1, d)),
                  single((d, 2 * FFN_HIDDEN)), single((FFN_HIDDEN, d))],
        out_specs=row_spec(d),
        out_shape=jax.ShapeDtypeStruct((rows, d), F32),
        compiler_params=_params(("arbitrary",)),
        name="sample_ffn",
    )(x, o, w['w_mo'], w['norm_ffn'], w['w_gu'], w['w_down'])


def kernel(x_prompt, x_sample, mem_prompt, cache_swa_k, cache_swa_v, state_ret, cache_mem_k, cache_mem_v,
           norm_mix, w_in, q_norm_a, k_norm_a, sinks, w_out, norm_cross, norm_mem, w_mq, w_mkv,
           q_norm_m, k_norm_m, w_mo, norm_ffn, w_gu, w_down):
    assert norm_mix.shape[0] == 1, "single-layer kernel"
    b, l, d = x_prompt.shape
    sb, st, _ = x_sample.shape
    half = (jnp.arange(LANES) // HEAD_DIM)[:, None] == (jnp.arange(LANES) // HEAD_DIM)[None, :]
    w = {
        'norm_mix': norm_mix, 'norm_cross': norm_cross, 'norm_mem': norm_mem, 'norm_ffn': norm_ffn,
        'w_out': w_out[0].astype(BF16), 'w_mq': w_mq[0].astype(BF16),
        'w_mkv': w_mkv[0].astype(BF16), 'w_mo': w_mo[0].astype(BF16), 'w_gu': w_gu[0].astype(BF16),
        'w_down': w_down[0].astype(BF16),
        'w_in_t': w_in[0].T.astype(BF16),
        'qg': jnp.tile(q_norm_a, (1, 2)), 'kg': jnp.tile(k_norm_a, (1, 2)),
        'qg_col': q_norm_a.reshape(HEAD_DIM, 1), 'kg_col': k_norm_a.reshape(HEAD_DIM, 1),
        'qgm': q_norm_m, 'kgm': k_norm_m, 'sinks': sinks[0],
        'bd': half.astype(BF16),
    }

    mk, mv = _mem_kv(mem_prompt.reshape(b * MEM_LEN, d), w, tile=256)
    xp, kwin, vwin, ret_p = _prompt_mixer(x_prompt, w, tile=1024)
    xp = xp.reshape(b * l, d)
    yp = _prompt_cross_ffn(xp, mk.reshape(b, MEM_LEN, MEM_W), mv.reshape(b, MEM_LEN, MEM_W), w, tile=512)
    yp = yp.reshape(b, l, d)
    ret_p = ret_p.reshape(1, b, RET_HEADS, RET_DIM, RET_DIM)

    nbuf = cache_swa_k.shape[2]
    assert nbuf == WINDOW
    to_feature_major = lambda c: jnp.transpose(c[0], (0, 2, 3, 1)).reshape(sb, SWA_KV_W, nbuf)
    from_feature_major = lambda c: jnp.transpose(c.reshape(sb, SWA_KV_HEADS, HEAD_DIM, nbuf), (0, 3, 1, 2))[None]
    xs, k_s, v_s, s_s = _sample_mixer(x_sample, to_feature_major(cache_swa_k), to_feature_major(cache_swa_v),
                                      jnp.transpose(state_ret[0], (1, 2, 3, 0)), w, tile=256, nseq=8)
    mem_rows = (sb, MEM_LEN * MEM_HEADS, MEM_HEAD_DIM)
    os_ = _sample_cross_attn(xs, cache_mem_k.reshape(mem_rows), cache_mem_v.reshape(mem_rows), w, nseq=8, tq=st)
    ys = _sample_cross_out_ffn(xs, os_, w, tile=512).reshape(sb, st, d)

    kv_shape = (1, b, WINDOW, SWA_KV_HEADS, HEAD_DIM)
    mem_shape = (1, b, MEM_LEN, MEM_HEADS, MEM_HEAD_DIM)
    return (yp, ys, kwin.reshape(kv_shape), vwin.reshape(kv_shape), ret_p,
            mk.reshape(mem_shape), mv.reshape(mem_shape),
            from_feature_major(k_s), from_feature_major(v_s),
            jnp.transpose(s_s, (3, 0, 1, 2))[None])
```

```python
import functools

import jax
import jax.numpy as jnp
from jax import lax
from jax.experimental import pallas as pl
from jax.experimental.pallas import tpu as pltpu

F32 = jnp.float32
BF16 = jnp.bfloat16

LANES = 128
D_MODEL = 1024
HEAD_DIM = 64
SWA_HEADS = 8
SWA_KV_HEADS = 2
WINDOW = 128
RET_HEADS = 8
RET_DIM = 64
RET_CHUNK = 128
RET_THETA = 10000.0
ROPE_THETA = 10000.0
MEM_LEN = 256
MEM_HEADS = 4
MEM_HEAD_DIM = 128
MEM_W = MEM_HEADS * MEM_HEAD_DIM
FFN_HIDDEN = 2816
RMS_EPS = 1e-6
NEG_INF = -1e30
LOG2E = 1.4426950408889634
PAST_LEN = 16384

SWA_Q_W = SWA_HEADS * HEAD_DIM
SWA_KV_W = SWA_KV_HEADS * HEAD_DIM
RET_W = RET_HEADS * RET_DIM
IN_COLS = SWA_Q_W + 2 * SWA_KV_W + 4 * RET_W
C_QA, C_KA, C_VA = 0, SWA_Q_W, SWA_Q_W + SWA_KV_W
C_QR = SWA_Q_W + 2 * SWA_KV_W
C_KR, C_VR, C_G = C_QR + RET_W, C_QR + 2 * RET_W, C_QR + 3 * RET_W
C_GATE_S = C_QR
SAMPLE_FEAT = C_QR + RET_W
N_QA_SLABS = SWA_Q_W // LANES
N_RET_SLABS = RET_W // LANES

VMEM_LIMIT = 56 * 1024 * 1024


def _dot(a, b):
    return jnp.dot(a.astype(BF16), b.astype(BF16), preferred_element_type=F32)


def _dot_nt(a, b):
    return lax.dot_general(a.astype(BF16), b.astype(BF16), (((1,), (1,)), ((), ())),
                           preferred_element_type=F32)


def _rms(x):
    return x * lax.rsqrt(jnp.mean(x * x, axis=-1, keepdims=True) + RMS_EPS)


def _lane_consts():
    lane = lax.broadcasted_iota(jnp.int32, (1, LANES), 1)
    m_left = (lane < HEAD_DIM).astype(F32)
    m_right = 1.0 - m_left
    first_half = (lane % HEAD_DIM) < (HEAD_DIM // 2)
    even = (lane % 2) == 0
    return m_left, m_right, first_half, even


def _head_norm(y, bd):
    ss = jnp.dot((y * y).astype(BF16), bd, preferred_element_type=F32) * (1.0 / HEAD_DIM)
    return y * lax.rsqrt(ss + RMS_EPS)


def _rot_half(y, cos, sin_signed, first_half):
    swapped = jnp.where(first_half, pltpu.roll(y, LANES - HEAD_DIM // 2, 1), pltpu.roll(y, HEAD_DIM // 2, 1))
    return y * cos + swapped * sin_signed


def _dot_tn(a, b):
    return lax.dot_general(a.astype(BF16), b.astype(BF16), (((0,), (0,)), ((), ())),
                           preferred_element_type=F32)


def _head_norm_t(y):
    return y * lax.rsqrt(jnp.mean(y * y, axis=0, keepdims=True) + RMS_EPS)


def _rot_half_t(y, cos, sin):
    half = HEAD_DIM // 2
    y1, y2 = y[0:half], y[half:]
    return jnp.concatenate([y1 * cos - y2 * sin, y2 * cos + y1 * sin], axis=0)


def _rot_pairs_t(y, cos, sin_signed, even_row):
    n = y.shape[0]
    swapped = jnp.where(even_row, pltpu.roll(y, n - 1, 0), pltpu.roll(y, 1, 0))
    return y * cos + swapped * sin_signed


def _prompt_mixer_kernel(x_ref, ca_ref, sa_ref, cr_ref, sr_ref, nmix_ref, wint_ref, qg_ref, kg_ref,
                         sink_ref, wout_ref, dm_ref, qd_ref, kd_ref, cd_ref,
                         y_ref, kwin_ref, vwin_ref, sout_ref,
                         pk_ref, pv_ref, s_ref, *, tile, sub):
    t = pl.program_id(1)
    nblk = sub // WINDOW
    hd = HEAD_DIM
    group = SWA_HEADS // SWA_KV_HEADS

    @pl.when(t == 0)
    def _():
        pk_ref[...] = jnp.zeros_like(pk_ref)
        pv_ref[...] = jnp.zeros_like(pv_ref)
        s_ref[...] = jnp.zeros_like(s_ref)

    qg, kg = qg_ref[...], kg_ref[...]
    qs = qg * (hd ** -0.5 * LOG2E)
    head = lambda a, h: a[h * hd:(h + 1) * hd]
    chunk = lambda a, h, j: a[h * hd:(h + 1) * hd, j * RET_CHUNK:(j + 1) * RET_CHUNK]
    keys_of = lambda j: slice(j * WINDOW, (j + 2) * WINDOW)
    units = [(g, j) for j in range(nblk) for g in range(SWA_KV_HEADS)]
    key = lax.broadcasted_iota(jnp.int32, (2 * WINDOW, group * WINDOW), 0)
    qry = lax.broadcasted_iota(jnp.int32, (2 * WINDOW, group * WINDOW), 1) % WINDOW
    band = (key > qry) & (key <= qry + WINDOW)
    even_row = (lax.broadcasted_iota(jnp.int32, (RET_W, sub), 0) % 2) == 0
    zeros = jnp.zeros((hd, RET_CHUNK), F32)
    carry = {'pk': pk_ref[...], 'pv': pv_ref[...], 'state': [s_ref[h] for h in range(RET_HEADS)]}

    def stages(idx):
        toks = slice(idx * sub, (idx + 1) * sub)
        v = {}

        def project():
            v['x'] = x_ref[0, toks]
            hb = (_rms(v['x']) * nmix_ref[...]).astype(BF16)
            proj = lambda lo, hi: lax.dot_general(wint_ref[lo:hi, :], hb, (((1,), (1,)), ((), ())),
                                                  preferred_element_type=F32)
            v['swa'], v['ret'] = proj(C_QA, C_QR), proj(C_QR, C_VR)
            v['vr'], v['gate'] = proj(C_VR, C_G), proj(C_G, IN_COLS)

        def swa_prepare():
            ca, sa = ca_ref[:, toks], sa_ref[:, toks]
            swa_t = v['swa']
            v['qa'] = [_rot_half_t(_head_norm_t(head(swa_t, h)) * qs, ca, sa) for h in range(SWA_HEADS)]
            ka = jnp.concatenate([_rot_half_t(_head_norm_t(head(swa_t, SWA_HEADS + g)) * kg, ca, sa)
                                  for g in range(SWA_KV_HEADS)], axis=0)
            va = swa_t[C_VA:C_QR]
            v['kfull'] = jnp.concatenate([carry['pk'], ka], axis=1).astype(BF16)
            v['vfull'] = jnp.concatenate([carry['pv'], va], axis=1).astype(BF16)
            carry['pk'], carry['pv'] = ka[:, sub - WINDOW:], va[:, sub - WINDOW:]

        def swa_scores():
            v['scores'] = []
            for g, j in units:
                blk = slice(j * WINDOW, (j + 1) * WINDOW)
                q4 = jnp.concatenate([v['qa'][group * g + u][:, blk] for u in range(group)], axis=1)
                v['scores'].append(_dot_tn(head(v['kfull'], g)[:, keys_of(j)], q4))

        def ret_rotate():
            cr8 = jnp.concatenate([cr_ref[:, toks]] * RET_HEADS, axis=0)
            sr8 = jnp.concatenate([sr_ref[:, toks]] * RET_HEADS, axis=0)
            v['qr'] = _rot_pairs_t(v['ret'][0:RET_W], cr8, sr8, even_row)
            v['kr'] = _rot_pairs_t(v['ret'][RET_W:], cr8, sr8, even_row) * (RET_DIM ** -0.5)

        def ret_scores():
            qr, kr, vr = v['qr'], v['kr'], v['vr']
            v['inner'] = [[None] * nblk for _ in range(N_RET_SLABS)]
            for j in range(nblk):
                for pr in range(N_RET_SLABS):
                    q_bd = jnp.concatenate([jnp.concatenate([chunk(qr, 2 * pr, j), zeros], axis=1),
                                            jnp.concatenate([zeros, chunk(qr, 2 * pr + 1, j)], axis=1)], axis=0)
                    k2 = kr[pr * LANES:(pr + 1) * LANES, j * RET_CHUNK:(j + 1) * RET_CHUNK]
                    v['inner'][pr][j] = _dot_tn(k2, q_bd) * dm_ref[pr]
            v['incr'] = [[_dot_nt(chunk(vr, h, j), chunk(kr, h, j) * kd_ref[h]) for j in range(nblk)]
                         for h in range(RET_HEADS)]

        def swa_softmax():
            v['probs'], v['denoms'] = [], []
            for (g, j), s in zip(units, v['scores']):
                sink = sink_ref[g] * LOG2E
                valid = band & (key >= WINDOW * (1 - t)) if (idx == 0 and j == 0) else band
                s = jnp.where(valid, s, NEG_INF)
                m = jnp.maximum(jnp.max(s, axis=0, keepdims=True), sink)
                p = jnp.exp2(s - m)
                v['denoms'].append(jnp.sum(p, axis=0, keepdims=True) + jnp.exp2(sink - m))
                v['probs'].append(p.astype(BF16))

        def swa_values():
            v['oa'] = [[None] * nblk for _ in range(SWA_HEADS)]
            for (g, j), p, denom in zip(units, v['probs'], v['denoms']):
                o = jnp.dot(head(v['vfull'], g)[:, keys_of(j)], p, preferred_element_type=F32) / denom
                for u in range(group):
                    v['oa'][group * g + u][j] = o[:, u * WINDOW:(u + 1) * WINDOW]
            v['state'] = [[carry['state'][h]] for h in range(RET_HEADS)]
            for h in range(RET_HEADS):
                for j in range(nblk):
                    v['state'][h].append(v['state'][h][j] * cd_ref[h] + v['incr'][h][j])
                carry['state'][h] = v['state'][h][nblk]

        def ret_outputs():
            v['or'] = [[None] * nblk for _ in range(RET_HEADS)]
            for j in range(nblk):
                for h in range(RET_HEADS):
                    u = h % 2
                    lhs = jnp.concatenate([chunk(v['vr'], h, j), v['state'][h][j]], axis=1)
                    rhs = jnp.concatenate([v['inner'][h // 2][j][:, u * RET_CHUNK:(u + 1) * RET_CHUNK],
                                           chunk(v['qr'], h, j) * qd_ref[h]], axis=0)
                    v['or'][h][j] = _dot(lhs, rhs)

        def gate_mix():
            mix = [jnp.concatenate(blocks, axis=1) for blocks in v['oa']]
            for h in range(RET_HEADS):
                o_h = _head_norm_t(jnp.concatenate(v['or'][h], axis=1))
                g_h = head(v['gate'], h)
                mix.append(o_h * (g_h * jax.nn.sigmoid(g_h)))
            v['mix'] = jnp.concatenate(mix, axis=0).astype(BF16)

        def out_project():
            y_ref[0, toks] = v['x'] + lax.dot_general(v['mix'], wout_ref[...], (((0,), (0,)), ((), ())),
                                                      preferred_element_type=F32)

        return [project, swa_prepare, swa_scores, ret_rotate, ret_scores, swa_softmax, swa_values,
                ret_outputs, gate_mix, out_project]

    pipelines = [stages(i) for i in range(tile // sub)]
    nstage = len(pipelines[0])
    for slot in range(nstage + len(pipelines) - 1):
        for i, pipe in enumerate(pipelines):
            if 0 <= slot - i < nstage:
                pipe[slot - i]()

    pk_ref[...], pv_ref[...] = carry['pk'], carry['pv']
    kwin_ref[0], vwin_ref[0] = carry['pk'].T, carry['pv'].T
    for h in range(RET_HEADS):
        s_ref[h] = carry['state'][h]

    @pl.when(t == pl.num_programs(1) - 1)
    def _():
        for h in range(RET_HEADS):
            sout_ref[0, h] = carry['state'][h].T


def _sample_in_kernel(x_ref, ca_ref, sa_ref, nmix_ref, wint_ref, qg_ref, kg_ref, bd_ref, feat_ref):
    _, _, first_half, _ = _lane_consts()
    ca, sa, bd = ca_ref[...], sa_ref[...], bd_ref[...]
    hb = (_rms(x_ref[...]) * nmix_ref[...]).astype(BF16)
    proj = lambda lo, hi: lax.dot_general(hb, wint_ref[lo:hi, :], (((1,), (1,)), ((), ())),
                                          preferred_element_type=F32)
    swa = proj(C_QA, C_QR)
    slabs = [_rot_half(_head_norm(swa[:, s * LANES:(s + 1) * LANES], bd) * qg_ref[...], ca, sa, first_half)
             for s in range(N_QA_SLABS)]
    slabs.append(_rot_half(_head_norm(swa[:, C_KA:C_VA], bd) * kg_ref[...], ca, sa, first_half))
    feat_ref[...] = jnp.concatenate(slabs + [swa[:, C_VA:C_QR], proj(C_G, IN_COLS)], axis=1)


def _sample_swa_kernel(feat_ref, ckt_ref, cvt_ref, sink_ref, o_ref, kout_ref, vout_ref,
                       kn_ref, vn_ref, *, nseq, ntok, unroll):
    @pl.when(pl.program_id(0) == 0)
    def _():
        kn_ref[...] = jnp.zeros_like(kn_ref)
        vn_ref[...] = jnp.zeros_like(vn_ref)

    m_left, m_right, _, _ = _lane_consts()
    tail = WINDOW - ntok
    nrow = SWA_HEADS * ntok
    row_tok = lax.broadcasted_iota(jnp.int32, (nrow, WINDOW), 0) % ntok
    col = lax.broadcasted_iota(jnp.int32, (nrow, WINDOW), 1)
    valid_cache = col > row_tok
    valid_new = (col >= tail) & (col - tail <= row_tok)
    in_tail = lax.broadcasted_iota(jnp.int32, (LANES, WINDOW), 1) >= tail
    sink = sink_ref[...]
    shift = lambda a: pltpu.roll(a, HEAD_DIM, 1)

    def body(step, carry):
        seqs = [step * unroll + u for u in range(unroll)]
        rows = [pl.ds(pl.multiple_of(i * ntok, ntok), ntok) for i in seqs]
        q_rows, caches = [], []
        for u, i in enumerate(seqs):
            slab = lambda c, u=u: feat_ref[rows[u], c:c + LANES]
            kn_ref[u, tail:WINDOW] = slab(C_KA)
            vn_ref[u, tail:WINDOW] = slab(C_VA)
            s0, s1, s2, s3 = [slab(C_QA + s * LANES) for s in range(N_QA_SLABS)]
            q_rows.append(jnp.concatenate(
                [s0 * m_left, shift(s0 * m_right), s1 * m_left, shift(s1 * m_right),
                 shift(s2 * m_left), s2 * m_right, shift(s3 * m_left), s3 * m_right], axis=0))
            caches.append((ckt_ref[i], cvt_ref[i], kn_ref[u], vn_ref[u]))
        scores = [(_dot(q, kt), _dot_nt(q, kn)) for q, (kt, _, kn, _) in zip(q_rows, caches)]
        probs = []
        for s_c, s_n in scores:
            s_c = jnp.where(valid_cache, s_c * (HEAD_DIM ** -0.5), NEG_INF)
            s_n = jnp.where(valid_new, s_n * (HEAD_DIM ** -0.5), NEG_INF)
            m = jnp.maximum(jnp.maximum(jnp.max(s_c, axis=-1, keepdims=True),
                                        jnp.max(s_n, axis=-1, keepdims=True)), sink)
            p_c, p_n = jnp.exp(s_c - m), jnp.exp(s_n - m)
            denom = (jnp.sum(p_c, axis=-1, keepdims=True) + jnp.sum(p_n, axis=-1, keepdims=True)
                     + jnp.exp(sink - m))
            probs.append((p_c, p_n, denom))
        for u, i in enumerate(seqs):
            kt, vt, kn, vn = caches[u]
            p_c, p_n, denom = probs[u]
            o = (_dot_nt(p_c, vt) + _dot(p_n, vn)) / denom
            n = ntok
            left = lambda h: o[h * n:(h + 1) * n] * m_left
            right = lambda h: o[h * n:(h + 1) * n] * m_right
            o_ref[rows[u], 0:LANES] = left(0) + shift(left(1))
            o_ref[rows[u], LANES:2 * LANES] = left(2) + shift(left(3))
            o_ref[rows[u], 2 * LANES:3 * LANES] = shift(right(4)) + right(5)
            o_ref[rows[u], 3 * LANES:4 * LANES] = shift(right(6)) + right(7)
            kout_ref[i] = jnp.where(in_tail, kn.T, pltpu.roll(kt, tail, 1))
            vout_ref[i] = jnp.where(in_tail, vn.T, pltpu.roll(vt, tail, 1))
        return carry

    lax.fori_loop(0, nseq // unroll, body, 0)


def _sample_ret_in_kernel(x_ref, cr_ref, sr_ref, nmix_ref, wint_ref, qkv_ref):
    hb = (_rms(x_ref[...]) * nmix_ref[...]).astype(BF16)
    qkv = lax.dot_general(wint_ref[...], hb, (((1,), (1,)), ((), ())), preferred_element_type=F32)
    n = qkv.shape[1]
    even_row = (lax.broadcasted_iota(jnp.int32, (RET_W, n), 0) % 2) == 0
    cr8 = jnp.concatenate([cr_ref[...]] * RET_HEADS, axis=0)
    sr8 = jnp.concatenate([sr_ref[...]] * RET_HEADS, axis=0)
    qkv_ref[0:RET_W] = _rot_pairs_t(qkv[0:RET_W], cr8, sr8, even_row)
    qkv_ref[RET_W:2 * RET_W] = _rot_pairs_t(qkv[RET_W:2 * RET_W], cr8, sr8, even_row) * (RET_DIM ** -0.5)
    qkv_ref[2 * RET_W:] = qkv[2 * RET_W:]


def _sample_ret_kernel(q_ref, k_ref, v_ref, s_ref, c_ref, o_ref, so_ref, ks_ref, *, ntok):
    hd, nseq = RET_DIM, LANES
    tok = lambda t: slice(t * nseq, (t + 1) * nseq)
    row_qd, row_kd, row_cd = ntok * ntok, ntok * ntok + ntok, ntok * ntok + 2 * ntok
    group = 8
    pair_outs = []
    for u in range(2):
        feats = slice(u * hd, (u + 1) * hd)
        const = lambda r: c_ref[u, r:r + 1, :]
        for t in range(ntok):
            ks_ref[t] = k_ref[feats, tok(t)] * const(row_kd + t)
        outs = []
        for t in range(ntok):
            q_t = q_ref[feats, tok(t)]
            acc = jnp.zeros((hd, nseq), F32)
            for k in range(t + 1):
                w_tk = jnp.sum(q_t * k_ref[feats, tok(k)], axis=0, keepdims=True) * const(t * ntok + k)
                acc = acc + w_tk * v_ref[feats, tok(k)]

            def cross(j, carry, t=t):
                d0 = pl.multiple_of(j * group, group)
                q_rows = q_ref[pl.ds(u * hd + d0, group), tok(t)]
                for i in range(group):
                    carry = carry + q_rows[i:i + 1, :] * s_ref[u, d0 + i]
                return carry
            carried = lax.fori_loop(0, hd // group, cross, jnp.zeros((hd, nseq), F32))
            outs.append(acc + carried * const(row_qd + t))

        def update(j, carry):
            d0 = pl.multiple_of(j * group, group)
            k_rows = [ks_ref[t, pl.ds(d0, group), :] for t in range(ntok)]
            for i in range(group):
                new = s_ref[u, d0 + i] * const(row_cd)
                for t in range(ntok):
                    new = new + k_rows[t][i:i + 1, :] * v_ref[feats, tok(t)]
                so_ref[u, d0 + i] = new
            return carry
        lax.fori_loop(0, hd // group, update, 0)
        pair_outs.append(outs)
    for t in range(ntok):
        o_ref[t] = jnp.concatenate([pair_outs[0][t], pair_outs[1][t]], axis=0).T


def _sample_out_kernel(x_ref, oa_ref, or_ref, feat_ref, bd_ref, wout_ref, y_ref):
    outs = [oa_ref[...]]
    for p in range(N_RET_SLABS):
        gate = feat_ref[:, C_GATE_S + p * LANES:C_GATE_S + (p + 1) * LANES]
        o_ret = or_ref[:, p * LANES:(p + 1) * LANES]
        outs.append(_head_norm(o_ret, bd_ref[...]) * (gate * jax.nn.sigmoid(gate)))
    mix = jnp.concatenate(outs, axis=1).astype(BF16)
    y_ref[...] = x_ref[...] + jnp.dot(mix, wout_ref[...], preferred_element_type=F32)


def _mem_kv_kernel(m_ref, nmem_ref, wkv_ref, kg_ref, k_ref, v_ref):
    hb = (_rms(m_ref[...]) * nmem_ref[...]).astype(BF16)
    kv = jnp.dot(hb, wkv_ref[...], preferred_element_type=F32)
    slots = m_ref.shape[0]
    for h in range(MEM_HEADS):
        rows = pl.ds(h, slots, stride=MEM_HEADS)
        k_ref[rows, :] = _rms(kv[:, h * LANES:(h + 1) * LANES]) * kg_ref[...]
        v_ref[rows, :] = kv[:, MEM_W + h * LANES:MEM_W + (h + 1) * LANES]


def _mem_queries(x, ncross, wq_ref, qg):
    hb = (_rms(x) * ncross).astype(BF16)
    q = jnp.dot(hb, wq_ref[...], preferred_element_type=F32)
    qs = qg * (MEM_HEAD_DIM ** -0.5 * LOG2E)
    return [_rms(q[:, h * LANES:(h + 1) * LANES]) * qs for h in range(MEM_HEADS)]


def _prompt_cross_ffn_kernel(x_ref, mk_ref, mv_ref, ncross_ref, wq_ref, qg_ref, wo_ref, nffn_ref, wgu_ref,
                             wdown_ref, y_ref, xa_ref):
    @pl.when(pl.program_id(0) == 0)
    def _():
        xa_ref[...] = jnp.zeros_like(xa_ref)

    xa = xa_ref[...]
    hb = (_rms(xa) * nffn_ref[...]).astype(BF16)
    x = x_ref[...]
    qn = _mem_queries(x, ncross_ref[...], wq_ref, qg_ref[...])
    g = jnp.dot(hb, wgu_ref[:, 0:FFN_HIDDEN], preferred_element_type=F32)
    head = lambda ref, h: ref[0, pl.ds(h, MEM_LEN, stride=MEM_HEADS), :].astype(BF16)
    scores = [_dot_nt(head(mk_ref, h), qn[h]) for h in range(MEM_HEADS)]
    u = jnp.dot(hb, wgu_ref[:, FFN_HIDDEN:], preferred_element_type=F32)
    probs, denoms = [], []
    for s in scores:
        p = jnp.exp2(s - jnp.max(s, axis=0, keepdims=True))
        denoms.append(jnp.sum(p, axis=0, keepdims=True))
        probs.append(p.astype(BF16))
    act = (g * jax.nn.sigmoid(g) * u).astype(BF16)
    o_t = jnp.concatenate([_dot_tn(head(mv_ref, h), probs[h]) / denoms[h] for h in range(MEM_HEADS)], axis=0)
    y_ref[...] = xa + jnp.dot(act, wdown_ref[...], preferred_element_type=F32)
    xa_ref[...] = x + lax.dot_general(o_t.astype(BF16), wo_ref[...], (((0,), (0,)), ((), ())),
                                      preferred_element_type=F32)


def _sample_cross_attn_kernel(x_ref, mk_ref, mv_ref, ncross_ref, wq_ref, qg_ref, o_ref, *, nseq, tq):
    qn = _mem_queries(x_ref[...], ncross_ref[...], wq_ref, qg_ref[...])
    nrow = MEM_HEADS * tq
    nmem = MEM_LEN * MEM_HEADS
    own = (lax.broadcasted_iota(jnp.int32, (nrow, nmem), 0) // tq
           == lax.broadcasted_iota(jnp.int32, (nrow, nmem), 1) % MEM_HEADS)
    scores = []
    for i in range(nseq):
        q_stack = jnp.concatenate([qn[h][i * tq:(i + 1) * tq] for h in range(MEM_HEADS)], axis=0)
        scores.append(_dot_nt(q_stack, mk_ref[i]))
    probs, denoms = [], []
    for s in scores:
        s = jnp.where(own, s, NEG_INF)
        p = jnp.exp2(s - jnp.max(s, axis=-1, keepdims=True))
        denoms.append(jnp.sum(p, axis=-1, keepdims=True))
        probs.append(p.astype(BF16))
    for i in range(nseq):
        o = jnp.dot(probs[i], mv_ref[i].astype(BF16), preferred_element_type=F32) / denoms[i]
        for h in range(MEM_HEADS):
            o_ref[i * tq:(i + 1) * tq, h * LANES:(h + 1) * LANES] = o[h * tq:(h + 1) * tq]


def _cross_out_ffn_kernel(x_ref, o_ref, wo_ref, nffn_ref, wgu_ref, wdown_ref, y_ref):
    x = x_ref[...] + jnp.dot(o_ref[...].astype(BF16), wo_ref[...], preferred_element_type=F32)
    hb = (_rms(x) * nffn_ref[...]).astype(BF16)
    g = jnp.dot(hb, wgu_ref[:, 0:FFN_HIDDEN], preferred_element_type=F32)
    u = jnp.dot(hb, wgu_ref[:, FFN_HIDDEN:], preferred_element_type=F32)
    act = (g * jax.nn.sigmoid(g) * u).astype(BF16)
    y_ref[...] = x + jnp.dot(act, wdown_ref[...], preferred_element_type=F32)


def _rope_tables(pos):
    half = HEAD_DIM // 2
    inv = 1.0 / (ROPE_THETA ** (jnp.arange(half, dtype=F32) / half))
    ang = pos.astype(F32)[:, None] * inv[None, :]
    cos, sin = jnp.cos(ang), jnp.sin(ang)
    c64 = jnp.concatenate([cos, cos], axis=-1)
    s64 = jnp.concatenate([-sin, sin], axis=-1)
    return jnp.tile(c64, (1, 2)), jnp.tile(s64, (1, 2))


def _sample_decay_rows(c):
    log_g = jnp.log(1.0 - jnp.exp2(-5.0 - jnp.arange(RET_HEADS, dtype=F32)))
    idx = jnp.arange(c, dtype=F32)
    diff = idx[:, None] - idx[None, :]
    dmat = jnp.where(diff >= 0, jnp.exp(jnp.maximum(diff, 0.0)[None] * log_g[:, None, None]), 0.0)
    qd = jnp.exp((idx + 1.0)[None, :] * log_g[:, None])
    kd = jnp.exp((c - 1.0 - idx)[None, :] * log_g[:, None])
    cd = jnp.exp(c * log_g)[:, None]
    rows = jnp.concatenate([dmat.reshape(RET_HEADS, c * c), qd, kd, cd], axis=1)
    rows = jnp.pad(rows, ((0, 0), (0, -rows.shape[1] % 8)))
    return jnp.broadcast_to(rows[:, :, None], rows.shape + (LANES,))


def _full(shape):
    nd = len(shape)
    return pl.BlockSpec(shape, lambda *_: (0,) * nd)


def _params(sem):
    return pltpu.CompilerParams(dimension_semantics=sem, vmem_limit_bytes=VMEM_LIMIT)


def _prompt_tables_t(pos):
    half = HEAD_DIM // 2
    inv = 1.0 / (ROPE_THETA ** (jnp.arange(half, dtype=F32) / half))
    ang = inv[:, None] * pos.astype(F32)[None, :]
    inv_r = RET_THETA ** (-jnp.linspace(0.0, 1.0, RET_DIM // 2, dtype=F32))
    ang_r = inv_r[:, None] * pos.astype(F32)[None, :]
    cos_r, sin_r = jnp.cos(ang_r), jnp.sin(ang_r)
    cr = jnp.repeat(cos_r, 2, axis=0)
    sr = jnp.stack([-sin_r, sin_r], axis=1).reshape(RET_DIM, pos.shape[0])
    return jnp.cos(ang), jnp.sin(ang), cr, sr


def _decay_consts_t(c):
    log_g = jnp.log(1.0 - jnp.exp2(-5.0 - jnp.arange(RET_HEADS, dtype=F32)))
    idx = jnp.arange(c, dtype=F32)
    diff = idx[:, None] - idx[None, :]
    dmat = jnp.where(diff >= 0, jnp.exp(jnp.maximum(diff, 0.0)[None] * log_g[:, None, None]), 0.0)
    dm = dmat.transpose(0, 2, 1).reshape(N_RET_SLABS, 2, c, c).transpose(0, 2, 1, 3).reshape(N_RET_SLABS, c, 2 * c)
    qd = jnp.exp((idx + 1.0)[None, :] * log_g[:, None])[:, None, :]
    kd = jnp.exp((c - 1.0 - idx)[None, :] * log_g[:, None])[:, None, :]
    cd = jnp.broadcast_to(jnp.exp(c * log_g)[:, None, None], (RET_HEADS, 1, RET_DIM))
    return dm, qd, kd, cd


def _prompt_mixer(x, w, tile, sub):
    b, l, d = x.shape
    ca, sa, cr, sr = _prompt_tables_t(jnp.arange(l, dtype=jnp.int32))
    dm, qd, kd, cd = _decay_consts_t(RET_CHUNK)
    sink = jnp.repeat(w['sinks'].reshape(SWA_KV_HEADS, 1, -1), WINDOW, axis=-1)
    tab = lambda rows: pl.BlockSpec((rows, tile), lambda i, t: (0, t))
    xspec = pl.BlockSpec((1, tile, d), lambda i, t: (i, t, 0))
    win_spec = pl.BlockSpec((1, WINDOW, LANES), lambda i, t: (i, 0, 0))
    st_spec = pl.BlockSpec((1, RET_HEADS, RET_DIM, RET_DIM), lambda i, t: (i, 0, 0, 0))
    return pl.pallas_call(
        functools.partial(_prompt_mixer_kernel, tile=tile, sub=sub),
        grid=(b, l // tile),
        in_specs=[xspec, tab(HEAD_DIM // 2), tab(HEAD_DIM // 2), tab(RET_DIM), tab(RET_DIM), _full((1, d)),
                  _full((IN_COLS, d)), _full((HEAD_DIM, 1)), _full((HEAD_DIM, 1)), _full(sink.shape),
                  _full((d, d)), _full(dm.shape), _full(qd.shape), _full(kd.shape), _full(cd.shape)],
        out_specs=[xspec, win_spec, win_spec, st_spec],
        out_shape=[jax.ShapeDtypeStruct(x.shape, F32),
                   jax.ShapeDtypeStruct((b, WINDOW, LANES), F32),
                   jax.ShapeDtypeStruct((b, WINDOW, LANES), F32),
                   jax.ShapeDtypeStruct((b, RET_HEADS, RET_DIM, RET_DIM), F32)],
        scratch_shapes=[pltpu.VMEM((SWA_KV_W, WINDOW), F32), pltpu.VMEM((SWA_KV_W, WINDOW), F32),
                        pltpu.VMEM((RET_HEADS, RET_DIM, RET_DIM), F32)],
        compiler_params=_params(("arbitrary", "arbitrary")),
        name="prompt_mixer",
    )(x, ca, sa, cr, sr, w['norm_mix'], w['w_in_t'], w['qg_col'], w['kg_col'], sink, w['w_out'], dm, qd, kd,
      cd)


def _sample_mixer(x, cache_k, cache_v, state, w, tile, nseq):
    b, ntok, d = x.shape
    rows = b * ntok
    xf = x.reshape(rows, d)
    pos = jnp.tile(PAST_LEN + jnp.arange(ntok, dtype=jnp.int32), b)
    ca, sa = _rope_tables(pos)
    tab = pl.BlockSpec((tile, LANES), lambda i: (i, 0))
    row_spec = lambda width: pl.BlockSpec((tile, width), lambda i: (i, 0))
    feat = pl.pallas_call(
        _sample_in_kernel,
        grid=(rows // tile,),
        in_specs=[row_spec(d), tab, tab, _full((1, d)), _full((IN_COLS, d)), _full((1, LANES)),
                  _full((1, LANES)), _full((LANES, LANES))],
        out_specs=row_spec(SAMPLE_FEAT),
        out_shape=jax.ShapeDtypeStruct((rows, SAMPLE_FEAT), F32),
        compiler_params=_params(("arbitrary",)),
        name="sample_in",
    )(xf, ca, sa, w['norm_mix'], w['w_in_t'], w['qg'], w['kg'], w['bd'])

    sink = jnp.repeat(w['sinks'], ntok)[:, None]
    seq_rows = nseq * ntok
    unroll = 4
    cache_spec = pl.BlockSpec((nseq, LANES, WINDOW), lambda i: (i, 0, 0))
    o_a, k_out, v_out = pl.pallas_call(
        functools.partial(_sample_swa_kernel, nseq=nseq, ntok=ntok, unroll=unroll),
        grid=(b // nseq,),
        in_specs=[pl.BlockSpec((seq_rows, SAMPLE_FEAT), lambda i: (i, 0)), cache_spec, cache_spec,
                  _full(sink.shape)],
        out_specs=[pl.BlockSpec((seq_rows, SWA_Q_W), lambda i: (i, 0)), cache_spec, cache_spec],
        out_shape=[jax.ShapeDtypeStruct((rows, SWA_Q_W), F32),
                   jax.ShapeDtypeStruct(cache_k.shape, F32),
                   jax.ShapeDtypeStruct(cache_v.shape, F32)],
        scratch_shapes=[pltpu.VMEM((unroll, WINDOW, LANES), F32), pltpu.VMEM((unroll, WINDOW, LANES), F32)],
        compiler_params=_params(("arbitrary",)),
        name="sample_swa",
    )(feat, cache_k, cache_v, sink)

    assert b == LANES, "the retention step puts one sequence per lane"
    xt = jnp.transpose(x, (1, 0, 2)).reshape(rows, d)
    pos_t = jnp.repeat(PAST_LEN + jnp.arange(ntok, dtype=jnp.int32), b)
    _, _, cr_t, sr_t = _prompt_tables_t(pos_t)
    half = rows // 2
    tab_t = pl.BlockSpec((RET_DIM, half), lambda i: (0, i))
    qkv = pl.pallas_call(
        _sample_ret_in_kernel,
        grid=(2,),
        in_specs=[pl.BlockSpec((half, d), lambda i: (i, 0)), tab_t, tab_t, _full((1, d)),
                  _full((3 * RET_W, d))],
        out_specs=pl.BlockSpec((3 * RET_W, half), lambda i: (0, i)),
        out_shape=jax.ShapeDtypeStruct((3 * RET_W, rows), F32),
        compiler_params=_params(("arbitrary",)),
        name="sample_ret_in",
    )(xt, cr_t, sr_t, w['norm_mix'], w['w_in_t'][C_QR:C_G])
    consts = _sample_decay_rows(ntok)
    pair_rows = lambda off: pl.BlockSpec((LANES, rows), lambda p: (off + p, 0))
    st_spec = pl.BlockSpec((2, RET_DIM, RET_DIM, b), lambda p: (p, 0, 0, 0))
    o_r, s_out = pl.pallas_call(
        functools.partial(_sample_ret_kernel, ntok=ntok),
        grid=(N_RET_SLABS,),
        in_specs=[pair_rows(0), pair_rows(N_RET_SLABS), pair_rows(2 * N_RET_SLABS), st_spec,
                  pl.BlockSpec((2,) + consts.shape[1:], lambda p: (p, 0, 0))],
        out_specs=[pl.BlockSpec((ntok, b, LANES), lambda p: (0, 0, p)), st_spec],
        out_shape=[jax.ShapeDtypeStruct((ntok, b, RET_W), F32), jax.ShapeDtypeStruct(state.shape, F32)],
        scratch_shapes=[pltpu.VMEM((ntok, RET_DIM, b), F32)],
        compiler_params=_params(("arbitrary",)),
        name="sample_ret",
    )(qkv, qkv, qkv, state, consts)
    o_r = jnp.transpose(o_r, (1, 0, 2)).reshape(rows, RET_W)

    y = pl.pallas_call(
        _sample_out_kernel,
        grid=(rows // tile,),
        in_specs=[row_spec(d), row_spec(SWA_Q_W), row_spec(RET_W), row_spec(SAMPLE_FEAT), _full((LANES, LANES)),
                  _full((d, d))],
        out_specs=row_spec(d),
        out_shape=jax.ShapeDtypeStruct((rows, d), F32),
        compiler_params=_params(("arbitrary",)),
        name="sample_out",
    )(xf, o_a, o_r, feat, w['bd'], w['w_out'])
    return y, k_out, v_out, s_out


def _mem_kv(mem, w, tile):
    rows, d = mem.shape
    row_spec = lambda width: pl.BlockSpec((tile, width), lambda i: (i, 0))
    return pl.pallas_call(
        _mem_kv_kernel,
        grid=(rows // tile,),
        in_specs=[row_spec(d), _full((1, d)), _full((d, 2 * MEM_W)), _full((1, LANES))],
        out_specs=[pl.BlockSpec((tile * MEM_HEADS, MEM_HEAD_DIM), lambda i: (i, 0))] * 2,
        out_shape=[jax.ShapeDtypeStruct((rows * MEM_HEADS, MEM_HEAD_DIM), F32)] * 2,
        compiler_params=_params(("arbitrary",)),
        name="mem_kv",
    )(mem, w['norm_mem'], w['w_mkv'], w['kgm'])


def _prompt_cross_ffn(x, mk, mv, w, tile):
    rows, d = x.shape
    n = rows // tile
    per_mem = n // mk.shape[0]
    cur = lambda i: jnp.minimum(i, n - 1)
    single = lambda shape: pl.BlockSpec(shape, lambda i: (0,) * len(shape), pipeline_mode=pl.Buffered(1))
    mem_spec = pl.BlockSpec((1, MEM_LEN * MEM_HEADS, MEM_HEAD_DIM), lambda i: (cur(i) // per_mem, 0, 0))
    return pl.pallas_call(
        _prompt_cross_ffn_kernel,
        grid=(n + 1,),
        in_specs=[pl.BlockSpec((tile, d), lambda i: (cur(i), 0)), mem_spec, mem_spec, _full((1, d)),
                  single((d, MEM_W)), _full((1, LANES)), single((MEM_W, d)), _full((1, d)),
                  single((d, 2 * FFN_HIDDEN)), single((FFN_HIDDEN, d))],
        out_specs=pl.BlockSpec((tile, d), lambda i: (jnp.maximum(i - 1, 0), 0)),
        out_shape=jax.ShapeDtypeStruct((rows, d), F32),
        scratch_shapes=[pltpu.VMEM((tile, d), F32)],
        compiler_params=_params(("arbitrary",)),
        name="prompt_cross_ffn",
    )(x, mk, mv, w['norm_cross'], w['w_mq'], w['qgm'], w['w_mo'], w['norm_ffn'], w['w_gu'], w['w_down'])


def _sample_cross_attn(x, mk, mv, w, nseq, tq):
    rows, d = x.shape
    blk = nseq * tq
    mem_spec = pl.BlockSpec((nseq, MEM_LEN * MEM_HEADS, MEM_HEAD_DIM), lambda i: (i, 0, 0))
    return pl.pallas_call(
        functools.partial(_sample_cross_attn_kernel, nseq=nseq, tq=tq),
        grid=(rows // blk,),
        in_specs=[pl.BlockSpec((blk, d), lambda i: (i, 0)), mem_spec, mem_spec, _full((1, d)),
                  _full((d, MEM_W)), _full((1, LANES))],
        out_specs=pl.BlockSpec((blk, MEM_W), lambda i: (i, 0)),
        out_shape=jax.ShapeDtypeStruct((rows, MEM_W), F32),
        compiler_params=_params(("arbitrary",)),
        name="sample_cross_attn",
    )(x, mk, mv, w['norm_cross'], w['w_mq'], w['qgm'])


def _sample_cross_out_ffn(x, o, w, tile):
    rows, d = x.shape
    row_spec = lambda width: pl.BlockSpec((tile, width), lambda i: (i, 0))
    single = lambda shape: pl.BlockSpec(shape, lambda i: (0,) * len(shape), pipeline_mode=pl.Buffered(1))
    return pl.pallas_call(
        _cross_out_ffn_kernel,
        grid=(rows // tile,),
        in_specs=[row_spec(d), row_spec(MEM_W), single((MEM_W, d)), _full((1, d)),
                  single((d, 2 * FFN_HIDDEN)), single((FFN_HIDDEN, d))],
        out_specs=row_spec(d),
        out_shape=jax.ShapeDtypeStruct((rows, d), F32),
        compiler_params=_params(("arbitrary",)),
        name="sample_ffn",
    )(x, o, w['w_mo'], w['norm_ffn'], w['w_gu'], w['w_down'])


def kernel(x_prompt, x_sample, mem_prompt, cache_swa_k, cache_swa_v, state_ret, cache_mem_k, cache_mem_v,
           norm_mix, w_in, q_norm_a, k_norm_a, sinks, w_out, norm_cross, norm_mem, w_mq, w_mkv,
           q_norm_m, k_norm_m, w_mo, norm_ffn, w_gu, w_down):
    assert norm_mix.shape[0] == 1, "single-layer kernel"
    b, l, d = x_prompt.shape
    sb, st, _ = x_sample.shape
    half = (jnp.arange(LANES) // HEAD_DIM)[:, None] == (jnp.arange(LANES) // HEAD_DIM)[None, :]
    w = {
        'norm_mix': norm_mix, 'norm_cross': norm_cross, 'norm_mem': norm_mem, 'norm_ffn': norm_ffn,
        'w_out': w_out[0].astype(BF16), 'w_mq': w_mq[0].astype(BF16),
        'w_mkv': w_mkv[0].astype(BF16), 'w_mo': w_mo[0].astype(BF16), 'w_gu': w_gu[0].astype(BF16),
        'w_down': w_down[0].astype(BF16),
        'w_in_t': w_in[0].T.astype(BF16),
        'qg': jnp.tile(q_norm_a, (1, 2)), 'kg': jnp.tile(k_norm_a, (1, 2)),
        'qg_col': q_norm_a.reshape(HEAD_DIM, 1), 'kg_col': k_norm_a.reshape(HEAD_DIM, 1),
        'qgm': q_norm_m, 'kgm': k_norm_m, 'sinks': sinks[0],
        'bd': half.astype(BF16),
    }

    mk, mv = _mem_kv(mem_prompt.reshape(b * MEM_LEN, d), w, tile=256)
    xp, kwin, vwin, ret_p = _prompt_mixer(x_prompt, w, tile=1024, sub=512)
    xp = xp.reshape(b * l, d)
    mem_rows_p = (b, MEM_LEN * MEM_HEADS, MEM_HEAD_DIM)
    yp = _prompt_cross_ffn(xp, mk.reshape(mem_rows_p), mv.reshape(mem_rows_p), w, tile=512)
    yp = yp.reshape(b, l, d)
    ret_p = ret_p.reshape(1, b, RET_HEADS, RET_DIM, RET_DIM)

    nbuf = cache_swa_k.shape[2]
    assert nbuf == WINDOW
    to_feature_major = lambda c: jnp.transpose(c[0], (0, 2, 3, 1)).reshape(sb, SWA_KV_W, nbuf)
    from_feature_major = lambda c: jnp.transpose(c.reshape(sb, SWA_KV_HEADS, HEAD_DIM, nbuf), (0, 3, 1, 2))[None]
    xs, k_s, v_s, s_s = _sample_mixer(x_sample, to_feature_major(cache_swa_k), to_feature_major(cache_swa_v),
                                      jnp.transpose(state_ret[0], (1, 2, 3, 0)), w, tile=256, nseq=8)
    mem_rows = (sb, MEM_LEN * MEM_HEADS, MEM_HEAD_DIM)
    os_ = _sample_cross_attn(xs, cache_mem_k.reshape(mem_rows), cache_mem_v.reshape(mem_rows), w, nseq=8, tq=st)
    ys = _sample_cross_out_ffn(xs, os_, w, tile=512).reshape(sb, st, d)

    kv_shape = (1, b, WINDOW, SWA_KV_HEADS, HEAD_DIM)
    mem_shape = (1, b, MEM_LEN, MEM_HEADS, MEM_HEAD_DIM)
    return (yp, ys, kwin.reshape(kv_shape), vwin.reshape(kv_shape), ret_p,
            mk.reshape(mem_shape), mv.reshape(mem_shape),
            from_feature_major(k_s), from_feature_major(v_s),
            jnp.transpose(s_s, (3, 0, 1, 2))[None])
```

```python
import functools

import jax
import jax.numpy as jnp
import numpy as np
from jax import lax
from jax.experimental import pallas as pl
from jax.experimental.pallas import tpu as pltpu

F32 = jnp.float32
BF16 = jnp.bfloat16

LANES = 128
D_MODEL = 1024
HEAD_DIM = 64
SWA_HEADS = 8
SWA_KV_HEADS = 2
WINDOW = 128
RET_HEADS = 8
RET_DIM = 64
RET_CHUNK = 128
RET_THETA = 10000.0
ROPE_THETA = 10000.0
MEM_LEN = 256
MEM_HEADS = 4
MEM_HEAD_DIM = 128
MEM_W = MEM_HEADS * MEM_HEAD_DIM
FFN_HIDDEN = 2816
RMS_EPS = 1e-6
NEG_INF = -1e30
LOG2E = 1.4426950408889634
PAST_LEN = 16384

SWA_Q_W = SWA_HEADS * HEAD_DIM
SWA_KV_W = SWA_KV_HEADS * HEAD_DIM
RET_W = RET_HEADS * RET_DIM
IN_COLS = SWA_Q_W + 2 * SWA_KV_W + 4 * RET_W
C_QA, C_KA, C_VA = 0, SWA_Q_W, SWA_Q_W + SWA_KV_W
C_QR = SWA_Q_W + 2 * SWA_KV_W
C_KR, C_VR, C_G = C_QR + RET_W, C_QR + 2 * RET_W, C_QR + 3 * RET_W
C_GATE_S = C_QR
SAMPLE_FEAT = C_QR + RET_W
N_QA_SLABS = SWA_Q_W // LANES
N_RET_SLABS = RET_W // LANES

VMEM_LIMIT = 56 * 1024 * 1024


def _dot(a, b):
    return jnp.dot(a.astype(BF16), b.astype(BF16), preferred_element_type=F32)


def _dot_nt(a, b):
    return lax.dot_general(a.astype(BF16), b.astype(BF16), (((1,), (1,)), ((), ())),
                           preferred_element_type=F32)


def _rms(x):
    return x * lax.rsqrt(jnp.mean(x * x, axis=-1, keepdims=True) + RMS_EPS)


def _lane_consts():
    lane = lax.broadcasted_iota(jnp.int32, (1, LANES), 1)
    m_left = (lane < HEAD_DIM).astype(F32)
    m_right = 1.0 - m_left
    first_half = (lane % HEAD_DIM) < (HEAD_DIM // 2)
    even = (lane % 2) == 0
    return m_left, m_right, first_half, even


def _head_norm(y, bd):
    ss = jnp.dot((y * y).astype(BF16), bd, preferred_element_type=F32) * (1.0 / HEAD_DIM)
    return y * lax.rsqrt(ss + RMS_EPS)


def _rot_half(y, cos, sin_signed, first_half):
    swapped = jnp.where(first_half, pltpu.roll(y, LANES - HEAD_DIM // 2, 1), pltpu.roll(y, HEAD_DIM // 2, 1))
    return y * cos + swapped * sin_signed


def _dot_tn(a, b):
    return lax.dot_general(a.astype(BF16), b.astype(BF16), (((0,), (0,)), ((), ())),
                           preferred_element_type=F32)


def _head_norm_t(y):
    return y * lax.rsqrt(jnp.mean(y * y, axis=0, keepdims=True) + RMS_EPS)


def _rot_half_t(y, cos, sin):
    half = HEAD_DIM // 2
    y1, y2 = y[0:half], y[half:]
    return jnp.concatenate([y1 * cos - y2 * sin, y2 * cos + y1 * sin], axis=0)


def _rot_pairs_t(y, cos, sin_signed, even_row):
    n = y.shape[0]
    swapped = jnp.where(even_row, pltpu.roll(y, n - 1, 0), pltpu.roll(y, 1, 0))
    return y * cos + swapped * sin_signed


def _prompt_mixer_kernel(x_ref, ca_ref, sa_ref, cr_ref, sr_ref, nmix_ref, wint_ref, qg_ref, kg_ref,
                         sink_ref, wout_ref, dm_ref, qd_ref, kd_ref, cd_ref,
                         y_ref, kwin_ref, vwin_ref, sout_ref,
                         pk_ref, pv_ref, s_ref, *, tile, sub):
    t = pl.program_id(1)
    nblk = sub // WINDOW
    hd = HEAD_DIM
    group = SWA_HEADS // SWA_KV_HEADS

    @pl.when(t == 0)
    def _():
        pk_ref[...] = jnp.zeros_like(pk_ref)
        pv_ref[...] = jnp.zeros_like(pv_ref)
        s_ref[...] = jnp.zeros_like(s_ref)

    qg, kg = qg_ref[...], kg_ref[...]
    qs = qg * (hd ** -0.5 * LOG2E)
    head = lambda a, h: a[h * hd:(h + 1) * hd]
    chunk = lambda a, h, j: a[h * hd:(h + 1) * hd, j * RET_CHUNK:(j + 1) * RET_CHUNK]
    keys_of = lambda j: slice(j * WINDOW, (j + 2) * WINDOW)
    units = [(g, j) for j in range(nblk) for g in range(SWA_KV_HEADS)]
    key = lax.broadcasted_iota(jnp.int32, (2 * WINDOW, group * WINDOW), 0)
    qry = lax.broadcasted_iota(jnp.int32, (2 * WINDOW, group * WINDOW), 1) % WINDOW
    band = (key > qry) & (key <= qry + WINDOW)
    even_row = (lax.broadcasted_iota(jnp.int32, (RET_W, sub), 0) % 2) == 0
    zeros = jnp.zeros((hd, RET_CHUNK), F32)
    carry = {'pk': pk_ref[...], 'pv': pv_ref[...], 'state': [s_ref[h] for h in range(RET_HEADS)]}

    def stages(idx):
        toks = slice(idx * sub, (idx + 1) * sub)
        v = {}

        def project():
            v['x'] = x_ref[0, toks]
            hb = (_rms(v['x']) * nmix_ref[...]).astype(BF16)
            proj = lambda lo, hi: lax.dot_general(wint_ref[lo:hi, :], hb, (((1,), (1,)), ((), ())),
                                                  preferred_element_type=F32)
            v['swa'], v['ret'] = proj(C_QA, C_QR), proj(C_QR, C_VR)
            v['vr'], v['gate'] = proj(C_VR, C_G), proj(C_G, IN_COLS)

        def swa_prepare():
            ca, sa = ca_ref[:, toks], sa_ref[:, toks]
            swa_t = v['swa']
            v['qa'] = [_rot_half_t(_head_norm_t(head(swa_t, h)) * qs, ca, sa) for h in range(SWA_HEADS)]
            ka = jnp.concatenate([_rot_half_t(_head_norm_t(head(swa_t, SWA_HEADS + g)) * kg, ca, sa)
                                  for g in range(SWA_KV_HEADS)], axis=0)
            va = swa_t[C_VA:C_QR]
            v['kfull'] = jnp.concatenate([carry['pk'], ka], axis=1).astype(BF16)
            v['vfull'] = jnp.concatenate([carry['pv'], va], axis=1).astype(BF16)
            carry['pk'], carry['pv'] = ka[:, sub - WINDOW:], va[:, sub - WINDOW:]

        def swa_scores():
            v['scores'] = []
            for g, j in units:
                blk = slice(j * WINDOW, (j + 1) * WINDOW)
                q4 = jnp.concatenate([v['qa'][group * g + u][:, blk] for u in range(group)], axis=1)
                v['scores'].append(_dot_tn(head(v['kfull'], g)[:, keys_of(j)], q4))

        def ret_rotate():
            cr8 = jnp.concatenate([cr_ref[:, toks]] * RET_HEADS, axis=0)
            sr8 = jnp.concatenate([sr_ref[:, toks]] * RET_HEADS, axis=0)
            v['qr'] = _rot_pairs_t(v['ret'][0:RET_W], cr8, sr8, even_row)
            v['kr'] = _rot_pairs_t(v['ret'][RET_W:], cr8, sr8, even_row) * (RET_DIM ** -0.5)

        def ret_scores():
            qr, kr, vr = v['qr'], v['kr'], v['vr']
            v['inner'] = [[None] * nblk for _ in range(N_RET_SLABS)]
            for j in range(nblk):
                for pr in range(N_RET_SLABS):
                    q_bd = jnp.concatenate([jnp.concatenate([chunk(qr, 2 * pr, j), zeros], axis=1),
                                            jnp.concatenate([zeros, chunk(qr, 2 * pr + 1, j)], axis=1)], axis=0)
                    k2 = kr[pr * LANES:(pr + 1) * LANES, j * RET_CHUNK:(j + 1) * RET_CHUNK]
                    v['inner'][pr][j] = _dot_tn(k2, q_bd) * dm_ref[pr]
            v['incr'] = [[_dot_nt(chunk(vr, h, j), chunk(kr, h, j) * kd_ref[h]) for j in range(nblk)]
                         for h in range(RET_HEADS)]

        def swa_softmax():
            v['probs'], v['denoms'] = [], []
            for (g, j), s in zip(units, v['scores']):
                sink = sink_ref[g] * LOG2E
                valid = band & (key >= WINDOW * (1 - t)) if (idx == 0 and j == 0) else band
                s = jnp.where(valid, s, NEG_INF)
                m = jnp.maximum(jnp.max(s, axis=0, keepdims=True), sink)
                p = jnp.exp2(s - m)
                v['denoms'].append(jnp.sum(p, axis=0, keepdims=True) + jnp.exp2(sink - m))
                v['probs'].append(p.astype(BF16))

        def swa_values():
            v['oa'] = [[None] * nblk for _ in range(SWA_HEADS)]
            for (g, j), p, denom in zip(units, v['probs'], v['denoms']):
                o = jnp.dot(head(v['vfull'], g)[:, keys_of(j)], p, preferred_element_type=F32) / denom
                for u in range(group):
                    v['oa'][group * g + u][j] = o[:, u * WINDOW:(u + 1) * WINDOW]
            v['state'] = [[carry['state'][h]] for h in range(RET_HEADS)]
            for h in range(RET_HEADS):
                for j in range(nblk):
                    v['state'][h].append(v['state'][h][j] * cd_ref[h] + v['incr'][h][j])
                carry['state'][h] = v['state'][h][nblk]

        def ret_outputs():
            v['or'] = [[None] * nblk for _ in range(RET_HEADS)]
            for j in range(nblk):
                for h in range(RET_HEADS):
                    u = h % 2
                    lhs = jnp.concatenate([chunk(v['vr'], h, j), v['state'][h][j]], axis=1)
                    rhs = jnp.concatenate([v['inner'][h // 2][j][:, u * RET_CHUNK:(u + 1) * RET_CHUNK],
                                           chunk(v['qr'], h, j) * qd_ref[h]], axis=0)
                    v['or'][h][j] = _dot(lhs, rhs)

        def gate_mix():
            mix = [jnp.concatenate(blocks, axis=1) for blocks in v['oa']]
            for h in range(RET_HEADS):
                o_h = _head_norm_t(jnp.concatenate(v['or'][h], axis=1))
                g_h = head(v['gate'], h)
                mix.append(o_h * (g_h * jax.nn.sigmoid(g_h)))
            v['mix'] = jnp.concatenate(mix, axis=0).astype(BF16)

        def out_project():
            y_ref[0, toks] = v['x'] + lax.dot_general(v['mix'], wout_ref[...], (((0,), (0,)), ((), ())),
                                                      preferred_element_type=F32)

        return [project, swa_prepare, swa_scores, ret_rotate, ret_scores, swa_softmax, swa_values,
                ret_outputs, gate_mix, out_project]

    pipelines = [stages(i) for i in range(tile // sub)]
    nstage = len(pipelines[0])
    for slot in range(nstage + len(pipelines) - 1):
        for i, pipe in enumerate(pipelines):
            if 0 <= slot - i < nstage:
                pipe[slot - i]()

    pk_ref[...], pv_ref[...] = carry['pk'], carry['pv']
    kwin_ref[0], vwin_ref[0] = carry['pk'].T, carry['pv'].T
    for h in range(RET_HEADS):
        s_ref[h] = carry['state'][h]

    @pl.when(t == pl.num_programs(1) - 1)
    def _():
        for h in range(RET_HEADS):
            sout_ref[0, h] = carry['state'][h].T


def _sample_in_kernel(x_ref, ca_ref, sa_ref, nmix_ref, wint_ref, qg_ref, kg_ref, bd_ref, feat_ref):
    _, _, first_half, _ = _lane_consts()
    ca, sa, bd = ca_ref[...], sa_ref[...], bd_ref[...]
    hb = (_rms(x_ref[...]) * nmix_ref[...]).astype(BF16)
    proj = lambda lo, hi: lax.dot_general(hb, wint_ref[lo:hi, :], (((1,), (1,)), ((), ())),
                                          preferred_element_type=F32)
    swa = proj(C_QA, C_QR)
    slabs = [_rot_half(_head_norm(swa[:, s * LANES:(s + 1) * LANES], bd) * qg_ref[...], ca, sa, first_half)
             for s in range(N_QA_SLABS)]
    slabs.append(_rot_half(_head_norm(swa[:, C_KA:C_VA], bd) * kg_ref[...], ca, sa, first_half))
    feat_ref[...] = jnp.concatenate(slabs + [swa[:, C_VA:C_QR], proj(C_G, IN_COLS)], axis=1)


def _sample_swa_kernel(feat_ref, ckt_ref, cvt_ref, sink_ref, o_ref, kout_ref, vout_ref,
                       kn_ref, vn_ref, *, nseq, ntok, unroll):
    @pl.when(pl.program_id(0) == 0)
    def _():
        kn_ref[...] = jnp.zeros_like(kn_ref)
        vn_ref[...] = jnp.zeros_like(vn_ref)

    m_left, m_right, _, _ = _lane_consts()
    tail = WINDOW - ntok
    nrow = SWA_HEADS * ntok
    row_tok = lax.broadcasted_iota(jnp.int32, (nrow, WINDOW), 0) % ntok
    col = lax.broadcasted_iota(jnp.int32, (nrow, WINDOW), 1)
    valid_cache = col > row_tok
    valid_new = (col >= tail) & (col - tail <= row_tok)
    in_tail = lax.broadcasted_iota(jnp.int32, (LANES, WINDOW), 1) >= tail
    sink = sink_ref[...]
    shift = lambda a: pltpu.roll(a, HEAD_DIM, 1)

    def body(step, carry):
        seqs = [step * unroll + u for u in range(unroll)]
        rows = [pl.ds(pl.multiple_of(i * ntok, ntok), ntok) for i in seqs]
        q_rows, caches = [], []
        for u, i in enumerate(seqs):
            slab = lambda c, u=u: feat_ref[rows[u], c:c + LANES]
            kn_ref[u, tail:WINDOW] = slab(C_KA)
            vn_ref[u, tail:WINDOW] = slab(C_VA)
            s0, s1, s2, s3 = [slab(C_QA + s * LANES) for s in range(N_QA_SLABS)]
            q_rows.append(jnp.concatenate(
                [s0 * m_left, shift(s0 * m_right), s1 * m_left, shift(s1 * m_right),
                 shift(s2 * m_left), s2 * m_right, shift(s3 * m_left), s3 * m_right], axis=0))
            caches.append((ckt_ref[i], cvt_ref[i], kn_ref[u], vn_ref[u]))
        scores = [(_dot(q, kt), _dot_nt(q, kn)) for q, (kt, _, kn, _) in zip(q_rows, caches)]
        probs = []
        for s_c, s_n in scores:
            s_c = jnp.where(valid_cache, s_c * (HEAD_DIM ** -0.5), NEG_INF)
            s_n = jnp.where(valid_new, s_n * (HEAD_DIM ** -0.5), NEG_INF)
            m = jnp.maximum(jnp.maximum(jnp.max(s_c, axis=-1, keepdims=True),
                                        jnp.max(s_n, axis=-1, keepdims=True)), sink)
            p_c, p_n = jnp.exp(s_c - m), jnp.exp(s_n - m)
            denom = (jnp.sum(p_c, axis=-1, keepdims=True) + jnp.sum(p_n, axis=-1, keepdims=True)
                     + jnp.exp(sink - m))
            probs.append((p_c, p_n, denom))
        for u, i in enumerate(seqs):
            kt, vt, kn, vn = caches[u]
            p_c, p_n, denom = probs[u]
            o = (_dot_nt(p_c, vt) + _dot(p_n, vn)) / denom
            n = ntok
            left = lambda h: o[h * n:(h + 1) * n] * m_left
            right = lambda h: o[h * n:(h + 1) * n] * m_right
            o_ref[rows[u], 0:LANES] = left(0) + shift(left(1))
            o_ref[rows[u], LANES:2 * LANES] = left(2) + shift(left(3))
            o_ref[rows[u], 2 * LANES:3 * LANES] = shift(right(4)) + right(5)
            o_ref[rows[u], 3 * LANES:4 * LANES] = shift(right(6)) + right(7)
            kout_ref[i] = jnp.where(in_tail, kn.T, pltpu.roll(kt, tail, 1))
            vout_ref[i] = jnp.where(in_tail, vn.T, pltpu.roll(vt, tail, 1))
        return carry

    lax.fori_loop(0, nseq // unroll, body, 0)


def _sample_ret_in_kernel(x_ref, cr_ref, sr_ref, nmix_ref, wint_ref, qkv_ref):
    hb = (_rms(x_ref[...]) * nmix_ref[...]).astype(BF16)
    qkv = lax.dot_general(wint_ref[...], hb, (((1,), (1,)), ((), ())), preferred_element_type=F32)
    n = qkv.shape[1]
    even_row = (lax.broadcasted_iota(jnp.int32, (RET_W, n), 0) % 2) == 0
    cr8 = jnp.concatenate([cr_ref[...]] * RET_HEADS, axis=0)
    sr8 = jnp.concatenate([sr_ref[...]] * RET_HEADS, axis=0)
    qkv_ref[0:RET_W] = _rot_pairs_t(qkv[0:RET_W], cr8, sr8, even_row)
    qkv_ref[RET_W:2 * RET_W] = _rot_pairs_t(qkv[RET_W:2 * RET_W], cr8, sr8, even_row) * (RET_DIM ** -0.5)
    qkv_ref[2 * RET_W:] = qkv[2 * RET_W:]


def _sample_ret_kernel(q_ref, k_ref, v_ref, s_ref, c_ref, o_ref, so_ref, ks_ref, *, ntok):
    hd, nseq = RET_DIM, LANES
    tok = lambda t: slice(t * nseq, (t + 1) * nseq)
    row_qd, row_kd, row_cd = ntok * ntok, ntok * ntok + ntok, ntok * ntok + 2 * ntok
    group = 8
    pair_outs = []
    for u in range(2):
        feats = slice(u * hd, (u + 1) * hd)
        const = lambda r: c_ref[u, r:r + 1, :]
        for t in range(ntok):
            ks_ref[t] = k_ref[feats, tok(t)] * const(row_kd + t)
        outs = []
        for t in range(ntok):
            q_t = q_ref[feats, tok(t)]
            acc = jnp.zeros((hd, nseq), F32)
            for k in range(t + 1):
                w_tk = jnp.sum(q_t * k_ref[feats, tok(k)], axis=0, keepdims=True) * const(t * ntok + k)
                acc = acc + w_tk * v_ref[feats, tok(k)]

            def cross(j, carry, t=t):
                d0 = pl.multiple_of(j * group, group)
                q_rows = q_ref[pl.ds(u * hd + d0, group), tok(t)]
                for i in range(group):
                    carry = carry + q_rows[i:i + 1, :] * s_ref[u, d0 + i]
                return carry
            carried = lax.fori_loop(0, hd // group, cross, jnp.zeros((hd, nseq), F32))
            outs.append(acc + carried * const(row_qd + t))

        def update(j, carry):
            d0 = pl.multiple_of(j * group, group)
            k_rows = [ks_ref[t, pl.ds(d0, group), :] for t in range(ntok)]
            for i in range(group):
                new = s_ref[u, d0 + i] * const(row_cd)
                for t in range(ntok):
                    new = new + k_rows[t][i:i + 1, :] * v_ref[feats, tok(t)]
                so_ref[u, d0 + i] = new
            return carry
        lax.fori_loop(0, hd // group, update, 0)
        pair_outs.append(outs)
    for t in range(ntok):
        o_ref[t] = jnp.concatenate([pair_outs[0][t], pair_outs[1][t]], axis=0).T


def _sample_out_kernel(x_ref, oa_ref, or_ref, feat_ref, bd_ref, wout_ref, y_ref):
    outs = [oa_ref[...]]
    for p in range(N_RET_SLABS):
        gate = feat_ref[:, C_GATE_S + p * LANES:C_GATE_S + (p + 1) * LANES]
        o_ret = or_ref[:, p * LANES:(p + 1) * LANES]
        outs.append(_head_norm(o_ret, bd_ref[...]) * (gate * jax.nn.sigmoid(gate)))
    mix = jnp.concatenate(outs, axis=1).astype(BF16)
    y_ref[...] = x_ref[...] + jnp.dot(mix, wout_ref[...], preferred_element_type=F32)


def _mem_kv_kernel(m_ref, nmem_ref, wkv_ref, kg_ref, k_ref, v_ref):
    hb = (_rms(m_ref[...]) * nmem_ref[...]).astype(BF16)
    kv = jnp.dot(hb, wkv_ref[...], preferred_element_type=F32)
    slots = m_ref.shape[0]
    for h in range(MEM_HEADS):
        rows = pl.ds(h, slots, stride=MEM_HEADS)
        k_ref[rows, :] = _rms(kv[:, h * LANES:(h + 1) * LANES]) * kg_ref[...]
        v_ref[rows, :] = kv[:, MEM_W + h * LANES:MEM_W + (h + 1) * LANES]


def _mem_queries(x, ncross, wq_ref, qg):
    hb = (_rms(x) * ncross).astype(BF16)
    q = jnp.dot(hb, wq_ref[...], preferred_element_type=F32)
    qs = qg * (MEM_HEAD_DIM ** -0.5 * LOG2E)
    return [_rms(q[:, h * LANES:(h + 1) * LANES]) * qs for h in range(MEM_HEADS)]


def _prompt_cross_ffn_kernel(x_ref, mk_ref, mv_ref, ncross_ref, wq_ref, qg_ref, wo_ref, nffn_ref, wgu_ref,
                             wdown_ref, y_ref, xa_ref):
    @pl.when(pl.program_id(0) == 0)
    def _():
        xa_ref[...] = jnp.zeros_like(xa_ref)

    xa = xa_ref[...]
    hb = (_rms(xa) * nffn_ref[...]).astype(BF16)
    x = x_ref[...]
    qn = _mem_queries(x, ncross_ref[...], wq_ref, qg_ref[...])
    g = jnp.dot(hb, wgu_ref[:, 0:FFN_HIDDEN], preferred_element_type=F32)
    head = lambda ref, h: ref[0, pl.ds(h, MEM_LEN, stride=MEM_HEADS), :].astype(BF16)
    scores = [_dot_nt(head(mk_ref, h), qn[h]) for h in range(MEM_HEADS)]
    u = jnp.dot(hb, wgu_ref[:, FFN_HIDDEN:], preferred_element_type=F32)
    probs, denoms = [], []
    for s in scores:
        p = jnp.exp2(s - jnp.max(s, axis=0, keepdims=True))
        denoms.append(jnp.sum(p, axis=0, keepdims=True))
        probs.append(p.astype(BF16))
    act = (g * jax.nn.sigmoid(g) * u).astype(BF16)
    o_t = jnp.concatenate([_dot_tn(head(mv_ref, h), probs[h]) / denoms[h] for h in range(MEM_HEADS)], axis=0)
    y_ref[...] = xa + jnp.dot(act, wdown_ref[...], preferred_element_type=F32)
    xa_ref[...] = x + lax.dot_general(o_t.astype(BF16), wo_ref[...], (((0,), (0,)), ((), ())),
                                      preferred_element_type=F32)


def _sample_cross_attn_kernel(x_ref, mk_ref, mv_ref, ncross_ref, wq_ref, qg_ref, o_ref, *, nseq, tq):
    qn = _mem_queries(x_ref[...], ncross_ref[...], wq_ref, qg_ref[...])
    nrow = MEM_HEADS * tq
    nmem = MEM_LEN * MEM_HEADS
    own = (lax.broadcasted_iota(jnp.int32, (nrow, nmem), 0) // tq
           == lax.broadcasted_iota(jnp.int32, (nrow, nmem), 1) % MEM_HEADS)
    scores = []
    for i in range(nseq):
        q_stack = jnp.concatenate([qn[h][i * tq:(i + 1) * tq] for h in range(MEM_HEADS)], axis=0)
        scores.append(_dot_nt(q_stack, mk_ref[i]))
    probs, denoms = [], []
    for s in scores:
        s = jnp.where(own, s, NEG_INF)
        p = jnp.exp2(s - jnp.max(s, axis=-1, keepdims=True))
        denoms.append(jnp.sum(p, axis=-1, keepdims=True))
        probs.append(p.astype(BF16))
    for i in range(nseq):
        o = jnp.dot(probs[i], mv_ref[i].astype(BF16), preferred_element_type=F32) / denoms[i]
        for h in range(MEM_HEADS):
            o_ref[i * tq:(i + 1) * tq, h * LANES:(h + 1) * LANES] = o[h * tq:(h + 1) * tq]


def _cross_out_ffn_kernel(x_ref, o_ref, wo_ref, nffn_ref, wgu_ref, wdown_ref, y_ref):
    x = x_ref[...] + jnp.dot(o_ref[...].astype(BF16), wo_ref[...], preferred_element_type=F32)
    hb = (_rms(x) * nffn_ref[...]).astype(BF16)
    g = jnp.dot(hb, wgu_ref[:, 0:FFN_HIDDEN], preferred_element_type=F32)
    u = jnp.dot(hb, wgu_ref[:, FFN_HIDDEN:], preferred_element_type=F32)
    act = (g * jax.nn.sigmoid(g) * u).astype(BF16)
    y_ref[...] = x + jnp.dot(act, wdown_ref[...], preferred_element_type=F32)


def _f32(*arrays):
    return [np.ascontiguousarray(a, dtype=np.float32) for a in arrays]


def _rope_angles(pos):
    half = HEAD_DIM // 2
    inv = 1.0 / (ROPE_THETA ** (np.arange(half, dtype=np.float64) / half))
    return pos.astype(np.float64)[:, None] * inv[None, :]


def _ret_angles(pos):
    inv = RET_THETA ** (-np.linspace(0.0, 1.0, RET_DIM // 2, dtype=np.float64))
    return pos.astype(np.float64)[:, None] * inv[None, :]


def _rope_tables(pos):
    ang = _rope_angles(pos)
    cos, sin = np.cos(ang), np.sin(ang)
    c64 = np.concatenate([cos, cos], axis=-1)
    s64 = np.concatenate([-sin, sin], axis=-1)
    return _f32(np.tile(c64, (1, 2)), np.tile(s64, (1, 2)))


def _retention_decays(c):
    log_g = np.log(1.0 - np.exp2(-5.0 - np.arange(RET_HEADS, dtype=np.float64)))
    idx = np.arange(c, dtype=np.float64)
    diff = idx[:, None] - idx[None, :]
    dmat = np.where(diff >= 0, np.exp(np.maximum(diff, 0.0)[None] * log_g[:, None, None]), 0.0)
    qd = np.exp((idx + 1.0)[None, :] * log_g[:, None])
    kd = np.exp((c - 1.0 - idx)[None, :] * log_g[:, None])
    cd = np.exp(c * log_g)
    return dmat, qd, kd, cd


def _sample_decay_rows(c):
    dmat, qd, kd, cd = _retention_decays(c)
    rows = np.concatenate([dmat.reshape(RET_HEADS, c * c), qd, kd, cd[:, None]], axis=1)
    rows = np.pad(rows, ((0, 0), (0, -rows.shape[1] % 8)))
    return _f32(np.broadcast_to(rows[:, :, None], rows.shape + (LANES,)))[0]


def _full(shape):
    nd = len(shape)
    return pl.BlockSpec(shape, lambda *_: (0,) * nd)


def _params(sem):
    return pltpu.CompilerParams(dimension_semantics=sem, vmem_limit_bytes=VMEM_LIMIT)


def _prompt_tables_t(pos):
    ang = _rope_angles(pos).T
    ang_r = _ret_angles(pos).T
    cos_r, sin_r = np.cos(ang_r), np.sin(ang_r)
    cr = np.repeat(cos_r, 2, axis=0)
    sr = np.stack([-sin_r, sin_r], axis=1).reshape(RET_DIM, pos.shape[0])
    return _f32(np.cos(ang), np.sin(ang), cr, sr)


def _decay_consts_t(c):
    dmat, qd, kd, cd = _retention_decays(c)
    dm = dmat.transpose(0, 2, 1).reshape(N_RET_SLABS, 2, c, c).transpose(0, 2, 1, 3).reshape(N_RET_SLABS, c, 2 * c)
    cd = np.broadcast_to(cd[:, None, None], (RET_HEADS, 1, RET_DIM))
    return _f32(dm, qd[:, None, :], kd[:, None, :], cd)


def _prompt_mixer(x, w, tile, sub):
    b, l, d = x.shape
    ca, sa, cr, sr = _prompt_tables_t(np.arange(l, dtype=np.int32))
    dm, qd, kd, cd = _decay_consts_t(RET_CHUNK)
    sink = jnp.repeat(w['sinks'].reshape(SWA_KV_HEADS, 1, -1), WINDOW, axis=-1)
    tab = lambda rows: pl.BlockSpec((rows, tile), lambda i, t: (0, t))
    xspec = pl.BlockSpec((1, tile, d), lambda i, t: (i, t, 0))
    win_spec = pl.BlockSpec((1, WINDOW, LANES), lambda i, t: (i, 0, 0))
    st_spec = pl.BlockSpec((1, RET_HEADS, RET_DIM, RET_DIM), lambda i, t: (i, 0, 0, 0))
    return pl.pallas_call(
        functools.partial(_prompt_mixer_kernel, tile=tile, sub=sub),
        grid=(b, l // tile),
        in_specs=[xspec, tab(HEAD_DIM // 2), tab(HEAD_DIM // 2), tab(RET_DIM), tab(RET_DIM), _full((1, d)),
                  _full((IN_COLS, d)), _full((HEAD_DIM, 1)), _full((HEAD_DIM, 1)), _full(sink.shape),
                  _full((d, d)), _full(dm.shape), _full(qd.shape), _full(kd.shape), _full(cd.shape)],
        out_specs=[xspec, win_spec, win_spec, st_spec],
        out_shape=[jax.ShapeDtypeStruct(x.shape, F32),
                   jax.ShapeDtypeStruct((b, WINDOW, LANES), F32),
                   jax.ShapeDtypeStruct((b, WINDOW, LANES), F32),
                   jax.ShapeDtypeStruct((b, RET_HEADS, RET_DIM, RET_DIM), F32)],
        scratch_shapes=[pltpu.VMEM((SWA_KV_W, WINDOW), F32), pltpu.VMEM((SWA_KV_W, WINDOW), F32),
                        pltpu.VMEM((RET_HEADS, RET_DIM, RET_DIM), F32)],
        compiler_params=_params(("arbitrary", "arbitrary")),
        name="prompt_mixer",
    )(x, ca, sa, cr, sr, w['norm_mix'], w['w_in_t'], w['qg_col'], w['kg_col'], sink, w['w_out'], dm, qd, kd,
      cd)


def _sample_mixer(x, cache_k, cache_v, state, w, tile, nseq):
    b, ntok, d = x.shape
    rows = b * ntok
    xf = x.reshape(rows, d)
    pos = np.tile(PAST_LEN + np.arange(ntok, dtype=np.int32), b)
    ca, sa = _rope_tables(pos)
    tab = pl.BlockSpec((tile, LANES), lambda i: (i, 0))
    row_spec = lambda width: pl.BlockSpec((tile, width), lambda i: (i, 0))
    feat = pl.pallas_call(
        _sample_in_kernel,
        grid=(rows // tile,),
        in_specs=[row_spec(d), tab, tab, _full((1, d)), _full((IN_COLS, d)), _full((1, LANES)),
                  _full((1, LANES)), _full((LANES, LANES))],
        out_specs=row_spec(SAMPLE_FEAT),
        out_shape=jax.ShapeDtypeStruct((rows, SAMPLE_FEAT), F32),
        compiler_params=_params(("arbitrary",)),
        name="sample_in",
    )(xf, ca, sa, w['norm_mix'], w['w_in_t'], w['qg'], w['kg'], w['bd'])

    sink = jnp.repeat(w['sinks'], ntok)[:, None]
    seq_rows = nseq * ntok
    unroll = 8
    cache_spec = pl.BlockSpec((nseq, LANES, WINDOW), lambda i: (i, 0, 0))
    o_a, k_out, v_out = pl.pallas_call(
        functools.partial(_sample_swa_kernel, nseq=nseq, ntok=ntok, unroll=unroll),
        grid=(b // nseq,),
        in_specs=[pl.BlockSpec((seq_rows, SAMPLE_FEAT), lambda i: (i, 0)), cache_spec, cache_spec,
                  _full(sink.shape)],
        out_specs=[pl.BlockSpec((seq_rows, SWA_Q_W), lambda i: (i, 0)), cache_spec, cache_spec],
        out_shape=[jax.ShapeDtypeStruct((rows, SWA_Q_W), F32),
                   jax.ShapeDtypeStruct(cache_k.shape, F32),
                   jax.ShapeDtypeStruct(cache_v.shape, F32)],
        scratch_shapes=[pltpu.VMEM((unroll, WINDOW, LANES), F32), pltpu.VMEM((unroll, WINDOW, LANES), F32)],
        compiler_params=_params(("arbitrary",)),
        name="sample_swa",
    )(feat, cache_k, cache_v, sink)

    assert b == LANES, "the retention step puts one sequence per lane"
    xt = jnp.transpose(x, (1, 0, 2)).reshape(rows, d)
    pos_t = np.repeat(PAST_LEN + np.arange(ntok, dtype=np.int32), b)
    _, _, cr_t, sr_t = _prompt_tables_t(pos_t)
    half = rows // 2
    tab_t = pl.BlockSpec((RET_DIM, half), lambda i: (0, i))
    qkv = pl.pallas_call(
        _sample_ret_in_kernel,
        grid=(2,),
        in_specs=[pl.BlockSpec((half, d), lambda i: (i, 0)), tab_t, tab_t, _full((1, d)),
                  _full((3 * RET_W, d))],
        out_specs=pl.BlockSpec((3 * RET_W, half), lambda i: (0, i)),
        out_shape=jax.ShapeDtypeStruct((3 * RET_W, rows), F32),
        compiler_params=_params(("arbitrary",)),
        name="sample_ret_in",
    )(xt, cr_t, sr_t, w['norm_mix'], w['w_in_t'][C_QR:C_G])
    consts = _sample_decay_rows(ntok)
    pair_rows = lambda off: pl.BlockSpec((LANES, rows), lambda p: (off + p, 0))
    st_spec = pl.BlockSpec((2, RET_DIM, RET_DIM, b), lambda p: (p, 0, 0, 0))
    o_r, s_out = pl.pallas_call(
        functools.partial(_sample_ret_kernel, ntok=ntok),
        grid=(N_RET_SLABS,),
        in_specs=[pair_rows(0), pair_rows(N_RET_SLABS), pair_rows(2 * N_RET_SLABS), st_spec,
                  pl.BlockSpec((2,) + consts.shape[1:], lambda p: (p, 0, 0))],
        out_specs=[pl.BlockSpec((ntok, b, LANES), lambda p: (0, 0, p)), st_spec],
        out_shape=[jax.ShapeDtypeStruct((ntok, b, RET_W), F32), jax.ShapeDtypeStruct(state.shape, F32)],
        scratch_shapes=[pltpu.VMEM((ntok, RET_DIM, b), F32)],
        compiler_params=_params(("arbitrary",)),
        name="sample_ret",
    )(qkv, qkv, qkv, state, consts)
    o_r = jnp.transpose(o_r, (1, 0, 2)).reshape(rows, RET_W)

    y = pl.pallas_call(
        _sample_out_kernel,
        grid=(rows // tile,),
        in_specs=[row_spec(d), row_spec(SWA_Q_W), row_spec(RET_W), row_spec(SAMPLE_FEAT), _full((LANES, LANES)),
                  _full((d, d))],
        out_specs=row_spec(d),
        out_shape=jax.ShapeDtypeStruct((rows, d), F32),
        compiler_params=_params(("arbitrary",)),
        name="sample_out",
    )(xf, o_a, o_r, feat, w['bd'], w['w_out'])
    return y, k_out, v_out, s_out


def _mem_kv(mem, w, tile):
    rows, d = mem.shape
    row_spec = lambda width: pl.BlockSpec((tile, width), lambda i: (i, 0))
    return pl.pallas_call(
        _mem_kv_kernel,
        grid=(rows // tile,),
        in_specs=[row_spec(d), _full((1, d)), _full((d, 2 * MEM_W)), _full((1, LANES))],
        out_specs=[pl.BlockSpec((tile * MEM_HEADS, MEM_HEAD_DIM), lambda i: (i, 0))] * 2,
        out_shape=[jax.ShapeDtypeStruct((rows * MEM_HEADS, MEM_HEAD_DIM), F32)] * 2,
        compiler_params=_params(("arbitrary",)),
        name="mem_kv",
    )(mem, w['norm_mem'], w['w_mkv'], w['kgm'])


def _prompt_cross_ffn(x, mk, mv, w, tile):
    rows, d = x.shape
    n = rows // tile
    per_mem = n // mk.shape[0]
    cur = lambda i: jnp.minimum(i, n - 1)
    single = lambda shape: pl.BlockSpec(shape, lambda i: (0,) * len(shape), pipeline_mode=pl.Buffered(1))
    mem_spec = pl.BlockSpec((1, MEM_LEN * MEM_HEADS, MEM_HEAD_DIM), lambda i: (cur(i) // per_mem, 0, 0))
    return pl.pallas_call(
        _prompt_cross_ffn_kernel,
        grid=(n + 1,),
        in_specs=[pl.BlockSpec((tile, d), lambda i: (cur(i), 0)), mem_spec, mem_spec, _full((1, d)),
                  single((d, MEM_W)), _full((1, LANES)), single((MEM_W, d)), _full((1, d)),
                  single((d, 2 * FFN_HIDDEN)), single((FFN_HIDDEN, d))],
        out_specs=pl.BlockSpec((tile, d), lambda i: (jnp.maximum(i - 1, 0), 0)),
        out_shape=jax.ShapeDtypeStruct((rows, d), F32),
        scratch_shapes=[pltpu.VMEM((tile, d), F32)],
        compiler_params=_params(("arbitrary",)),
        name="prompt_cross_ffn",
    )(x, mk, mv, w['norm_cross'], w['w_mq'], w['qgm'], w['w_mo'], w['norm_ffn'], w['w_gu'], w['w_down'])


def _sample_cross_attn(x, mk, mv, w, nseq, tq):
    rows, d = x.shape
    blk = nseq * tq
    mem_spec = pl.BlockSpec((nseq, MEM_LEN * MEM_HEADS, MEM_HEAD_DIM), lambda i: (i, 0, 0))
    return pl.pallas_call(
        functools.partial(_sample_cross_attn_kernel, nseq=nseq, tq=tq),
        grid=(rows // blk,),
        in_specs=[pl.BlockSpec((blk, d), lambda i: (i, 0)), mem_spec, mem_spec, _full((1, d)),
                  _full((d, MEM_W)), _full((1, LANES))],
        out_specs=pl.BlockSpec((blk, MEM_W), lambda i: (i, 0)),
        out_shape=jax.ShapeDtypeStruct((rows, MEM_W), F32),
        compiler_params=_params(("arbitrary",)),
        name="sample_cross_attn",
    )(x, mk, mv, w['norm_cross'], w['w_mq'], w['qgm'])


def _sample_cross_out_ffn(x, o, w, tile):
    rows, d = x.shape
    row_spec = lambda width: pl.BlockSpec((tile, width), lambda i: (i, 0))
    single = lambda shape: pl.BlockSpec(shape, lambda i: (0,) * len(shape), pipeline_mode=pl.Buffered(1))
    return pl.pallas_call(
        _cross_out_ffn_kernel,
        grid=(rows // tile,),
        in_specs=[row_spec(d), row_spec(MEM_W), single((MEM_W, d)), _full((1, d)),
                  single((d, 2 * FFN_HIDDEN)), single((FFN_HIDDEN, d))],
        out_specs=row_spec(d),
        out_shape=jax.ShapeDtypeStruct((rows, d), F32),
        compiler_params=_params(("arbitrary",)),
        name="sample_ffn",
    )(x, o, w['w_mo'], w['norm_ffn'], w['w_gu'], w['w_down'])


def kernel(x_prompt, x_sample, mem_prompt, cache_swa_k, cache_swa_v, state_ret, cache_mem_k, cache_mem_v,
           norm_mix, w_in, q_norm_a, k_norm_a, sinks, w_out, norm_cross, norm_mem, w_mq, w_mkv,
           q_norm_m, k_norm_m, w_mo, norm_ffn, w_gu, w_down):
    assert norm_mix.shape[0] == 1, "single-layer kernel"
    b, l, d = x_prompt.shape
    sb, st, _ = x_sample.shape
    half = (np.arange(LANES) // HEAD_DIM)[:, None] == (np.arange(LANES) // HEAD_DIM)[None, :]
    w = {
        'norm_mix': norm_mix, 'norm_cross': norm_cross, 'norm_mem': norm_mem, 'norm_ffn': norm_ffn,
        'w_out': w_out[0].astype(BF16), 'w_mq': w_mq[0].astype(BF16),
        'w_mkv': w_mkv[0].astype(BF16), 'w_mo': w_mo[0].astype(BF16), 'w_gu': w_gu[0].astype(BF16),
        'w_down': w_down[0].astype(BF16),
        'w_in_t': w_in[0].T.astype(BF16),
        'qg': jnp.tile(q_norm_a, (1, 2)), 'kg': jnp.tile(k_norm_a, (1, 2)),
        'qg_col': q_norm_a.reshape(HEAD_DIM, 1), 'kg_col': k_norm_a.reshape(HEAD_DIM, 1),
        'qgm': q_norm_m, 'kgm': k_norm_m, 'sinks': sinks[0],
        'bd': half.astype(BF16),
    }

    mk, mv = _mem_kv(mem_prompt.reshape(b * MEM_LEN, d), w, tile=256)
    xp, kwin, vwin, ret_p = _prompt_mixer(x_prompt, w, tile=1024, sub=512)
    xp = xp.reshape(b * l, d)
    mem_rows_p = (b, MEM_LEN * MEM_HEADS, MEM_HEAD_DIM)
    yp = _prompt_cross_ffn(xp, mk.reshape(mem_rows_p), mv.reshape(mem_rows_p), w, tile=512)
    yp = yp.reshape(b, l, d)
    ret_p = ret_p.reshape(1, b, RET_HEADS, RET_DIM, RET_DIM)

    nbuf = cache_swa_k.shape[2]
    assert nbuf == WINDOW
    to_feature_major = lambda c: jnp.transpose(c[0], (0, 2, 3, 1)).reshape(sb, SWA_KV_W, nbuf)
    from_feature_major = lambda c: jnp.transpose(c.reshape(sb, SWA_KV_HEADS, HEAD_DIM, nbuf), (0, 3, 1, 2))[None]
    xs, k_s, v_s, s_s = _sample_mixer(x_sample, to_feature_major(cache_swa_k), to_feature_major(cache_swa_v),
                                      jnp.transpose(state_ret[0], (1, 2, 3, 0)), w, tile=256, nseq=8)
    mem_rows = (sb, MEM_LEN * MEM_HEADS, MEM_HEAD_DIM)
    os_ = _sample_cross_attn(xs, cache_mem_k.reshape(mem_rows), cache_mem_v.reshape(mem_rows), w, nseq=8, tq=st)
    ys = _sample_cross_out_ffn(xs, os_, w, tile=512).reshape(sb, st, d)

    kv_shape = (1, b, WINDOW, SWA_KV_HEADS, HEAD_DIM)
    mem_shape = (1, b, MEM_LEN, MEM_HEADS, MEM_HEAD_DIM)
    return (yp, ys, kwin.reshape(kv_shape), vwin.reshape(kv_shape), ret_p,
            mk.reshape(mem_shape), mv.reshape(mem_shape),
            from_feature_major(k_s), from_feature_major(v_s),
            jnp.transpose(s_s, (3, 0, 1, 2))[None])
```

```python
import functools

import jax
import jax.numpy as jnp
import numpy as np
from jax import lax
from jax.experimental import pallas as pl
from jax.experimental.pallas import tpu as pltpu

F32 = jnp.float32
BF16 = jnp.bfloat16

LANES = 128
D_MODEL = 1024
HEAD_DIM = 64
SWA_HEADS = 8
SWA_KV_HEADS = 2
WINDOW = 128
RET_HEADS = 8
RET_DIM = 64
RET_CHUNK = 128
RET_THETA = 10000.0
ROPE_THETA = 10000.0
MEM_LEN = 256
MEM_HEADS = 4
MEM_HEAD_DIM = 128
MEM_W = MEM_HEADS * MEM_HEAD_DIM
FFN_HIDDEN = 2816
RMS_EPS = 1e-6
NEG_INF = -1e30
LOG2E = 1.4426950408889634
PAST_LEN = 16384

SWA_Q_W = SWA_HEADS * HEAD_DIM
SWA_KV_W = SWA_KV_HEADS * HEAD_DIM
RET_W = RET_HEADS * RET_DIM
IN_COLS = SWA_Q_W + 2 * SWA_KV_W + 4 * RET_W
C_QA, C_KA, C_VA = 0, SWA_Q_W, SWA_Q_W + SWA_KV_W
C_QR = SWA_Q_W + 2 * SWA_KV_W
C_KR, C_VR, C_G = C_QR + RET_W, C_QR + 2 * RET_W, C_QR + 3 * RET_W
C_GATE_S = C_QR
SAMPLE_FEAT = C_QR + RET_W
N_QA_SLABS = SWA_Q_W // LANES
N_RET_SLABS = RET_W // LANES

VMEM_LIMIT = 56 * 1024 * 1024


def _dot(a, b):
    return jnp.dot(a.astype(BF16), b.astype(BF16), preferred_element_type=F32)


def _dot_nt(a, b):
    return lax.dot_general(a.astype(BF16), b.astype(BF16), (((1,), (1,)), ((), ())),
                           preferred_element_type=F32)


def _rms(x):
    return x * lax.rsqrt(jnp.mean(x * x, axis=-1, keepdims=True) + RMS_EPS)


def _lane_consts():
    lane = lax.broadcasted_iota(jnp.int32, (1, LANES), 1)
    m_left = (lane < HEAD_DIM).astype(F32)
    m_right = 1.0 - m_left
    first_half = (lane % HEAD_DIM) < (HEAD_DIM // 2)
    even = (lane % 2) == 0
    return m_left, m_right, first_half, even


def _head_norm(y, bd):
    ss = jnp.dot((y * y).astype(BF16), bd, preferred_element_type=F32) * (1.0 / HEAD_DIM)
    return y * lax.rsqrt(ss + RMS_EPS)


def _rot_half(y, cos, sin_signed, first_half):
    swapped = jnp.where(first_half, pltpu.roll(y, LANES - HEAD_DIM // 2, 1), pltpu.roll(y, HEAD_DIM // 2, 1))
    return y * cos + swapped * sin_signed


def _dot_tn(a, b):
    return lax.dot_general(a.astype(BF16), b.astype(BF16), (((0,), (0,)), ((), ())),
                           preferred_element_type=F32)


def _head_norm_t(y):
    return y * lax.rsqrt(jnp.mean(y * y, axis=0, keepdims=True) + RMS_EPS)


def _rot_half_t(y, cos, sin):
    half = HEAD_DIM // 2
    y1, y2 = y[0:half], y[half:]
    return jnp.concatenate([y1 * cos - y2 * sin, y2 * cos + y1 * sin], axis=0)


def _rot_pairs_t(y, cos, sin_signed, even_row):
    n = y.shape[0]
    swapped = jnp.where(even_row, pltpu.roll(y, n - 1, 0), pltpu.roll(y, 1, 0))
    return y * cos + swapped * sin_signed


def _prompt_mixer_kernel(x_ref, ca_ref, sa_ref, cr_ref, sr_ref, nmix_ref, wint_ref, qg_ref, kg_ref,
                         sink_ref, wout_ref, dm_ref, qd_ref, kd_ref, cd_ref,
                         y_ref, kwin_ref, vwin_ref, sout_ref,
                         pk_ref, pv_ref, s_ref, *, tile, sub):
    t = pl.program_id(1)
    nblk = sub // WINDOW
    hd = HEAD_DIM
    group = SWA_HEADS // SWA_KV_HEADS

    @pl.when(t == 0)
    def _():
        pk_ref[...] = jnp.zeros_like(pk_ref)
        pv_ref[...] = jnp.zeros_like(pv_ref)
        s_ref[...] = jnp.zeros_like(s_ref)

    qg, kg = qg_ref[...], kg_ref[...]
    qs = qg * (hd ** -0.5 * LOG2E)
    head = lambda a, h: a[h * hd:(h + 1) * hd]
    chunk = lambda a, h, j: a[h * hd:(h + 1) * hd, j * RET_CHUNK:(j + 1) * RET_CHUNK]
    keys_of = lambda j: slice(j * WINDOW, (j + 2) * WINDOW)
    units = [(g, j) for j in range(nblk) for g in range(SWA_KV_HEADS)]
    key = lax.broadcasted_iota(jnp.int32, (2 * WINDOW, group * WINDOW), 0)
    qry = lax.broadcasted_iota(jnp.int32, (2 * WINDOW, group * WINDOW), 1) % WINDOW
    band = (key > qry) & (key <= qry + WINDOW)
    even_row = (lax.broadcasted_iota(jnp.int32, (RET_W, sub), 0) % 2) == 0
    zeros = jnp.zeros((hd, RET_CHUNK), F32)
    carry = {'pk': pk_ref[...], 'pv': pv_ref[...], 'state': [s_ref[h] for h in range(RET_HEADS)]}

    def stages(idx):
        toks = slice(idx * sub, (idx + 1) * sub)
        v = {}

        def project():
            v['x'] = x_ref[0, toks]
            hb = (_rms(v['x']) * nmix_ref[...]).astype(BF16)
            proj = lambda lo, hi: lax.dot_general(wint_ref[lo:hi, :], hb, (((1,), (1,)), ((), ())),
                                                  preferred_element_type=F32)
            v['swa'], v['ret'] = proj(C_QA, C_QR), proj(C_QR, C_VR)
            v['vr'], v['gate'] = proj(C_VR, C_G), proj(C_G, IN_COLS)

        def swa_prepare():
            ca, sa = ca_ref[:, toks], sa_ref[:, toks]
            swa_t = v['swa']
            v['qa'] = [_rot_half_t(_head_norm_t(head(swa_t, h)) * qs, ca, sa) for h in range(SWA_HEADS)]
            ka = jnp.concatenate([_rot_half_t(_head_norm_t(head(swa_t, SWA_HEADS + g)) * kg, ca, sa)
                                  for g in range(SWA_KV_HEADS)], axis=0)
            va = swa_t[C_VA:C_QR]
            v['kfull'] = jnp.concatenate([carry['pk'], ka], axis=1).astype(BF16)
            v['vfull'] = jnp.concatenate([carry['pv'], va], axis=1).astype(BF16)
            carry['pk'], carry['pv'] = ka[:, sub - WINDOW:], va[:, sub - WINDOW:]

        def swa_scores():
            v['scores'] = []
            for g, j in units:
                blk = slice(j * WINDOW, (j + 1) * WINDOW)
                q4 = jnp.concatenate([v['qa'][group * g + u][:, blk] for u in range(group)], axis=1)
                v['scores'].append(_dot_tn(head(v['kfull'], g)[:, keys_of(j)], q4))

        def ret_rotate():
            cr8 = jnp.concatenate([cr_ref[:, toks]] * RET_HEADS, axis=0)
            sr8 = jnp.concatenate([sr_ref[:, toks]] * RET_HEADS, axis=0)
            v['qr'] = _rot_pairs_t(v['ret'][0:RET_W], cr8, sr8, even_row)
            v['kr'] = _rot_pairs_t(v['ret'][RET_W:], cr8, sr8, even_row) * (RET_DIM ** -0.5)

        def ret_scores():
            qr, kr, vr = v['qr'], v['kr'], v['vr']
            v['inner'] = [[None] * nblk for _ in range(N_RET_SLABS)]
            for j in range(nblk):
                for pr in range(N_RET_SLABS):
                    q_bd = jnp.concatenate([jnp.concatenate([chunk(qr, 2 * pr, j), zeros], axis=1),
                                            jnp.concatenate([zeros, chunk(qr, 2 * pr + 1, j)], axis=1)], axis=0)
                    k2 = kr[pr * LANES:(pr + 1) * LANES, j * RET_CHUNK:(j + 1) * RET_CHUNK]
                    v['inner'][pr][j] = _dot_tn(k2, q_bd) * dm_ref[pr]
            v['incr'] = [[_dot_nt(chunk(vr, h, j), chunk(kr, h, j) * kd_ref[h]) for j in range(nblk)]
                         for h in range(RET_HEADS)]

        def swa_softmax():
            v['probs'], v['denoms'] = [], []
            for (g, j), s in zip(units, v['scores']):
                sink = sink_ref[g] * LOG2E
                valid = band & (key >= WINDOW * (1 - t)) if (idx == 0 and j == 0) else band
                s = jnp.where(valid, s, NEG_INF)
                m = jnp.maximum(jnp.max(s, axis=0, keepdims=True), sink)
                p = jnp.exp2(s - m)
                v['denoms'].append(jnp.sum(p, axis=0, keepdims=True) + jnp.exp2(sink - m))
                v['probs'].append(p.astype(BF16))

        def swa_values():
            v['oa'] = [[None] * nblk for _ in range(SWA_HEADS)]
            for (g, j), p, denom in zip(units, v['probs'], v['denoms']):
                o = jnp.dot(head(v['vfull'], g)[:, keys_of(j)], p, preferred_element_type=F32) / denom
                for u in range(group):
                    v['oa'][group * g + u][j] = o[:, u * WINDOW:(u + 1) * WINDOW]
            v['state'] = [[carry['state'][h]] for h in range(RET_HEADS)]
            for h in range(RET_HEADS):
                for j in range(nblk):
                    v['state'][h].append(v['state'][h][j] * cd_ref[h] + v['incr'][h][j])
                carry['state'][h] = v['state'][h][nblk]

        def ret_outputs():
            v['or'] = [[None] * nblk for _ in range(RET_HEADS)]
            for j in range(nblk):
                for h in range(RET_HEADS):
                    u = h % 2
                    lhs = jnp.concatenate([chunk(v['vr'], h, j), v['state'][h][j]], axis=1)
                    rhs = jnp.concatenate([v['inner'][h // 2][j][:, u * RET_CHUNK:(u + 1) * RET_CHUNK],
                                           chunk(v['qr'], h, j) * qd_ref[h]], axis=0)
                    v['or'][h][j] = _dot(lhs, rhs)

        def gate_mix():
            mix = [jnp.concatenate(blocks, axis=1) for blocks in v['oa']]
            for h in range(RET_HEADS):
                o_h = _head_norm_t(jnp.concatenate(v['or'][h], axis=1))
                g_h = head(v['gate'], h)
                mix.append(o_h * (g_h * jax.nn.sigmoid(g_h)))
            v['mix'] = jnp.concatenate(mix, axis=0).astype(BF16)

        def out_project():
            y_ref[0, toks] = v['x'] + lax.dot_general(v['mix'], wout_ref[...], (((0,), (0,)), ((), ())),
                                                      preferred_element_type=F32)

        return [project, swa_prepare, swa_scores, ret_rotate, ret_scores, swa_softmax, swa_values,
                ret_outputs, gate_mix, out_project]

    pipelines = [stages(i) for i in range(tile // sub)]
    nstage = len(pipelines[0])
    for slot in range(nstage + len(pipelines) - 1):
        for i, pipe in enumerate(pipelines):
            if 0 <= slot - i < nstage:
                pipe[slot - i]()

    pk_ref[...], pv_ref[...] = carry['pk'], carry['pv']
    kwin_ref[0], vwin_ref[0] = carry['pk'].T, carry['pv'].T
    for h in range(RET_HEADS):
        s_ref[h] = carry['state'][h]

    @pl.when(t == pl.num_programs(1) - 1)
    def _():
        for h in range(RET_HEADS):
            sout_ref[0, h] = carry['state'][h].T


def _sample_in_kernel(x_ref, ca_ref, sa_ref, nmix_ref, wint_ref, qg_ref, kg_ref, bd_ref, feat_ref):
    _, _, first_half, _ = _lane_consts()
    ca, sa, bd = ca_ref[...], sa_ref[...], bd_ref[...]
    hb = (_rms(x_ref[...]) * nmix_ref[...]).astype(BF16)
    proj = lambda lo, hi: lax.dot_general(hb, wint_ref[lo:hi, :], (((1,), (1,)), ((), ())),
                                          preferred_element_type=F32)
    swa = proj(C_QA, C_QR)
    slabs = [_rot_half(_head_norm(swa[:, s * LANES:(s + 1) * LANES], bd) * qg_ref[...], ca, sa, first_half)
             for s in range(N_QA_SLABS)]
    slabs.append(_rot_half(_head_norm(swa[:, C_KA:C_VA], bd) * kg_ref[...], ca, sa, first_half))
    feat_ref[...] = jnp.concatenate(slabs + [swa[:, C_VA:C_QR], proj(C_G, IN_COLS)], axis=1)


def _sample_swa_kernel(feat_ref, ckt_ref, cvt_ref, sink_ref, o_ref, kout_ref, vout_ref,
                       kn_ref, vn_ref, *, nseq, ntok, unroll):
    @pl.when(pl.program_id(0) == 0)
    def _():
        kn_ref[...] = jnp.zeros_like(kn_ref)
        vn_ref[...] = jnp.zeros_like(vn_ref)

    m_left, m_right, _, _ = _lane_consts()
    tail = WINDOW - ntok
    nrow = SWA_HEADS * ntok
    row_tok = lax.broadcasted_iota(jnp.int32, (nrow, WINDOW), 0) % ntok
    col = lax.broadcasted_iota(jnp.int32, (nrow, WINDOW), 1)
    valid_cache = col > row_tok
    valid_new = (col >= tail) & (col - tail <= row_tok)
    in_tail = lax.broadcasted_iota(jnp.int32, (LANES, WINDOW), 1) >= tail
    sink = sink_ref[...]
    shift = lambda a: pltpu.roll(a, HEAD_DIM, 1)

    def body(step, carry):
        seqs = [step * unroll + u for u in range(unroll)]
        rows = [pl.ds(pl.multiple_of(i * ntok, ntok), ntok) for i in seqs]
        q_rows, caches = [], []
        for u, i in enumerate(seqs):
            slab = lambda c, u=u: feat_ref[rows[u], c:c + LANES]
            kn_ref[u, tail:WINDOW] = slab(C_KA)
            vn_ref[u, tail:WINDOW] = slab(C_VA)
            s0, s1, s2, s3 = [slab(C_QA + s * LANES) for s in range(N_QA_SLABS)]
            q_rows.append(jnp.concatenate(
                [s0 * m_left, shift(s0 * m_right), s1 * m_left, shift(s1 * m_right),
                 shift(s2 * m_left), s2 * m_right, shift(s3 * m_left), s3 * m_right], axis=0))
            caches.append((ckt_ref[i], cvt_ref[i], kn_ref[u], vn_ref[u]))
        scores = [(_dot(q, kt), _dot_nt(q, kn)) for q, (kt, _, kn, _) in zip(q_rows, caches)]
        probs = []
        for s_c, s_n in scores:
            s_c = jnp.where(valid_cache, s_c * (HEAD_DIM ** -0.5), NEG_INF)
            s_n = jnp.where(valid_new, s_n * (HEAD_DIM ** -0.5), NEG_INF)
            m = jnp.maximum(jnp.maximum(jnp.max(s_c, axis=-1, keepdims=True),
                                        jnp.max(s_n, axis=-1, keepdims=True)), sink)
            p_c, p_n = jnp.exp(s_c - m), jnp.exp(s_n - m)
            denom = (jnp.sum(p_c, axis=-1, keepdims=True) + jnp.sum(p_n, axis=-1, keepdims=True)
                     + jnp.exp(sink - m))
            probs.append((p_c, p_n, denom))
        for u, i in enumerate(seqs):
            kt, vt, kn, vn = caches[u]
            p_c, p_n, denom = probs[u]
            o = (_dot_nt(p_c, vt) + _dot(p_n, vn)) / denom
            n = ntok
            left = lambda h: o[h * n:(h + 1) * n] * m_left
            right = lambda h: o[h * n:(h + 1) * n] * m_right
            o_ref[rows[u], 0:LANES] = left(0) + shift(left(1))
            o_ref[rows[u], LANES:2 * LANES] = left(2) + shift(left(3))
            o_ref[rows[u], 2 * LANES:3 * LANES] = shift(right(4)) + right(5)
            o_ref[rows[u], 3 * LANES:4 * LANES] = shift(right(6)) + right(7)
            kout_ref[i] = jnp.where(in_tail, kn.T, pltpu.roll(kt, tail, 1))
            vout_ref[i] = jnp.where(in_tail, vn.T, pltpu.roll(vt, tail, 1))
        return carry

    lax.fori_loop(0, nseq // unroll, body, 0)


def _sample_ret_in_kernel(x_ref, cr_ref, sr_ref, nmix_ref, wint_ref, qkv_ref):
    hb = (_rms(x_ref[...]) * nmix_ref[...]).astype(BF16)
    qkv = lax.dot_general(wint_ref[C_QR:C_G, :], hb, (((1,), (1,)), ((), ())), preferred_element_type=F32)
    n = qkv.shape[1]
    even_row = (lax.broadcasted_iota(jnp.int32, (RET_W, n), 0) % 2) == 0
    cr8 = jnp.concatenate([cr_ref[...]] * RET_HEADS, axis=0)
    sr8 = jnp.concatenate([sr_ref[...]] * RET_HEADS, axis=0)
    qkv_ref[0:RET_W] = _rot_pairs_t(qkv[0:RET_W], cr8, sr8, even_row)
    qkv_ref[RET_W:2 * RET_W] = _rot_pairs_t(qkv[RET_W:2 * RET_W], cr8, sr8, even_row) * (RET_DIM ** -0.5)
    qkv_ref[2 * RET_W:] = qkv[2 * RET_W:]


def _sample_ret_kernel(q_ref, k_ref, v_ref, s_ref, c_ref, o_ref, so_ref, ks_ref, *, ntok):
    hd, nseq = RET_DIM, LANES
    tok = lambda t: slice(t * nseq, (t + 1) * nseq)
    row_qd, row_kd, row_cd = ntok * ntok, ntok * ntok + ntok, ntok * ntok + 2 * ntok
    group = 8
    pair_outs = []
    for u in range(2):
        feats = slice(u * hd, (u + 1) * hd)
        const = lambda r: c_ref[u, r:r + 1, :]
        for t in range(ntok):
            ks_ref[t] = k_ref[feats, tok(t)] * const(row_kd + t)
        outs = []
        for t in range(ntok):
            q_t = q_ref[feats, tok(t)]
            acc = jnp.zeros((hd, nseq), F32)
            for k in range(t + 1):
                w_tk = jnp.sum(q_t * k_ref[feats, tok(k)], axis=0, keepdims=True) * const(t * ntok + k)
                acc = acc + w_tk * v_ref[feats, tok(k)]

            def cross(j, carry, t=t):
                d0 = pl.multiple_of(j * group, group)
                q_rows = q_ref[pl.ds(u * hd + d0, group), tok(t)]
                for i in range(group):
                    carry = carry + q_rows[i:i + 1, :] * s_ref[u, d0 + i]
                return carry
            carried = lax.fori_loop(0, hd // group, cross, jnp.zeros((hd, nseq), F32))
            outs.append(acc + carried * const(row_qd + t))

        def update(j, carry):
            d0 = pl.multiple_of(j * group, group)
            k_rows = [ks_ref[t, pl.ds(d0, group), :] for t in range(ntok)]
            for i in range(group):
                new = s_ref[u, d0 + i] * const(row_cd)
                for t in range(ntok):
                    new = new + k_rows[t][i:i + 1, :] * v_ref[feats, tok(t)]
                so_ref[u, d0 + i] = new
            return carry
        lax.fori_loop(0, hd // group, update, 0)
        pair_outs.append(outs)
    for t in range(ntok):
        o_ref[t] = jnp.concatenate([pair_outs[0][t], pair_outs[1][t]], axis=0).T


def _sample_out_kernel(x_ref, oa_ref, or_ref, feat_ref, bd_ref, wout_ref, y_ref):
    outs = [oa_ref[...]]
    for p in range(N_RET_SLABS):
        gate = feat_ref[:, C_GATE_S + p * LANES:C_GATE_S + (p + 1) * LANES]
        o_ret = or_ref[:, p * LANES:(p + 1) * LANES]
        outs.append(_head_norm(o_ret, bd_ref[...]) * (gate * jax.nn.sigmoid(gate)))
    mix = jnp.concatenate(outs, axis=1).astype(BF16)
    y_ref[...] = x_ref[...] + jnp.dot(mix, wout_ref[...], preferred_element_type=F32)


def _mem_kv_kernel(m_ref, nmem_ref, wkv_ref, kg_ref, k_ref, v_ref):
    hb = (_rms(m_ref[...]) * nmem_ref[...]).astype(BF16)
    kv = jnp.dot(hb, wkv_ref[...], preferred_element_type=F32)
    slots = m_ref.shape[0]
    for h in range(MEM_HEADS):
        rows = pl.ds(h, slots, stride=MEM_HEADS)
        k_ref[rows, :] = _rms(kv[:, h * LANES:(h + 1) * LANES]) * kg_ref[...]
        v_ref[rows, :] = kv[:, MEM_W + h * LANES:MEM_W + (h + 1) * LANES]


def _mem_queries(x, ncross, wq_ref, qg):
    hb = (_rms(x) * ncross).astype(BF16)
    q = jnp.dot(hb, wq_ref[...], preferred_element_type=F32)
    qs = qg * (MEM_HEAD_DIM ** -0.5 * LOG2E)
    return [_rms(q[:, h * LANES:(h + 1) * LANES]) * qs for h in range(MEM_HEADS)]


def _prompt_cross_ffn_kernel(x_ref, mk_ref, mv_ref, ncross_ref, wq_ref, qg_ref, wo_ref, nffn_ref, wgu_ref,
                             wdown_ref, y_ref, xa_ref):
    i, last = pl.program_id(0), pl.num_programs(0) - 1
    head = lambda ref, h: ref[0, pl.ds(h, MEM_LEN, stride=MEM_HEADS), :].astype(BF16)
    a, f = {}, {}

    def attn_queries():
        a['x'] = x_ref[...]
        a['qn'] = _mem_queries(a['x'], ncross_ref[...], wq_ref, qg_ref[...])

    def attn_scores():
        a['scores'] = [_dot_nt(head(mk_ref, h), a['qn'][h]) for h in range(MEM_HEADS)]

    def attn_softmax():
        a['probs'], a['denoms'] = [], []
        for s in a['scores']:
            p = jnp.exp2(s - jnp.max(s, axis=0, keepdims=True))
            a['denoms'].append(jnp.sum(p, axis=0, keepdims=True))
            a['probs'].append(p.astype(BF16))

    def attn_output():
        o_t = jnp.concatenate([_dot_tn(head(mv_ref, h), a['probs'][h]) / a['denoms'][h]
                               for h in range(MEM_HEADS)], axis=0)
        a['xa'] = a['x'] + lax.dot_general(o_t.astype(BF16), wo_ref[...], (((0,), (0,)), ((), ())),
                                           preferred_element_type=F32)

    def ffn_norm():
        f['xa'] = xa_ref[...]
        f['hb'] = (_rms(f['xa']) * nffn_ref[...]).astype(BF16)

    def ffn_gate():
        f['g'] = jnp.dot(f['hb'], wgu_ref[:, 0:FFN_HIDDEN], preferred_element_type=F32)

    def ffn_up():
        f['u'] = jnp.dot(f['hb'], wgu_ref[:, FFN_HIDDEN:], preferred_element_type=F32)

    def ffn_act():
        f['act'] = (f['g'] * jax.nn.sigmoid(f['g']) * f['u']).astype(BF16)

    def ffn_down():
        y_ref[...] = f['xa'] + jnp.dot(f['act'], wdown_ref[...], preferred_element_type=F32)

    attn = [attn_queries, attn_scores, attn_softmax, attn_output]
    ffn = [ffn_norm, ffn_gate, ffn_up, ffn_act, ffn_down]

    @pl.when(i == 0)
    def _():
        for stage in attn:
            stage()
        xa_ref[...] = a['xa']

    @pl.when((i > 0) & (i < last))
    def _():
        for stage in [ffn_norm, attn_queries, ffn_gate, attn_scores, ffn_up, attn_softmax, ffn_act,
                      attn_output, ffn_down]:
            stage()
        xa_ref[...] = a['xa']

    @pl.when(i == last)
    def _():
        for stage in ffn:
            stage()


def _sample_cross_attn_kernel(x_ref, mk_ref, mv_ref, ncross_ref, wq_ref, qg_ref, o_ref, *, nseq, tq):
    qn = _mem_queries(x_ref[...], ncross_ref[...], wq_ref, qg_ref[...])
    nrow = MEM_HEADS * tq
    nmem = MEM_LEN * MEM_HEADS
    own = (lax.broadcasted_iota(jnp.int32, (nrow, nmem), 0) // tq
           == lax.broadcasted_iota(jnp.int32, (nrow, nmem), 1) % MEM_HEADS)
    scores = []
    for i in range(nseq):
        q_stack = jnp.concatenate([qn[h][i * tq:(i + 1) * tq] for h in range(MEM_HEADS)], axis=0)
        scores.append(_dot_nt(q_stack, mk_ref[i]))
    probs, denoms = [], []
    for s in scores:
        s = jnp.where(own, s, NEG_INF)
        p = jnp.exp2(s - jnp.max(s, axis=-1, keepdims=True))
        denoms.append(jnp.sum(p, axis=-1, keepdims=True))
        probs.append(p.astype(BF16))
    for i in range(nseq):
        o = jnp.dot(probs[i], mv_ref[i].astype(BF16), preferred_element_type=F32) / denoms[i]
        for h in range(MEM_HEADS):
            o_ref[i * tq:(i + 1) * tq, h * LANES:(h + 1) * LANES] = o[h * tq:(h + 1) * tq]


def _cross_out_ffn_kernel(x_ref, o_ref, wo_ref, nffn_ref, wgu_ref, wdown_ref, y_ref):
    x = x_ref[...] + jnp.dot(o_ref[...].astype(BF16), wo_ref[...], preferred_element_type=F32)
    hb = (_rms(x) * nffn_ref[...]).astype(BF16)
    g = jnp.dot(hb, wgu_ref[:, 0:FFN_HIDDEN], preferred_element_type=F32)
    u = jnp.dot(hb, wgu_ref[:, FFN_HIDDEN:], preferred_element_type=F32)
    act = (g * jax.nn.sigmoid(g) * u).astype(BF16)
    y_ref[...] = x + jnp.dot(act, wdown_ref[...], preferred_element_type=F32)


def _f32(*arrays):
    return [np.ascontiguousarray(a, dtype=np.float32) for a in arrays]


def _rope_angles(pos):
    half = HEAD_DIM // 2
    inv = 1.0 / (ROPE_THETA ** (np.arange(half, dtype=np.float64) / half))
    return pos.astype(np.float64)[:, None] * inv[None, :]


def _ret_angles(pos):
    inv = RET_THETA ** (-np.linspace(0.0, 1.0, RET_DIM // 2, dtype=np.float64))
    return pos.astype(np.float64)[:, None] * inv[None, :]


def _rope_tables(pos):
    ang = _rope_angles(pos)
    cos, sin = np.cos(ang), np.sin(ang)
    c64 = np.concatenate([cos, cos], axis=-1)
    s64 = np.concatenate([-sin, sin], axis=-1)
    return _f32(np.tile(c64, (1, 2)), np.tile(s64, (1, 2)))


def _retention_decays(c):
    log_g = np.log(1.0 - np.exp2(-5.0 - np.arange(RET_HEADS, dtype=np.float64)))
    idx = np.arange(c, dtype=np.float64)
    diff = idx[:, None] - idx[None, :]
    dmat = np.where(diff >= 0, np.exp(np.maximum(diff, 0.0)[None] * log_g[:, None, None]), 0.0)
    qd = np.exp((idx + 1.0)[None, :] * log_g[:, None])
    kd = np.exp((c - 1.0 - idx)[None, :] * log_g[:, None])
    cd = np.exp(c * log_g)
    return dmat, qd, kd, cd


def _sample_decay_rows(c):
    dmat, qd, kd, cd = _retention_decays(c)
    rows = np.concatenate([dmat.reshape(RET_HEADS, c * c), qd, kd, cd[:, None]], axis=1)
    rows = np.pad(rows, ((0, 0), (0, -rows.shape[1] % 8)))
    return _f32(np.broadcast_to(rows[:, :, None], rows.shape + (LANES,)))[0]


def _full(shape):
    nd = len(shape)
    return pl.BlockSpec(shape, lambda *_: (0,) * nd)


def _params(sem):
    return pltpu.CompilerParams(dimension_semantics=sem, vmem_limit_bytes=VMEM_LIMIT)


def _prompt_tables_t(pos):
    ang = _rope_angles(pos).T
    ang_r = _ret_angles(pos).T
    cos_r, sin_r = np.cos(ang_r), np.sin(ang_r)
    cr = np.repeat(cos_r, 2, axis=0)
    sr = np.stack([-sin_r, sin_r], axis=1).reshape(RET_DIM, pos.shape[0])
    return _f32(np.cos(ang), np.sin(ang), cr, sr)


def _decay_consts_t(c):
    dmat, qd, kd, cd = _retention_decays(c)
    dm = dmat.transpose(0, 2, 1).reshape(N_RET_SLABS, 2, c, c).transpose(0, 2, 1, 3).reshape(N_RET_SLABS, c, 2 * c)
    cd = np.broadcast_to(cd[:, None, None], (RET_HEADS, 1, RET_DIM))
    return _f32(dm, qd[:, None, :], kd[:, None, :], cd)


def _prompt_mixer(x, w, tile, sub):
    b, l, d = x.shape
    ca, sa, cr, sr = _prompt_tables_t(np.arange(l, dtype=np.int32))
    dm, qd, kd, cd = _decay_consts_t(RET_CHUNK)
    sink = jnp.repeat(w['sinks'].reshape(SWA_KV_HEADS, 1, -1), WINDOW, axis=-1)
    tab = lambda rows: pl.BlockSpec((rows, tile), lambda i, t: (0, t))
    xspec = pl.BlockSpec((1, tile, d), lambda i, t: (i, t, 0))
    win_spec = pl.BlockSpec((1, WINDOW, LANES), lambda i, t: (i, 0, 0))
    st_spec = pl.BlockSpec((1, RET_HEADS, RET_DIM, RET_DIM), lambda i, t: (i, 0, 0, 0))
    return pl.pallas_call(
        functools.partial(_prompt_mixer_kernel, tile=tile, sub=sub),
        grid=(b, l // tile),
        in_specs=[xspec, tab(HEAD_DIM // 2), tab(HEAD_DIM // 2), tab(RET_DIM), tab(RET_DIM), _full((1, d)),
                  _full((IN_COLS, d)), _full((HEAD_DIM, 1)), _full((HEAD_DIM, 1)), _full(sink.shape),
                  _full((d, d)), _full(dm.shape), _full(qd.shape), _full(kd.shape), _full(cd.shape)],
        out_specs=[xspec, win_spec, win_spec, st_spec],
        out_shape=[jax.ShapeDtypeStruct(x.shape, F32),
                   jax.ShapeDtypeStruct((b, WINDOW, LANES), F32),
                   jax.ShapeDtypeStruct((b, WINDOW, LANES), F32),
                   jax.ShapeDtypeStruct((b, RET_HEADS, RET_DIM, RET_DIM), F32)],
        scratch_shapes=[pltpu.VMEM((SWA_KV_W, WINDOW), F32), pltpu.VMEM((SWA_KV_W, WINDOW), F32),
                        pltpu.VMEM((RET_HEADS, RET_DIM, RET_DIM), F32)],
        compiler_params=_params(("arbitrary", "arbitrary")),
        name="prompt_mixer",
    )(x, ca, sa, cr, sr, w['norm_mix'], w['w_in_t'], w['qg_col'], w['kg_col'], sink, w['w_out'], dm, qd, kd,
      cd)


def _sample_mixer(x, cache_k, cache_v, state, w, tile, nseq):
    b, ntok, d = x.shape
    rows = b * ntok
    xf = x.reshape(rows, d)
    pos = np.tile(PAST_LEN + np.arange(ntok, dtype=np.int32), b)
    ca, sa = _rope_tables(pos)
    tab = pl.BlockSpec((tile, LANES), lambda i: (i, 0))
    row_spec = lambda width: pl.BlockSpec((tile, width), lambda i: (i, 0))
    feat = pl.pallas_call(
        _sample_in_kernel,
        grid=(rows // tile,),
        in_specs=[row_spec(d), tab, tab, _full((1, d)), _full((IN_COLS, d)), _full((1, LANES)),
                  _full((1, LANES)), _full((LANES, LANES))],
        out_specs=row_spec(SAMPLE_FEAT),
        out_shape=jax.ShapeDtypeStruct((rows, SAMPLE_FEAT), F32),
        compiler_params=_params(("arbitrary",)),
        name="sample_in",
    )(xf, ca, sa, w['norm_mix'], w['w_in_t'], w['qg'], w['kg'], w['bd'])

    sink = jnp.repeat(w['sinks'], ntok)[:, None]
    seq_rows = nseq * ntok
    unroll = 8
    cache_spec = pl.BlockSpec((nseq, LANES, WINDOW), lambda i: (i, 0, 0))
    o_a, k_out, v_out = pl.pallas_call(
        functools.partial(_sample_swa_kernel, nseq=nseq, ntok=ntok, unroll=unroll),
        grid=(b // nseq,),
        in_specs=[pl.BlockSpec((seq_rows, SAMPLE_FEAT), lambda i: (i, 0)), cache_spec, cache_spec,
                  _full(sink.shape)],
        out_specs=[pl.BlockSpec((seq_rows, SWA_Q_W), lambda i: (i, 0)), cache_spec, cache_spec],
        out_shape=[jax.ShapeDtypeStruct((rows, SWA_Q_W), F32),
                   jax.ShapeDtypeStruct(cache_k.shape, F32),
                   jax.ShapeDtypeStruct(cache_v.shape, F32)],
        scratch_shapes=[pltpu.VMEM((unroll, WINDOW, LANES), F32), pltpu.VMEM((unroll, WINDOW, LANES), F32)],
        compiler_params=_params(("arbitrary",)),
        name="sample_swa",
    )(feat, cache_k, cache_v, sink)

    assert b == LANES, "the retention step puts one sequence per lane"
    xt = jnp.transpose(x, (1, 0, 2)).reshape(rows, d)
    pos_t = np.repeat(PAST_LEN + np.arange(ntok, dtype=np.int32), b)
    _, _, cr_t, sr_t = _prompt_tables_t(pos_t)
    half = rows // 2
    tab_t = pl.BlockSpec((RET_DIM, half), lambda i: (0, i))
    qkv = pl.pallas_call(
        _sample_ret_in_kernel,
        grid=(2,),
        in_specs=[pl.BlockSpec((half, d), lambda i: (i, 0)), tab_t, tab_t, _full((1, d)),
                  _full((IN_COLS, d))],
        out_specs=pl.BlockSpec((3 * RET_W, half), lambda i: (0, i)),
        out_shape=jax.ShapeDtypeStruct((3 * RET_W, rows), F32),
        compiler_params=_params(("arbitrary",)),
        name="sample_ret_in",
    )(xt, cr_t, sr_t, w['norm_mix'], w['w_in_t'])
    consts = _sample_decay_rows(ntok)
    pair_rows = lambda off: pl.BlockSpec((LANES, rows), lambda p: (off + p, 0))
    st_spec = pl.BlockSpec((2, RET_DIM, RET_DIM, b), lambda p: (p, 0, 0, 0))
    o_r, s_out = pl.pallas_call(
        functools.partial(_sample_ret_kernel, ntok=ntok),
        grid=(N_RET_SLABS,),
        in_specs=[pair_rows(0), pair_rows(N_RET_SLABS), pair_rows(2 * N_RET_SLABS), st_spec,
                  pl.BlockSpec((2,) + consts.shape[1:], lambda p: (p, 0, 0))],
        out_specs=[pl.BlockSpec((ntok, b, LANES), lambda p: (0, 0, p)), st_spec],
        out_shape=[jax.ShapeDtypeStruct((ntok, b, RET_W), F32), jax.ShapeDtypeStruct(state.shape, F32)],
        scratch_shapes=[pltpu.VMEM((ntok, RET_DIM, b), F32)],
        compiler_params=_params(("arbitrary",)),
        name="sample_ret",
    )(qkv, qkv, qkv, state, consts)
    o_r = jnp.transpose(o_r, (1, 0, 2)).reshape(rows, RET_W)

    y = pl.pallas_call(
        _sample_out_kernel,
        grid=(rows // tile,),
        in_specs=[row_spec(d), row_spec(SWA_Q_W), row_spec(RET_W), row_spec(SAMPLE_FEAT), _full((LANES, LANES)),
                  _full((d, d))],
        out_specs=row_spec(d),
        out_shape=jax.ShapeDtypeStruct((rows, d), F32),
        compiler_params=_params(("arbitrary",)),
        name="sample_out",
    )(xf, o_a, o_r, feat, w['bd'], w['w_out'])
    return y, k_out, v_out, s_out


def _mem_kv(mem, w, tile):
    rows, d = mem.shape
    row_spec = lambda width: pl.BlockSpec((tile, width), lambda i: (i, 0))
    return pl.pallas_call(
        _mem_kv_kernel,
        grid=(rows // tile,),
        in_specs=[row_spec(d), _full((1, d)), _full((d, 2 * MEM_W)), _full((1, LANES))],
        out_specs=[pl.BlockSpec((tile * MEM_HEADS, MEM_HEAD_DIM), lambda i: (i, 0))] * 2,
        out_shape=[jax.ShapeDtypeStruct((rows * MEM_HEADS, MEM_HEAD_DIM), F32)] * 2,
        compiler_params=_params(("arbitrary",)),
        name="mem_kv",
    )(mem, w['norm_mem'], w['w_mkv'], w['kgm'])


def _prompt_cross_ffn(x, mk, mv, w, tile):
    rows, d = x.shape
    n = rows // tile
    per_mem = n // mk.shape[0]
    cur = lambda i: jnp.minimum(i, n - 1)
    single = lambda shape: pl.BlockSpec(shape, lambda i: (0,) * len(shape), pipeline_mode=pl.Buffered(1))
    mem_spec = pl.BlockSpec((1, MEM_LEN * MEM_HEADS, MEM_HEAD_DIM), lambda i: (cur(i) // per_mem, 0, 0))
    return pl.pallas_call(
        _prompt_cross_ffn_kernel,
        grid=(n + 1,),
        in_specs=[pl.BlockSpec((tile, d), lambda i: (cur(i), 0)), mem_spec, mem_spec, _full((1, d)),
                  single((d, MEM_W)), _full((1, LANES)), single((MEM_W, d)), _full((1, d)),
                  single((d, 2 * FFN_HIDDEN)), single((FFN_HIDDEN, d))],
        out_specs=pl.BlockSpec((tile, d), lambda i: (jnp.maximum(i - 1, 0), 0)),
        out_shape=jax.ShapeDtypeStruct((rows, d), F32),
        scratch_shapes=[pltpu.VMEM((tile, d), F32)],
        compiler_params=_params(("arbitrary",)),
        name="prompt_cross_ffn",
    )(x, mk, mv, w['norm_cross'], w['w_mq'], w['qgm'], w['w_mo'], w['norm_ffn'], w['w_gu'], w['w_down'])


def _sample_cross_attn(x, mk, mv, w, nseq, tq):
    rows, d = x.shape
    blk = nseq * tq
    mem_spec = pl.BlockSpec((nseq, MEM_LEN * MEM_HEADS, MEM_HEAD_DIM), lambda i: (i, 0, 0))
    return pl.pallas_call(
        functools.partial(_sample_cross_attn_kernel, nseq=nseq, tq=tq),
        grid=(rows // blk,),
        in_specs=[pl.BlockSpec((blk, d), lambda i: (i, 0)), mem_spec, mem_spec, _full((1, d)),
                  _full((d, MEM_W)), _full((1, LANES))],
        out_specs=pl.BlockSpec((blk, MEM_W), lambda i: (i, 0)),
        out_shape=jax.ShapeDtypeStruct((rows, MEM_W), F32),
        compiler_params=_params(("arbitrary",)),
        name="sample_cross_attn",
    )(x, mk, mv, w['norm_cross'], w['w_mq'], w['qgm'])


def _sample_cross_out_ffn(x, o, w, tile):
    rows, d = x.shape
    row_spec = lambda width: pl.BlockSpec((tile, width), lambda i: (i, 0))
    single = lambda shape: pl.BlockSpec(shape, lambda i: (0,) * len(shape), pipeline_mode=pl.Buffered(1))
    return pl.pallas_call(
        _cross_out_ffn_kernel,
        grid=(rows // tile,),
        in_specs=[row_spec(d), row_spec(MEM_W), single((MEM_W, d)), _full((1, d)),
                  single((d, 2 * FFN_HIDDEN)), single((FFN_HIDDEN, d))],
        out_specs=row_spec(d),
        out_shape=jax.ShapeDtypeStruct((rows, d), F32),
        compiler_params=_params(("arbitrary",)),
        name="sample_ffn",
    )(x, o, w['w_mo'], w['norm_ffn'], w['w_gu'], w['w_down'])


def kernel(x_prompt, x_sample, mem_prompt, cache_swa_k, cache_swa_v, state_ret, cache_mem_k, cache_mem_v,
           norm_mix, w_in, q_norm_a, k_norm_a, sinks, w_out, norm_cross, norm_mem, w_mq, w_mkv,
           q_norm_m, k_norm_m, w_mo, norm_ffn, w_gu, w_down):
    assert norm_mix.shape[0] == 1, "single-layer kernel"
    b, l, d = x_prompt.shape
    sb, st, _ = x_sample.shape
    half = (np.arange(LANES) // HEAD_DIM)[:, None] == (np.arange(LANES) // HEAD_DIM)[None, :]
    w = {
        'norm_mix': norm_mix, 'norm_cross': norm_cross, 'norm_mem': norm_mem, 'norm_ffn': norm_ffn,
        'w_out': w_out[0].astype(BF16), 'w_mq': w_mq[0].astype(BF16),
        'w_mkv': w_mkv[0].astype(BF16), 'w_mo': w_mo[0].astype(BF16), 'w_gu': w_gu[0].astype(BF16),
        'w_down': w_down[0].astype(BF16),
        'w_in_t': w_in[0].T.astype(BF16),
        'qg': jnp.tile(q_norm_a, (1, 2)), 'kg': jnp.tile(k_norm_a, (1, 2)),
        'qg_col': q_norm_a.reshape(HEAD_DIM, 1), 'kg_col': k_norm_a.reshape(HEAD_DIM, 1),
        'qgm': q_norm_m, 'kgm': k_norm_m, 'sinks': sinks[0],
        'bd': half.astype(BF16),
    }

    mk, mv = _mem_kv(mem_prompt.reshape(b * MEM_LEN, d), w, tile=256)
    xp, kwin, vwin, ret_p = _prompt_mixer(x_prompt, w, tile=1024, sub=512)
    xp = xp.reshape(b * l, d)
    mem_rows_p = (b, MEM_LEN * MEM_HEADS, MEM_HEAD_DIM)
    yp = _prompt_cross_ffn(xp, mk.reshape(mem_rows_p), mv.reshape(mem_rows_p), w, tile=512)
    yp = yp.reshape(b, l, d)
    ret_p = ret_p.reshape(1, b, RET_HEADS, RET_DIM, RET_DIM)

    nbuf = cache_swa_k.shape[2]
    assert nbuf == WINDOW
    to_feature_major = lambda c: jnp.transpose(c[0], (0, 2, 3, 1)).reshape(sb, SWA_KV_W, nbuf)
    from_feature_major = lambda c: jnp.transpose(c.reshape(sb, SWA_KV_HEADS, HEAD_DIM, nbuf), (0, 3, 1, 2))[None]
    xs, k_s, v_s, s_s = _sample_mixer(x_sample, to_feature_major(cache_swa_k), to_feature_major(cache_swa_v),
                                      jnp.transpose(state_ret[0], (1, 2, 3, 0)), w, tile=256, nseq=8)
    mem_rows = (sb, MEM_LEN * MEM_HEADS, MEM_HEAD_DIM)
    os_ = _sample_cross_attn(xs, cache_mem_k.reshape(mem_rows), cache_mem_v.reshape(mem_rows), w, nseq=8, tq=st)
    ys = _sample_cross_out_ffn(xs, os_, w, tile=512).reshape(sb, st, d)

    kv_shape = (1, b, WINDOW, SWA_KV_HEADS, HEAD_DIM)
    mem_shape = (1, b, MEM_LEN, MEM_HEADS, MEM_HEAD_DIM)
    return (yp, ys, kwin.reshape(kv_shape), vwin.reshape(kv_shape), ret_p,
            mk.reshape(mem_shape), mv.reshape(mem_shape),
            from_feature_major(k_s), from_feature_major(v_s),
            jnp.transpose(s_s, (3, 0, 1, 2))[None])
```

```python
import functools

import jax
import jax.numpy as jnp
import numpy as np
from jax import lax
from jax.experimental import pallas as pl
from jax.experimental.pallas import tpu as pltpu

F32 = jnp.float32
BF16 = jnp.bfloat16

LANES = 128
SUBLANES = 8
D_MODEL = 1024
HEAD_DIM = 64
SWA_HEADS = 8
SWA_KV_HEADS = 2
WINDOW = 128
RET_HEADS = 8
RET_DIM = 64
RET_CHUNK = 128
RET_THETA = 10000.0
ROPE_THETA = 10000.0
MEM_LEN = 256
MEM_HEADS = 4
MEM_HEAD_DIM = 128
MEM_W = MEM_HEADS * MEM_HEAD_DIM
FFN_HIDDEN = 2816
RMS_EPS = 1e-6
NEG_INF = -1e30
LOG2E = 1.4426950408889634
PAST_LEN = 16384

SWA_Q_W = SWA_HEADS * HEAD_DIM
SWA_KV_W = SWA_KV_HEADS * HEAD_DIM
RET_W = RET_HEADS * RET_DIM
IN_COLS = SWA_Q_W + 2 * SWA_KV_W + 4 * RET_W
C_QA, C_KA, C_VA = 0, SWA_Q_W, SWA_Q_W + SWA_KV_W
C_QR = SWA_Q_W + 2 * SWA_KV_W
C_KR, C_VR, C_G = C_QR + RET_W, C_QR + 2 * RET_W, C_QR + 3 * RET_W
C_GATE_S = C_QR
SAMPLE_FEAT = C_QR + RET_W
N_QA_SLABS = SWA_Q_W // LANES
N_RET_SLABS = RET_W // LANES

VMEM_LIMIT = 56 * 1024 * 1024
PROMPT_MIXER_TILE, PROMPT_MIXER_SUB = 1024, 512
PROMPT_FFN_TILE = 512
MEM_KV_TILE = 256
SAMPLE_TILE = 256
SAMPLE_FFN_TILE = 512
SAMPLE_SEQS = 8
STAGE_SKEW = 1


def _dot(a, b):
    return jnp.dot(a.astype(BF16), b.astype(BF16), preferred_element_type=F32)


def _dot_nt(a, b):
    return lax.dot_general(a.astype(BF16), b.astype(BF16), (((1,), (1,)), ((), ())),
                           preferred_element_type=F32)


def _rms(x):
    return x * lax.rsqrt(jnp.mean(x * x, axis=-1, keepdims=True) + RMS_EPS)


def _lane_consts():
    lane = lax.broadcasted_iota(jnp.int32, (1, LANES), 1)
    m_left = (lane < HEAD_DIM).astype(F32)
    m_right = 1.0 - m_left
    first_half = (lane % HEAD_DIM) < (HEAD_DIM // 2)
    return m_left, m_right, first_half


def _head_norm(y, bd):
    ss = jnp.dot((y * y).astype(BF16), bd, preferred_element_type=F32) * (1.0 / HEAD_DIM)
    return y * lax.rsqrt(ss + RMS_EPS)


def _rot_half(y, cos, sin_signed, first_half):
    swapped = jnp.where(first_half, pltpu.roll(y, LANES - HEAD_DIM // 2, 1), pltpu.roll(y, HEAD_DIM // 2, 1))
    return y * cos + swapped * sin_signed


def _dot_tn(a, b):
    return lax.dot_general(a.astype(BF16), b.astype(BF16), (((0,), (0,)), ((), ())),
                           preferred_element_type=F32)


def _head_norm_t(y):
    return y * lax.rsqrt(jnp.mean(y * y, axis=0, keepdims=True) + RMS_EPS)


def _rot_half_t(y, cos, sin):
    half = HEAD_DIM // 2
    y1, y2 = y[0:half], y[half:]
    return jnp.concatenate([y1 * cos - y2 * sin, y2 * cos + y1 * sin], axis=0)


def _rot_pairs_t(y, cos, sin_signed, even_row):
    n = y.shape[0]
    swapped = jnp.where(even_row, pltpu.roll(y, n - 1, 0), pltpu.roll(y, 1, 0))
    return y * cos + swapped * sin_signed


def _prompt_mixer_kernel(x_ref, ca_ref, sa_ref, cr_ref, sr_ref, nmix_ref, wint_ref, qg_ref, kg_ref,
                         sink_ref, wout_ref, dm_ref, qd_ref, kd_ref, cd_ref,
                         y_ref, kwin_ref, vwin_ref, sout_ref,
                         pk_ref, pv_ref, s_ref, *, tile, sub):
    t = pl.program_id(1)
    nblk = sub // WINDOW
    hd = HEAD_DIM
    group = SWA_HEADS // SWA_KV_HEADS

    @pl.when(t == 0)
    def _():
        pk_ref[...] = jnp.zeros_like(pk_ref)
        pv_ref[...] = jnp.zeros_like(pv_ref)
        s_ref[...] = jnp.zeros_like(s_ref)

    qg, kg = qg_ref[...], kg_ref[...]
    qs = qg * (hd ** -0.5 * LOG2E)
    head = lambda a, h: a[h * hd:(h + 1) * hd]
    chunk = lambda a, h, j: a[h * hd:(h + 1) * hd, j * RET_CHUNK:(j + 1) * RET_CHUNK]
    keys_of = lambda j: slice(j * WINDOW, (j + 2) * WINDOW)
    units = [(g, j) for j in range(nblk) for g in range(SWA_KV_HEADS)]
    key = lax.broadcasted_iota(jnp.int32, (2 * WINDOW, group * WINDOW), 0)
    qry = lax.broadcasted_iota(jnp.int32, (2 * WINDOW, group * WINDOW), 1) % WINDOW
    band = (key > qry) & (key <= qry + WINDOW)
    even_row = (lax.broadcasted_iota(jnp.int32, (RET_W, sub), 0) % 2) == 0
    zeros = jnp.zeros((hd, RET_CHUNK), F32)
    carry = {'pk': pk_ref[...], 'pv': pv_ref[...], 'state': [s_ref[h] for h in range(RET_HEADS)]}

    def stages(idx):
        toks = slice(idx * sub, (idx + 1) * sub)
        v = {}

        def project():
            v['x'] = x_ref[0, toks]
            hb = (_rms(v['x']) * nmix_ref[...]).astype(BF16)
            proj = lambda lo, hi: lax.dot_general(wint_ref[lo:hi, :], hb, (((1,), (1,)), ((), ())),
                                                  preferred_element_type=F32)
            v['swa'], v['ret'] = proj(C_QA, C_QR), proj(C_QR, C_VR)
            v['vr'], v['gate'] = proj(C_VR, C_G), proj(C_G, IN_COLS)

        def swa_prepare():
            ca, sa = ca_ref[:, toks], sa_ref[:, toks]
            swa_t = v['swa']
            v['qa'] = [_rot_half_t(_head_norm_t(head(swa_t, h)) * qs, ca, sa) for h in range(SWA_HEADS)]
            ka = jnp.concatenate([_rot_half_t(_head_norm_t(head(swa_t, SWA_HEADS + g)) * kg, ca, sa)
                                  for g in range(SWA_KV_HEADS)], axis=0)
            va = swa_t[C_VA:C_QR]
            v['kfull'] = jnp.concatenate([carry['pk'], ka], axis=1).astype(BF16)
            v['vfull'] = jnp.concatenate([carry['pv'], va], axis=1).astype(BF16)
            carry['pk'], carry['pv'] = ka[:, sub - WINDOW:], va[:, sub - WINDOW:]

        def swa_scores():
            v['scores'] = []
            for g, j in units:
                blk = slice(j * WINDOW, (j + 1) * WINDOW)
                q4 = jnp.concatenate([v['qa'][group * g + u][:, blk] for u in range(group)], axis=1)
                v['scores'].append(_dot_tn(head(v['kfull'], g)[:, keys_of(j)], q4))

        def ret_rotate():
            cr8 = jnp.concatenate([cr_ref[:, toks]] * RET_HEADS, axis=0)
            sr8 = jnp.concatenate([sr_ref[:, toks]] * RET_HEADS, axis=0)
            v['qr'] = _rot_pairs_t(v['ret'][0:RET_W], cr8, sr8, even_row)
            v['kr'] = _rot_pairs_t(v['ret'][RET_W:], cr8, sr8, even_row) * (RET_DIM ** -0.5)

        def ret_scores():
            qr, kr, vr = v['qr'], v['kr'], v['vr']
            v['inner'] = [[None] * nblk for _ in range(N_RET_SLABS)]
            for j in range(nblk):
                for pr in range(N_RET_SLABS):
                    q_bd = jnp.concatenate([jnp.concatenate([chunk(qr, 2 * pr, j), zeros], axis=1),
                                            jnp.concatenate([zeros, chunk(qr, 2 * pr + 1, j)], axis=1)], axis=0)
                    k2 = kr[pr * LANES:(pr + 1) * LANES, j * RET_CHUNK:(j + 1) * RET_CHUNK]
                    v['inner'][pr][j] = _dot_tn(k2, q_bd) * dm_ref[pr]
            v['incr'] = [[_dot_nt(chunk(vr, h, j), chunk(kr, h, j) * kd_ref[h]) for j in range(nblk)]
                         for h in range(RET_HEADS)]

        def swa_softmax():
            v['probs'], v['denoms'] = [], []
            for (g, j), s in zip(units, v['scores']):
                sink = sink_ref[g] * LOG2E
                valid = band & (key >= WINDOW * (1 - t)) if (idx == 0 and j == 0) else band
                s = jnp.where(valid, s, NEG_INF)
                m = jnp.maximum(jnp.max(s, axis=0, keepdims=True), sink)
                p = jnp.exp2(s - m)
                v['denoms'].append(jnp.sum(p, axis=0, keepdims=True) + jnp.exp2(sink - m))
                v['probs'].append(p.astype(BF16))

        def swa_values():
            v['oa'] = [[None] * nblk for _ in range(SWA_HEADS)]
            for (g, j), p, denom in zip(units, v['probs'], v['denoms']):
                o = jnp.dot(head(v['vfull'], g)[:, keys_of(j)], p, preferred_element_type=F32) / denom
                for u in range(group):
                    v['oa'][group * g + u][j] = o[:, u * WINDOW:(u + 1) * WINDOW]
            v['state'] = [[carry['state'][h]] for h in range(RET_HEADS)]
            for h in range(RET_HEADS):
                for j in range(nblk):
                    v['state'][h].append(v['state'][h][j] * cd_ref[h] + v['incr'][h][j])
                carry['state'][h] = v['state'][h][nblk]

        def ret_outputs():
            v['or'] = [[None] * nblk for _ in range(RET_HEADS)]
            for j in range(nblk):
                for h in range(RET_HEADS):
                    u = h % 2
                    lhs = jnp.concatenate([chunk(v['vr'], h, j), v['state'][h][j]], axis=1)
                    rhs = jnp.concatenate([v['inner'][h // 2][j][:, u * RET_CHUNK:(u + 1) * RET_CHUNK],
                                           chunk(v['qr'], h, j) * qd_ref[h]], axis=0)
                    v['or'][h][j] = _dot(lhs, rhs)

        def gate_mix():
            mix = [jnp.concatenate(blocks, axis=1) for blocks in v['oa']]
            for h in range(RET_HEADS):
                o_h = _head_norm_t(jnp.concatenate(v['or'][h], axis=1))
                g_h = head(v['gate'], h)
                mix.append(o_h * (g_h * jax.nn.sigmoid(g_h)))
            v['mix'] = jnp.concatenate(mix, axis=0).astype(BF16)

        def out_project():
            y_ref[0, toks] = v['x'] + lax.dot_general(v['mix'], wout_ref[...], (((0,), (0,)), ((), ())),
                                                      preferred_element_type=F32)

        return [project, swa_prepare, swa_scores, ret_rotate, ret_scores, swa_softmax, swa_values,
                ret_outputs, gate_mix, out_project]

    pipelines = [stages(i) for i in range(tile // sub)]
    nstage = len(pipelines[0])
    for slot in range(nstage + STAGE_SKEW * (len(pipelines) - 1)):
        for i, pipe in enumerate(pipelines):
            if 0 <= slot - STAGE_SKEW * i < nstage:
                pipe[slot - STAGE_SKEW * i]()

    pk_ref[...], pv_ref[...] = carry['pk'], carry['pv']
    kwin_ref[0], vwin_ref[0] = carry['pk'].T, carry['pv'].T
    for h in range(RET_HEADS):
        s_ref[h] = carry['state'][h]

    @pl.when(t == pl.num_programs(1) - 1)
    def _():
        for h in range(RET_HEADS):
            sout_ref[0, h] = carry['state'][h].T


def _sample_in_kernel(x_ref, ca_ref, sa_ref, nmix_ref, wint_ref, qg_ref, kg_ref, bd_ref, feat_ref):
    _, _, first_half = _lane_consts()
    ca, sa, bd = ca_ref[...], sa_ref[...], bd_ref[...]
    hb = (_rms(x_ref[...]) * nmix_ref[...]).astype(BF16)
    proj = lambda lo, hi: lax.dot_general(hb, wint_ref[lo:hi, :], (((1,), (1,)), ((), ())),
                                          preferred_element_type=F32)
    swa = proj(C_QA, C_QR)
    slabs = [_rot_half(_head_norm(swa[:, s * LANES:(s + 1) * LANES], bd) * qg_ref[...], ca, sa, first_half)
             for s in range(N_QA_SLABS)]
    slabs.append(_rot_half(_head_norm(swa[:, C_KA:C_VA], bd) * kg_ref[...], ca, sa, first_half))
    feat_ref[...] = jnp.concatenate(slabs + [swa[:, C_VA:C_QR], proj(C_G, IN_COLS)], axis=1)


def _sample_swa_kernel(feat_ref, ckt_ref, cvt_ref, sink_ref, o_ref, kout_ref, vout_ref,
                       kn_ref, vn_ref, *, nseq, ntok, unroll):
    @pl.when(pl.program_id(0) == 0)
    def _():
        kn_ref[...] = jnp.zeros_like(kn_ref)
        vn_ref[...] = jnp.zeros_like(vn_ref)

    m_left, m_right, _ = _lane_consts()
    tail = WINDOW - ntok
    nrow = SWA_HEADS * ntok
    row_tok = lax.broadcasted_iota(jnp.int32, (nrow, WINDOW), 0) % ntok
    col = lax.broadcasted_iota(jnp.int32, (nrow, WINDOW), 1)
    valid_cache = col > row_tok
    valid_new = (col >= tail) & (col - tail <= row_tok)
    in_tail = lax.broadcasted_iota(jnp.int32, (LANES, WINDOW), 1) >= tail
    sink = sink_ref[...]
    shift = lambda a: pltpu.roll(a, HEAD_DIM, 1)

    def body(step, carry):
        seqs = [step * unroll + u for u in range(unroll)]
        rows = [pl.ds(pl.multiple_of(i * ntok, ntok), ntok) for i in seqs]
        q_rows, caches = [], []
        for u, i in enumerate(seqs):
            slab = lambda c, u=u: feat_ref[rows[u], c:c + LANES]
            kn_ref[u, tail:WINDOW] = slab(C_KA)
            vn_ref[u, tail:WINDOW] = slab(C_VA)
            s0, s1, s2, s3 = [slab(C_QA + s * LANES) for s in range(N_QA_SLABS)]
            q_rows.append(jnp.concatenate(
                [s0 * m_left, shift(s0 * m_right), s1 * m_left, shift(s1 * m_right),
                 shift(s2 * m_left), s2 * m_right, shift(s3 * m_left), s3 * m_right], axis=0))
            caches.append((ckt_ref[i], cvt_ref[i], kn_ref[u], vn_ref[u]))
        scores = [(_dot(q, kt), _dot_nt(q, kn)) for q, (kt, _, kn, _) in zip(q_rows, caches)]
        probs = []
        for s_c, s_n in scores:
            s_c = jnp.where(valid_cache, s_c * (HEAD_DIM ** -0.5), NEG_INF)
            s_n = jnp.where(valid_new, s_n * (HEAD_DIM ** -0.5), NEG_INF)
            m = jnp.maximum(jnp.maximum(jnp.max(s_c, axis=-1, keepdims=True),
                                        jnp.max(s_n, axis=-1, keepdims=True)), sink)
            p_c, p_n = jnp.exp(s_c - m), jnp.exp(s_n - m)
            denom = (jnp.sum(p_c, axis=-1, keepdims=True) + jnp.sum(p_n, axis=-1, keepdims=True)
                     + jnp.exp(sink - m))
            probs.append((p_c, p_n, denom))
        for u, i in enumerate(seqs):
            kt, vt, kn, vn = caches[u]
            p_c, p_n, denom = probs[u]
            o = (_dot_nt(p_c, vt) + _dot(p_n, vn)) / denom
            n = ntok
            left = lambda h: o[h * n:(h + 1) * n] * m_left
            right = lambda h: o[h * n:(h + 1) * n] * m_right
            o_ref[rows[u], 0:LANES] = left(0) + shift(left(1))
            o_ref[rows[u], LANES:2 * LANES] = left(2) + shift(left(3))
            o_ref[rows[u], 2 * LANES:3 * LANES] = shift(right(4)) + right(5)
            o_ref[rows[u], 3 * LANES:4 * LANES] = shift(right(6)) + right(7)
            kout_ref[i] = jnp.where(in_tail, kn.T, pltpu.roll(kt, tail, 1))
            vout_ref[i] = jnp.where(in_tail, vn.T, pltpu.roll(vt, tail, 1))
        return carry

    lax.fori_loop(0, nseq // unroll, body, 0)


def _sample_ret_in_kernel(x_ref, cr_ref, sr_ref, nmix_ref, wint_ref, qkv_ref):
    hb = (_rms(x_ref[...]) * nmix_ref[...]).astype(BF16)
    qkv = lax.dot_general(wint_ref[C_QR:C_G, :], hb, (((1,), (1,)), ((), ())), preferred_element_type=F32)
    n = qkv.shape[1]
    even_row = (lax.broadcasted_iota(jnp.int32, (RET_W, n), 0) % 2) == 0
    cr8 = jnp.concatenate([cr_ref[...]] * RET_HEADS, axis=0)
    sr8 = jnp.concatenate([sr_ref[...]] * RET_HEADS, axis=0)
    qkv_ref[0:RET_W] = _rot_pairs_t(qkv[0:RET_W], cr8, sr8, even_row)
    qkv_ref[RET_W:2 * RET_W] = _rot_pairs_t(qkv[RET_W:2 * RET_W], cr8, sr8, even_row) * (RET_DIM ** -0.5)
    qkv_ref[2 * RET_W:] = qkv[2 * RET_W:]


def _sample_ret_kernel(q_ref, k_ref, v_ref, s_ref, c_ref, o_ref, so_ref, ks_ref, *, ntok):
    hd, nseq = RET_DIM, LANES
    tok = lambda t: slice(t * nseq, (t + 1) * nseq)
    row_qd, row_kd, row_cd = ntok * ntok, ntok * ntok + ntok, ntok * ntok + 2 * ntok
    group = SUBLANES
    pair_outs = []
    for u in range(2):
        feats = slice(u * hd, (u + 1) * hd)
        const = lambda r: c_ref[u, r:r + 1, :]
        for t in range(ntok):
            ks_ref[t] = k_ref[feats, tok(t)] * const(row_kd + t)
        outs = []
        for t in range(ntok):
            q_t = q_ref[feats, tok(t)]
            acc = jnp.zeros((hd, nseq), F32)
            for k in range(t + 1):
                w_tk = jnp.sum(q_t * k_ref[feats, tok(k)], axis=0, keepdims=True) * const(t * ntok + k)
                acc = acc + w_tk * v_ref[feats, tok(k)]

            def cross(j, carry, t=t):
                d0 = pl.multiple_of(j * group, group)
                q_rows = q_ref[pl.ds(u * hd + d0, group), tok(t)]
                for i in range(group):
                    carry = carry + q_rows[i:i + 1, :] * s_ref[u, d0 + i]
                return carry
            carried = lax.fori_loop(0, hd // group, cross, jnp.zeros((hd, nseq), F32))
            outs.append(acc + carried * const(row_qd + t))

        def update(j, carry):
            d0 = pl.multiple_of(j * group, group)
            k_rows = [ks_ref[t, pl.ds(d0, group), :] for t in range(ntok)]
            for i in range(group):
                new = s_ref[u, d0 + i] * const(row_cd)
                for t in range(ntok):
                    new = new + k_rows[t][i:i + 1, :] * v_ref[feats, tok(t)]
                so_ref[u, d0 + i] = new
            return carry
        lax.fori_loop(0, hd // group, update, 0)
        pair_outs.append(outs)
    for t in range(ntok):
        o_ref[t] = jnp.concatenate([pair_outs[0][t], pair_outs[1][t]], axis=0).T


def _sample_out_kernel(x_ref, oa_ref, or_ref, feat_ref, bd_ref, wout_ref, y_ref):
    outs = [oa_ref[...]]
    for p in range(N_RET_SLABS):
        gate = feat_ref[:, C_GATE_S + p * LANES:C_GATE_S + (p + 1) * LANES]
        o_ret = or_ref[:, p * LANES:(p + 1) * LANES]
        outs.append(_head_norm(o_ret, bd_ref[...]) * (gate * jax.nn.sigmoid(gate)))
    mix = jnp.concatenate(outs, axis=1).astype(BF16)
    y_ref[...] = x_ref[...] + jnp.dot(mix, wout_ref[...], preferred_element_type=F32)


def _mem_kv_kernel(m_ref, nmem_ref, wkv_ref, kg_ref, k_ref, v_ref):
    hb = (_rms(m_ref[...]) * nmem_ref[...]).astype(BF16)
    kv = jnp.dot(hb, wkv_ref[...], preferred_element_type=F32)
    slots = m_ref.shape[0]
    for h in range(MEM_HEADS):
        rows = pl.ds(h, slots, stride=MEM_HEADS)
        k_ref[rows, :] = _rms(kv[:, h * LANES:(h + 1) * LANES]) * kg_ref[...]
        v_ref[rows, :] = kv[:, MEM_W + h * LANES:MEM_W + (h + 1) * LANES]


def _mem_queries(x, ncross, wq_ref, qg):
    hb = (_rms(x) * ncross).astype(BF16)
    q = jnp.dot(hb, wq_ref[...], preferred_element_type=F32)
    qs = qg * (MEM_HEAD_DIM ** -0.5 * LOG2E)
    return [_rms(q[:, h * LANES:(h + 1) * LANES]) * qs for h in range(MEM_HEADS)]


def _prompt_cross_ffn_kernel(x_ref, mk_ref, mv_ref, ncross_ref, wq_ref, qg_ref, wo_ref, nffn_ref, wgu_ref,
                             wdown_ref, y_ref, xa_ref):
    @pl.when(pl.program_id(0) == 0)
    def _():
        xa_ref[...] = jnp.zeros_like(xa_ref)

    xa = xa_ref[...]
    hb = (_rms(xa) * nffn_ref[...]).astype(BF16)
    x = x_ref[...]
    qn = _mem_queries(x, ncross_ref[...], wq_ref, qg_ref[...])
    g = jnp.dot(hb, wgu_ref[:, 0:FFN_HIDDEN], preferred_element_type=F32)
    head = lambda ref, h: ref[0, pl.ds(h, MEM_LEN, stride=MEM_HEADS), :].astype(BF16)
    scores = [_dot_nt(head(mk_ref, h), qn[h]) for h in range(MEM_HEADS)]
    u = jnp.dot(hb, wgu_ref[:, FFN_HIDDEN:], preferred_element_type=F32)
    probs, denoms = [], []
    for s in scores:
        p = jnp.exp2(s - jnp.max(s, axis=0, keepdims=True))
        denoms.append(jnp.sum(p, axis=0, keepdims=True))
        probs.append(p.astype(BF16))
    act = (g * jax.nn.sigmoid(g) * u).astype(BF16)
    o_t = jnp.concatenate([_dot_tn(head(mv_ref, h), probs[h]) / denoms[h] for h in range(MEM_HEADS)], axis=0)
    y_ref[...] = xa + jnp.dot(act, wdown_ref[...], preferred_element_type=F32)
    xa_ref[...] = x + lax.dot_general(o_t.astype(BF16), wo_ref[...], (((0,), (0,)), ((), ())),
                                      preferred_element_type=F32)


def _sample_cross_attn_kernel(x_ref, mk_ref, mv_ref, ncross_ref, wq_ref, qg_ref, o_ref, *, nseq, tq):
    qn = _mem_queries(x_ref[...], ncross_ref[...], wq_ref, qg_ref[...])
    nrow = MEM_HEADS * tq
    nmem = MEM_LEN * MEM_HEADS
    own = (lax.broadcasted_iota(jnp.int32, (nrow, nmem), 0) // tq
           == lax.broadcasted_iota(jnp.int32, (nrow, nmem), 1) % MEM_HEADS)
    scores = []
    for i in range(nseq):
        q_stack = jnp.concatenate([qn[h][i * tq:(i + 1) * tq] for h in range(MEM_HEADS)], axis=0)
        scores.append(_dot_nt(q_stack, mk_ref[i]))
    probs, denoms = [], []
    for s in scores:
        s = jnp.where(own, s, NEG_INF)
        p = jnp.exp2(s - jnp.max(s, axis=-1, keepdims=True))
        denoms.append(jnp.sum(p, axis=-1, keepdims=True))
        probs.append(p.astype(BF16))
    for i in range(nseq):
        o = jnp.dot(probs[i], mv_ref[i].astype(BF16), preferred_element_type=F32) / denoms[i]
        for h in range(MEM_HEADS):
            o_ref[i * tq:(i + 1) * tq, h * LANES:(h + 1) * LANES] = o[h * tq:(h + 1) * tq]


def _cross_out_ffn_kernel(x_ref, o_ref, wo_ref, nffn_ref, wgu_ref, wdown_ref, y_ref):
    x = x_ref[...] + jnp.dot(o_ref[...].astype(BF16), wo_ref[...], preferred_element_type=F32)
    hb = (_rms(x) * nffn_ref[...]).astype(BF16)
    g = jnp.dot(hb, wgu_ref[:, 0:FFN_HIDDEN], preferred_element_type=F32)
    u = jnp.dot(hb, wgu_ref[:, FFN_HIDDEN:], preferred_element_type=F32)
    act = (g * jax.nn.sigmoid(g) * u).astype(BF16)
    y_ref[...] = x + jnp.dot(act, wdown_ref[...], preferred_element_type=F32)


def _f32(*arrays):
    return [np.ascontiguousarray(a, dtype=np.float32) for a in arrays]


def _rope_angles(pos):
    half = HEAD_DIM // 2
    inv = 1.0 / (ROPE_THETA ** (np.arange(half, dtype=np.float64) / half))
    return pos.astype(np.float64)[:, None] * inv[None, :]


def _ret_angles(pos):
    inv = RET_THETA ** (-np.linspace(0.0, 1.0, RET_DIM // 2, dtype=np.float64))
    return pos.astype(np.float64)[:, None] * inv[None, :]


def _rope_tables(pos):
    ang = _rope_angles(pos)
    cos, sin = np.cos(ang), np.sin(ang)
    c64 = np.concatenate([cos, cos], axis=-1)
    s64 = np.concatenate([-sin, sin], axis=-1)
    return _f32(np.tile(c64, (1, 2)), np.tile(s64, (1, 2)))


def _retention_decays(c):
    log_g = np.log(1.0 - np.exp2(-5.0 - np.arange(RET_HEADS, dtype=np.float64)))
    idx = np.arange(c, dtype=np.float64)
    diff = idx[:, None] - idx[None, :]
    dmat = np.where(diff >= 0, np.exp(np.maximum(diff, 0.0)[None] * log_g[:, None, None]), 0.0)
    qd = np.exp((idx + 1.0)[None, :] * log_g[:, None])
    kd = np.exp((c - 1.0 - idx)[None, :] * log_g[:, None])
    cd = np.exp(c * log_g)
    return dmat, qd, kd, cd


def _sample_decay_rows(c):
    dmat, qd, kd, cd = _retention_decays(c)
    rows = np.concatenate([dmat.reshape(RET_HEADS, c * c), qd, kd, cd[:, None]], axis=1)
    rows = np.pad(rows, ((0, 0), (0, -rows.shape[1] % 8)))
    return _f32(np.broadcast_to(rows[:, :, None], rows.shape + (LANES,)))[0]


def _full(shape):
    nd = len(shape)
    return pl.BlockSpec(shape, lambda *_: (0,) * nd)


def _params(sem):
    return pltpu.CompilerParams(dimension_semantics=sem, vmem_limit_bytes=VMEM_LIMIT)


def _prompt_tables_t(pos):
    ang = _rope_angles(pos).T
    ang_r = _ret_angles(pos).T
    cos_r, sin_r = np.cos(ang_r), np.sin(ang_r)
    cr = np.repeat(cos_r, 2, axis=0)
    sr = np.stack([-sin_r, sin_r], axis=1).reshape(RET_DIM, pos.shape[0])
    return _f32(np.cos(ang), np.sin(ang), cr, sr)


def _decay_consts_t(c):
    dmat, qd, kd, cd = _retention_decays(c)
    dm = dmat.transpose(0, 2, 1).reshape(N_RET_SLABS, 2, c, c).transpose(0, 2, 1, 3).reshape(N_RET_SLABS, c, 2 * c)
    cd = np.broadcast_to(cd[:, None, None], (RET_HEADS, 1, RET_DIM))
    return _f32(dm, qd[:, None, :], kd[:, None, :], cd)


def _prompt_mixer(x, w, tile, sub):
    b, l, d = x.shape
    ca, sa, cr, sr = _prompt_tables_t(np.arange(l, dtype=np.int32))
    dm, qd, kd, cd = _decay_consts_t(RET_CHUNK)
    sink = jnp.repeat(w['sinks'].reshape(SWA_KV_HEADS, 1, -1), WINDOW, axis=-1)
    tab = lambda rows: pl.BlockSpec((rows, tile), lambda i, t: (0, t))
    xspec = pl.BlockSpec((1, tile, d), lambda i, t: (i, t, 0))
    win_spec = pl.BlockSpec((1, WINDOW, LANES), lambda i, t: (i, 0, 0))
    st_spec = pl.BlockSpec((1, RET_HEADS, RET_DIM, RET_DIM), lambda i, t: (i, 0, 0, 0))
    return pl.pallas_call(
        functools.partial(_prompt_mixer_kernel, tile=tile, sub=sub),
        grid=(b, l // tile),
        in_specs=[xspec, tab(HEAD_DIM // 2), tab(HEAD_DIM // 2), tab(RET_DIM), tab(RET_DIM), _full((1, d)),
                  _full((IN_COLS, d)), _full((HEAD_DIM, 1)), _full((HEAD_DIM, 1)), _full(sink.shape),
                  _full((d, d)), _full(dm.shape), _full(qd.shape), _full(kd.shape), _full(cd.shape)],
        out_specs=[xspec, win_spec, win_spec, st_spec],
        out_shape=[jax.ShapeDtypeStruct(x.shape, F32),
                   jax.ShapeDtypeStruct((b, WINDOW, LANES), F32),
                   jax.ShapeDtypeStruct((b, WINDOW, LANES), F32),
                   jax.ShapeDtypeStruct((b, RET_HEADS, RET_DIM, RET_DIM), F32)],
        scratch_shapes=[pltpu.VMEM((SWA_KV_W, WINDOW), F32), pltpu.VMEM((SWA_KV_W, WINDOW), F32),
                        pltpu.VMEM((RET_HEADS, RET_DIM, RET_DIM), F32)],
        compiler_params=_params(("arbitrary", "arbitrary")),
        name="prompt_mixer",
    )(x, ca, sa, cr, sr, w['norm_mix'], w['w_in_t'], w['qg_col'], w['kg_col'], sink, w['w_out'], dm, qd, kd,
      cd)


def _sample_mixer(x, cache_k, cache_v, state, w, tile, nseq):
    b, ntok, d = x.shape
    rows = b * ntok
    xf = x.reshape(rows, d)
    pos = np.tile(PAST_LEN + np.arange(ntok, dtype=np.int32), b)
    ca, sa = _rope_tables(pos)
    tab = pl.BlockSpec((tile, LANES), lambda i: (i, 0))
    row_spec = lambda width: pl.BlockSpec((tile, width), lambda i: (i, 0))
    feat = pl.pallas_call(
        _sample_in_kernel,
        grid=(rows // tile,),
        in_specs=[row_spec(d), tab, tab, _full((1, d)), _full((IN_COLS, d)), _full((1, LANES)),
                  _full((1, LANES)), _full((LANES, LANES))],
        out_specs=row_spec(SAMPLE_FEAT),
        out_shape=jax.ShapeDtypeStruct((rows, SAMPLE_FEAT), F32),
        compiler_params=_params(("arbitrary",)),
        name="sample_in",
    )(xf, ca, sa, w['norm_mix'], w['w_in_t'], w['qg'], w['kg'], w['bd'])

    sink = jnp.repeat(w['sinks'], ntok)[:, None]
    seq_rows = nseq * ntok
    unroll = nseq
    cache_spec = pl.BlockSpec((nseq, LANES, WINDOW), lambda i: (i, 0, 0))
    o_a, k_out, v_out = pl.pallas_call(
        functools.partial(_sample_swa_kernel, nseq=nseq, ntok=ntok, unroll=unroll),
        grid=(b // nseq,),
        in_specs=[pl.BlockSpec((seq_rows, SAMPLE_FEAT), lambda i: (i, 0)), cache_spec, cache_spec,
                  _full(sink.shape)],
        out_specs=[pl.BlockSpec((seq_rows, SWA_Q_W), lambda i: (i, 0)), cache_spec, cache_spec],
        out_shape=[jax.ShapeDtypeStruct((rows, SWA_Q_W), F32),
                   jax.ShapeDtypeStruct(cache_k.shape, F32),
                   jax.ShapeDtypeStruct(cache_v.shape, F32)],
        scratch_shapes=[pltpu.VMEM((unroll, WINDOW, LANES), F32), pltpu.VMEM((unroll, WINDOW, LANES), F32)],
        compiler_params=_params(("arbitrary",)),
        name="sample_swa",
    )(feat, cache_k, cache_v, sink)

    assert b == LANES, "the retention step puts one sequence per lane"
    xt = jnp.transpose(x, (1, 0, 2)).reshape(rows, d)
    pos_t = np.repeat(PAST_LEN + np.arange(ntok, dtype=np.int32), b)
    _, _, cr_t, sr_t = _prompt_tables_t(pos_t)
    half = rows // 2
    tab_t = pl.BlockSpec((RET_DIM, half), lambda i: (0, i))
    qkv = pl.pallas_call(
        _sample_ret_in_kernel,
        grid=(2,),
        in_specs=[pl.BlockSpec((half, d), lambda i: (i, 0)), tab_t, tab_t, _full((1, d)),
                  _full((IN_COLS, d))],
        out_specs=pl.BlockSpec((3 * RET_W, half), lambda i: (0, i)),
        out_shape=jax.ShapeDtypeStruct((3 * RET_W, rows), F32),
        compiler_params=_params(("arbitrary",)),
        name="sample_ret_in",
    )(xt, cr_t, sr_t, w['norm_mix'], w['w_in_t'])
    consts = _sample_decay_rows(ntok)
    pair_rows = lambda off: pl.BlockSpec((LANES, rows), lambda p: (off + p, 0))
    st_spec = pl.BlockSpec((2, RET_DIM, RET_DIM, b), lambda p: (p, 0, 0, 0))
    o_r, s_out = pl.pallas_call(
        functools.partial(_sample_ret_kernel, ntok=ntok),
        grid=(N_RET_SLABS,),
        in_specs=[pair_rows(0), pair_rows(N_RET_SLABS), pair_rows(2 * N_RET_SLABS), st_spec,
                  pl.BlockSpec((2,) + consts.shape[1:], lambda p: (p, 0, 0))],
        out_specs=[pl.BlockSpec((ntok, b, LANES), lambda p: (0, 0, p)), st_spec],
        out_shape=[jax.ShapeDtypeStruct((ntok, b, RET_W), F32), jax.ShapeDtypeStruct(state.shape, F32)],
        scratch_shapes=[pltpu.VMEM((ntok, RET_DIM, b), F32)],
        compiler_params=_params(("arbitrary",)),
        name="sample_ret",
    )(qkv, qkv, qkv, state, consts)
    o_r = jnp.transpose(o_r, (1, 0, 2)).reshape(rows, RET_W)

    y = pl.pallas_call(
        _sample_out_kernel,
        grid=(rows // tile,),
        in_specs=[row_spec(d), row_spec(SWA_Q_W), row_spec(RET_W), row_spec(SAMPLE_FEAT), _full((LANES, LANES)),
                  _full((d, d))],
        out_specs=row_spec(d),
        out_shape=jax.ShapeDtypeStruct((rows, d), F32),
        compiler_params=_params(("arbitrary",)),
        name="sample_out",
    )(xf, o_a, o_r, feat, w['bd'], w['w_out'])
    return y, k_out, v_out, s_out


def _mem_kv(mem, w, tile):
    rows, d = mem.shape
    row_spec = lambda width: pl.BlockSpec((tile, width), lambda i: (i, 0))
    return pl.pallas_call(
        _mem_kv_kernel,
        grid=(rows // tile,),
        in_specs=[row_spec(d), _full((1, d)), _full((d, 2 * MEM_W)), _full((1, LANES))],
        out_specs=[pl.BlockSpec((tile * MEM_HEADS, MEM_HEAD_DIM), lambda i: (i, 0))] * 2,
        out_shape=[jax.ShapeDtypeStruct((rows * MEM_HEADS, MEM_HEAD_DIM), F32)] * 2,
        compiler_params=_params(("arbitrary",)),
        name="mem_kv",
    )(mem, w['norm_mem'], w['w_mkv'], w['kgm'])


def _prompt_cross_ffn(x, mk, mv, w, tile):
    rows, d = x.shape
    n = rows // tile
    per_mem = n // mk.shape[0]
    cur = lambda i: jnp.minimum(i, n - 1)
    single = lambda shape: pl.BlockSpec(shape, lambda i: (0,) * len(shape), pipeline_mode=pl.Buffered(1))
    mem_spec = pl.BlockSpec((1, MEM_LEN * MEM_HEADS, MEM_HEAD_DIM), lambda i: (cur(i) // per_mem, 0, 0))
    return pl.pallas_call(
        _prompt_cross_ffn_kernel,
        grid=(n + 1,),
        in_specs=[pl.BlockSpec((tile, d), lambda i: (cur(i), 0)), mem_spec, mem_spec, _full((1, d)),
                  single((d, MEM_W)), _full((1, LANES)), single((MEM_W, d)), _full((1, d)),
                  single((d, 2 * FFN_HIDDEN)), single((FFN_HIDDEN, d))],
        out_specs=pl.BlockSpec((tile, d), lambda i: (jnp.maximum(i - 1, 0), 0)),
        out_shape=jax.ShapeDtypeStruct((rows, d), F32),
        scratch_shapes=[pltpu.VMEM((tile, d), F32)],
        compiler_params=_params(("arbitrary",)),
        name="prompt_cross_ffn",
    )(x, mk, mv, w['norm_cross'], w['w_mq'], w['qgm'], w['w_mo'], w['norm_ffn'], w['w_gu'], w['w_down'])


def _sample_cross_attn(x, mk, mv, w, nseq, tq):
    rows, d = x.shape
    blk = nseq * tq
    mem_spec = pl.BlockSpec((nseq, MEM_LEN * MEM_HEADS, MEM_HEAD_DIM), lambda i: (i, 0, 0))
    return pl.pallas_call(
        functools.partial(_sample_cross_attn_kernel, nseq=nseq, tq=tq),
        grid=(rows // blk,),
        in_specs=[pl.BlockSpec((blk, d), lambda i: (i, 0)), mem_spec, mem_spec, _full((1, d)),
                  _full((d, MEM_W)), _full((1, LANES))],
        out_specs=pl.BlockSpec((blk, MEM_W), lambda i: (i, 0)),
        out_shape=jax.ShapeDtypeStruct((rows, MEM_W), F32),
        compiler_params=_params(("arbitrary",)),
        name="sample_cross_attn",
    )(x, mk, mv, w['norm_cross'], w['w_mq'], w['qgm'])


def _sample_cross_out_ffn(x, o, w, tile):
    rows, d = x.shape
    row_spec = lambda width: pl.BlockSpec((tile, width), lambda i: (i, 0))
    single = lambda shape: pl.BlockSpec(shape, lambda i: (0,) * len(shape), pipeline_mode=pl.Buffered(1))
    return pl.pallas_call(
        _cross_out_ffn_kernel,
        grid=(rows // tile,),
        in_specs=[row_spec(d), row_spec(MEM_W), single((MEM_W, d)), _full((1, d)),
                  single((d, 2 * FFN_HIDDEN)), single((FFN_HIDDEN, d))],
        out_specs=row_spec(d),
        out_shape=jax.ShapeDtypeStruct((rows, d), F32),
        compiler_params=_params(("arbitrary",)),
        name="sample_ffn",
    )(x, o, w['w_mo'], w['norm_ffn'], w['w_gu'], w['w_down'])


def kernel(x_prompt, x_sample, mem_prompt, cache_swa_k, cache_swa_v, state_ret, cache_mem_k, cache_mem_v,
           norm_mix, w_in, q_norm_a, k_norm_a, sinks, w_out, norm_cross, norm_mem, w_mq, w_mkv,
           q_norm_m, k_norm_m, w_mo, norm_ffn, w_gu, w_down):
    assert norm_mix.shape[0] == 1, "single-layer kernel"
    b, l, d = x_prompt.shape
    sb, st, _ = x_sample.shape
    half = (np.arange(LANES) // HEAD_DIM)[:, None] == (np.arange(LANES) // HEAD_DIM)[None, :]
    w = {
        'norm_mix': norm_mix, 'norm_cross': norm_cross, 'norm_mem': norm_mem, 'norm_ffn': norm_ffn,
        'w_out': w_out[0].astype(BF16), 'w_mq': w_mq[0].astype(BF16),
        'w_mkv': w_mkv[0].astype(BF16), 'w_mo': w_mo[0].astype(BF16), 'w_gu': w_gu[0].astype(BF16),
        'w_down': w_down[0].astype(BF16),
        'w_in_t': w_in[0].T.astype(BF16),
        'qg': jnp.tile(q_norm_a, (1, 2)), 'kg': jnp.tile(k_norm_a, (1, 2)),
        'qg_col': q_norm_a.reshape(HEAD_DIM, 1), 'kg_col': k_norm_a.reshape(HEAD_DIM, 1),
        'qgm': q_norm_m, 'kgm': k_norm_m, 'sinks': sinks[0],
        'bd': half.astype(BF16),
    }

    mk, mv = _mem_kv(mem_prompt.reshape(b * MEM_LEN, d), w, tile=MEM_KV_TILE)
    xp, kwin, vwin, ret_p = _prompt_mixer(x_prompt, w, tile=PROMPT_MIXER_TILE, sub=PROMPT_MIXER_SUB)
    xp = xp.reshape(b * l, d)
    mem_rows_p = (b, MEM_LEN * MEM_HEADS, MEM_HEAD_DIM)
    yp = _prompt_cross_ffn(xp, mk.reshape(mem_rows_p), mv.reshape(mem_rows_p), w, tile=PROMPT_FFN_TILE)
    yp = yp.reshape(b, l, d)
    ret_p = ret_p.reshape(1, b, RET_HEADS, RET_DIM, RET_DIM)

    nbuf = cache_swa_k.shape[2]
    assert nbuf == WINDOW
    to_feature_major = lambda c: jnp.transpose(c[0], (0, 2, 3, 1)).reshape(sb, SWA_KV_W, nbuf)
    from_feature_major = lambda c: jnp.transpose(c.reshape(sb, SWA_KV_HEADS, HEAD_DIM, nbuf), (0, 3, 1, 2))[None]
    xs, k_s, v_s, s_s = _sample_mixer(x_sample, to_feature_major(cache_swa_k), to_feature_major(cache_swa_v),
                                      jnp.transpose(state_ret[0], (1, 2, 3, 0)), w, tile=SAMPLE_TILE,
                                      nseq=SAMPLE_SEQS)
    mem_rows = (sb, MEM_LEN * MEM_HEADS, MEM_HEAD_DIM)
    os_ = _sample_cross_attn(xs, cache_mem_k.reshape(mem_rows), cache_mem_v.reshape(mem_rows), w,
                             nseq=SAMPLE_SEQS, tq=st)
    ys = _sample_cross_out_ffn(xs, os_, w, tile=SAMPLE_FFN_TILE).reshape(sb, st, d)

    kv_shape = (1, b, WINDOW, SWA_KV_HEADS, HEAD_DIM)
    mem_shape = (1, b, MEM_LEN, MEM_HEADS, MEM_HEAD_DIM)
    return (yp, ys, kwin.reshape(kv_shape), vwin.reshape(kv_shape), ret_p,
            mk.reshape(mem_shape), mv.reshape(mem_shape),
            from_feature_major(k_s), from_feature_major(v_s),
            jnp.transpose(s_s, (3, 0, 1, 2))[None])
```

```python
import functools

import jax
import jax.numpy as jnp
import numpy as np
from jax import lax
from jax.experimental import pallas as pl
from jax.experimental.pallas import tpu as pltpu

F32 = jnp.float32
BF16 = jnp.bfloat16

LANES = 128
SUBLANES = 8
D_MODEL = 1024
HEAD_DIM = 64
SWA_HEADS = 8
SWA_KV_HEADS = 2
WINDOW = 128
RET_HEADS = 8
RET_DIM = 64
RET_CHUNK = 128
RET_THETA = 10000.0
ROPE_THETA = 10000.0
MEM_LEN = 256
MEM_HEADS = 4
MEM_HEAD_DIM = 128
MEM_W = MEM_HEADS * MEM_HEAD_DIM
FFN_HIDDEN = 2816
RMS_EPS = 1e-6
NEG_INF = -1e30
LOG2E = 1.4426950408889634
PAST_LEN = 16384

SWA_Q_W = SWA_HEADS * HEAD_DIM
SWA_KV_W = SWA_KV_HEADS * HEAD_DIM
RET_W = RET_HEADS * RET_DIM
IN_COLS = SWA_Q_W + 2 * SWA_KV_W + 4 * RET_W
C_QA, C_KA, C_VA = 0, SWA_Q_W, SWA_Q_W + SWA_KV_W
C_QR = SWA_Q_W + 2 * SWA_KV_W
C_KR, C_VR, C_G = C_QR + RET_W, C_QR + 2 * RET_W, C_QR + 3 * RET_W
C_GATE_S = C_QR
SAMPLE_FEAT = C_QR + RET_W
N_QA_SLABS = SWA_Q_W // LANES
N_RET_SLABS = RET_W // LANES

VMEM_LIMIT = 56 * 1024 * 1024
PROMPT_MIXER_TILE, PROMPT_MIXER_SUB = 1024, 512
PROMPT_FFN_TILE = 512
MEM_KV_TILE = 256
SAMPLE_TILE = 256
SAMPLE_FFN_TILE = 512
SAMPLE_SEQS = 8
STAGE_SKEW = 1


def _dot(a, b):
    return jnp.dot(a.astype(BF16), b.astype(BF16), preferred_element_type=F32)


def _dot_nt(a, b):
    return lax.dot_general(a.astype(BF16), b.astype(BF16), (((1,), (1,)), ((), ())),
                           preferred_element_type=F32)


def _rms(x):
    return x * lax.rsqrt(jnp.mean(x * x, axis=-1, keepdims=True) + RMS_EPS)


def _lane_consts():
    lane = lax.broadcasted_iota(jnp.int32, (1, LANES), 1)
    m_left = (lane < HEAD_DIM).astype(F32)
    m_right = 1.0 - m_left
    first_half = (lane % HEAD_DIM) < (HEAD_DIM // 2)
    return m_left, m_right, first_half


def _head_norm(y, bd):
    ss = jnp.dot((y * y).astype(BF16), bd, preferred_element_type=F32) * (1.0 / HEAD_DIM)
    return y * lax.rsqrt(ss + RMS_EPS)


def _rot_half(y, cos, sin_signed, first_half):
    swapped = jnp.where(first_half, pltpu.roll(y, LANES - HEAD_DIM // 2, 1), pltpu.roll(y, HEAD_DIM // 2, 1))
    return y * cos + swapped * sin_signed


def _dot_tn(a, b):
    return lax.dot_general(a.astype(BF16), b.astype(BF16), (((0,), (0,)), ((), ())),
                           preferred_element_type=F32)


def _head_norm_t(y):
    return y * lax.rsqrt(jnp.mean(y * y, axis=0, keepdims=True) + RMS_EPS)


def _rot_half_t(y, cos, sin):
    half = HEAD_DIM // 2
    y1, y2 = y[0:half], y[half:]
    return jnp.concatenate([y1 * cos - y2 * sin, y2 * cos + y1 * sin], axis=0)


def _rot_pairs_t(y, cos, sin_signed, even_row):
    n = y.shape[0]
    swapped = jnp.where(even_row, pltpu.roll(y, n - 1, 0), pltpu.roll(y, 1, 0))
    return y * cos + swapped * sin_signed


def _prompt_mixer_kernel(x_ref, ca_ref, sa_ref, cr_ref, sr_ref, nmix_ref, wint_ref, qg_ref, kg_ref,
                         sink_ref, wout_ref, dm_ref, qd_ref, kd_ref, cd_ref,
                         y_ref, kwin_ref, vwin_ref, sout_ref,
                         pk_ref, pv_ref, s_ref, *, tile, sub):
    t = pl.program_id(1)
    nblk = sub // WINDOW
    hd = HEAD_DIM
    group = SWA_HEADS // SWA_KV_HEADS

    @pl.when(t == 0)
    def _():
        pk_ref[...] = jnp.zeros_like(pk_ref)
        pv_ref[...] = jnp.zeros_like(pv_ref)
        s_ref[...] = jnp.zeros_like(s_ref)

    qg, kg = qg_ref[...], kg_ref[...]
    qs = qg * (hd ** -0.5 * LOG2E)
    head = lambda a, h: a[h * hd:(h + 1) * hd]
    chunk = lambda a, h, j: a[h * hd:(h + 1) * hd, j * RET_CHUNK:(j + 1) * RET_CHUNK]
    keys_of = lambda j: slice(j * WINDOW, (j + 2) * WINDOW)
    units = [(g, j) for j in range(nblk) for g in range(SWA_KV_HEADS)]
    key = lax.broadcasted_iota(jnp.int32, (2 * WINDOW, group * WINDOW), 0)
    qry = lax.broadcasted_iota(jnp.int32, (2 * WINDOW, group * WINDOW), 1) % WINDOW
    band = (key > qry) & (key <= qry + WINDOW)
    even_row = (lax.broadcasted_iota(jnp.int32, (RET_DIM, sub), 0) % 2) == 0
    zeros = jnp.zeros((hd, RET_CHUNK), F32)
    carry = {'pk': pk_ref[...], 'pv': pv_ref[...], 'state': [s_ref[h] for h in range(RET_HEADS)]}

    def stages(idx):
        toks = slice(idx * sub, (idx + 1) * sub)
        v = {}

        def project():
            v['x'] = x_ref[0, toks]
            hb = (_rms(v['x']) * nmix_ref[...]).astype(BF16)
            proj = lambda lo, hi: lax.dot_general(wint_ref[lo:hi, :], hb, (((1,), (1,)), ((), ())),
                                                  preferred_element_type=F32)
            v['swa'], v['ret'] = proj(C_QA, C_QR), proj(C_QR, C_VR)
            v['vr'], v['gate'] = proj(C_VR, C_G), proj(C_G, IN_COLS)

        def swa_prepare():
            ca, sa = ca_ref[:, toks], sa_ref[:, toks]
            swa_t = v['swa']
            v['qa'] = [_rot_half_t(_head_norm_t(head(swa_t, h)) * qs, ca, sa) for h in range(SWA_HEADS)]
            ka = jnp.concatenate([_rot_half_t(_head_norm_t(head(swa_t, SWA_HEADS + g)) * kg, ca, sa)
                                  for g in range(SWA_KV_HEADS)], axis=0)
            va = swa_t[C_VA:C_QR]
            v['kfull'] = jnp.concatenate([carry['pk'], ka], axis=1).astype(BF16)
            v['vfull'] = jnp.concatenate([carry['pv'], va], axis=1).astype(BF16)
            carry['pk'], carry['pv'] = ka[:, sub - WINDOW:], va[:, sub - WINDOW:]

        def swa_scores():
            v['scores'] = []
            for g, j in units:
                blk = slice(j * WINDOW, (j + 1) * WINDOW)
                q4 = jnp.concatenate([v['qa'][group * g + u][:, blk] for u in range(group)], axis=1)
                v['scores'].append(_dot_tn(head(v['kfull'], g)[:, keys_of(j)], q4))

        def ret_rotate():
            cr, sr = cr_ref[:, toks], sr_ref[:, toks]
            rot = lambda a: jnp.concatenate([_rot_pairs_t(head(a, h), cr, sr, even_row)
                                             for h in range(RET_HEADS)], axis=0)
            v['qr'] = rot(v['ret'][0:RET_W])
            v['kr'] = rot(v['ret'][RET_W:]) * (RET_DIM ** -0.5)

        def ret_scores():
            qr, kr, vr = v['qr'], v['kr'], v['vr']
            v['inner'] = [[None] * nblk for _ in range(N_RET_SLABS)]
            for j in range(nblk):
                for pr in range(N_RET_SLABS):
                    q_bd = jnp.concatenate([jnp.concatenate([chunk(qr, 2 * pr, j), zeros], axis=1),
                                            jnp.concatenate([zeros, chunk(qr, 2 * pr + 1, j)], axis=1)], axis=0)
                    k2 = kr[pr * LANES:(pr + 1) * LANES, j * RET_CHUNK:(j + 1) * RET_CHUNK]
                    v['inner'][pr][j] = _dot_tn(k2, q_bd) * dm_ref[pr]
            v['incr'] = [[_dot_nt(chunk(vr, h, j), chunk(kr, h, j) * kd_ref[h]) for j in range(nblk)]
                         for h in range(RET_HEADS)]

        def swa_softmax():
            v['probs'], v['denoms'] = [], []
            for (g, j), s in zip(units, v['scores']):
                sink = sink_ref[g] * LOG2E
                valid = band & (key >= WINDOW * (1 - t)) if (idx == 0 and j == 0) else band
                s = jnp.where(valid, s, NEG_INF)
                m = jnp.maximum(jnp.max(s, axis=0, keepdims=True), sink)
                p = jnp.exp2(s - m)
                v['denoms'].append(jnp.sum(p, axis=0, keepdims=True) + jnp.exp2(sink - m))
                v['probs'].append(p.astype(BF16))

        def swa_values():
            v['oa'] = [[None] * nblk for _ in range(SWA_HEADS)]
            for (g, j), p, denom in zip(units, v['probs'], v['denoms']):
                o = jnp.dot(head(v['vfull'], g)[:, keys_of(j)], p, preferred_element_type=F32) / denom
                for u in range(group):
                    v['oa'][group * g + u][j] = o[:, u * WINDOW:(u + 1) * WINDOW]
            v['state'] = [[carry['state'][h]] for h in range(RET_HEADS)]
            for h in range(RET_HEADS):
                for j in range(nblk):
                    v['state'][h].append(v['state'][h][j] * cd_ref[h] + v['incr'][h][j])
                carry['state'][h] = v['state'][h][nblk]

        def ret_outputs():
            v['or'] = [[None] * nblk for _ in range(RET_HEADS)]
            for j in range(nblk):
                for h in range(RET_HEADS):
                    u = h % 2
                    lhs = jnp.concatenate([chunk(v['vr'], h, j), v['state'][h][j]], axis=1)
                    rhs = jnp.concatenate([v['inner'][h // 2][j][:, u * RET_CHUNK:(u + 1) * RET_CHUNK],
                                           chunk(v['qr'], h, j) * qd_ref[h]], axis=0)
                    v['or'][h][j] = _dot(lhs, rhs)

        def gate_mix():
            mix = [jnp.concatenate(blocks, axis=1) for blocks in v['oa']]
            for h in range(RET_HEADS):
                o_h = _head_norm_t(jnp.concatenate(v['or'][h], axis=1))
                g_h = head(v['gate'], h)
                mix.append(o_h * (g_h * jax.nn.sigmoid(g_h)))
            v['mix'] = jnp.concatenate(mix, axis=0).astype(BF16)

        def out_project():
            y_ref[0, toks] = v['x'] + lax.dot_general(v['mix'], wout_ref[...], (((0,), (0,)), ((), ())),
                                                      preferred_element_type=F32)

        return [project, swa_prepare, swa_scores, ret_rotate, ret_scores, swa_softmax, swa_values,
                ret_outputs, gate_mix, out_project]

    pipelines = [stages(i) for i in range(tile // sub)]
    nstage = len(pipelines[0])
    for slot in range(nstage + STAGE_SKEW * (len(pipelines) - 1)):
        for i, pipe in enumerate(pipelines):
            if 0 <= slot - STAGE_SKEW * i < nstage:
                pipe[slot - STAGE_SKEW * i]()

    pk_ref[...], pv_ref[...] = carry['pk'], carry['pv']
    kwin_ref[0], vwin_ref[0] = carry['pk'].T, carry['pv'].T
    for h in range(RET_HEADS):
        s_ref[h] = carry['state'][h]

    @pl.when(t == pl.num_programs(1) - 1)
    def _():
        for h in range(RET_HEADS):
            sout_ref[0, h] = carry['state'][h].T


def _sample_in_kernel(x_ref, ca_ref, sa_ref, nmix_ref, wint_ref, qg_ref, kg_ref, bd_ref, feat_ref):
    _, _, first_half = _lane_consts()
    ca, sa, bd = ca_ref[...], sa_ref[...], bd_ref[...]
    hb = (_rms(x_ref[...]) * nmix_ref[...]).astype(BF16)
    proj = lambda lo, hi: lax.dot_general(hb, wint_ref[lo:hi, :], (((1,), (1,)), ((), ())),
                                          preferred_element_type=F32)
    swa = proj(C_QA, C_QR)
    slabs = [_rot_half(_head_norm(swa[:, s * LANES:(s + 1) * LANES], bd) * qg_ref[...], ca, sa, first_half)
             for s in range(N_QA_SLABS)]
    slabs.append(_rot_half(_head_norm(swa[:, C_KA:C_VA], bd) * kg_ref[...], ca, sa, first_half))
    feat_ref[...] = jnp.concatenate(slabs + [swa[:, C_VA:C_QR], proj(C_G, IN_COLS)], axis=1)


def _sample_swa_kernel(feat_ref, ckt_ref, cvt_ref, sink_ref, o_ref, kout_ref, vout_ref,
                       kn_ref, vn_ref, *, nseq, ntok, unroll):
    @pl.when(pl.program_id(0) == 0)
    def _():
        kn_ref[...] = jnp.zeros_like(kn_ref)
        vn_ref[...] = jnp.zeros_like(vn_ref)

    m_left, m_right, _ = _lane_consts()
    tail = WINDOW - ntok
    nrow = SWA_HEADS * ntok
    row_tok = lax.broadcasted_iota(jnp.int32, (nrow, WINDOW), 0) % ntok
    col = lax.broadcasted_iota(jnp.int32, (nrow, WINDOW), 1)
    valid_cache = col > row_tok
    valid_new = (col >= tail) & (col - tail <= row_tok)
    in_tail = lax.broadcasted_iota(jnp.int32, (LANES, WINDOW), 1) >= tail
    sink = sink_ref[...]
    shift = lambda a: pltpu.roll(a, HEAD_DIM, 1)

    def body(step, carry):
        seqs = [step * unroll + u for u in range(unroll)]
        rows = [pl.ds(pl.multiple_of(i * ntok, ntok), ntok) for i in seqs]
        q_rows, caches = [], []
        for u, i in enumerate(seqs):
            slab = lambda c, u=u: feat_ref[rows[u], c:c + LANES]
            kn_ref[u, tail:WINDOW] = slab(C_KA)
            vn_ref[u, tail:WINDOW] = slab(C_VA)
            s0, s1, s2, s3 = [slab(C_QA + s * LANES) for s in range(N_QA_SLABS)]
            q_rows.append(jnp.concatenate(
                [s0 * m_left, shift(s0 * m_right), s1 * m_left, shift(s1 * m_right),
                 shift(s2 * m_left), s2 * m_right, shift(s3 * m_left), s3 * m_right], axis=0))
            caches.append((ckt_ref[i], cvt_ref[i], kn_ref[u], vn_ref[u]))
        scores = [(_dot(q, kt), _dot_nt(q, kn)) for q, (kt, _, kn, _) in zip(q_rows, caches)]
        probs = []
        for s_c, s_n in scores:
            s_c = jnp.where(valid_cache, s_c * (HEAD_DIM ** -0.5), NEG_INF)
            s_n = jnp.where(valid_new, s_n * (HEAD_DIM ** -0.5), NEG_INF)
            m = jnp.maximum(jnp.maximum(jnp.max(s_c, axis=-1, keepdims=True),
                                        jnp.max(s_n, axis=-1, keepdims=True)), sink)
            p_c, p_n = jnp.exp(s_c - m), jnp.exp(s_n - m)
            denom = (jnp.sum(p_c, axis=-1, keepdims=True) + jnp.sum(p_n, axis=-1, keepdims=True)
                     + jnp.exp(sink - m))
            probs.append((p_c, p_n, denom))
        for u, i in enumerate(seqs):
            kt, vt, kn, vn = caches[u]
            p_c, p_n, denom = probs[u]
            o = (_dot_nt(p_c, vt) + _dot(p_n, vn)) / denom
            n = ntok
            left = lambda h: o[h * n:(h + 1) * n] * m_left
            right = lambda h: o[h * n:(h + 1) * n] * m_right
            o_ref[rows[u], 0:LANES] = left(0) + shift(left(1))
            o_ref[rows[u], LANES:2 * LANES] = left(2) + shift(left(3))
            o_ref[rows[u], 2 * LANES:3 * LANES] = shift(right(4)) + right(5)
            o_ref[rows[u], 3 * LANES:4 * LANES] = shift(right(6)) + right(7)
            kout_ref[i] = jnp.where(in_tail, kn.T, pltpu.roll(kt, tail, 1))
            vout_ref[i] = jnp.where(in_tail, vn.T, pltpu.roll(vt, tail, 1))
        return carry

    lax.fori_loop(0, nseq // unroll, body, 0)


def _sample_ret_in_kernel(x_ref, cr_ref, sr_ref, nmix_ref, wint_ref, qkv_ref):
    hb = (_rms(x_ref[...]) * nmix_ref[...]).astype(BF16)
    qkv = lax.dot_general(wint_ref[C_QR:C_G, :], hb, (((1,), (1,)), ((), ())), preferred_element_type=F32)
    n = qkv.shape[1]
    even_row = (lax.broadcasted_iota(jnp.int32, (RET_W, n), 0) % 2) == 0
    cr8 = jnp.concatenate([cr_ref[...]] * RET_HEADS, axis=0)
    sr8 = jnp.concatenate([sr_ref[...]] * RET_HEADS, axis=0)
    qkv_ref[0:RET_W] = _rot_pairs_t(qkv[0:RET_W], cr8, sr8, even_row)
    qkv_ref[RET_W:2 * RET_W] = _rot_pairs_t(qkv[RET_W:2 * RET_W], cr8, sr8, even_row) * (RET_DIM ** -0.5)
    qkv_ref[2 * RET_W:] = qkv[2 * RET_W:]


def _sample_ret_kernel(q_ref, k_ref, v_ref, s_ref, c_ref, o_ref, so_ref, ks_ref, *, ntok):
    hd, nseq = RET_DIM, LANES
    tok = lambda t: slice(t * nseq, (t + 1) * nseq)
    row_qd, row_kd, row_cd = ntok * ntok, ntok * ntok + ntok, ntok * ntok + 2 * ntok
    group = SUBLANES
    pair_outs = []
    for u in range(2):
        feats = slice(u * hd, (u + 1) * hd)
        const = lambda r: c_ref[u, r:r + 1, :]
        for t in range(ntok):
            ks_ref[t] = k_ref[feats, tok(t)] * const(row_kd + t)
        outs = []
        for t in range(ntok):
            q_t = q_ref[feats, tok(t)]
            acc = jnp.zeros((hd, nseq), F32)
            for k in range(t + 1):
                w_tk = jnp.sum(q_t * k_ref[feats, tok(k)], axis=0, keepdims=True) * const(t * ntok + k)
                acc = acc + w_tk * v_ref[feats, tok(k)]

            def cross(j, carry, t=t):
                d0 = pl.multiple_of(j * group, group)
                q_rows = q_ref[pl.ds(u * hd + d0, group), tok(t)]
                for i in range(group):
                    carry = carry + q_rows[i:i + 1, :] * s_ref[u, d0 + i]
                return carry
            carried = lax.fori_loop(0, hd // group, cross, jnp.zeros((hd, nseq), F32))
            outs.append(acc + carried * const(row_qd + t))

        def update(j, carry):
            d0 = pl.multiple_of(j * group, group)
            k_rows = [ks_ref[t, pl.ds(d0, group), :] for t in range(ntok)]
            for i in range(group):
                new = s_ref[u, d0 + i] * const(row_cd)
                for t in range(ntok):
                    new = new + k_rows[t][i:i + 1, :] * v_ref[feats, tok(t)]
                so_ref[u, d0 + i] = new
            return carry
        lax.fori_loop(0, hd // group, update, 0)
        pair_outs.append(outs)
    for t in range(ntok):
        o_ref[t] = jnp.concatenate([pair_outs[0][t], pair_outs[1][t]], axis=0).T


def _sample_out_kernel(x_ref, oa_ref, or_ref, feat_ref, bd_ref, wout_ref, y_ref):
    outs = [oa_ref[...]]
    for p in range(N_RET_SLABS):
        gate = feat_ref[:, C_GATE_S + p * LANES:C_GATE_S + (p + 1) * LANES]
        o_ret = or_ref[:, p * LANES:(p + 1) * LANES]
        outs.append(_head_norm(o_ret, bd_ref[...]) * (gate * jax.nn.sigmoid(gate)))
    mix = jnp.concatenate(outs, axis=1).astype(BF16)
    y_ref[...] = x_ref[...] + jnp.dot(mix, wout_ref[...], preferred_element_type=F32)


def _mem_kv_kernel(m_ref, nmem_ref, wkv_ref, kg_ref, k_ref, v_ref):
    hb = (_rms(m_ref[...]) * nmem_ref[...]).astype(BF16)
    kv = jnp.dot(hb, wkv_ref[...], preferred_element_type=F32)
    slots = m_ref.shape[0]
    for h in range(MEM_HEADS):
        rows = pl.ds(h, slots, stride=MEM_HEADS)
        k_ref[rows, :] = _rms(kv[:, h * LANES:(h + 1) * LANES]) * kg_ref[...]
        v_ref[rows, :] = kv[:, MEM_W + h * LANES:MEM_W + (h + 1) * LANES]


def _mem_queries(x, ncross, wq_ref, qg):
    hb = (_rms(x) * ncross).astype(BF16)
    q = jnp.dot(hb, wq_ref[...], preferred_element_type=F32)
    qs = qg * (MEM_HEAD_DIM ** -0.5 * LOG2E)
    return [_rms(q[:, h * LANES:(h + 1) * LANES]) * qs for h in range(MEM_HEADS)]


def _prompt_cross_ffn_kernel(x_ref, mk_ref, mv_ref, ncross_ref, wq_ref, qg_ref, wo_ref, nffn_ref, wgu_ref,
                             wdown_ref, y_ref, xa_ref):
    @pl.when(pl.program_id(0) == 0)
    def _():
        xa_ref[...] = jnp.zeros_like(xa_ref)

    xa = xa_ref[...]
    hb = (_rms(xa) * nffn_ref[...]).astype(BF16)
    x = x_ref[...]
    qn = _mem_queries(x, ncross_ref[...], wq_ref, qg_ref[...])
    g = jnp.dot(hb, wgu_ref[:, 0:FFN_HIDDEN], preferred_element_type=F32)
    head = lambda ref, h: ref[0, pl.ds(h, MEM_LEN, stride=MEM_HEADS), :].astype(BF16)
    scores = [_dot_nt(head(mk_ref, h), qn[h]) for h in range(MEM_HEADS)]
    u = jnp.dot(hb, wgu_ref[:, FFN_HIDDEN:], preferred_element_type=F32)
    probs, denoms = [], []
    for s in scores:
        p = jnp.exp2(s - jnp.max(s, axis=0, keepdims=True))
        denoms.append(jnp.sum(p, axis=0, keepdims=True))
        probs.append(p.astype(BF16))
    act = (g * jax.nn.sigmoid(g) * u).astype(BF16)
    o_t = jnp.concatenate([_dot_tn(head(mv_ref, h), probs[h]) / denoms[h] for h in range(MEM_HEADS)], axis=0)
    y_ref[...] = xa + jnp.dot(act, wdown_ref[...], preferred_element_type=F32)
    xa_ref[...] = x + lax.dot_general(o_t.astype(BF16), wo_ref[...], (((0,), (0,)), ((), ())),
                                      preferred_element_type=F32)


def _sample_cross_attn_kernel(x_ref, mk_ref, mv_ref, ncross_ref, wq_ref, qg_ref, o_ref, *, nseq, tq):
    qn = _mem_queries(x_ref[...], ncross_ref[...], wq_ref, qg_ref[...])
    nrow = MEM_HEADS * tq
    nmem = MEM_LEN * MEM_HEADS
    own = (lax.broadcasted_iota(jnp.int32, (nrow, nmem), 0) // tq
           == lax.broadcasted_iota(jnp.int32, (nrow, nmem), 1) % MEM_HEADS)
    scores = []
    for i in range(nseq):
        q_stack = jnp.concatenate([qn[h][i * tq:(i + 1) * tq] for h in range(MEM_HEADS)], axis=0)
        scores.append(_dot_nt(q_stack, mk_ref[i]))
    probs, denoms = [], []
    for s in scores:
        s = jnp.where(own, s, NEG_INF)
        p = jnp.exp2(s - jnp.max(s, axis=-1, keepdims=True))
        denoms.append(jnp.sum(p, axis=-1, keepdims=True))
        probs.append(p.astype(BF16))
    for i in range(nseq):
        o = jnp.dot(probs[i], mv_ref[i].astype(BF16), preferred_element_type=F32) / denoms[i]
        for h in range(MEM_HEADS):
            o_ref[i * tq:(i + 1) * tq, h * LANES:(h + 1) * LANES] = o[h * tq:(h + 1) * tq]


def _cross_out_ffn_kernel(x_ref, o_ref, wo_ref, nffn_ref, wgu_ref, wdown_ref, y_ref):
    x = x_ref[...] + jnp.dot(o_ref[...].astype(BF16), wo_ref[...], preferred_element_type=F32)
    hb = (_rms(x) * nffn_ref[...]).astype(BF16)
    g = jnp.dot(hb, wgu_ref[:, 0:FFN_HIDDEN], preferred_element_type=F32)
    u = jnp.dot(hb, wgu_ref[:, FFN_HIDDEN:], preferred_element_type=F32)
    act = (g * jax.nn.sigmoid(g) * u).astype(BF16)
    y_ref[...] = x + jnp.dot(act, wdown_ref[...], preferred_element_type=F32)


def _f32(*arrays):
    return [np.ascontiguousarray(a, dtype=np.float32) for a in arrays]


def _rope_angles(pos):
    half = HEAD_DIM // 2
    inv = 1.0 / (ROPE_THETA ** (np.arange(half, dtype=np.float64) / half))
    return pos.astype(np.float64)[:, None] * inv[None, :]


def _ret_angles(pos):
    inv = RET_THETA ** (-np.linspace(0.0, 1.0, RET_DIM // 2, dtype=np.float64))
    return pos.astype(np.float64)[:, None] * inv[None, :]


def _rope_tables(pos):
    ang = _rope_angles(pos)
    cos, sin = np.cos(ang), np.sin(ang)
    c64 = np.concatenate([cos, cos], axis=-1)
    s64 = np.concatenate([-sin, sin], axis=-1)
    return _f32(np.tile(c64, (1, 2)), np.tile(s64, (1, 2)))


def _retention_decays(c):
    log_g = np.log(1.0 - np.exp2(-5.0 - np.arange(RET_HEADS, dtype=np.float64)))
    idx = np.arange(c, dtype=np.float64)
    diff = idx[:, None] - idx[None, :]
    dmat = np.where(diff >= 0, np.exp(np.maximum(diff, 0.0)[None] * log_g[:, None, None]), 0.0)
    qd = np.exp((idx + 1.0)[None, :] * log_g[:, None])
    kd = np.exp((c - 1.0 - idx)[None, :] * log_g[:, None])
    cd = np.exp(c * log_g)
    return dmat, qd, kd, cd


def _sample_decay_rows(c):
    dmat, qd, kd, cd = _retention_decays(c)
    rows = np.concatenate([dmat.reshape(RET_HEADS, c * c), qd, kd, cd[:, None]], axis=1)
    rows = np.pad(rows, ((0, 0), (0, -rows.shape[1] % 8)))
    return _f32(np.broadcast_to(rows[:, :, None], rows.shape + (LANES,)))[0]


def _full(shape):
    nd = len(shape)
    return pl.BlockSpec(shape, lambda *_: (0,) * nd)


def _params(sem):
    return pltpu.CompilerParams(dimension_semantics=sem, vmem_limit_bytes=VMEM_LIMIT)


def _prompt_tables_t(pos):
    ang = _rope_angles(pos).T
    ang_r = _ret_angles(pos).T
    cos_r, sin_r = np.cos(ang_r), np.sin(ang_r)
    cr = np.repeat(cos_r, 2, axis=0)
    sr = np.stack([-sin_r, sin_r], axis=1).reshape(RET_DIM, pos.shape[0])
    return _f32(np.cos(ang), np.sin(ang), cr, sr)


def _decay_consts_t(c):
    dmat, qd, kd, cd = _retention_decays(c)
    dm = dmat.transpose(0, 2, 1).reshape(N_RET_SLABS, 2, c, c).transpose(0, 2, 1, 3).reshape(N_RET_SLABS, c, 2 * c)
    cd = np.broadcast_to(cd[:, None, None], (RET_HEADS, 1, RET_DIM))
    return _f32(dm, qd[:, None, :], kd[:, None, :], cd)


def _prompt_mixer(x, w, tile, sub):
    b, l, d = x.shape
    ca, sa, cr, sr = _prompt_tables_t(np.arange(l, dtype=np.int32))
    dm, qd, kd, cd = _decay_consts_t(RET_CHUNK)
    sink = jnp.repeat(w['sinks'].reshape(SWA_KV_HEADS, 1, -1), WINDOW, axis=-1)
    tab = lambda rows: pl.BlockSpec((rows, tile), lambda i, t: (0, t))
    xspec = pl.BlockSpec((1, tile, d), lambda i, t: (i, t, 0))
    win_spec = pl.BlockSpec((1, WINDOW, LANES), lambda i, t: (i, 0, 0))
    st_spec = pl.BlockSpec((1, RET_HEADS, RET_DIM, RET_DIM), lambda i, t: (i, 0, 0, 0))
    return pl.pallas_call(
        functools.partial(_prompt_mixer_kernel, tile=tile, sub=sub),
        grid=(b, l // tile),
        in_specs=[xspec, tab(HEAD_DIM // 2), tab(HEAD_DIM // 2), tab(RET_DIM), tab(RET_DIM), _full((1, d)),
                  _full((IN_COLS, d)), _full((HEAD_DIM, 1)), _full((HEAD_DIM, 1)), _full(sink.shape),
                  _full((d, d)), _full(dm.shape), _full(qd.shape), _full(kd.shape), _full(cd.shape)],
        out_specs=[xspec, win_spec, win_spec, st_spec],
        out_shape=[jax.ShapeDtypeStruct(x.shape, F32),
                   jax.ShapeDtypeStruct((b, WINDOW, LANES), F32),
                   jax.ShapeDtypeStruct((b, WINDOW, LANES), F32),
                   jax.ShapeDtypeStruct((b, RET_HEADS, RET_DIM, RET_DIM), F32)],
        scratch_shapes=[pltpu.VMEM((SWA_KV_W, WINDOW), F32), pltpu.VMEM((SWA_KV_W, WINDOW), F32),
                        pltpu.VMEM((RET_HEADS, RET_DIM, RET_DIM), F32)],
        compiler_params=_params(("arbitrary", "arbitrary")),
        name="prompt_mixer",
    )(x, ca, sa, cr, sr, w['norm_mix'], w['w_in_t'], w['qg_col'], w['kg_col'], sink, w['w_out'], dm, qd, kd,
      cd)


def _sample_mixer(x, cache_k, cache_v, state, w, tile, nseq):
    b, ntok, d = x.shape
    rows = b * ntok
    xf = x.reshape(rows, d)
    pos = np.tile(PAST_LEN + np.arange(ntok, dtype=np.int32), b)
    ca, sa = _rope_tables(pos)
    tab = pl.BlockSpec((tile, LANES), lambda i: (i, 0))
    row_spec = lambda width: pl.BlockSpec((tile, width), lambda i: (i, 0))
    feat = pl.pallas_call(
        _sample_in_kernel,
        grid=(rows // tile,),
        in_specs=[row_spec(d), tab, tab, _full((1, d)), _full((IN_COLS, d)), _full((1, LANES)),
                  _full((1, LANES)), _full((LANES, LANES))],
        out_specs=row_spec(SAMPLE_FEAT),
        out_shape=jax.ShapeDtypeStruct((rows, SAMPLE_FEAT), F32),
        compiler_params=_params(("arbitrary",)),
        name="sample_in",
    )(xf, ca, sa, w['norm_mix'], w['w_in_t'], w['qg'], w['kg'], w['bd'])

    sink = jnp.repeat(w['sinks'], ntok)[:, None]
    seq_rows = nseq * ntok
    unroll = nseq
    cache_spec = pl.BlockSpec((nseq, LANES, WINDOW), lambda i: (i, 0, 0))
    o_a, k_out, v_out = pl.pallas_call(
        functools.partial(_sample_swa_kernel, nseq=nseq, ntok=ntok, unroll=unroll),
        grid=(b // nseq,),
        in_specs=[pl.BlockSpec((seq_rows, SAMPLE_FEAT), lambda i: (i, 0)), cache_spec, cache_spec,
                  _full(sink.shape)],
        out_specs=[pl.BlockSpec((seq_rows, SWA_Q_W), lambda i: (i, 0)), cache_spec, cache_spec],
        out_shape=[jax.ShapeDtypeStruct((rows, SWA_Q_W), F32),
                   jax.ShapeDtypeStruct(cache_k.shape, F32),
                   jax.ShapeDtypeStruct(cache_v.shape, F32)],
        scratch_shapes=[pltpu.VMEM((unroll, WINDOW, LANES), F32), pltpu.VMEM((unroll, WINDOW, LANES), F32)],
        compiler_params=_params(("arbitrary",)),
        name="sample_swa",
    )(feat, cache_k, cache_v, sink)

    assert b == LANES, "the retention step puts one sequence per lane"
    xt = jnp.transpose(x, (1, 0, 2)).reshape(rows, d)
    pos_t = np.repeat(PAST_LEN + np.arange(ntok, dtype=np.int32), b)
    _, _, cr_t, sr_t = _prompt_tables_t(pos_t)
    half = rows // 2
    tab_t = pl.BlockSpec((RET_DIM, half), lambda i: (0, i))
    qkv = pl.pallas_call(
        _sample_ret_in_kernel,
        grid=(2,),
        in_specs=[pl.BlockSpec((half, d), lambda i: (i, 0)), tab_t, tab_t, _full((1, d)),
                  _full((IN_COLS, d))],
        out_specs=pl.BlockSpec((3 * RET_W, half), lambda i: (0, i)),
        out_shape=jax.ShapeDtypeStruct((3 * RET_W, rows), F32),
        compiler_params=_params(("arbitrary",)),
        name="sample_ret_in",
    )(xt, cr_t, sr_t, w['norm_mix'], w['w_in_t'])
    consts = _sample_decay_rows(ntok)
    pair_rows = lambda off: pl.BlockSpec((LANES, rows), lambda p: (off + p, 0))
    st_spec = pl.BlockSpec((2, RET_DIM, RET_DIM, b), lambda p: (p, 0, 0, 0))
    o_r, s_out = pl.pallas_call(
        functools.partial(_sample_ret_kernel, ntok=ntok),
        grid=(N_RET_SLABS,),
        in_specs=[pair_rows(0), pair_rows(N_RET_SLABS), pair_rows(2 * N_RET_SLABS), st_spec,
                  pl.BlockSpec((2,) + consts.shape[1:], lambda p: (p, 0, 0))],
        out_specs=[pl.BlockSpec((ntok, b, LANES), lambda p: (0, 0, p)), st_spec],
        out_shape=[jax.ShapeDtypeStruct((ntok, b, RET_W), F32), jax.ShapeDtypeStruct(state.shape, F32)],
        scratch_shapes=[pltpu.VMEM((ntok, RET_DIM, b), F32)],
        compiler_params=_params(("arbitrary",)),
        name="sample_ret",
    )(qkv, qkv, qkv, state, consts)
    o_r = jnp.transpose(o_r, (1, 0, 2)).reshape(rows, RET_W)

    y = pl.pallas_call(
        _sample_out_kernel,
        grid=(rows // tile,),
        in_specs=[row_spec(d), row_spec(SWA_Q_W), row_spec(RET_W), row_spec(SAMPLE_FEAT), _full((LANES, LANES)),
                  _full((d, d))],
        out_specs=row_spec(d),
        out_shape=jax.ShapeDtypeStruct((rows, d), F32),
        compiler_params=_params(("arbitrary",)),
        name="sample_out",
    )(xf, o_a, o_r, feat, w['bd'], w['w_out'])
    return y, k_out, v_out, s_out


def _mem_kv(mem, w, tile):
    rows, d = mem.shape
    row_spec = lambda width: pl.BlockSpec((tile, width), lambda i: (i, 0))
    return pl.pallas_call(
        _mem_kv_kernel,
        grid=(rows // tile,),
        in_specs=[row_spec(d), _full((1, d)), _full((d, 2 * MEM_W)), _full((1, LANES))],
        out_specs=[pl.BlockSpec((tile * MEM_HEADS, MEM_HEAD_DIM), lambda i: (i, 0))] * 2,
        out_shape=[jax.ShapeDtypeStruct((rows * MEM_HEADS, MEM_HEAD_DIM), F32)] * 2,
        compiler_params=_params(("arbitrary",)),
        name="mem_kv",
    )(mem, w['norm_mem'], w['w_mkv'], w['kgm'])


def _prompt_cross_ffn(x, mk, mv, w, tile):
    rows, d = x.shape
    n = rows // tile
    per_mem = n // mk.shape[0]
    cur = lambda i: jnp.minimum(i, n - 1)
    single = lambda shape: pl.BlockSpec(shape, lambda i: (0,) * len(shape), pipeline_mode=pl.Buffered(1))
    mem_spec = pl.BlockSpec((1, MEM_LEN * MEM_HEADS, MEM_HEAD_DIM), lambda i: (cur(i) // per_mem, 0, 0))
    return pl.pallas_call(
        _prompt_cross_ffn_kernel,
        grid=(n + 1,),
        in_specs=[pl.BlockSpec((tile, d), lambda i: (cur(i), 0)), mem_spec, mem_spec, _full((1, d)),
                  single((d, MEM_W)), _full((1, LANES)), single((MEM_W, d)), _full((1, d)),
                  single((d, 2 * FFN_HIDDEN)), single((FFN_HIDDEN, d))],
        out_specs=pl.BlockSpec((tile, d), lambda i: (jnp.maximum(i - 1, 0), 0)),
        out_shape=jax.ShapeDtypeStruct((rows, d), F32),
        scratch_shapes=[pltpu.VMEM((tile, d), F32)],
        compiler_params=_params(("arbitrary",)),
        name="prompt_cross_ffn",
    )(x, mk, mv, w['norm_cross'], w['w_mq'], w['qgm'], w['w_mo'], w['norm_ffn'], w['w_gu'], w['w_down'])


def _sample_cross_attn(x, mk, mv, w, nseq, tq):
    rows, d = x.shape
    blk = nseq * tq
    mem_spec = pl.BlockSpec((nseq, MEM_LEN * MEM_HEADS, MEM_HEAD_DIM), lambda i: (i, 0, 0))
    return pl.pallas_call(
        functools.partial(_sample_cross_attn_kernel, nseq=nseq, tq=tq),
        grid=(rows // blk,),
        in_specs=[pl.BlockSpec((blk, d), lambda i: (i, 0)), mem_spec, mem_spec, _full((1, d)),
                  _full((d, MEM_W)), _full((1, LANES))],
        out_specs=pl.BlockSpec((blk, MEM_W), lambda i: (i, 0)),
        out_shape=jax.ShapeDtypeStruct((rows, MEM_W), F32),
        compiler_params=_params(("arbitrary",)),
        name="sample_cross_attn",
    )(x, mk, mv, w['norm_cross'], w['w_mq'], w['qgm'])


def _sample_cross_out_ffn(x, o, w, tile):
    rows, d = x.shape
    row_spec = lambda width: pl.BlockSpec((tile, width), lambda i: (i, 0))
    single = lambda shape: pl.BlockSpec(shape, lambda i: (0,) * len(shape), pipeline_mode=pl.Buffered(1))
    return pl.pallas_call(
        _cross_out_ffn_kernel,
        grid=(rows // tile,),
        in_specs=[row_spec(d), row_spec(MEM_W), single((MEM_W, d)), _full((1, d)),
                  single((d, 2 * FFN_HIDDEN)), single((FFN_HIDDEN, d))],
        out_specs=row_spec(d),
        out_shape=jax.ShapeDtypeStruct((rows, d), F32),
        compiler_params=_params(("arbitrary",)),
        name="sample_ffn",
    )(x, o, w['w_mo'], w['norm_ffn'], w['w_gu'], w['w_down'])


def kernel(x_prompt, x_sample, mem_prompt, cache_swa_k, cache_swa_v, state_ret, cache_mem_k, cache_mem_v,
           norm_mix, w_in, q_norm_a, k_norm_a, sinks, w_out, norm_cross, norm_mem, w_mq, w_mkv,
           q_norm_m, k_norm_m, w_mo, norm_ffn, w_gu, w_down):
    assert norm_mix.shape[0] == 1, "single-layer kernel"
    b, l, d = x_prompt.shape
    sb, st, _ = x_sample.shape
    half = (np.arange(LANES) // HEAD_DIM)[:, None] == (np.arange(LANES) // HEAD_DIM)[None, :]
    w = {
        'norm_mix': norm_mix, 'norm_cross': norm_cross, 'norm_mem': norm_mem, 'norm_ffn': norm_ffn,
        'w_out': w_out[0].astype(BF16), 'w_mq': w_mq[0].astype(BF16),
        'w_mkv': w_mkv[0].astype(BF16), 'w_mo': w_mo[0].astype(BF16), 'w_gu': w_gu[0].astype(BF16),
        'w_down': w_down[0].astype(BF16),
        'w_in_t': w_in[0].T.astype(BF16),
        'qg': jnp.tile(q_norm_a, (1, 2)), 'kg': jnp.tile(k_norm_a, (1, 2)),
        'qg_col': q_norm_a.reshape(HEAD_DIM, 1), 'kg_col': k_norm_a.reshape(HEAD_DIM, 1),
        'qgm': q_norm_m, 'kgm': k_norm_m, 'sinks': sinks[0],
        'bd': half.astype(BF16),
    }

    mk, mv = _mem_kv(mem_prompt.reshape(b * MEM_LEN, d), w, tile=MEM_KV_TILE)
    xp, kwin, vwin, ret_p = _prompt_mixer(x_prompt, w, tile=PROMPT_MIXER_TILE, sub=PROMPT_MIXER_SUB)
    xp = xp.reshape(b * l, d)
    mem_rows_p = (b, MEM_LEN * MEM_HEADS, MEM_HEAD_DIM)
    yp = _prompt_cross_ffn(xp, mk.reshape(mem_rows_p), mv.reshape(mem_rows_p), w, tile=PROMPT_FFN_TILE)
    yp = yp.reshape(b, l, d)
    ret_p = ret_p.reshape(1, b, RET_HEADS, RET_DIM, RET_DIM)

    nbuf = cache_swa_k.shape[2]
    assert nbuf == WINDOW
    to_feature_major = lambda c: jnp.transpose(c[0], (0, 2, 3, 1)).reshape(sb, SWA_KV_W, nbuf)
    from_feature_major = lambda c: jnp.transpose(c.reshape(sb, SWA_KV_HEADS, HEAD_DIM, nbuf), (0, 3, 1, 2))[None]
    xs, k_s, v_s, s_s = _sample_mixer(x_sample, to_feature_major(cache_swa_k), to_feature_major(cache_swa_v),
                                      jnp.transpose(state_ret[0], (1, 2, 3, 0)), w, tile=SAMPLE_TILE,
                                      nseq=SAMPLE_SEQS)
    mem_rows = (sb, MEM_LEN * MEM_HEADS, MEM_HEAD_DIM)
    os_ = _sample_cross_attn(xs, cache_mem_k.reshape(mem_rows), cache_mem_v.reshape(mem_rows), w,
                             nseq=SAMPLE_SEQS, tq=st)
    ys = _sample_cross_out_ffn(xs, os_, w, tile=SAMPLE_FFN_TILE).reshape(sb, st, d)

    kv_shape = (1, b, WINDOW, SWA_KV_HEADS, HEAD_DIM)
    mem_shape = (1, b, MEM_LEN, MEM_HEADS, MEM_HEAD_DIM)
    return (yp, ys, kwin.reshape(kv_shape), vwin.reshape(kv_shape), ret_p,
            mk.reshape(mem_shape), mv.reshape(mem_shape),
            from_feature_major(k_s), from_feature_major(v_s),
            jnp.transpose(s_s, (3, 0, 1, 2))[None])
```

```python
import functools

import jax
import jax.numpy as jnp
import numpy as np
from jax import lax
from jax.experimental import pallas as pl
from jax.experimental.pallas import tpu as pltpu

F32 = jnp.float32
BF16 = jnp.bfloat16

LANES = 128
SUBLANES = 8
D_MODEL = 1024
HEAD_DIM = 64
SWA_HEADS = 8
SWA_KV_HEADS = 2
WINDOW = 128
RET_HEADS = 8
RET_DIM = 64
RET_CHUNK = 128
RET_THETA = 10000.0
ROPE_THETA = 10000.0
MEM_LEN = 256
MEM_HEADS = 4
MEM_HEAD_DIM = 128
MEM_W = MEM_HEADS * MEM_HEAD_DIM
FFN_HIDDEN = 2816
RMS_EPS = 1e-6
NEG_INF = -1e30
LOG2E = 1.4426950408889634
PAST_LEN = 16384

SWA_Q_W = SWA_HEADS * HEAD_DIM
SWA_KV_W = SWA_KV_HEADS * HEAD_DIM
RET_W = RET_HEADS * RET_DIM
IN_COLS = SWA_Q_W + 2 * SWA_KV_W + 4 * RET_W
C_QA, C_KA, C_VA = 0, SWA_Q_W, SWA_Q_W + SWA_KV_W
C_QR = SWA_Q_W + 2 * SWA_KV_W
C_KR, C_VR, C_G = C_QR + RET_W, C_QR + 2 * RET_W, C_QR + 3 * RET_W
C_GATE_S = C_QR
SAMPLE_FEAT = C_QR + RET_W
N_QA_SLABS = SWA_Q_W // LANES
N_RET_SLABS = RET_W // LANES

VMEM_LIMIT = 56 * 1024 * 1024
PROMPT_MIXER_TILE, PROMPT_MIXER_SUB = 1024, 512
PROMPT_FFN_TILE = 512
MEM_KV_TILE = 512
SAMPLE_TILE = 512
SAMPLE_FFN_TILE = 512
SAMPLE_SEQS = 8
STAGE_SKEW = 1


def _dot(a, b):
    return jnp.dot(a.astype(BF16), b.astype(BF16), preferred_element_type=F32)


def _dot_nt(a, b):
    return lax.dot_general(a.astype(BF16), b.astype(BF16), (((1,), (1,)), ((), ())),
                           preferred_element_type=F32)


def _rms(x):
    return x * lax.rsqrt(jnp.mean(x * x, axis=-1, keepdims=True) + RMS_EPS)


def _lane_consts():
    lane = lax.broadcasted_iota(jnp.int32, (1, LANES), 1)
    m_left = (lane < HEAD_DIM).astype(F32)
    m_right = 1.0 - m_left
    first_half = (lane % HEAD_DIM) < (HEAD_DIM // 2)
    return m_left, m_right, first_half


def _head_norm(y, bd):
    ss = jnp.dot((y * y).astype(BF16), bd, preferred_element_type=F32) * (1.0 / HEAD_DIM)
    return y * lax.rsqrt(ss + RMS_EPS)


def _rot_half(y, cos, sin_signed, first_half):
    swapped = jnp.where(first_half, pltpu.roll(y, LANES - HEAD_DIM // 2, 1), pltpu.roll(y, HEAD_DIM // 2, 1))
    return y * cos + swapped * sin_signed


def _dot_tn(a, b):
    return lax.dot_general(a.astype(BF16), b.astype(BF16), (((0,), (0,)), ((), ())),
                           preferred_element_type=F32)


def _head_norm_t(y):
    return y * lax.rsqrt(jnp.mean(y * y, axis=0, keepdims=True) + RMS_EPS)


def _rot_half_t(y, cos, sin):
    half = HEAD_DIM // 2
    y1, y2 = y[0:half], y[half:]
    return jnp.concatenate([y1 * cos - y2 * sin, y2 * cos + y1 * sin], axis=0)


def _rot_pairs_t(y, cos, sin_signed, even_row):
    n = y.shape[0]
    swapped = jnp.where(even_row, pltpu.roll(y, n - 1, 0), pltpu.roll(y, 1, 0))
    return y * cos + swapped * sin_signed


def _prompt_mixer_kernel(x_ref, ca_ref, sa_ref, cr_ref, sr_ref, nmix_ref, wint_ref, qg_ref, kg_ref,
                         sink_ref, wout_ref, dm_ref, qd_ref, kd_ref, cd_ref,
                         y_ref, kwin_ref, vwin_ref, sout_ref,
                         pk_ref, pv_ref, s_ref, *, tile, sub):
    t = pl.program_id(1)
    nblk = sub // WINDOW
    hd = HEAD_DIM
    group = SWA_HEADS // SWA_KV_HEADS

    @pl.when(t == 0)
    def _():
        pk_ref[...] = jnp.zeros_like(pk_ref)
        pv_ref[...] = jnp.zeros_like(pv_ref)
        s_ref[...] = jnp.zeros_like(s_ref)

    qg, kg = qg_ref[...], kg_ref[...]
    qs = qg * (hd ** -0.5 * LOG2E)
    head = lambda a, h: a[h * hd:(h + 1) * hd]
    chunk = lambda a, h, j: a[h * hd:(h + 1) * hd, j * RET_CHUNK:(j + 1) * RET_CHUNK]
    keys_of = lambda j: slice(j * WINDOW, (j + 2) * WINDOW)
    units = [(g, j) for j in range(nblk) for g in range(SWA_KV_HEADS)]
    key = lax.broadcasted_iota(jnp.int32, (2 * WINDOW, group * WINDOW), 0)
    qry = lax.broadcasted_iota(jnp.int32, (2 * WINDOW, group * WINDOW), 1) % WINDOW
    band = (key > qry) & (key <= qry + WINDOW)
    even_row = (lax.broadcasted_iota(jnp.int32, (RET_DIM, sub), 0) % 2) == 0
    zeros = jnp.zeros((hd, RET_CHUNK), F32)
    carry = {'pk': pk_ref[...], 'pv': pv_ref[...], 'state': [s_ref[h] for h in range(RET_HEADS)]}

    def stages(idx):
        toks = slice(idx * sub, (idx + 1) * sub)
        v = {}

        def project():
            v['x'] = x_ref[0, toks]
            hb = (_rms(v['x']) * nmix_ref[...]).astype(BF16)
            proj = lambda lo, hi: lax.dot_general(wint_ref[lo:hi, :], hb, (((1,), (1,)), ((), ())),
                                                  preferred_element_type=F32)
            v['swa'], v['ret'] = proj(C_QA, C_QR), proj(C_QR, C_VR)
            v['vr'], v['gate'] = proj(C_VR, C_G), proj(C_G, IN_COLS)

        def swa_prepare():
            ca, sa = ca_ref[:, toks], sa_ref[:, toks]
            swa_t = v['swa']
            v['qa'] = [_rot_half_t(_head_norm_t(head(swa_t, h)) * qs, ca, sa) for h in range(SWA_HEADS)]
            ka = jnp.concatenate([_rot_half_t(_head_norm_t(head(swa_t, SWA_HEADS + g)) * kg, ca, sa)
                                  for g in range(SWA_KV_HEADS)], axis=0)
            va = swa_t[C_VA:C_QR]
            v['kfull'] = jnp.concatenate([carry['pk'], ka], axis=1).astype(BF16)
            v['vfull'] = jnp.concatenate([carry['pv'], va], axis=1).astype(BF16)
            carry['pk'], carry['pv'] = ka[:, sub - WINDOW:], va[:, sub - WINDOW:]

        def swa_scores():
            v['scores'] = []
            for g, j in units:
                blk = slice(j * WINDOW, (j + 1) * WINDOW)
                q4 = jnp.concatenate([v['qa'][group * g + u][:, blk] for u in range(group)], axis=1)
                v['scores'].append(_dot_tn(head(v['kfull'], g)[:, keys_of(j)], q4))

        def ret_rotate():
            cr, sr = cr_ref[:, toks], sr_ref[:, toks]
            rot = lambda a: jnp.concatenate([_rot_pairs_t(head(a, h), cr, sr, even_row)
                                             for h in range(RET_HEADS)], axis=0)
            v['qr'] = rot(v['ret'][0:RET_W])
            v['kr'] = rot(v['ret'][RET_W:]) * (RET_DIM ** -0.5)

        def ret_scores():
            qr, kr, vr = v['qr'], v['kr'], v['vr']
            v['inner'] = [[None] * nblk for _ in range(N_RET_SLABS)]
            for j in range(nblk):
                for pr in range(N_RET_SLABS):
                    q_bd = jnp.concatenate([jnp.concatenate([chunk(qr, 2 * pr, j), zeros], axis=1),
                                            jnp.concatenate([zeros, chunk(qr, 2 * pr + 1, j)], axis=1)], axis=0)
                    k2 = kr[pr * LANES:(pr + 1) * LANES, j * RET_CHUNK:(j + 1) * RET_CHUNK]
                    v['inner'][pr][j] = _dot_tn(k2, q_bd) * dm_ref[pr]
            v['incr'] = [[_dot_nt(chunk(vr, h, j), chunk(kr, h, j) * kd_ref[h]) for j in range(nblk)]
                         for h in range(RET_HEADS)]

        def swa_softmax():
            v['probs'], v['denoms'] = [], []
            for (g, j), s in zip(units, v['scores']):
                sink = sink_ref[g] * LOG2E
                valid = band & (key >= WINDOW * (1 - t)) if (idx == 0 and j == 0) else band
                s = jnp.where(valid, s, NEG_INF)
                m = jnp.maximum(jnp.max(s, axis=0, keepdims=True), sink)
                p = jnp.exp2(s - m)
                v['denoms'].append(jnp.sum(p, axis=0, keepdims=True) + jnp.exp2(sink - m))
                v['probs'].append(p.astype(BF16))

        def swa_values():
            v['oa'] = [[None] * nblk for _ in range(SWA_HEADS)]
            for (g, j), p, denom in zip(units, v['probs'], v['denoms']):
                o = jnp.dot(head(v['vfull'], g)[:, keys_of(j)], p, preferred_element_type=F32) / denom
                for u in range(group):
                    v['oa'][group * g + u][j] = o[:, u * WINDOW:(u + 1) * WINDOW]
            v['state'] = [[carry['state'][h]] for h in range(RET_HEADS)]
            for h in range(RET_HEADS):
                for j in range(nblk):
                    v['state'][h].append(v['state'][h][j] * cd_ref[h] + v['incr'][h][j])
                carry['state'][h] = v['state'][h][nblk]

        def ret_outputs():
            v['or'] = [[None] * nblk for _ in range(RET_HEADS)]
            for j in range(nblk):
                for h in range(RET_HEADS):
                    u = h % 2
                    lhs = jnp.concatenate([chunk(v['vr'], h, j), v['state'][h][j]], axis=1)
                    rhs = jnp.concatenate([v['inner'][h // 2][j][:, u * RET_CHUNK:(u + 1) * RET_CHUNK],
                                           chunk(v['qr'], h, j) * qd_ref[h]], axis=0)
                    v['or'][h][j] = _dot(lhs, rhs)

        def gate_mix():
            mix = [jnp.concatenate(blocks, axis=1) for blocks in v['oa']]
            for h in range(RET_HEADS):
                o_h = _head_norm_t(jnp.concatenate(v['or'][h], axis=1))
                g_h = head(v['gate'], h)
                mix.append(o_h * (g_h * jax.nn.sigmoid(g_h)))
            v['mix'] = jnp.concatenate(mix, axis=0).astype(BF16)

        def out_project():
            y_ref[0, toks] = v['x'] + lax.dot_general(v['mix'], wout_ref[...], (((0,), (0,)), ((), ())),
                                                      preferred_element_type=F32)

        return [project, swa_prepare, swa_scores, ret_rotate, ret_scores, swa_softmax, swa_values,
                ret_outputs, gate_mix, out_project]

    pipelines = [stages(i) for i in range(tile // sub)]
    nstage = len(pipelines[0])
    for slot in range(nstage + STAGE_SKEW * (len(pipelines) - 1)):
        for i, pipe in enumerate(pipelines):
            if 0 <= slot - STAGE_SKEW * i < nstage:
                pipe[slot - STAGE_SKEW * i]()

    pk_ref[...], pv_ref[...] = carry['pk'], carry['pv']
    kwin_ref[0], vwin_ref[0] = carry['pk'].T, carry['pv'].T
    for h in range(RET_HEADS):
        s_ref[h] = carry['state'][h]

    @pl.when(t == pl.num_programs(1) - 1)
    def _():
        for h in range(RET_HEADS):
            sout_ref[0, h] = carry['state'][h].T


def _sample_in_kernel(x_ref, ca_ref, sa_ref, nmix_ref, wint_ref, qg_ref, kg_ref, bd_ref, feat_ref):
    _, _, first_half = _lane_consts()
    ca, sa, bd = ca_ref[...], sa_ref[...], bd_ref[...]
    hb = (_rms(x_ref[...]) * nmix_ref[...]).astype(BF16)
    proj = lambda lo, hi: lax.dot_general(hb, wint_ref[lo:hi, :], (((1,), (1,)), ((), ())),
                                          preferred_element_type=F32)
    swa = proj(C_QA, C_QR)
    slabs = [_rot_half(_head_norm(swa[:, s * LANES:(s + 1) * LANES], bd) * qg_ref[...], ca, sa, first_half)
             for s in range(N_QA_SLABS)]
    slabs.append(_rot_half(_head_norm(swa[:, C_KA:C_VA], bd) * kg_ref[...], ca, sa, first_half))
    feat_ref[...] = jnp.concatenate(slabs + [swa[:, C_VA:C_QR], proj(C_G, IN_COLS)], axis=1)


def _sample_swa_kernel(feat_ref, ckt_ref, cvt_ref, sink_ref, o_ref, kout_ref, vout_ref,
                       kn_ref, vn_ref, *, nseq, ntok, unroll):
    @pl.when(pl.program_id(0) == 0)
    def _():
        kn_ref[...] = jnp.zeros_like(kn_ref)
        vn_ref[...] = jnp.zeros_like(vn_ref)

    m_left, m_right, _ = _lane_consts()
    tail = WINDOW - ntok
    nrow = SWA_HEADS * ntok
    row_tok = lax.broadcasted_iota(jnp.int32, (nrow, WINDOW), 0) % ntok
    col = lax.broadcasted_iota(jnp.int32, (nrow, WINDOW), 1)
    valid_cache = col > row_tok
    valid_new = (col >= tail) & (col - tail <= row_tok)
    in_tail = lax.broadcasted_iota(jnp.int32, (LANES, WINDOW), 1) >= tail
    sink = sink_ref[...]
    shift = lambda a: pltpu.roll(a, HEAD_DIM, 1)

    def body(step, carry):
        seqs = [step * unroll + u for u in range(unroll)]
        rows = [pl.ds(pl.multiple_of(i * ntok, ntok), ntok) for i in seqs]
        q_rows, caches = [], []
        for u, i in enumerate(seqs):
            slab = lambda c, u=u: feat_ref[rows[u], c:c + LANES]
            kn_ref[u, tail:WINDOW] = slab(C_KA)
            vn_ref[u, tail:WINDOW] = slab(C_VA)
            s0, s1, s2, s3 = [slab(C_QA + s * LANES) for s in range(N_QA_SLABS)]
            q_rows.append(jnp.concatenate(
                [s0 * m_left, shift(s0 * m_right), s1 * m_left, shift(s1 * m_right),
                 shift(s2 * m_left), s2 * m_right, shift(s3 * m_left), s3 * m_right], axis=0))
            caches.append((ckt_ref[i], cvt_ref[i], kn_ref[u], vn_ref[u]))
        scores = [(_dot(q, kt), _dot_nt(q, kn)) for q, (kt, _, kn, _) in zip(q_rows, caches)]
        probs = []
        for s_c, s_n in scores:
            s_c = jnp.where(valid_cache, s_c * (HEAD_DIM ** -0.5), NEG_INF)
            s_n = jnp.where(valid_new, s_n * (HEAD_DIM ** -0.5), NEG_INF)
            m = jnp.maximum(jnp.maximum(jnp.max(s_c, axis=-1, keepdims=True),
                                        jnp.max(s_n, axis=-1, keepdims=True)), sink)
            p_c, p_n = jnp.exp(s_c - m), jnp.exp(s_n - m)
            denom = (jnp.sum(p_c, axis=-1, keepdims=True) + jnp.sum(p_n, axis=-1, keepdims=True)
                     + jnp.exp(sink - m))
            probs.append((p_c, p_n, denom))
        for u, i in enumerate(seqs):
            kt, vt, kn, vn = caches[u]
            p_c, p_n, denom = probs[u]
            o = (_dot_nt(p_c, vt) + _dot(p_n, vn)) / denom
            n = ntok
            left = lambda h: o[h * n:(h + 1) * n] * m_left
            right = lambda h: o[h * n:(h + 1) * n] * m_right
            o_ref[rows[u], 0:LANES] = left(0) + shift(left(1))
            o_ref[rows[u], LANES:2 * LANES] = left(2) + shift(left(3))
            o_ref[rows[u], 2 * LANES:3 * LANES] = shift(right(4)) + right(5)
            o_ref[rows[u], 3 * LANES:4 * LANES] = shift(right(6)) + right(7)
            kout_ref[i] = jnp.where(in_tail, kn.T, pltpu.roll(kt, tail, 1))
            vout_ref[i] = jnp.where(in_tail, vn.T, pltpu.roll(vt, tail, 1))
        return carry

    lax.fori_loop(0, nseq // unroll, body, 0)


def _sample_ret_in_kernel(x_ref, cr_ref, sr_ref, nmix_ref, wint_ref, qkv_ref):
    hb = (_rms(x_ref[...]) * nmix_ref[...]).astype(BF16)
    qkv = lax.dot_general(wint_ref[C_QR:C_G, :], hb, (((1,), (1,)), ((), ())), preferred_element_type=F32)
    n = qkv.shape[1]
    even_row = (lax.broadcasted_iota(jnp.int32, (RET_DIM, n), 0) % 2) == 0
    cr, sr = cr_ref[...], sr_ref[...]
    for h in range(RET_HEADS):
        q_rows = slice(h * RET_DIM, (h + 1) * RET_DIM)
        k_rows = slice(RET_W + h * RET_DIM, RET_W + (h + 1) * RET_DIM)
        qkv_ref[q_rows] = _rot_pairs_t(qkv[q_rows], cr, sr, even_row)
        qkv_ref[k_rows] = _rot_pairs_t(qkv[k_rows], cr, sr, even_row) * (RET_DIM ** -0.5)
    qkv_ref[2 * RET_W:] = qkv[2 * RET_W:]


def _sample_ret_kernel(q_ref, k_ref, v_ref, s_ref, c_ref, o_ref, so_ref, ks_ref, *, ntok):
    hd, nseq = RET_DIM, LANES
    tok = lambda t: slice(t * nseq, (t + 1) * nseq)
    row_qd, row_kd, row_cd = ntok * ntok, ntok * ntok + ntok, ntok * ntok + 2 * ntok
    group = SUBLANES
    pair_outs = []
    for u in range(2):
        feats = slice(u * hd, (u + 1) * hd)
        const = lambda r: c_ref[u, r:r + 1, :]
        for t in range(ntok):
            ks_ref[t] = k_ref[feats, tok(t)] * const(row_kd + t)
        outs = []
        for t in range(ntok):
            q_t = q_ref[feats, tok(t)]
            acc = jnp.zeros((hd, nseq), F32)
            for k in range(t + 1):
                w_tk = jnp.sum(q_t * k_ref[feats, tok(k)], axis=0, keepdims=True) * const(t * ntok + k)
                acc = acc + w_tk * v_ref[feats, tok(k)]

            def cross(j, carry, t=t):
                d0 = pl.multiple_of(j * group, group)
                q_rows = q_ref[pl.ds(u * hd + d0, group), tok(t)]
                for i in range(group):
                    carry = carry + q_rows[i:i + 1, :] * s_ref[u, d0 + i]
                return carry
            carried = lax.fori_loop(0, hd // group, cross, jnp.zeros((hd, nseq), F32))
            outs.append(acc + carried * const(row_qd + t))

        def update(j, carry):
            d0 = pl.multiple_of(j * group, group)
            k_rows = [ks_ref[t, pl.ds(d0, group), :] for t in range(ntok)]
            for i in range(group):
                new = s_ref[u, d0 + i] * const(row_cd)
                for t in range(ntok):
                    new = new + k_rows[t][i:i + 1, :] * v_ref[feats, tok(t)]
                so_ref[u, d0 + i] = new
            return carry
        lax.fori_loop(0, hd // group, update, 0)
        pair_outs.append(outs)
    for t in range(ntok):
        o_ref[t] = jnp.concatenate([pair_outs[0][t], pair_outs[1][t]], axis=0).T


def _sample_out_kernel(x_ref, oa_ref, or_ref, feat_ref, bd_ref, wout_ref, y_ref):
    outs = [oa_ref[...]]
    for p in range(N_RET_SLABS):
        gate = feat_ref[:, C_GATE_S + p * LANES:C_GATE_S + (p + 1) * LANES]
        o_ret = or_ref[:, p * LANES:(p + 1) * LANES]
        outs.append(_head_norm(o_ret, bd_ref[...]) * (gate * jax.nn.sigmoid(gate)))
    mix = jnp.concatenate(outs, axis=1).astype(BF16)
    y_ref[...] = x_ref[...] + jnp.dot(mix, wout_ref[...], preferred_element_type=F32)


def _mem_kv_kernel(m_ref, nmem_ref, wkv_ref, kg_ref, k_ref, v_ref):
    hb = (_rms(m_ref[...]) * nmem_ref[...]).astype(BF16)
    kv = jnp.dot(hb, wkv_ref[...], preferred_element_type=F32)
    slots = m_ref.shape[0]
    for h in range(MEM_HEADS):
        rows = pl.ds(h, slots, stride=MEM_HEADS)
        k_ref[rows, :] = _rms(kv[:, h * LANES:(h + 1) * LANES]) * kg_ref[...]
        v_ref[rows, :] = kv[:, MEM_W + h * LANES:MEM_W + (h + 1) * LANES]


def _mem_queries(x, ncross, wq_ref, qg):
    hb = (_rms(x) * ncross).astype(BF16)
    q = jnp.dot(hb, wq_ref[...], preferred_element_type=F32)
    qs = qg * (MEM_HEAD_DIM ** -0.5 * LOG2E)
    return [_rms(q[:, h * LANES:(h + 1) * LANES]) * qs for h in range(MEM_HEADS)]


def _prompt_cross_ffn_kernel(x_ref, mk_ref, mv_ref, ncross_ref, wq_ref, qg_ref, wo_ref, nffn_ref, wgu_ref,
                             wdown_ref, y_ref, xa_ref):
    @pl.when(pl.program_id(0) == 0)
    def _():
        xa_ref[...] = jnp.zeros_like(xa_ref)

    xa = xa_ref[...]
    hb = (_rms(xa) * nffn_ref[...]).astype(BF16)
    x = x_ref[...]
    qn = _mem_queries(x, ncross_ref[...], wq_ref, qg_ref[...])
    g = jnp.dot(hb, wgu_ref[:, 0:FFN_HIDDEN], preferred_element_type=F32)
    head = lambda ref, h: ref[0, pl.ds(h, MEM_LEN, stride=MEM_HEADS), :].astype(BF16)
    scores = [_dot_nt(head(mk_ref, h), qn[h]) for h in range(MEM_HEADS)]
    u = jnp.dot(hb, wgu_ref[:, FFN_HIDDEN:], preferred_element_type=F32)
    probs, denoms = [], []
    for s in scores:
        p = jnp.exp2(s - jnp.max(s, axis=0, keepdims=True))
        denoms.append(jnp.sum(p, axis=0, keepdims=True))
        probs.append(p.astype(BF16))
    act = (g * jax.nn.sigmoid(g) * u).astype(BF16)
    o_t = jnp.concatenate([_dot_tn(head(mv_ref, h), probs[h]) / denoms[h] for h in range(MEM_HEADS)], axis=0)
    y_ref[...] = xa + jnp.dot(act, wdown_ref[...], preferred_element_type=F32)
    xa_ref[...] = x + lax.dot_general(o_t.astype(BF16), wo_ref[...], (((0,), (0,)), ((), ())),
                                      preferred_element_type=F32)


def _sample_cross_attn_kernel(x_ref, mk_ref, mv_ref, ncross_ref, wq_ref, qg_ref, o_ref, *, nseq, tq):
    qn = _mem_queries(x_ref[...], ncross_ref[...], wq_ref, qg_ref[...])
    nrow = MEM_HEADS * tq
    nmem = MEM_LEN * MEM_HEADS
    own = (lax.broadcasted_iota(jnp.int32, (nrow, nmem), 0) // tq
           == lax.broadcasted_iota(jnp.int32, (nrow, nmem), 1) % MEM_HEADS)
    scores = []
    for i in range(nseq):
        q_stack = jnp.concatenate([qn[h][i * tq:(i + 1) * tq] for h in range(MEM_HEADS)], axis=0)
        scores.append(_dot_nt(q_stack, mk_ref[i]))
    probs, denoms = [], []
    for s in scores:
        s = jnp.where(own, s, NEG_INF)
        p = jnp.exp2(s - jnp.max(s, axis=-1, keepdims=True))
        denoms.append(jnp.sum(p, axis=-1, keepdims=True))
        probs.append(p.astype(BF16))
    for i in range(nseq):
        o = jnp.dot(probs[i], mv_ref[i].astype(BF16), preferred_element_type=F32) / denoms[i]
        for h in range(MEM_HEADS):
            o_ref[i * tq:(i + 1) * tq, h * LANES:(h + 1) * LANES] = o[h * tq:(h + 1) * tq]


def _cross_out_ffn_kernel(x_ref, o_ref, wo_ref, nffn_ref, wgu_ref, wdown_ref, y_ref):
    x = x_ref[...] + jnp.dot(o_ref[...].astype(BF16), wo_ref[...], preferred_element_type=F32)
    hb = (_rms(x) * nffn_ref[...]).astype(BF16)
    g = jnp.dot(hb, wgu_ref[:, 0:FFN_HIDDEN], preferred_element_type=F32)
    u = jnp.dot(hb, wgu_ref[:, FFN_HIDDEN:], preferred_element_type=F32)
    act = (g * jax.nn.sigmoid(g) * u).astype(BF16)
    y_ref[...] = x + jnp.dot(act, wdown_ref[...], preferred_element_type=F32)


def _f32(*arrays):
    return [np.ascontiguousarray(a, dtype=np.float32) for a in arrays]


def _rope_angles(pos):
    half = HEAD_DIM // 2
    inv = 1.0 / (ROPE_THETA ** (np.arange(half, dtype=np.float64) / half))
    return pos.astype(np.float64)[:, None] * inv[None, :]


def _ret_angles(pos):
    inv = RET_THETA ** (-np.linspace(0.0, 1.0, RET_DIM // 2, dtype=np.float64))
    return pos.astype(np.float64)[:, None] * inv[None, :]


def _rope_tables(pos):
    ang = _rope_angles(pos)
    cos, sin = np.cos(ang), np.sin(ang)
    c64 = np.concatenate([cos, cos], axis=-1)
    s64 = np.concatenate([-sin, sin], axis=-1)
    return _f32(np.tile(c64, (1, 2)), np.tile(s64, (1, 2)))


def _retention_decays(c):
    log_g = np.log(1.0 - np.exp2(-5.0 - np.arange(RET_HEADS, dtype=np.float64)))
    idx = np.arange(c, dtype=np.float64)
    diff = idx[:, None] - idx[None, :]
    dmat = np.where(diff >= 0, np.exp(np.maximum(diff, 0.0)[None] * log_g[:, None, None]), 0.0)
    qd = np.exp((idx + 1.0)[None, :] * log_g[:, None])
    kd = np.exp((c - 1.0 - idx)[None, :] * log_g[:, None])
    cd = np.exp(c * log_g)
    return dmat, qd, kd, cd


def _sample_decay_rows(c):
    dmat, qd, kd, cd = _retention_decays(c)
    rows = np.concatenate([dmat.reshape(RET_HEADS, c * c), qd, kd, cd[:, None]], axis=1)
    rows = np.pad(rows, ((0, 0), (0, -rows.shape[1] % 8)))
    return _f32(np.broadcast_to(rows[:, :, None], rows.shape + (LANES,)))[0]


def _full(shape):
    nd = len(shape)
    return pl.BlockSpec(shape, lambda *_: (0,) * nd)


def _params(sem):
    return pltpu.CompilerParams(dimension_semantics=sem, vmem_limit_bytes=VMEM_LIMIT)


def _prompt_tables_t(pos):
    ang = _rope_angles(pos).T
    ang_r = _ret_angles(pos).T
    cos_r, sin_r = np.cos(ang_r), np.sin(ang_r)
    cr = np.repeat(cos_r, 2, axis=0)
    sr = np.stack([-sin_r, sin_r], axis=1).reshape(RET_DIM, pos.shape[0])
    return _f32(np.cos(ang), np.sin(ang), cr, sr)


def _decay_consts_t(c):
    dmat, qd, kd, cd = _retention_decays(c)
    dm = dmat.transpose(0, 2, 1).reshape(N_RET_SLABS, 2, c, c).transpose(0, 2, 1, 3).reshape(N_RET_SLABS, c, 2 * c)
    cd = np.broadcast_to(cd[:, None, None], (RET_HEADS, 1, RET_DIM))
    return _f32(dm, qd[:, None, :], kd[:, None, :], cd)


def _prompt_mixer(x, w, tile, sub):
    b, l, d = x.shape
    ca, sa, cr, sr = _prompt_tables_t(np.arange(l, dtype=np.int32))
    dm, qd, kd, cd = _decay_consts_t(RET_CHUNK)
    sink = jnp.repeat(w['sinks'].reshape(SWA_KV_HEADS, 1, -1), WINDOW, axis=-1)
    tab = lambda rows: pl.BlockSpec((rows, tile), lambda i, t: (0, t))
    xspec = pl.BlockSpec((1, tile, d), lambda i, t: (i, t, 0))
    win_spec = pl.BlockSpec((1, WINDOW, LANES), lambda i, t: (i, 0, 0))
    st_spec = pl.BlockSpec((1, RET_HEADS, RET_DIM, RET_DIM), lambda i, t: (i, 0, 0, 0))
    return pl.pallas_call(
        functools.partial(_prompt_mixer_kernel, tile=tile, sub=sub),
        grid=(b, l // tile),
        in_specs=[xspec, tab(HEAD_DIM // 2), tab(HEAD_DIM // 2), tab(RET_DIM), tab(RET_DIM), _full((1, d)),
                  _full((IN_COLS, d)), _full((HEAD_DIM, 1)), _full((HEAD_DIM, 1)), _full(sink.shape),
                  _full((d, d)), _full(dm.shape), _full(qd.shape), _full(kd.shape), _full(cd.shape)],
        out_specs=[xspec, win_spec, win_spec, st_spec],
        out_shape=[jax.ShapeDtypeStruct(x.shape, F32),
                   jax.ShapeDtypeStruct((b, WINDOW, LANES), F32),
                   jax.ShapeDtypeStruct((b, WINDOW, LANES), F32),
                   jax.ShapeDtypeStruct((b, RET_HEADS, RET_DIM, RET_DIM), F32)],
        scratch_shapes=[pltpu.VMEM((SWA_KV_W, WINDOW), F32), pltpu.VMEM((SWA_KV_W, WINDOW), F32),
                        pltpu.VMEM((RET_HEADS, RET_DIM, RET_DIM), F32)],
        compiler_params=_params(("arbitrary", "arbitrary")),
        name="prompt_mixer",
    )(x, ca, sa, cr, sr, w['norm_mix'], w['w_in_t'], w['qg_col'], w['kg_col'], sink, w['w_out'], dm, qd, kd,
      cd)


def _sample_mixer(x, cache_k, cache_v, state, w, tile, nseq):
    b, ntok, d = x.shape
    rows = b * ntok
    xf = x.reshape(rows, d)
    pos = np.tile(PAST_LEN + np.arange(ntok, dtype=np.int32), b)
    ca, sa = _rope_tables(pos)
    tab = pl.BlockSpec((tile, LANES), lambda i: (i, 0))
    row_spec = lambda width: pl.BlockSpec((tile, width), lambda i: (i, 0))
    feat = pl.pallas_call(
        _sample_in_kernel,
        grid=(rows // tile,),
        in_specs=[row_spec(d), tab, tab, _full((1, d)), _full((IN_COLS, d)), _full((1, LANES)),
                  _full((1, LANES)), _full((LANES, LANES))],
        out_specs=row_spec(SAMPLE_FEAT),
        out_shape=jax.ShapeDtypeStruct((rows, SAMPLE_FEAT), F32),
        compiler_params=_params(("arbitrary",)),
        name="sample_in",
    )(xf, ca, sa, w['norm_mix'], w['w_in_t'], w['qg'], w['kg'], w['bd'])

    sink = jnp.repeat(w['sinks'], ntok)[:, None]
    seq_rows = nseq * ntok
    unroll = nseq
    cache_spec = pl.BlockSpec((nseq, LANES, WINDOW), lambda i: (i, 0, 0))
    o_a, k_out, v_out = pl.pallas_call(
        functools.partial(_sample_swa_kernel, nseq=nseq, ntok=ntok, unroll=unroll),
        grid=(b // nseq,),
        in_specs=[pl.BlockSpec((seq_rows, SAMPLE_FEAT), lambda i: (i, 0)), cache_spec, cache_spec,
                  _full(sink.shape)],
        out_specs=[pl.BlockSpec((seq_rows, SWA_Q_W), lambda i: (i, 0)), cache_spec, cache_spec],
        out_shape=[jax.ShapeDtypeStruct((rows, SWA_Q_W), F32),
                   jax.ShapeDtypeStruct(cache_k.shape, F32),
                   jax.ShapeDtypeStruct(cache_v.shape, F32)],
        scratch_shapes=[pltpu.VMEM((unroll, WINDOW, LANES), F32), pltpu.VMEM((unroll, WINDOW, LANES), F32)],
        compiler_params=_params(("arbitrary",)),
        name="sample_swa",
    )(feat, cache_k, cache_v, sink)

    assert b == LANES, "the retention step puts one sequence per lane"
    xt = jnp.transpose(x, (1, 0, 2)).reshape(rows, d)
    pos_t = np.repeat(PAST_LEN + np.arange(ntok, dtype=np.int32), b)
    _, _, cr_t, sr_t = _prompt_tables_t(pos_t)
    half = rows // 2
    tab_t = pl.BlockSpec((RET_DIM, half), lambda i: (0, i))
    qkv = pl.pallas_call(
        _sample_ret_in_kernel,
        grid=(2,),
        in_specs=[pl.BlockSpec((half, d), lambda i: (i, 0)), tab_t, tab_t, _full((1, d)),
                  _full((IN_COLS, d))],
        out_specs=pl.BlockSpec((3 * RET_W, half), lambda i: (0, i)),
        out_shape=jax.ShapeDtypeStruct((3 * RET_W, rows), F32),
        compiler_params=_params(("arbitrary",)),
        name="sample_ret_in",
    )(xt, cr_t, sr_t, w['norm_mix'], w['w_in_t'])
    consts = _sample_decay_rows(ntok)
    pair_rows = lambda off: pl.BlockSpec((LANES, rows), lambda p: (off + p, 0))
    st_spec = pl.BlockSpec((2, RET_DIM, RET_DIM, b), lambda p: (p, 0, 0, 0))
    o_r, s_out = pl.pallas_call(
        functools.partial(_sample_ret_kernel, ntok=ntok),
        grid=(N_RET_SLABS,),
        in_specs=[pair_rows(0), pair_rows(N_RET_SLABS), pair_rows(2 * N_RET_SLABS), st_spec,
                  pl.BlockSpec((2,) + consts.shape[1:], lambda p: (p, 0, 0))],
        out_specs=[pl.BlockSpec((ntok, b, LANES), lambda p: (0, 0, p)), st_spec],
        out_shape=[jax.ShapeDtypeStruct((ntok, b, RET_W), F32), jax.ShapeDtypeStruct(state.shape, F32)],
        scratch_shapes=[pltpu.VMEM((ntok, RET_DIM, b), F32)],
        compiler_params=_params(("arbitrary",)),
        name="sample_ret",
    )(qkv, qkv, qkv, state, consts)
    o_r = jnp.transpose(o_r, (1, 0, 2)).reshape(rows, RET_W)

    y = pl.pallas_call(
        _sample_out_kernel,
        grid=(rows // tile,),
        in_specs=[row_spec(d), row_spec(SWA_Q_W), row_spec(RET_W), row_spec(SAMPLE_FEAT), _full((LANES, LANES)),
                  _full((d, d))],
        out_specs=row_spec(d),
        out_shape=jax.ShapeDtypeStruct((rows, d), F32),
        compiler_params=_params(("arbitrary",)),
        name="sample_out",
    )(xf, o_a, o_r, feat, w['bd'], w['w_out'])
    return y, k_out, v_out, s_out


def _mem_kv(mem, w, tile):
    rows, d = mem.shape
    row_spec = lambda width: pl.BlockSpec((tile, width), lambda i: (i, 0))
    return pl.pallas_call(
        _mem_kv_kernel,
        grid=(rows // tile,),
        in_specs=[row_spec(d), _full((1, d)), _full((d, 2 * MEM_W)), _full((1, LANES))],
        out_specs=[pl.BlockSpec((tile * MEM_HEADS, MEM_HEAD_DIM), lambda i: (i, 0))] * 2,
        out_shape=[jax.ShapeDtypeStruct((rows * MEM_HEADS, MEM_HEAD_DIM), F32)] * 2,
        compiler_params=_params(("arbitrary",)),
        name="mem_kv",
    )(mem, w['norm_mem'], w['w_mkv'], w['kgm'])


def _prompt_cross_ffn(x, mk, mv, w, tile):
    rows, d = x.shape
    n = rows // tile
    per_mem = n // mk.shape[0]
    cur = lambda i: jnp.minimum(i, n - 1)
    single = lambda shape: pl.BlockSpec(shape, lambda i: (0,) * len(shape), pipeline_mode=pl.Buffered(1))
    mem_spec = pl.BlockSpec((1, MEM_LEN * MEM_HEADS, MEM_HEAD_DIM), lambda i: (cur(i) // per_mem, 0, 0))
    return pl.pallas_call(
        _prompt_cross_ffn_kernel,
        grid=(n + 1,),
        in_specs=[pl.BlockSpec((tile, d), lambda i: (cur(i), 0)), mem_spec, mem_spec, _full((1, d)),
                  single((d, MEM_W)), _full((1, LANES)), single((MEM_W, d)), _full((1, d)),
                  single((d, 2 * FFN_HIDDEN)), single((FFN_HIDDEN, d))],
        out_specs=pl.BlockSpec((tile, d), lambda i: (jnp.maximum(i - 1, 0), 0)),
        out_shape=jax.ShapeDtypeStruct((rows, d), F32),
        scratch_shapes=[pltpu.VMEM((tile, d), F32)],
        compiler_params=_params(("arbitrary",)),
        name="prompt_cross_ffn",
    )(x, mk, mv, w['norm_cross'], w['w_mq'], w['qgm'], w['w_mo'], w['norm_ffn'], w['w_gu'], w['w_down'])


def _sample_cross_attn(x, mk, mv, w, nseq, tq):
    rows, d = x.shape
    blk = nseq * tq
    mem_spec = pl.BlockSpec((nseq, MEM_LEN * MEM_HEADS, MEM_HEAD_DIM), lambda i: (i, 0, 0))
    return pl.pallas_call(
        functools.partial(_sample_cross_attn_kernel, nseq=nseq, tq=tq),
        grid=(rows // blk,),
        in_specs=[pl.BlockSpec((blk, d), lambda i: (i, 0)), mem_spec, mem_spec, _full((1, d)),
                  _full((d, MEM_W)), _full((1, LANES))],
        out_specs=pl.BlockSpec((blk, MEM_W), lambda i: (i, 0)),
        out_shape=jax.ShapeDtypeStruct((rows, MEM_W), F32),
        compiler_params=_params(("arbitrary",)),
        name="sample_cross_attn",
    )(x, mk, mv, w['norm_cross'], w['w_mq'], w['qgm'])


def _sample_cross_out_ffn(x, o, w, tile):
    rows, d = x.shape
    row_spec = lambda width: pl.BlockSpec((tile, width), lambda i: (i, 0))
    single = lambda shape: pl.BlockSpec(shape, lambda i: (0,) * len(shape), pipeline_mode=pl.Buffered(1))
    return pl.pallas_call(
        _cross_out_ffn_kernel,
        grid=(rows // tile,),
        in_specs=[row_spec(d), row_spec(MEM_W), single((MEM_W, d)), _full((1, d)),
                  single((d, 2 * FFN_HIDDEN)), single((FFN_HIDDEN, d))],
        out_specs=row_spec(d),
        out_shape=jax.ShapeDtypeStruct((rows, d), F32),
        compiler_params=_params(("arbitrary",)),
        name="sample_ffn",
    )(x, o, w['w_mo'], w['norm_ffn'], w['w_gu'], w['w_down'])


def kernel(x_prompt, x_sample, mem_prompt, cache_swa_k, cache_swa_v, state_ret, cache_mem_k, cache_mem_v,
           norm_mix, w_in, q_norm_a, k_norm_a, sinks, w_out, norm_cross, norm_mem, w_mq, w_mkv,
           q_norm_m, k_norm_m, w_mo, norm_ffn, w_gu, w_down):
    assert norm_mix.shape[0] == 1, "single-layer kernel"
    b, l, d = x_prompt.shape
    sb, st, _ = x_sample.shape
    half = (np.arange(LANES) // HEAD_DIM)[:, None] == (np.arange(LANES) // HEAD_DIM)[None, :]
    w = {
        'norm_mix': norm_mix, 'norm_cross': norm_cross, 'norm_mem': norm_mem, 'norm_ffn': norm_ffn,
        'w_out': w_out[0].astype(BF16), 'w_mq': w_mq[0].astype(BF16),
        'w_mkv': w_mkv[0].astype(BF16), 'w_mo': w_mo[0].astype(BF16), 'w_gu': w_gu[0].astype(BF16),
        'w_down': w_down[0].astype(BF16),
        'w_in_t': w_in[0].T.astype(BF16),
        'qg': jnp.tile(q_norm_a, (1, 2)), 'kg': jnp.tile(k_norm_a, (1, 2)),
        'qg_col': q_norm_a.reshape(HEAD_DIM, 1), 'kg_col': k_norm_a.reshape(HEAD_DIM, 1),
        'qgm': q_norm_m, 'kgm': k_norm_m, 'sinks': sinks[0],
        'bd': half.astype(BF16),
    }

    mk, mv = _mem_kv(mem_prompt.reshape(b * MEM_LEN, d), w, tile=MEM_KV_TILE)
    xp, kwin, vwin, ret_p = _prompt_mixer(x_prompt, w, tile=PROMPT_MIXER_TILE, sub=PROMPT_MIXER_SUB)
    xp = xp.reshape(b * l, d)
    mem_rows_p = (b, MEM_LEN * MEM_HEADS, MEM_HEAD_DIM)
    yp = _prompt_cross_ffn(xp, mk.reshape(mem_rows_p), mv.reshape(mem_rows_p), w, tile=PROMPT_FFN_TILE)
    yp = yp.reshape(b, l, d)
    ret_p = ret_p.reshape(1, b, RET_HEADS, RET_DIM, RET_DIM)

    nbuf = cache_swa_k.shape[2]
    assert nbuf == WINDOW
    to_feature_major = lambda c: jnp.transpose(c[0], (0, 2, 3, 1)).reshape(sb, SWA_KV_W, nbuf)
    from_feature_major = lambda c: jnp.transpose(c.reshape(sb, SWA_KV_HEADS, HEAD_DIM, nbuf), (0, 3, 1, 2))[None]
    xs, k_s, v_s, s_s = _sample_mixer(x_sample, to_feature_major(cache_swa_k), to_feature_major(cache_swa_v),
                                      jnp.transpose(state_ret[0], (1, 2, 3, 0)), w, tile=SAMPLE_TILE,
                                      nseq=SAMPLE_SEQS)
    mem_rows = (sb, MEM_LEN * MEM_HEADS, MEM_HEAD_DIM)
    os_ = _sample_cross_attn(xs, cache_mem_k.reshape(mem_rows), cache_mem_v.reshape(mem_rows), w,
                             nseq=SAMPLE_SEQS, tq=st)
    ys = _sample_cross_out_ffn(xs, os_, w, tile=SAMPLE_FFN_TILE).reshape(sb, st, d)

    kv_shape = (1, b, WINDOW, SWA_KV_HEADS, HEAD_DIM)
    mem_shape = (1, b, MEM_LEN, MEM_HEADS, MEM_HEAD_DIM)
    return (yp, ys, kwin.reshape(kv_shape), vwin.reshape(kv_shape), ret_p,
            mk.reshape(mem_shape), mv.reshape(mem_shape),
            from_feature_major(k_s), from_feature_major(v_s),
            jnp.transpose(s_s, (3, 0, 1, 2))[None])
```

```python
import functools

import jax
import jax.numpy as jnp
import numpy as np
from jax import lax
from jax.experimental import pallas as pl
from jax.experimental.pallas import tpu as pltpu

F32 = jnp.float32
BF16 = jnp.bfloat16

LANES = 128
SUBLANES = 8
D_MODEL = 1024
HEAD_DIM = 64
SWA_HEADS = 8
SWA_KV_HEADS = 2
WINDOW = 128
RET_HEADS = 8
RET_DIM = 64
RET_CHUNK = 128
RET_THETA = 10000.0
ROPE_THETA = 10000.0
MEM_LEN = 256
MEM_HEADS = 4
MEM_HEAD_DIM = 128
MEM_W = MEM_HEADS * MEM_HEAD_DIM
FFN_HIDDEN = 2816
RMS_EPS = 1e-6
NEG_INF = -1e30
LOG2E = 1.4426950408889634
PAST_LEN = 16384

SWA_Q_W = SWA_HEADS * HEAD_DIM
SWA_KV_W = SWA_KV_HEADS * HEAD_DIM
RET_W = RET_HEADS * RET_DIM
IN_COLS = SWA_Q_W + 2 * SWA_KV_W + 4 * RET_W
C_QA, C_KA, C_VA = 0, SWA_Q_W, SWA_Q_W + SWA_KV_W
C_QR = SWA_Q_W + 2 * SWA_KV_W
C_KR, C_VR, C_G = C_QR + RET_W, C_QR + 2 * RET_W, C_QR + 3 * RET_W
C_GATE_S = C_QR
SAMPLE_FEAT = C_QR + RET_W
N_QA_SLABS = SWA_Q_W // LANES
N_RET_SLABS = RET_W // LANES

V7X_VMEM_BYTES = 64 * 1024 * 1024
VMEM_LIMIT = V7X_VMEM_BYTES - 8 * 1024 * 1024
PROMPT_MIXER_TILE, PROMPT_MIXER_SUB = 1024, 512
PROMPT_FFN_TILE = 512
MEM_KV_TILE = 512
SAMPLE_TILE = 512
SAMPLE_FFN_TILE = 512
SAMPLE_SEQS = 8
STAGE_SKEW = 1


def _dot(a, b):
    return jnp.dot(a.astype(BF16), b.astype(BF16), preferred_element_type=F32)


def _dot_nt(a, b):
    return lax.dot_general(a.astype(BF16), b.astype(BF16), (((1,), (1,)), ((), ())),
                           preferred_element_type=F32)


def _rms(x):
    return x * lax.rsqrt(jnp.mean(x * x, axis=-1, keepdims=True) + RMS_EPS)


def _lane_consts():
    lane = lax.broadcasted_iota(jnp.int32, (1, LANES), 1)
    m_left = (lane < HEAD_DIM).astype(F32)
    m_right = 1.0 - m_left
    first_half = (lane % HEAD_DIM) < (HEAD_DIM // 2)
    return m_left, m_right, first_half


def _head_norm(y, bd):
    ss = jnp.dot((y * y).astype(BF16), bd, preferred_element_type=F32) * (1.0 / HEAD_DIM)
    return y * lax.rsqrt(ss + RMS_EPS)


def _rot_half(y, cos, sin_signed, first_half):
    swapped = jnp.where(first_half, pltpu.roll(y, LANES - HEAD_DIM // 2, 1), pltpu.roll(y, HEAD_DIM // 2, 1))
    return y * cos + swapped * sin_signed


def _dot_tn(a, b):
    return lax.dot_general(a.astype(BF16), b.astype(BF16), (((0,), (0,)), ((), ())),
                           preferred_element_type=F32)


def _head_norm_t(y):
    return y * lax.rsqrt(jnp.mean(y * y, axis=0, keepdims=True) + RMS_EPS)


def _rot_half_t(y, cos, sin):
    half = HEAD_DIM // 2
    y1, y2 = y[0:half], y[half:]
    return jnp.concatenate([y1 * cos - y2 * sin, y2 * cos + y1 * sin], axis=0)


def _rot_pairs_t(y, cos, sin_signed, even_row):
    n = y.shape[0]
    swapped = jnp.where(even_row, pltpu.roll(y, n - 1, 0), pltpu.roll(y, 1, 0))
    return y * cos + swapped * sin_signed


def _prompt_mixer_kernel(x_ref, ca_ref, sa_ref, cr_ref, sr_ref, nmix_ref, wint_ref, qg_ref, kg_ref,
                         sink_ref, wout_ref, dm_ref, qd_ref, kd_ref, cd_ref,
                         y_ref, kwin_ref, vwin_ref, sout_ref,
                         pk_ref, pv_ref, s_ref, *, tile, sub):
    t = pl.program_id(1)
    nblk = sub // WINDOW
    hd = HEAD_DIM
    group = SWA_HEADS // SWA_KV_HEADS

    @pl.when(t == 0)
    def _():
        pk_ref[...] = jnp.zeros_like(pk_ref)
        pv_ref[...] = jnp.zeros_like(pv_ref)
        s_ref[...] = jnp.zeros_like(s_ref)

    qg, kg = qg_ref[...], kg_ref[...]
    qs = qg * (hd ** -0.5 * LOG2E)
    head = lambda a, h: a[h * hd:(h + 1) * hd]
    chunk = lambda a, h, j: a[h * hd:(h + 1) * hd, j * RET_CHUNK:(j + 1) * RET_CHUNK]
    keys_of = lambda j: slice(j * WINDOW, (j + 2) * WINDOW)
    units = [(g, j) for j in range(nblk) for g in range(SWA_KV_HEADS)]
    key = lax.broadcasted_iota(jnp.int32, (2 * WINDOW, group * WINDOW), 0)
    qry = lax.broadcasted_iota(jnp.int32, (2 * WINDOW, group * WINDOW), 1) % WINDOW
    band = (key > qry) & (key <= qry + WINDOW)
    even_row = (lax.broadcasted_iota(jnp.int32, (RET_DIM, sub), 0) % 2) == 0
    zeros = jnp.zeros((hd, RET_CHUNK), F32)
    carry = {'pk': pk_ref[...], 'pv': pv_ref[...], 'state': [s_ref[h] for h in range(RET_HEADS)]}

    def stages(idx):
        toks = slice(idx * sub, (idx + 1) * sub)
        v = {}

        def project():
            v['x'] = x_ref[0, toks]
            hb = (_rms(v['x']) * nmix_ref[...]).astype(BF16)
            proj = lambda lo, hi: lax.dot_general(wint_ref[lo:hi, :], hb, (((1,), (1,)), ((), ())),
                                                  preferred_element_type=F32)
            v['swa'], v['ret'] = proj(C_QA, C_QR), proj(C_QR, C_VR)
            v['vr'], v['gate'] = proj(C_VR, C_G), proj(C_G, IN_COLS)

        def swa_prepare():
            ca, sa = ca_ref[:, toks], sa_ref[:, toks]
            swa_t = v['swa']
            v['qa'] = [_rot_half_t(_head_norm_t(head(swa_t, h)) * qs, ca, sa) for h in range(SWA_HEADS)]
            ka = jnp.concatenate([_rot_half_t(_head_norm_t(head(swa_t, SWA_HEADS + g)) * kg, ca, sa)
                                  for g in range(SWA_KV_HEADS)], axis=0)
            va = swa_t[C_VA:C_QR]
            v['kfull'] = jnp.concatenate([carry['pk'], ka], axis=1).astype(BF16)
            v['vfull'] = jnp.concatenate([carry['pv'], va], axis=1).astype(BF16)
            carry['pk'], carry['pv'] = ka[:, sub - WINDOW:], va[:, sub - WINDOW:]

        def swa_scores():
            v['scores'] = []
            for g, j in units:
                blk = slice(j * WINDOW, (j + 1) * WINDOW)
                q4 = jnp.concatenate([v['qa'][group * g + u][:, blk] for u in range(group)], axis=1)
                v['scores'].append(_dot_tn(head(v['kfull'], g)[:, keys_of(j)], q4))

        def ret_rotate():
            cr, sr = cr_ref[:, toks], sr_ref[:, toks]
            rot = lambda a: jnp.concatenate([_rot_pairs_t(head(a, h), cr, sr, even_row)
                                             for h in range(RET_HEADS)], axis=0)
            v['qr'] = rot(v['ret'][0:RET_W])
            v['kr'] = rot(v['ret'][RET_W:]) * (RET_DIM ** -0.5)

        def ret_scores():
            qr, kr, vr = v['qr'], v['kr'], v['vr']
            v['inner'] = [[None] * nblk for _ in range(N_RET_SLABS)]
            for j in range(nblk):
                for pr in range(N_RET_SLABS):
                    q_bd = jnp.concatenate([jnp.concatenate([chunk(qr, 2 * pr, j), zeros], axis=1),
                                            jnp.concatenate([zeros, chunk(qr, 2 * pr + 1, j)], axis=1)], axis=0)
                    k2 = kr[pr * LANES:(pr + 1) * LANES, j * RET_CHUNK:(j + 1) * RET_CHUNK]
                    v['inner'][pr][j] = _dot_tn(k2, q_bd) * dm_ref[pr]
            v['incr'] = [[_dot_nt(chunk(vr, h, j), chunk(kr, h, j) * kd_ref[h]) for j in range(nblk)]
                         for h in range(RET_HEADS)]

        def swa_softmax():
            v['probs'], v['denoms'] = [], []
            for (g, j), s in zip(units, v['scores']):
                sink = sink_ref[g] * LOG2E
                valid = band & (key >= WINDOW * (1 - t)) if (idx == 0 and j == 0) else band
                s = jnp.where(valid, s, NEG_INF)
                m = jnp.maximum(jnp.max(s, axis=0, keepdims=True), sink)
                p = jnp.exp2(s - m)
                v['denoms'].append(jnp.sum(p, axis=0, keepdims=True) + jnp.exp2(sink - m))
                v['probs'].append(p.astype(BF16))

        def swa_values():
            v['oa'] = [[None] * nblk for _ in range(SWA_HEADS)]
            for (g, j), p, denom in zip(units, v['probs'], v['denoms']):
                o = jnp.dot(head(v['vfull'], g)[:, keys_of(j)], p, preferred_element_type=F32) / denom
                for u in range(group):
                    v['oa'][group * g + u][j] = o[:, u * WINDOW:(u + 1) * WINDOW]
            v['state'] = [[carry['state'][h]] for h in range(RET_HEADS)]
            for h in range(RET_HEADS):
                for j in range(nblk):
                    v['state'][h].append(v['state'][h][j] * cd_ref[h] + v['incr'][h][j])
                carry['state'][h] = v['state'][h][nblk]

        def ret_outputs():
            v['or'] = [[None] * nblk for _ in range(RET_HEADS)]
            for j in range(nblk):
                for h in range(RET_HEADS):
                    u = h % 2
                    lhs = jnp.concatenate([chunk(v['vr'], h, j), v['state'][h][j]], axis=1)
                    rhs = jnp.concatenate([v['inner'][h // 2][j][:, u * RET_CHUNK:(u + 1) * RET_CHUNK],
                                           chunk(v['qr'], h, j) * qd_ref[h]], axis=0)
                    v['or'][h][j] = _dot(lhs, rhs)

        def gate_mix():
            mix = [jnp.concatenate(blocks, axis=1) for blocks in v['oa']]
            for h in range(RET_HEADS):
                o_h = _head_norm_t(jnp.concatenate(v['or'][h], axis=1))
                g_h = head(v['gate'], h)
                mix.append(o_h * (g_h * jax.nn.sigmoid(g_h)))
            v['mix'] = jnp.concatenate(mix, axis=0).astype(BF16)

        def out_project():
            y_ref[0, toks] = v['x'] + lax.dot_general(v['mix'], wout_ref[...], (((0,), (0,)), ((), ())),
                                                      preferred_element_type=F32)

        return [project, swa_prepare, swa_scores, ret_rotate, ret_scores, swa_softmax, swa_values,
                ret_outputs, gate_mix, out_project]

    pipelines = [stages(i) for i in range(tile // sub)]
    nstage = len(pipelines[0])
    for slot in range(nstage + STAGE_SKEW * (len(pipelines) - 1)):
        for i, pipe in enumerate(pipelines):
            if 0 <= slot - STAGE_SKEW * i < nstage:
                pipe[slot - STAGE_SKEW * i]()

    pk_ref[...], pv_ref[...] = carry['pk'], carry['pv']
    kwin_ref[0], vwin_ref[0] = carry['pk'].T, carry['pv'].T
    for h in range(RET_HEADS):
        s_ref[h] = carry['state'][h]

    @pl.when(t == pl.num_programs(1) - 1)
    def _():
        for h in range(RET_HEADS):
            sout_ref[0, h] = carry['state'][h].T


def _sample_in_kernel(x_ref, ca_ref, sa_ref, nmix_ref, wint_ref, qg_ref, kg_ref, bd_ref, feat_ref):
    _, _, first_half = _lane_consts()
    ca, sa, bd = ca_ref[...], sa_ref[...], bd_ref[...]
    hb = (_rms(x_ref[...]) * nmix_ref[...]).astype(BF16)
    proj = lambda lo, hi: lax.dot_general(hb, wint_ref[lo:hi, :], (((1,), (1,)), ((), ())),
                                          preferred_element_type=F32)
    swa = proj(C_QA, C_QR)
    slabs = [_rot_half(_head_norm(swa[:, s * LANES:(s + 1) * LANES], bd) * qg_ref[...], ca, sa, first_half)
             for s in range(N_QA_SLABS)]
    slabs.append(_rot_half(_head_norm(swa[:, C_KA:C_VA], bd) * kg_ref[...], ca, sa, first_half))
    feat_ref[...] = jnp.concatenate(slabs + [swa[:, C_VA:C_QR], proj(C_G, IN_COLS)], axis=1)


def _sample_swa_kernel(feat_ref, ckt_ref, cvt_ref, sink_ref, o_ref, kout_ref, vout_ref,
                       kn_ref, vn_ref, *, nseq, ntok, unroll):
    @pl.when(pl.program_id(0) == 0)
    def _():
        kn_ref[...] = jnp.zeros_like(kn_ref)
        vn_ref[...] = jnp.zeros_like(vn_ref)

    m_left, m_right, _ = _lane_consts()
    tail = WINDOW - ntok
    nrow = SWA_HEADS * ntok
    row_tok = lax.broadcasted_iota(jnp.int32, (nrow, WINDOW), 0) % ntok
    col = lax.broadcasted_iota(jnp.int32, (nrow, WINDOW), 1)
    valid_cache = col > row_tok
    valid_new = (col >= tail) & (col - tail <= row_tok)
    in_tail = lax.broadcasted_iota(jnp.int32, (LANES, WINDOW), 1) >= tail
    sink = sink_ref[...]
    shift = lambda a: pltpu.roll(a, HEAD_DIM, 1)

    def body(step, carry):
        seqs = [step * unroll + u for u in range(unroll)]
        rows = [pl.ds(pl.multiple_of(i * ntok, ntok), ntok) for i in seqs]
        q_rows, caches = [], []
        for u, i in enumerate(seqs):
            slab = lambda c, u=u: feat_ref[rows[u], c:c + LANES]
            kn_ref[u, tail:WINDOW] = slab(C_KA)
            vn_ref[u, tail:WINDOW] = slab(C_VA)
            s0, s1, s2, s3 = [slab(C_QA + s * LANES) for s in range(N_QA_SLABS)]
            q_rows.append(jnp.concatenate(
                [s0 * m_left, shift(s0 * m_right), s1 * m_left, shift(s1 * m_right),
                 shift(s2 * m_left), s2 * m_right, shift(s3 * m_left), s3 * m_right], axis=0))
            caches.append((ckt_ref[i], cvt_ref[i], kn_ref[u], vn_ref[u]))
        scores = [(_dot(q, kt), _dot_nt(q, kn)) for q, (kt, _, kn, _) in zip(q_rows, caches)]
        probs = []
        for s_c, s_n in scores:
            s_c = jnp.where(valid_cache, s_c * (HEAD_DIM ** -0.5), NEG_INF)
            s_n = jnp.where(valid_new, s_n * (HEAD_DIM ** -0.5), NEG_INF)
            m = jnp.maximum(jnp.maximum(jnp.max(s_c, axis=-1, keepdims=True),
                                        jnp.max(s_n, axis=-1, keepdims=True)), sink)
            p_c, p_n = jnp.exp(s_c - m), jnp.exp(s_n - m)
            denom = (jnp.sum(p_c, axis=-1, keepdims=True) + jnp.sum(p_n, axis=-1, keepdims=True)
                     + jnp.exp(sink - m))
            probs.append((p_c, p_n, denom))
        for u, i in enumerate(seqs):
            kt, vt, kn, vn = caches[u]
            p_c, p_n, denom = probs[u]
            o = (_dot_nt(p_c, vt) + _dot(p_n, vn)) / denom
            n = ntok
            left = lambda h: o[h * n:(h + 1) * n] * m_left
            right = lambda h: o[h * n:(h + 1) * n] * m_right
            o_ref[rows[u], 0:LANES] = left(0) + shift(left(1))
            o_ref[rows[u], LANES:2 * LANES] = left(2) + shift(left(3))
            o_ref[rows[u], 2 * LANES:3 * LANES] = shift(right(4)) + right(5)
            o_ref[rows[u], 3 * LANES:4 * LANES] = shift(right(6)) + right(7)
            kout_ref[i] = jnp.where(in_tail, kn.T, pltpu.roll(kt, tail, 1))
            vout_ref[i] = jnp.where(in_tail, vn.T, pltpu.roll(vt, tail, 1))
        return carry

    lax.fori_loop(0, nseq // unroll, body, 0)


def _sample_ret_in_kernel(x_ref, cr_ref, sr_ref, nmix_ref, wint_ref, qkv_ref):
    hb = (_rms(x_ref[...]) * nmix_ref[...]).astype(BF16)
    qkv = lax.dot_general(wint_ref[C_QR:C_G, :], hb, (((1,), (1,)), ((), ())), preferred_element_type=F32)
    n = qkv.shape[1]
    even_row = (lax.broadcasted_iota(jnp.int32, (RET_DIM, n), 0) % 2) == 0
    cr, sr = cr_ref[...], sr_ref[...]
    for h in range(RET_HEADS):
        q_rows = slice(h * RET_DIM, (h + 1) * RET_DIM)
        k_rows = slice(RET_W + h * RET_DIM, RET_W + (h + 1) * RET_DIM)
        qkv_ref[q_rows] = _rot_pairs_t(qkv[q_rows], cr, sr, even_row)
        qkv_ref[k_rows] = _rot_pairs_t(qkv[k_rows], cr, sr, even_row) * (RET_DIM ** -0.5)
    qkv_ref[2 * RET_W:] = qkv[2 * RET_W:]


def _sample_ret_kernel(q_ref, k_ref, v_ref, s_ref, c_ref, o_ref, so_ref, ks_ref, *, ntok):
    hd, nseq = RET_DIM, LANES
    tok = lambda t: slice(t * nseq, (t + 1) * nseq)
    row_qd, row_kd, row_cd = ntok * ntok, ntok * ntok + ntok, ntok * ntok + 2 * ntok
    group = SUBLANES
    pair_outs = []
    for u in range(2):
        feats = slice(u * hd, (u + 1) * hd)
        const = lambda r: c_ref[u, r:r + 1, :]
        for t in range(ntok):
            ks_ref[t] = k_ref[feats, tok(t)] * const(row_kd + t)
        outs = []
        for t in range(ntok):
            q_t = q_ref[feats, tok(t)]
            acc = jnp.zeros((hd, nseq), F32)
            for k in range(t + 1):
                w_tk = jnp.sum(q_t * k_ref[feats, tok(k)], axis=0, keepdims=True) * const(t * ntok + k)
                acc = acc + w_tk * v_ref[feats, tok(k)]

            def cross(j, carry, t=t):
                d0 = pl.multiple_of(j * group, group)
                q_rows = q_ref[pl.ds(u * hd + d0, group), tok(t)]
                for i in range(group):
                    carry = carry + q_rows[i:i + 1, :] * s_ref[u, d0 + i]
                return carry
            carried = lax.fori_loop(0, hd // group, cross, jnp.zeros((hd, nseq), F32))
            outs.append(acc + carried * const(row_qd + t))

        def update(j, carry):
            d0 = pl.multiple_of(j * group, group)
            k_rows = [ks_ref[t, pl.ds(d0, group), :] for t in range(ntok)]
            for i in range(group):
                new = s_ref[u, d0 + i] * const(row_cd)
                for t in range(ntok):
                    new = new + k_rows[t][i:i + 1, :] * v_ref[feats, tok(t)]
                so_ref[u, d0 + i] = new
            return carry
        lax.fori_loop(0, hd // group, update, 0)
        pair_outs.append(outs)
    for t in range(ntok):
        o_ref[t] = jnp.concatenate([pair_outs[0][t], pair_outs[1][t]], axis=0).T


def _sample_out_kernel(x_ref, oa_ref, or_ref, feat_ref, bd_ref, wout_ref, y_ref):
    outs = [oa_ref[...]]
    for p in range(N_RET_SLABS):
        gate = feat_ref[:, C_GATE_S + p * LANES:C_GATE_S + (p + 1) * LANES]
        o_ret = or_ref[:, p * LANES:(p + 1) * LANES]
        outs.append(_head_norm(o_ret, bd_ref[...]) * (gate * jax.nn.sigmoid(gate)))
    mix = jnp.concatenate(outs, axis=1).astype(BF16)
    y_ref[...] = x_ref[...] + jnp.dot(mix, wout_ref[...], preferred_element_type=F32)


def _mem_kv_kernel(m_ref, nmem_ref, wkv_ref, kg_ref, k_ref, v_ref):
    hb = (_rms(m_ref[...]) * nmem_ref[...]).astype(BF16)
    kv = jnp.dot(hb, wkv_ref[...], preferred_element_type=F32)
    slots = m_ref.shape[0]
    for h in range(MEM_HEADS):
        rows = pl.ds(h, slots, stride=MEM_HEADS)
        k_ref[rows, :] = _rms(kv[:, h * LANES:(h + 1) * LANES]) * kg_ref[...]
        v_ref[rows, :] = kv[:, MEM_W + h * LANES:MEM_W + (h + 1) * LANES]


def _mem_queries(x, ncross, wq_ref, qg):
    hb = (_rms(x) * ncross).astype(BF16)
    q = jnp.dot(hb, wq_ref[...], preferred_element_type=F32)
    qs = qg * (MEM_HEAD_DIM ** -0.5 * LOG2E)
    return [_rms(q[:, h * LANES:(h + 1) * LANES]) * qs for h in range(MEM_HEADS)]


def _prompt_cross_ffn_kernel(x_ref, mk_ref, mv_ref, ncross_ref, wq_ref, qg_ref, wo_ref, nffn_ref, wgu_ref,
                             wdown_ref, y_ref, xa_ref):
    @pl.when(pl.program_id(0) == 0)
    def _():
        xa_ref[...] = jnp.zeros_like(xa_ref)

    xa = xa_ref[...]
    hb = (_rms(xa) * nffn_ref[...]).astype(BF16)
    x = x_ref[...]
    qn = _mem_queries(x, ncross_ref[...], wq_ref, qg_ref[...])
    g = jnp.dot(hb, wgu_ref[:, 0:FFN_HIDDEN], preferred_element_type=F32)
    head = lambda ref, h: ref[0, pl.ds(h, MEM_LEN, stride=MEM_HEADS), :].astype(BF16)
    scores = [_dot_nt(head(mk_ref, h), qn[h]) for h in range(MEM_HEADS)]
    u = jnp.dot(hb, wgu_ref[:, FFN_HIDDEN:], preferred_element_type=F32)
    probs, denoms = [], []
    for s in scores:
        p = jnp.exp2(s - jnp.max(s, axis=0, keepdims=True))
        denoms.append(jnp.sum(p, axis=0, keepdims=True))
        probs.append(p.astype(BF16))
    act = (g * jax.nn.sigmoid(g) * u).astype(BF16)
    o_t = jnp.concatenate([_dot_tn(head(mv_ref, h), probs[h]) / denoms[h] for h in range(MEM_HEADS)], axis=0)
    y_ref[...] = xa + jnp.dot(act, wdown_ref[...], preferred_element_type=F32)
    xa_ref[...] = x + lax.dot_general(o_t.astype(BF16), wo_ref[...], (((0,), (0,)), ((), ())),
                                      preferred_element_type=F32)


def _sample_cross_attn_kernel(x_ref, mk_ref, mv_ref, ncross_ref, wq_ref, qg_ref, o_ref, *, nseq, tq):
    qn = _mem_queries(x_ref[...], ncross_ref[...], wq_ref, qg_ref[...])
    nrow = MEM_HEADS * tq
    nmem = MEM_LEN * MEM_HEADS
    own = (lax.broadcasted_iota(jnp.int32, (nrow, nmem), 0) // tq
           == lax.broadcasted_iota(jnp.int32, (nrow, nmem), 1) % MEM_HEADS)
    scores = []
    for i in range(nseq):
        q_stack = jnp.concatenate([qn[h][i * tq:(i + 1) * tq] for h in range(MEM_HEADS)], axis=0)
        scores.append(_dot_nt(q_stack, mk_ref[i]))
    probs, denoms = [], []
    for s in scores:
        s = jnp.where(own, s, NEG_INF)
        p = jnp.exp2(s - jnp.max(s, axis=-1, keepdims=True))
        denoms.append(jnp.sum(p, axis=-1, keepdims=True))
        probs.append(p.astype(BF16))
    for i in range(nseq):
        o = jnp.dot(probs[i], mv_ref[i].astype(BF16), preferred_element_type=F32) / denoms[i]
        for h in range(MEM_HEADS):
            o_ref[i * tq:(i + 1) * tq, h * LANES:(h + 1) * LANES] = o[h * tq:(h + 1) * tq]


def _cross_out_ffn_kernel(x_ref, o_ref, wo_ref, nffn_ref, wgu_ref, wdown_ref, y_ref):
    x = x_ref[...] + jnp.dot(o_ref[...].astype(BF16), wo_ref[...], preferred_element_type=F32)
    hb = (_rms(x) * nffn_ref[...]).astype(BF16)
    g = jnp.dot(hb, wgu_ref[:, 0:FFN_HIDDEN], preferred_element_type=F32)
    u = jnp.dot(hb, wgu_ref[:, FFN_HIDDEN:], preferred_element_type=F32)
    act = (g * jax.nn.sigmoid(g) * u).astype(BF16)
    y_ref[...] = x + jnp.dot(act, wdown_ref[...], preferred_element_type=F32)


def _f32(*arrays):
    return [np.ascontiguousarray(a, dtype=np.float32) for a in arrays]


def _rope_angles(pos):
    half = HEAD_DIM // 2
    inv = 1.0 / (ROPE_THETA ** (np.arange(half, dtype=np.float64) / half))
    return pos.astype(np.float64)[:, None] * inv[None, :]


def _ret_angles(pos):
    inv = RET_THETA ** (-np.linspace(0.0, 1.0, RET_DIM // 2, dtype=np.float64))
    return pos.astype(np.float64)[:, None] * inv[None, :]


def _rope_tables(pos):
    ang = _rope_angles(pos)
    cos, sin = np.cos(ang), np.sin(ang)
    c64 = np.concatenate([cos, cos], axis=-1)
    s64 = np.concatenate([-sin, sin], axis=-1)
    return _f32(np.tile(c64, (1, 2)), np.tile(s64, (1, 2)))


def _retention_decays(c):
    log_g = np.log(1.0 - np.exp2(-5.0 - np.arange(RET_HEADS, dtype=np.float64)))
    idx = np.arange(c, dtype=np.float64)
    diff = idx[:, None] - idx[None, :]
    dmat = np.where(diff >= 0, np.exp(np.maximum(diff, 0.0)[None] * log_g[:, None, None]), 0.0)
    qd = np.exp((idx + 1.0)[None, :] * log_g[:, None])
    kd = np.exp((c - 1.0 - idx)[None, :] * log_g[:, None])
    cd = np.exp(c * log_g)
    return dmat, qd, kd, cd


def _sample_decay_rows(c):
    dmat, qd, kd, cd = _retention_decays(c)
    rows = np.concatenate([dmat.reshape(RET_HEADS, c * c), qd, kd, cd[:, None]], axis=1)
    rows = np.pad(rows, ((0, 0), (0, -rows.shape[1] % 8)))
    return _f32(np.broadcast_to(rows[:, :, None], rows.shape + (LANES,)))[0]


def _full(shape):
    nd = len(shape)
    return pl.BlockSpec(shape, lambda *_: (0,) * nd)


def _params(sem):
    return pltpu.CompilerParams(dimension_semantics=sem, vmem_limit_bytes=VMEM_LIMIT)


def _prompt_tables_t(pos):
    ang = _rope_angles(pos).T
    ang_r = _ret_angles(pos).T
    cos_r, sin_r = np.cos(ang_r), np.sin(ang_r)
    cr = np.repeat(cos_r, 2, axis=0)
    sr = np.stack([-sin_r, sin_r], axis=1).reshape(RET_DIM, pos.shape[0])
    return _f32(np.cos(ang), np.sin(ang), cr, sr)


def _decay_consts_t(c):
    dmat, qd, kd, cd = _retention_decays(c)
    dm = dmat.transpose(0, 2, 1).reshape(N_RET_SLABS, 2, c, c).transpose(0, 2, 1, 3).reshape(N_RET_SLABS, c, 2 * c)
    cd = np.broadcast_to(cd[:, None, None], (RET_HEADS, 1, RET_DIM))
    return _f32(dm, qd[:, None, :], kd[:, None, :], cd)


def _prompt_mixer(x, w, tile, sub):
    b, l, d = x.shape
    ca, sa, cr, sr = _prompt_tables_t(np.arange(l, dtype=np.int32))
    dm, qd, kd, cd = _decay_consts_t(RET_CHUNK)
    sink = jnp.repeat(w['sinks'].reshape(SWA_KV_HEADS, 1, -1), WINDOW, axis=-1)
    tab = lambda rows: pl.BlockSpec((rows, tile), lambda i, t: (0, t))
    xspec = pl.BlockSpec((1, tile, d), lambda i, t: (i, t, 0))
    win_spec = pl.BlockSpec((1, WINDOW, LANES), lambda i, t: (i, 0, 0))
    st_spec = pl.BlockSpec((1, RET_HEADS, RET_DIM, RET_DIM), lambda i, t: (i, 0, 0, 0))
    return pl.pallas_call(
        functools.partial(_prompt_mixer_kernel, tile=tile, sub=sub),
        grid=(b, l // tile),
        in_specs=[xspec, tab(HEAD_DIM // 2), tab(HEAD_DIM // 2), tab(RET_DIM), tab(RET_DIM), _full((1, d)),
                  _full((IN_COLS, d)), _full((HEAD_DIM, 1)), _full((HEAD_DIM, 1)), _full(sink.shape),
                  _full((d, d)), _full(dm.shape), _full(qd.shape), _full(kd.shape), _full(cd.shape)],
        out_specs=[xspec, win_spec, win_spec, st_spec],
        out_shape=[jax.ShapeDtypeStruct(x.shape, F32),
                   jax.ShapeDtypeStruct((b, WINDOW, LANES), F32),
                   jax.ShapeDtypeStruct((b, WINDOW, LANES), F32),
                   jax.ShapeDtypeStruct((b, RET_HEADS, RET_DIM, RET_DIM), F32)],
        scratch_shapes=[pltpu.VMEM((SWA_KV_W, WINDOW), F32), pltpu.VMEM((SWA_KV_W, WINDOW), F32),
                        pltpu.VMEM((RET_HEADS, RET_DIM, RET_DIM), F32)],
        compiler_params=_params(("arbitrary", "arbitrary")),
        name="prompt_mixer",
    )(x, ca, sa, cr, sr, w['norm_mix'], w['w_in_t'], w['qg_col'], w['kg_col'], sink, w['w_out'], dm, qd, kd,
      cd)


def _sample_mixer(x, cache_k, cache_v, state, w, tile, nseq):
    b, ntok, d = x.shape
    rows = b * ntok
    xf = x.reshape(rows, d)
    pos = np.tile(PAST_LEN + np.arange(ntok, dtype=np.int32), b)
    ca, sa = _rope_tables(pos)
    tab = pl.BlockSpec((tile, LANES), lambda i: (i, 0))
    row_spec = lambda width: pl.BlockSpec((tile, width), lambda i: (i, 0))
    feat = pl.pallas_call(
        _sample_in_kernel,
        grid=(rows // tile,),
        in_specs=[row_spec(d), tab, tab, _full((1, d)), _full((IN_COLS, d)), _full((1, LANES)),
                  _full((1, LANES)), _full((LANES, LANES))],
        out_specs=row_spec(SAMPLE_FEAT),
        out_shape=jax.ShapeDtypeStruct((rows, SAMPLE_FEAT), F32),
        compiler_params=_params(("arbitrary",)),
        name="sample_in",
    )(xf, ca, sa, w['norm_mix'], w['w_in_t'], w['qg'], w['kg'], w['bd'])

    sink = jnp.repeat(w['sinks'], ntok)[:, None]
    seq_rows = nseq * ntok
    unroll = nseq
    cache_spec = pl.BlockSpec((nseq, LANES, WINDOW), lambda i: (i, 0, 0))
    o_a, k_out, v_out = pl.pallas_call(
        functools.partial(_sample_swa_kernel, nseq=nseq, ntok=ntok, unroll=unroll),
        grid=(b // nseq,),
        in_specs=[pl.BlockSpec((seq_rows, SAMPLE_FEAT), lambda i: (i, 0)), cache_spec, cache_spec,
                  _full(sink.shape)],
        out_specs=[pl.BlockSpec((seq_rows, SWA_Q_W), lambda i: (i, 0)), cache_spec, cache_spec],
        out_shape=[jax.ShapeDtypeStruct((rows, SWA_Q_W), F32),
                   jax.ShapeDtypeStruct(cache_k.shape, F32),
                   jax.ShapeDtypeStruct(cache_v.shape, F32)],
        scratch_shapes=[pltpu.VMEM((unroll, WINDOW, LANES), F32), pltpu.VMEM((unroll, WINDOW, LANES), F32)],
        compiler_params=_params(("arbitrary",)),
        name="sample_swa",
    )(feat, cache_k, cache_v, sink)

    assert b == LANES, "the retention step puts one sequence per lane"
    xt = jnp.transpose(x, (1, 0, 2)).reshape(rows, d)
    pos_t = np.repeat(PAST_LEN + np.arange(ntok, dtype=np.int32), b)
    _, _, cr_t, sr_t = _prompt_tables_t(pos_t)
    half = rows // 2
    tab_t = pl.BlockSpec((RET_DIM, half), lambda i: (0, i))
    qkv = pl.pallas_call(
        _sample_ret_in_kernel,
        grid=(2,),
        in_specs=[pl.BlockSpec((half, d), lambda i: (i, 0)), tab_t, tab_t, _full((1, d)),
                  _full((IN_COLS, d))],
        out_specs=pl.BlockSpec((3 * RET_W, half), lambda i: (0, i)),
        out_shape=jax.ShapeDtypeStruct((3 * RET_W, rows), F32),
        compiler_params=_params(("arbitrary",)),
        name="sample_ret_in",
    )(xt, cr_t, sr_t, w['norm_mix'], w['w_in_t'])
    consts = _sample_decay_rows(ntok)
    pair_rows = lambda off: pl.BlockSpec((LANES, rows), lambda p: (off + p, 0))
    st_spec = pl.BlockSpec((2, RET_DIM, RET_DIM, b), lambda p: (p, 0, 0, 0))
    o_r, s_out = pl.pallas_call(
        functools.partial(_sample_ret_kernel, ntok=ntok),
        grid=(N_RET_SLABS,),
        in_specs=[pair_rows(0), pair_rows(N_RET_SLABS), pair_rows(2 * N_RET_SLABS), st_spec,
                  pl.BlockSpec((2,) + consts.shape[1:], lambda p: (p, 0, 0))],
        out_specs=[pl.BlockSpec((ntok, b, LANES), lambda p: (0, 0, p)), st_spec],
        out_shape=[jax.ShapeDtypeStruct((ntok, b, RET_W), F32), jax.ShapeDtypeStruct(state.shape, F32)],
        scratch_shapes=[pltpu.VMEM((ntok, RET_DIM, b), F32)],
        compiler_params=_params(("arbitrary",)),
        name="sample_ret",
    )(qkv, qkv, qkv, state, consts)
    o_r = jnp.transpose(o_r, (1, 0, 2)).reshape(rows, RET_W)

    y = pl.pallas_call(
        _sample_out_kernel,
        grid=(rows // tile,),
        in_specs=[row_spec(d), row_spec(SWA_Q_W), row_spec(RET_W), row_spec(SAMPLE_FEAT), _full((LANES, LANES)),
                  _full((d, d))],
        out_specs=row_spec(d),
        out_shape=jax.ShapeDtypeStruct((rows, d), F32),
        compiler_params=_params(("arbitrary",)),
        name="sample_out",
    )(xf, o_a, o_r, feat, w['bd'], w['w_out'])
    return y, k_out, v_out, s_out


def _mem_kv(mem, w, tile):
    rows, d = mem.shape
    row_spec = lambda width: pl.BlockSpec((tile, width), lambda i: (i, 0))
    return pl.pallas_call(
        _mem_kv_kernel,
        grid=(rows // tile,),
        in_specs=[row_spec(d), _full((1, d)), _full((d, 2 * MEM_W)), _full((1, LANES))],
        out_specs=[pl.BlockSpec((tile * MEM_HEADS, MEM_HEAD_DIM), lambda i: (i, 0))] * 2,
        out_shape=[jax.ShapeDtypeStruct((rows * MEM_HEADS, MEM_HEAD_DIM), F32)] * 2,
        compiler_params=_params(("arbitrary",)),
        name="mem_kv",
    )(mem, w['norm_mem'], w['w_mkv'], w['kgm'])


def _prompt_cross_ffn(x, mk, mv, w, tile):
    rows, d = x.shape
    n = rows // tile
    per_mem = n // mk.shape[0]
    cur = lambda i: jnp.minimum(i, n - 1)
    single = lambda shape: pl.BlockSpec(shape, lambda i: (0,) * len(shape), pipeline_mode=pl.Buffered(1))
    mem_spec = pl.BlockSpec((1, MEM_LEN * MEM_HEADS, MEM_HEAD_DIM), lambda i: (cur(i) // per_mem, 0, 0))
    return pl.pallas_call(
        _prompt_cross_ffn_kernel,
        grid=(n + 1,),
        in_specs=[pl.BlockSpec((tile, d), lambda i: (cur(i), 0)), mem_spec, mem_spec, _full((1, d)),
                  single((d, MEM_W)), _full((1, LANES)), single((MEM_W, d)), _full((1, d)),
                  single((d, 2 * FFN_HIDDEN)), single((FFN_HIDDEN, d))],
        out_specs=pl.BlockSpec((tile, d), lambda i: (jnp.maximum(i - 1, 0), 0)),
        out_shape=jax.ShapeDtypeStruct((rows, d), F32),
        scratch_shapes=[pltpu.VMEM((tile, d), F32)],
        compiler_params=_params(("arbitrary",)),
        name="prompt_cross_ffn",
    )(x, mk, mv, w['norm_cross'], w['w_mq'], w['qgm'], w['w_mo'], w['norm_ffn'], w['w_gu'], w['w_down'])


def _sample_cross_attn(x, mk, mv, w, nseq, tq):
    rows, d = x.shape
    blk = nseq * tq
    mem_spec = pl.BlockSpec((nseq, MEM_LEN * MEM_HEADS, MEM_HEAD_DIM), lambda i: (i, 0, 0))
    return pl.pallas_call(
        functools.partial(_sample_cross_attn_kernel, nseq=nseq, tq=tq),
        grid=(rows // blk,),
        in_specs=[pl.BlockSpec((blk, d), lambda i: (i, 0)), mem_spec, mem_spec, _full((1, d)),
                  _full((d, MEM_W)), _full((1, LANES))],
        out_specs=pl.BlockSpec((blk, MEM_W), lambda i: (i, 0)),
        out_shape=jax.ShapeDtypeStruct((rows, MEM_W), F32),
        compiler_params=_params(("arbitrary",)),
        name="sample_cross_attn",
    )(x, mk, mv, w['norm_cross'], w['w_mq'], w['qgm'])


def _sample_cross_out_ffn(x, o, w, tile):
    rows, d = x.shape
    row_spec = lambda width: pl.BlockSpec((tile, width), lambda i: (i, 0))
    single = lambda shape: pl.BlockSpec(shape, lambda i: (0,) * len(shape), pipeline_mode=pl.Buffered(1))
    return pl.pallas_call(
        _cross_out_ffn_kernel,
        grid=(rows // tile,),
        in_specs=[row_spec(d), row_spec(MEM_W), single((MEM_W, d)), _full((1, d)),
                  single((d, 2 * FFN_HIDDEN)), single((FFN_HIDDEN, d))],
        out_specs=row_spec(d),
        out_shape=jax.ShapeDtypeStruct((rows, d), F32),
        compiler_params=_params(("arbitrary",)),
        name="sample_ffn",
    )(x, o, w['w_mo'], w['norm_ffn'], w['w_gu'], w['w_down'])


def kernel(x_prompt, x_sample, mem_prompt, cache_swa_k, cache_swa_v, state_ret, cache_mem_k, cache_mem_v,
           norm_mix, w_in, q_norm_a, k_norm_a, sinks, w_out, norm_cross, norm_mem, w_mq, w_mkv,
           q_norm_m, k_norm_m, w_mo, norm_ffn, w_gu, w_down):
    assert norm_mix.shape[0] == 1, "single-layer kernel"
    b, l, d = x_prompt.shape
    sb, st, _ = x_sample.shape
    half = (np.arange(LANES) // HEAD_DIM)[:, None] == (np.arange(LANES) // HEAD_DIM)[None, :]
    w = {
        'norm_mix': norm_mix, 'norm_cross': norm_cross, 'norm_mem': norm_mem, 'norm_ffn': norm_ffn,
        'w_out': w_out[0].astype(BF16), 'w_mq': w_mq[0].astype(BF16),
        'w_mkv': w_mkv[0].astype(BF16), 'w_mo': w_mo[0].astype(BF16), 'w_gu': w_gu[0].astype(BF16),
        'w_down': w_down[0].astype(BF16),
        'w_in_t': w_in[0].T.astype(BF16),
        'qg': jnp.tile(q_norm_a, (1, 2)), 'kg': jnp.tile(k_norm_a, (1, 2)),
        'qg_col': q_norm_a.reshape(HEAD_DIM, 1), 'kg_col': k_norm_a.reshape(HEAD_DIM, 1),
        'qgm': q_norm_m, 'kgm': k_norm_m, 'sinks': sinks[0],
        'bd': half.astype(BF16),
    }

    mk, mv = _mem_kv(mem_prompt.reshape(b * MEM_LEN, d), w, tile=MEM_KV_TILE)
    xp, kwin, vwin, ret_p = _prompt_mixer(x_prompt, w, tile=PROMPT_MIXER_TILE, sub=PROMPT_MIXER_SUB)
    xp = xp.reshape(b * l, d)
    mem_rows_p = (b, MEM_LEN * MEM_HEADS, MEM_HEAD_DIM)
    yp = _prompt_cross_ffn(xp, mk.reshape(mem_rows_p), mv.reshape(mem_rows_p), w, tile=PROMPT_FFN_TILE)
    yp = yp.reshape(b, l, d)
    ret_p = ret_p.reshape(1, b, RET_HEADS, RET_DIM, RET_DIM)

    nbuf = cache_swa_k.shape[2]
    assert nbuf == WINDOW
    to_feature_major = lambda c: jnp.transpose(c[0], (0, 2, 3, 1)).reshape(sb, SWA_KV_W, nbuf)
    from_feature_major = lambda c: jnp.transpose(c.reshape(sb, SWA_KV_HEADS, HEAD_DIM, nbuf), (0, 3, 1, 2))[None]
    xs, k_s, v_s, s_s = _sample_mixer(x_sample, to_feature_major(cache_swa_k), to_feature_major(cache_swa_v),
                                      jnp.transpose(state_ret[0], (1, 2, 3, 0)), w, tile=SAMPLE_TILE,
                                      nseq=SAMPLE_SEQS)
    mem_rows = (sb, MEM_LEN * MEM_HEADS, MEM_HEAD_DIM)
    os_ = _sample_cross_attn(xs, cache_mem_k.reshape(mem_rows), cache_mem_v.reshape(mem_rows), w,
                             nseq=SAMPLE_SEQS, tq=st)
    ys = _sample_cross_out_ffn(xs, os_, w, tile=SAMPLE_FFN_TILE).reshape(sb, st, d)

    kv_shape = (1, b, WINDOW, SWA_KV_HEADS, HEAD_DIM)
    mem_shape = (1, b, MEM_LEN, MEM_HEADS, MEM_HEAD_DIM)
    return (yp, ys, kwin.reshape(kv_shape), vwin.reshape(kv_shape), ret_p,
            mk.reshape(mem_shape), mv.reshape(mem_shape),
            from_feature_major(k_s), from_feature_major(v_s),
            jnp.transpose(s_s, (3, 0, 1, 2))[None])
```

```python
import functools

import jax
import jax.numpy as jnp
import numpy as np
from jax import lax
from jax.experimental import pallas as pl
from jax.experimental.pallas import tpu as pltpu

F32 = jnp.float32
BF16 = jnp.bfloat16

LANES = 128
SUBLANES = 8
D_MODEL = 1024
HEAD_DIM = 64
SWA_HEADS = 8
SWA_KV_HEADS = 2
WINDOW = 128
RET_HEADS = 8
RET_DIM = 64
RET_CHUNK = 128
RET_THETA = 10000.0
ROPE_THETA = 10000.0
MEM_LEN = 256
MEM_HEADS = 4
MEM_HEAD_DIM = 128
MEM_W = MEM_HEADS * MEM_HEAD_DIM
FFN_HIDDEN = 2816
RMS_EPS = 1e-6
NEG_INF = -1e30
LOG2E = 1.4426950408889634
PAST_LEN = 16384

SWA_Q_W = SWA_HEADS * HEAD_DIM
SWA_KV_W = SWA_KV_HEADS * HEAD_DIM
RET_W = RET_HEADS * RET_DIM
IN_COLS = SWA_Q_W + 2 * SWA_KV_W + 4 * RET_W
C_QA, C_KA, C_VA = 0, SWA_Q_W, SWA_Q_W + SWA_KV_W
C_QR = SWA_Q_W + 2 * SWA_KV_W
C_KR, C_VR, C_G = C_QR + RET_W, C_QR + 2 * RET_W, C_QR + 3 * RET_W
C_GATE_S = C_QR
SAMPLE_FEAT = C_QR + RET_W
N_QA_SLABS = SWA_Q_W // LANES
N_RET_SLABS = RET_W // LANES

V7X_VMEM_BYTES = 64 * 1024 * 1024
VMEM_LIMIT = V7X_VMEM_BYTES - 8 * 1024 * 1024
PROMPT_MIXER_TILE, PROMPT_MIXER_SUB = 1024, 512
PROMPT_FFN_TILE = 512
MEM_KV_TILE = 512
SAMPLE_TILE = 512
SAMPLE_FFN_TILE = 512
SAMPLE_SEQS = 8
STAGE_SKEW = 1


def _dot(a, b):
    return jnp.dot(a.astype(BF16), b.astype(BF16), preferred_element_type=F32)


def _dot_nt(a, b):
    return lax.dot_general(a.astype(BF16), b.astype(BF16), (((1,), (1,)), ((), ())),
                           preferred_element_type=F32)


def _rms(x):
    return x * lax.rsqrt(jnp.mean(x * x, axis=-1, keepdims=True) + RMS_EPS)


def _lane_consts():
    lane = lax.broadcasted_iota(jnp.int32, (1, LANES), 1)
    m_left = (lane < HEAD_DIM).astype(F32)
    m_right = 1.0 - m_left
    first_half = (lane % HEAD_DIM) < (HEAD_DIM // 2)
    return m_left, m_right, first_half


def _head_norm(y, bd):
    ss = jnp.dot((y * y).astype(BF16), bd, preferred_element_type=F32) * (1.0 / HEAD_DIM)
    return y * lax.rsqrt(ss + RMS_EPS)


def _rot_half(y, cos, sin_signed, first_half):
    swapped = jnp.where(first_half, pltpu.roll(y, LANES - HEAD_DIM // 2, 1), pltpu.roll(y, HEAD_DIM // 2, 1))
    return y * cos + swapped * sin_signed


def _dot_tn(a, b):
    return lax.dot_general(a.astype(BF16), b.astype(BF16), (((0,), (0,)), ((), ())),
                           preferred_element_type=F32)


def _head_norm_t(y):
    return y * lax.rsqrt(jnp.mean(y * y, axis=0, keepdims=True) + RMS_EPS)


def _rot_half_t(y, cos, sin):
    half = HEAD_DIM // 2
    y1, y2 = y[0:half], y[half:]
    return jnp.concatenate([y1 * cos - y2 * sin, y2 * cos + y1 * sin], axis=0)


def _rot_pairs_t(y, cos, sin_signed, even_row):
    n = y.shape[0]
    swapped = jnp.where(even_row, pltpu.roll(y, n - 1, 0), pltpu.roll(y, 1, 0))
    return y * cos + swapped * sin_signed


def _prompt_mixer_kernel(x_ref, ca_ref, sa_ref, cr_ref, sr_ref, nmix_ref, wint_ref, qg_ref, kg_ref,
                         sink_ref, wout_ref, dm_ref, qd_ref, kd_ref, cd_ref,
                         y_ref, kwin_ref, vwin_ref, sout_ref,
                         pk_ref, pv_ref, s_ref, *, tile, sub):
    t = pl.program_id(1)
    nblk = sub // WINDOW
    hd = HEAD_DIM
    group = SWA_HEADS // SWA_KV_HEADS

    @pl.when(t == 0)
    def _():
        pk_ref[...] = jnp.zeros_like(pk_ref)
        pv_ref[...] = jnp.zeros_like(pv_ref)
        s_ref[...] = jnp.zeros_like(s_ref)

    qg, kg = qg_ref[...], kg_ref[...]
    qs = qg * (hd ** -0.5 * LOG2E)
    head = lambda a, h: a[h * hd:(h + 1) * hd]
    chunk = lambda a, h, j: a[h * hd:(h + 1) * hd, j * RET_CHUNK:(j + 1) * RET_CHUNK]
    keys_of = lambda j: slice(j * WINDOW, (j + 2) * WINDOW)
    units = [(g, j) for j in range(nblk) for g in range(SWA_KV_HEADS)]
    key = lax.broadcasted_iota(jnp.int32, (2 * WINDOW, group * WINDOW), 0)
    qry = lax.broadcasted_iota(jnp.int32, (2 * WINDOW, group * WINDOW), 1) % WINDOW
    band = (key > qry) & (key <= qry + WINDOW)
    even_row = (lax.broadcasted_iota(jnp.int32, (RET_DIM, sub), 0) % 2) == 0
    ones_rows = jnp.ones((2 * SUBLANES, 2 * WINDOW), BF16)
    zeros = jnp.zeros((hd, RET_CHUNK), F32)
    carry = {'pk': pk_ref[...], 'pv': pv_ref[...], 'state': [s_ref[h] for h in range(RET_HEADS)]}

    def stages(idx):
        toks = slice(idx * sub, (idx + 1) * sub)
        v = {}

        def project():
            v['x'] = x_ref[0, toks]
            hb = (_rms(v['x']) * nmix_ref[...]).astype(BF16)
            proj = lambda lo, hi: lax.dot_general(wint_ref[lo:hi, :], hb, (((1,), (1,)), ((), ())),
                                                  preferred_element_type=F32)
            v['swa'], v['ret'] = proj(C_QA, C_QR), proj(C_QR, C_VR)
            v['vr'], v['gate'] = proj(C_VR, C_G), proj(C_G, IN_COLS)

        def swa_prepare():
            ca, sa = ca_ref[:, toks], sa_ref[:, toks]
            swa_t = v['swa']
            v['qa'] = [_rot_half_t(_head_norm_t(head(swa_t, h)) * qs, ca, sa) for h in range(SWA_HEADS)]
            ka = jnp.concatenate([_rot_half_t(_head_norm_t(head(swa_t, SWA_HEADS + g)) * kg, ca, sa)
                                  for g in range(SWA_KV_HEADS)], axis=0)
            va = swa_t[C_VA:C_QR]
            v['kfull'] = jnp.concatenate([carry['pk'], ka], axis=1).astype(BF16)
            v['vfull'] = jnp.concatenate([carry['pv'], va], axis=1).astype(BF16)
            carry['pk'], carry['pv'] = ka[:, sub - WINDOW:], va[:, sub - WINDOW:]

        def swa_scores():
            v['scores'] = []
            for g, j in units:
                blk = slice(j * WINDOW, (j + 1) * WINDOW)
                q4 = jnp.concatenate([v['qa'][group * g + u][:, blk] for u in range(group)], axis=1)
                v['scores'].append(_dot_tn(head(v['kfull'], g)[:, keys_of(j)], q4))

        def ret_rotate():
            cr, sr = cr_ref[:, toks], sr_ref[:, toks]
            rot = lambda a: jnp.concatenate([_rot_pairs_t(head(a, h), cr, sr, even_row)
                                             for h in range(RET_HEADS)], axis=0)
            v['qr'] = rot(v['ret'][0:RET_W])
            v['kr'] = rot(v['ret'][RET_W:]) * (RET_DIM ** -0.5)

        def ret_scores():
            qr, kr, vr = v['qr'], v['kr'], v['vr']
            v['inner'] = [[None] * nblk for _ in range(N_RET_SLABS)]
            for j in range(nblk):
                for pr in range(N_RET_SLABS):
                    q_bd = jnp.concatenate([jnp.concatenate([chunk(qr, 2 * pr, j), zeros], axis=1),
                                            jnp.concatenate([zeros, chunk(qr, 2 * pr + 1, j)], axis=1)], axis=0)
                    k2 = kr[pr * LANES:(pr + 1) * LANES, j * RET_CHUNK:(j + 1) * RET_CHUNK]
                    v['inner'][pr][j] = _dot_tn(k2, q_bd) * dm_ref[pr]
            v['incr'] = [[_dot_nt(chunk(vr, h, j), chunk(kr, h, j) * kd_ref[h]) for j in range(nblk)]
                         for h in range(RET_HEADS)]

        def swa_softmax():
            v['probs'], v['sink_terms'] = [], []
            for (g, j), s in zip(units, v['scores']):
                sink = sink_ref[g] * LOG2E
                valid = band & (key >= WINDOW * (1 - t)) if (idx == 0 and j == 0) else band
                s = jnp.where(valid, s, NEG_INF)
                m = jnp.maximum(jnp.max(s, axis=0, keepdims=True), sink)
                v['probs'].append(jnp.exp2(s - m).astype(BF16))
                v['sink_terms'].append(jnp.exp2(sink - m))

        def swa_values():
            v['oa'] = [[None] * nblk for _ in range(SWA_HEADS)]
            for (g, j), p, sink_term in zip(units, v['probs'], v['sink_terms']):
                v_aug = jnp.concatenate([head(v['vfull'], g)[:, keys_of(j)], ones_rows], axis=0)
                o_aug = jnp.dot(v_aug, p, preferred_element_type=F32)
                o = o_aug[0:hd] / (o_aug[hd:hd + 1] + sink_term)
                for u in range(group):
                    v['oa'][group * g + u][j] = o[:, u * WINDOW:(u + 1) * WINDOW]
            v['state'] = [[carry['state'][h]] for h in range(RET_HEADS)]
            for h in range(RET_HEADS):
                for j in range(nblk):
                    v['state'][h].append(v['state'][h][j] * cd_ref[h] + v['incr'][h][j])
                carry['state'][h] = v['state'][h][nblk]

        def ret_outputs():
            v['or'] = [[None] * nblk for _ in range(RET_HEADS)]
            for j in range(nblk):
                for h in range(RET_HEADS):
                    u = h % 2
                    lhs = jnp.concatenate([chunk(v['vr'], h, j), v['state'][h][j]], axis=1)
                    rhs = jnp.concatenate([v['inner'][h // 2][j][:, u * RET_CHUNK:(u + 1) * RET_CHUNK],
                                           chunk(v['qr'], h, j) * qd_ref[h]], axis=0)
                    v['or'][h][j] = _dot(lhs, rhs)

        def gate_mix():
            mix = [jnp.concatenate(blocks, axis=1) for blocks in v['oa']]
            for h in range(RET_HEADS):
                o_h = _head_norm_t(jnp.concatenate(v['or'][h], axis=1))
                g_h = head(v['gate'], h)
                mix.append(o_h * (g_h * jax.nn.sigmoid(g_h)))
            v['mix'] = jnp.concatenate(mix, axis=0).astype(BF16)

        def out_project():
            y_ref[0, toks] = v['x'] + lax.dot_general(v['mix'], wout_ref[...], (((0,), (0,)), ((), ())),
                                                      preferred_element_type=F32)

        return [project, swa_prepare, swa_scores, ret_rotate, ret_scores, swa_softmax, swa_values,
                ret_outputs, gate_mix, out_project]

    pipelines = [stages(i) for i in range(tile // sub)]
    nstage = len(pipelines[0])
    for slot in range(nstage + STAGE_SKEW * (len(pipelines) - 1)):
        for i, pipe in enumerate(pipelines):
            if 0 <= slot - STAGE_SKEW * i < nstage:
                pipe[slot - STAGE_SKEW * i]()

    pk_ref[...], pv_ref[...] = carry['pk'], carry['pv']
    kwin_ref[0], vwin_ref[0] = carry['pk'].T, carry['pv'].T
    for h in range(RET_HEADS):
        s_ref[h] = carry['state'][h]

    @pl.when(t == pl.num_programs(1) - 1)
    def _():
        for h in range(RET_HEADS):
            sout_ref[0, h] = carry['state'][h].T


def _sample_in_kernel(x_ref, ca_ref, sa_ref, nmix_ref, wint_ref, qg_ref, kg_ref, bd_ref, feat_ref):
    _, _, first_half = _lane_consts()
    ca, sa, bd = ca_ref[...], sa_ref[...], bd_ref[...]
    hb = (_rms(x_ref[...]) * nmix_ref[...]).astype(BF16)
    proj = lambda lo, hi: lax.dot_general(hb, wint_ref[lo:hi, :], (((1,), (1,)), ((), ())),
                                          preferred_element_type=F32)
    swa = proj(C_QA, C_QR)
    slabs = [_rot_half(_head_norm(swa[:, s * LANES:(s + 1) * LANES], bd) * qg_ref[...], ca, sa, first_half)
             for s in range(N_QA_SLABS)]
    slabs.append(_rot_half(_head_norm(swa[:, C_KA:C_VA], bd) * kg_ref[...], ca, sa, first_half))
    feat_ref[...] = jnp.concatenate(slabs + [swa[:, C_VA:C_QR], proj(C_G, IN_COLS)], axis=1)


def _sample_swa_kernel(feat_ref, ckt_ref, cvt_ref, sink_ref, o_ref, kout_ref, vout_ref,
                       kn_ref, vn_ref, *, nseq, ntok, unroll):
    @pl.when(pl.program_id(0) == 0)
    def _():
        kn_ref[...] = jnp.zeros_like(kn_ref)
        vn_ref[...] = jnp.zeros_like(vn_ref)

    m_left, m_right, _ = _lane_consts()
    tail = WINDOW - ntok
    nrow = SWA_HEADS * ntok
    row_tok = lax.broadcasted_iota(jnp.int32, (nrow, WINDOW), 0) % ntok
    col = lax.broadcasted_iota(jnp.int32, (nrow, WINDOW), 1)
    valid_cache = col > row_tok
    valid_new = (col >= tail) & (col - tail <= row_tok)
    in_tail = lax.broadcasted_iota(jnp.int32, (LANES, WINDOW), 1) >= tail
    sink = sink_ref[...]
    shift = lambda a: pltpu.roll(a, HEAD_DIM, 1)

    def body(step, carry):
        seqs = [step * unroll + u for u in range(unroll)]
        rows = [pl.ds(pl.multiple_of(i * ntok, ntok), ntok) for i in seqs]
        q_rows, caches = [], []
        for u, i in enumerate(seqs):
            slab = lambda c, u=u: feat_ref[rows[u], c:c + LANES]
            kn_ref[u, tail:WINDOW] = slab(C_KA)
            vn_ref[u, tail:WINDOW] = slab(C_VA)
            s0, s1, s2, s3 = [slab(C_QA + s * LANES) for s in range(N_QA_SLABS)]
            q_rows.append(jnp.concatenate(
                [s0 * m_left, shift(s0 * m_right), s1 * m_left, shift(s1 * m_right),
                 shift(s2 * m_left), s2 * m_right, shift(s3 * m_left), s3 * m_right], axis=0))
            caches.append((ckt_ref[i], cvt_ref[i], kn_ref[u], vn_ref[u]))
        scores = [(_dot(q, kt), _dot_nt(q, kn)) for q, (kt, _, kn, _) in zip(q_rows, caches)]
        probs = []
        for s_c, s_n in scores:
            s_c = jnp.where(valid_cache, s_c * (HEAD_DIM ** -0.5), NEG_INF)
            s_n = jnp.where(valid_new, s_n * (HEAD_DIM ** -0.5), NEG_INF)
            m = jnp.maximum(jnp.maximum(jnp.max(s_c, axis=-1, keepdims=True),
                                        jnp.max(s_n, axis=-1, keepdims=True)), sink)
            p_c, p_n = jnp.exp(s_c - m), jnp.exp(s_n - m)
            denom = (jnp.sum(p_c, axis=-1, keepdims=True) + jnp.sum(p_n, axis=-1, keepdims=True)
                     + jnp.exp(sink - m))
            probs.append((p_c, p_n, denom))
        for u, i in enumerate(seqs):
            kt, vt, kn, vn = caches[u]
            p_c, p_n, denom = probs[u]
            o = (_dot_nt(p_c, vt) + _dot(p_n, vn)) / denom
            n = ntok
            left = lambda h: o[h * n:(h + 1) * n] * m_left
            right = lambda h: o[h * n:(h + 1) * n] * m_right
            o_ref[rows[u], 0:LANES] = left(0) + shift(left(1))
            o_ref[rows[u], LANES:2 * LANES] = left(2) + shift(left(3))
            o_ref[rows[u], 2 * LANES:3 * LANES] = shift(right(4)) + right(5)
            o_ref[rows[u], 3 * LANES:4 * LANES] = shift(right(6)) + right(7)
            kout_ref[i] = jnp.where(in_tail, kn.T, pltpu.roll(kt, tail, 1))
            vout_ref[i] = jnp.where(in_tail, vn.T, pltpu.roll(vt, tail, 1))
        return carry

    lax.fori_loop(0, nseq // unroll, body, 0)


def _sample_ret_in_kernel(x_ref, cr_ref, sr_ref, nmix_ref, wint_ref, qkv_ref):
    hb = (_rms(x_ref[...]) * nmix_ref[...]).astype(BF16)
    qkv = lax.dot_general(wint_ref[C_QR:C_G, :], hb, (((1,), (1,)), ((), ())), preferred_element_type=F32)
    n = qkv.shape[1]
    even_row = (lax.broadcasted_iota(jnp.int32, (RET_DIM, n), 0) % 2) == 0
    cr, sr = cr_ref[...], sr_ref[...]
    for h in range(RET_HEADS):
        q_rows = slice(h * RET_DIM, (h + 1) * RET_DIM)
        k_rows = slice(RET_W + h * RET_DIM, RET_W + (h + 1) * RET_DIM)
        qkv_ref[q_rows] = _rot_pairs_t(qkv[q_rows], cr, sr, even_row)
        qkv_ref[k_rows] = _rot_pairs_t(qkv[k_rows], cr, sr, even_row) * (RET_DIM ** -0.5)
    qkv_ref[2 * RET_W:] = qkv[2 * RET_W:]


def _sample_ret_kernel(q_ref, k_ref, v_ref, s_ref, c_ref, o_ref, so_ref, ks_ref, *, ntok):
    hd, nseq = RET_DIM, LANES
    tok = lambda t: slice(t * nseq, (t + 1) * nseq)
    row_qd, row_kd, row_cd = ntok * ntok, ntok * ntok + ntok, ntok * ntok + 2 * ntok
    group = SUBLANES
    pair_outs = []
    for u in range(2):
        feats = slice(u * hd, (u + 1) * hd)
        const = lambda r: c_ref[u, r:r + 1, :]
        for t in range(ntok):
            ks_ref[t] = k_ref[feats, tok(t)] * const(row_kd + t)
        outs = []
        for t in range(ntok):
            q_t = q_ref[feats, tok(t)]
            acc = jnp.zeros((hd, nseq), F32)
            for k in range(t + 1):
                w_tk = jnp.sum(q_t * k_ref[feats, tok(k)], axis=0, keepdims=True) * const(t * ntok + k)
                acc = acc + w_tk * v_ref[feats, tok(k)]

            def cross(j, carry, t=t):
                d0 = pl.multiple_of(j * group, group)
                q_rows = q_ref[pl.ds(u * hd + d0, group), tok(t)]
                for i in range(group):
                    carry = carry + q_rows[i:i + 1, :] * s_ref[u, d0 + i]
                return carry
            carried = lax.fori_loop(0, hd // group, cross, jnp.zeros((hd, nseq), F32))
            outs.append(acc + carried * const(row_qd + t))

        def update(j, carry):
            d0 = pl.multiple_of(j * group, group)
            k_rows = [ks_ref[t, pl.ds(d0, group), :] for t in range(ntok)]
            for i in range(group):
                new = s_ref[u, d0 + i] * const(row_cd)
                for t in range(ntok):
                    new = new + k_rows[t][i:i + 1, :] * v_ref[feats, tok(t)]
                so_ref[u, d0 + i] = new
            return carry
        lax.fori_loop(0, hd // group, update, 0)
        pair_outs.append(outs)
    for t in range(ntok):
        o_ref[t] = jnp.concatenate([pair_outs[0][t], pair_outs[1][t]], axis=0).T


def _sample_out_kernel(x_ref, oa_ref, or_ref, feat_ref, bd_ref, wout_ref, y_ref):
    outs = [oa_ref[...]]
    for p in range(N_RET_SLABS):
        gate = feat_ref[:, C_GATE_S + p * LANES:C_GATE_S + (p + 1) * LANES]
        o_ret = or_ref[:, p * LANES:(p + 1) * LANES]
        outs.append(_head_norm(o_ret, bd_ref[...]) * (gate * jax.nn.sigmoid(gate)))
    mix = jnp.concatenate(outs, axis=1).astype(BF16)
    y_ref[...] = x_ref[...] + jnp.dot(mix, wout_ref[...], preferred_element_type=F32)


def _mem_kv_kernel(m_ref, nmem_ref, wkv_ref, kg_ref, k_ref, v_ref):
    hb = (_rms(m_ref[...]) * nmem_ref[...]).astype(BF16)
    kv = jnp.dot(hb, wkv_ref[...], preferred_element_type=F32)
    slots = m_ref.shape[0]
    for h in range(MEM_HEADS):
        rows = pl.ds(h, slots, stride=MEM_HEADS)
        k_ref[rows, :] = _rms(kv[:, h * LANES:(h + 1) * LANES]) * kg_ref[...]
        v_ref[rows, :] = kv[:, MEM_W + h * LANES:MEM_W + (h + 1) * LANES]


def _mem_queries(x, ncross, wq_ref, qg):
    hb = (_rms(x) * ncross).astype(BF16)
    q = jnp.dot(hb, wq_ref[...], preferred_element_type=F32)
    qs = qg * (MEM_HEAD_DIM ** -0.5 * LOG2E)
    return [_rms(q[:, h * LANES:(h + 1) * LANES]) * qs for h in range(MEM_HEADS)]


def _prompt_cross_ffn_kernel(x_ref, mk_ref, mv_ref, ncross_ref, wq_ref, qg_ref, wo_ref, nffn_ref, wgu_ref,
                             wdown_ref, y_ref, xa_ref):
    @pl.when(pl.program_id(0) == 0)
    def _():
        xa_ref[...] = jnp.zeros_like(xa_ref)

    xa = xa_ref[...]
    hb = (_rms(xa) * nffn_ref[...]).astype(BF16)
    x = x_ref[...]
    qn = _mem_queries(x, ncross_ref[...], wq_ref, qg_ref[...])
    g = jnp.dot(hb, wgu_ref[:, 0:FFN_HIDDEN], preferred_element_type=F32)
    head = lambda ref, h: ref[0, pl.ds(h, MEM_LEN, stride=MEM_HEADS), :].astype(BF16)
    scores = [_dot_nt(head(mk_ref, h), qn[h]) for h in range(MEM_HEADS)]
    u = jnp.dot(hb, wgu_ref[:, FFN_HIDDEN:], preferred_element_type=F32)
    probs, denoms = [], []
    for s in scores:
        p = jnp.exp2(s - jnp.max(s, axis=0, keepdims=True))
        denoms.append(jnp.sum(p, axis=0, keepdims=True))
        probs.append(p.astype(BF16))
    act = (g * jax.nn.sigmoid(g) * u).astype(BF16)
    o_t = jnp.concatenate([_dot_tn(head(mv_ref, h), probs[h]) / denoms[h] for h in range(MEM_HEADS)], axis=0)
    y_ref[...] = xa + jnp.dot(act, wdown_ref[...], preferred_element_type=F32)
    xa_ref[...] = x + lax.dot_general(o_t.astype(BF16), wo_ref[...], (((0,), (0,)), ((), ())),
                                      preferred_element_type=F32)


def _sample_cross_attn_kernel(x_ref, mk_ref, mv_ref, ncross_ref, wq_ref, qg_ref, o_ref, *, nseq, tq):
    qn = _mem_queries(x_ref[...], ncross_ref[...], wq_ref, qg_ref[...])
    nrow = MEM_HEADS * tq
    nmem = MEM_LEN * MEM_HEADS
    own = (lax.broadcasted_iota(jnp.int32, (nrow, nmem), 0) // tq
           == lax.broadcasted_iota(jnp.int32, (nrow, nmem), 1) % MEM_HEADS)
    scores = []
    for i in range(nseq):
        q_stack = jnp.concatenate([qn[h][i * tq:(i + 1) * tq] for h in range(MEM_HEADS)], axis=0)
        scores.append(_dot_nt(q_stack, mk_ref[i]))
    probs, denoms = [], []
    for s in scores:
        s = jnp.where(own, s, NEG_INF)
        p = jnp.exp2(s - jnp.max(s, axis=-1, keepdims=True))
        denoms.append(jnp.sum(p, axis=-1, keepdims=True))
        probs.append(p.astype(BF16))
    for i in range(nseq):
        o = jnp.dot(probs[i], mv_ref[i].astype(BF16), preferred_element_type=F32) / denoms[i]
        for h in range(MEM_HEADS):
            o_ref[i * tq:(i + 1) * tq, h * LANES:(h + 1) * LANES] = o[h * tq:(h + 1) * tq]


def _cross_out_ffn_kernel(x_ref, o_ref, wo_ref, nffn_ref, wgu_ref, wdown_ref, y_ref):
    x = x_ref[...] + jnp.dot(o_ref[...].astype(BF16), wo_ref[...], preferred_element_type=F32)
    hb = (_rms(x) * nffn_ref[...]).astype(BF16)
    g = jnp.dot(hb, wgu_ref[:, 0:FFN_HIDDEN], preferred_element_type=F32)
    u = jnp.dot(hb, wgu_ref[:, FFN_HIDDEN:], preferred_element_type=F32)
    act = (g * jax.nn.sigmoid(g) * u).astype(BF16)
    y_ref[...] = x + jnp.dot(act, wdown_ref[...], preferred_element_type=F32)


def _f32(*arrays):
    return [np.ascontiguousarray(a, dtype=np.float32) for a in arrays]


def _rope_angles(pos):
    half = HEAD_DIM // 2
    inv = 1.0 / (ROPE_THETA ** (np.arange(half, dtype=np.float64) / half))
    return pos.astype(np.float64)[:, None] * inv[None, :]


def _ret_angles(pos):
    inv = RET_THETA ** (-np.linspace(0.0, 1.0, RET_DIM // 2, dtype=np.float64))
    return pos.astype(np.float64)[:, None] * inv[None, :]


def _rope_tables(pos):
    ang = _rope_angles(pos)
    cos, sin = np.cos(ang), np.sin(ang)
    c64 = np.concatenate([cos, cos], axis=-1)
    s64 = np.concatenate([-sin, sin], axis=-1)
    return _f32(np.tile(c64, (1, 2)), np.tile(s64, (1, 2)))


def _retention_decays(c):
    log_g = np.log(1.0 - np.exp2(-5.0 - np.arange(RET_HEADS, dtype=np.float64)))
    idx = np.arange(c, dtype=np.float64)
    diff = idx[:, None] - idx[None, :]
    dmat = np.where(diff >= 0, np.exp(np.maximum(diff, 0.0)[None] * log_g[:, None, None]), 0.0)
    qd = np.exp((idx + 1.0)[None, :] * log_g[:, None])
    kd = np.exp((c - 1.0 - idx)[None, :] * log_g[:, None])
    cd = np.exp(c * log_g)
    return dmat, qd, kd, cd


def _sample_decay_rows(c):
    dmat, qd, kd, cd = _retention_decays(c)
    rows = np.concatenate([dmat.reshape(RET_HEADS, c * c), qd, kd, cd[:, None]], axis=1)
    rows = np.pad(rows, ((0, 0), (0, -rows.shape[1] % 8)))
    return _f32(np.broadcast_to(rows[:, :, None], rows.shape + (LANES,)))[0]


def _full(shape):
    nd = len(shape)
    return pl.BlockSpec(shape, lambda *_: (0,) * nd)


def _params(sem):
    return pltpu.CompilerParams(dimension_semantics=sem, vmem_limit_bytes=VMEM_LIMIT)


def _prompt_tables_t(pos):
    ang = _rope_angles(pos).T
    ang_r = _ret_angles(pos).T
    cos_r, sin_r = np.cos(ang_r), np.sin(ang_r)
    cr = np.repeat(cos_r, 2, axis=0)
    sr = np.stack([-sin_r, sin_r], axis=1).reshape(RET_DIM, pos.shape[0])
    return _f32(np.cos(ang), np.sin(ang), cr, sr)


def _decay_consts_t(c):
    dmat, qd, kd, cd = _retention_decays(c)
    dm = dmat.transpose(0, 2, 1).reshape(N_RET_SLABS, 2, c, c).transpose(0, 2, 1, 3).reshape(N_RET_SLABS, c, 2 * c)
    cd = np.broadcast_to(cd[:, None, None], (RET_HEADS, 1, RET_DIM))
    return _f32(dm, qd[:, None, :], kd[:, None, :], cd)


def _prompt_mixer(x, w, tile, sub):
    b, l, d = x.shape
    ca, sa, cr, sr = _prompt_tables_t(np.arange(l, dtype=np.int32))
    dm, qd, kd, cd = _decay_consts_t(RET_CHUNK)
    sink = jnp.repeat(w['sinks'].reshape(SWA_KV_HEADS, 1, -1), WINDOW, axis=-1)
    tab = lambda rows: pl.BlockSpec((rows, tile), lambda i, t: (0, t))
    xspec = pl.BlockSpec((1, tile, d), lambda i, t: (i, t, 0))
    win_spec = pl.BlockSpec((1, WINDOW, LANES), lambda i, t: (i, 0, 0))
    st_spec = pl.BlockSpec((1, RET_HEADS, RET_DIM, RET_DIM), lambda i, t: (i, 0, 0, 0))
    return pl.pallas_call(
        functools.partial(_prompt_mixer_kernel, tile=tile, sub=sub),
        grid=(b, l // tile),
        in_specs=[xspec, tab(HEAD_DIM // 2), tab(HEAD_DIM // 2), tab(RET_DIM), tab(RET_DIM), _full((1, d)),
                  _full((IN_COLS, d)), _full((HEAD_DIM, 1)), _full((HEAD_DIM, 1)), _full(sink.shape),
                  _full((d, d)), _full(dm.shape), _full(qd.shape), _full(kd.shape), _full(cd.shape)],
        out_specs=[xspec, win_spec, win_spec, st_spec],
        out_shape=[jax.ShapeDtypeStruct(x.shape, F32),
                   jax.ShapeDtypeStruct((b, WINDOW, LANES), F32),
                   jax.ShapeDtypeStruct((b, WINDOW, LANES), F32),
                   jax.ShapeDtypeStruct((b, RET_HEADS, RET_DIM, RET_DIM), F32)],
        scratch_shapes=[pltpu.VMEM((SWA_KV_W, WINDOW), F32), pltpu.VMEM((SWA_KV_W, WINDOW), F32),
                        pltpu.VMEM((RET_HEADS, RET_DIM, RET_DIM), F32)],
        compiler_params=_params(("arbitrary", "arbitrary")),
        name="prompt_mixer",
    )(x, ca, sa, cr, sr, w['norm_mix'], w['w_in_t'], w['qg_col'], w['kg_col'], sink, w['w_out'], dm, qd, kd,
      cd)


def _sample_mixer(x, cache_k, cache_v, state, w, tile, nseq):
    b, ntok, d = x.shape
    rows = b * ntok
    xf = x.reshape(rows, d)
    pos = np.tile(PAST_LEN + np.arange(ntok, dtype=np.int32), b)
    ca, sa = _rope_tables(pos)
    tab = pl.BlockSpec((tile, LANES), lambda i: (i, 0))
    row_spec = lambda width: pl.BlockSpec((tile, width), lambda i: (i, 0))
    feat = pl.pallas_call(
        _sample_in_kernel,
        grid=(rows // tile,),
        in_specs=[row_spec(d), tab, tab, _full((1, d)), _full((IN_COLS, d)), _full((1, LANES)),
                  _full((1, LANES)), _full((LANES, LANES))],
        out_specs=row_spec(SAMPLE_FEAT),
        out_shape=jax.ShapeDtypeStruct((rows, SAMPLE_FEAT), F32),
        compiler_params=_params(("arbitrary",)),
        name="sample_in",
    )(xf, ca, sa, w['norm_mix'], w['w_in_t'], w['qg'], w['kg'], w['bd'])

    sink = jnp.repeat(w['sinks'], ntok)[:, None]
    seq_rows = nseq * ntok
    unroll = nseq
    cache_spec = pl.BlockSpec((nseq, LANES, WINDOW), lambda i: (i, 0, 0))
    o_a, k_out, v_out = pl.pallas_call(
        functools.partial(_sample_swa_kernel, nseq=nseq, ntok=ntok, unroll=unroll),
        grid=(b // nseq,),
        in_specs=[pl.BlockSpec((seq_rows, SAMPLE_FEAT), lambda i: (i, 0)), cache_spec, cache_spec,
                  _full(sink.shape)],
        out_specs=[pl.BlockSpec((seq_rows, SWA_Q_W), lambda i: (i, 0)), cache_spec, cache_spec],
        out_shape=[jax.ShapeDtypeStruct((rows, SWA_Q_W), F32),
                   jax.ShapeDtypeStruct(cache_k.shape, F32),
                   jax.ShapeDtypeStruct(cache_v.shape, F32)],
        scratch_shapes=[pltpu.VMEM((unroll, WINDOW, LANES), F32), pltpu.VMEM((unroll, WINDOW, LANES), F32)],
        compiler_params=_params(("arbitrary",)),
        name="sample_swa",
    )(feat, cache_k, cache_v, sink)

    assert b == LANES, "the retention step puts one sequence per lane"
    xt = jnp.transpose(x, (1, 0, 2)).reshape(rows, d)
    pos_t = np.repeat(PAST_LEN + np.arange(ntok, dtype=np.int32), b)
    _, _, cr_t, sr_t = _prompt_tables_t(pos_t)
    half = rows // 2
    tab_t = pl.BlockSpec((RET_DIM, half), lambda i: (0, i))
    qkv = pl.pallas_call(
        _sample_ret_in_kernel,
        grid=(2,),
        in_specs=[pl.BlockSpec((half, d), lambda i: (i, 0)), tab_t, tab_t, _full((1, d)),
                  _full((IN_COLS, d))],
        out_specs=pl.BlockSpec((3 * RET_W, half), lambda i: (0, i)),
        out_shape=jax.ShapeDtypeStruct((3 * RET_W, rows), F32),
        compiler_params=_params(("arbitrary",)),
        name="sample_ret_in",
    )(xt, cr_t, sr_t, w['norm_mix'], w['w_in_t'])
    consts = _sample_decay_rows(ntok)
    pair_rows = lambda off: pl.BlockSpec((LANES, rows), lambda p: (off + p, 0))
    st_spec = pl.BlockSpec((2, RET_DIM, RET_DIM, b), lambda p: (p, 0, 0, 0))
    o_r, s_out = pl.pallas_call(
        functools.partial(_sample_ret_kernel, ntok=ntok),
        grid=(N_RET_SLABS,),
        in_specs=[pair_rows(0), pair_rows(N_RET_SLABS), pair_rows(2 * N_RET_SLABS), st_spec,
                  pl.BlockSpec((2,) + consts.shape[1:], lambda p: (p, 0, 0))],
        out_specs=[pl.BlockSpec((ntok, b, LANES), lambda p: (0, 0, p)), st_spec],
        out_shape=[jax.ShapeDtypeStruct((ntok, b, RET_W), F32), jax.ShapeDtypeStruct(state.shape, F32)],
        scratch_shapes=[pltpu.VMEM((ntok, RET_DIM, b), F32)],
        compiler_params=_params(("arbitrary",)),
        name="sample_ret",
    )(qkv, qkv, qkv, state, consts)
    o_r = jnp.transpose(o_r, (1, 0, 2)).reshape(rows, RET_W)

    y = pl.pallas_call(
        _sample_out_kernel,
        grid=(rows // tile,),
        in_specs=[row_spec(d), row_spec(SWA_Q_W), row_spec(RET_W), row_spec(SAMPLE_FEAT), _full((LANES, LANES)),
                  _full((d, d))],
        out_specs=row_spec(d),
        out_shape=jax.ShapeDtypeStruct((rows, d), F32),
        compiler_params=_params(("arbitrary",)),
        name="sample_out",
    )(xf, o_a, o_r, feat, w['bd'], w['w_out'])
    return y, k_out, v_out, s_out


def _mem_kv(mem, w, tile):
    rows, d = mem.shape
    row_spec = lambda width: pl.BlockSpec((tile, width), lambda i: (i, 0))
    return pl.pallas_call(
        _mem_kv_kernel,
        grid=(rows // tile,),
        in_specs=[row_spec(d), _full((1, d)), _full((d, 2 * MEM_W)), _full((1, LANES))],
        out_specs=[pl.BlockSpec((tile * MEM_HEADS, MEM_HEAD_DIM), lambda i: (i, 0))] * 2,
        out_shape=[jax.ShapeDtypeStruct((rows * MEM_HEADS, MEM_HEAD_DIM), F32)] * 2,
        compiler_params=_params(("arbitrary",)),
        name="mem_kv",
    )(mem, w['norm_mem'], w['w_mkv'], w['kgm'])


def _prompt_cross_ffn(x, mk, mv, w, tile):
    rows, d = x.shape
    n = rows // tile
    per_mem = n // mk.shape[0]
    cur = lambda i: jnp.minimum(i, n - 1)
    single = lambda shape: pl.BlockSpec(shape, lambda i: (0,) * len(shape), pipeline_mode=pl.Buffered(1))
    mem_spec = pl.BlockSpec((1, MEM_LEN * MEM_HEADS, MEM_HEAD_DIM), lambda i: (cur(i) // per_mem, 0, 0))
    return pl.pallas_call(
        _prompt_cross_ffn_kernel,
        grid=(n + 1,),
        in_specs=[pl.BlockSpec((tile, d), lambda i: (cur(i), 0)), mem_spec, mem_spec, _full((1, d)),
                  single((d, MEM_W)), _full((1, LANES)), single((MEM_W, d)), _full((1, d)),
                  single((d, 2 * FFN_HIDDEN)), single((FFN_HIDDEN, d))],
        out_specs=pl.BlockSpec((tile, d), lambda i: (jnp.maximum(i - 1, 0), 0)),
        out_shape=jax.ShapeDtypeStruct((rows, d), F32),
        scratch_shapes=[pltpu.VMEM((tile, d), F32)],
        compiler_params=_params(("arbitrary",)),
        name="prompt_cross_ffn",
    )(x, mk, mv, w['norm_cross'], w['w_mq'], w['qgm'], w['w_mo'], w['norm_ffn'], w['w_gu'], w['w_down'])


def _sample_cross_attn(x, mk, mv, w, nseq, tq):
    rows, d = x.shape
    blk = nseq * tq
    mem_spec = pl.BlockSpec((nseq, MEM_LEN * MEM_HEADS, MEM_HEAD_DIM), lambda i: (i, 0, 0))
    return pl.pallas_call(
        functools.partial(_sample_cross_attn_kernel, nseq=nseq, tq=tq),
        grid=(rows // blk,),
        in_specs=[pl.BlockSpec((blk, d), lambda i: (i, 0)), mem_spec, mem_spec, _full((1, d)),
                  _full((d, MEM_W)), _full((1, LANES))],
        out_specs=pl.BlockSpec((blk, MEM_W), lambda i: (i, 0)),
        out_shape=jax.ShapeDtypeStruct((rows, MEM_W), F32),
        compiler_params=_params(("arbitrary",)),
        name="sample_cross_attn",
    )(x, mk, mv, w['norm_cross'], w['w_mq'], w['qgm'])


def _sample_cross_out_ffn(x, o, w, tile):
    rows, d = x.shape
    row_spec = lambda width: pl.BlockSpec((tile, width), lambda i: (i, 0))
    single = lambda shape: pl.BlockSpec(shape, lambda i: (0,) * len(shape), pipeline_mode=pl.Buffered(1))
    return pl.pallas_call(
        _cross_out_ffn_kernel,
        grid=(rows // tile,),
        in_specs=[row_spec(d), row_spec(MEM_W), single((MEM_W, d)), _full((1, d)),
                  single((d, 2 * FFN_HIDDEN)), single((FFN_HIDDEN, d))],
        out_specs=row_spec(d),
        out_shape=jax.ShapeDtypeStruct((rows, d), F32),
        compiler_params=_params(("arbitrary",)),
        name="sample_ffn",
    )(x, o, w['w_mo'], w['norm_ffn'], w['w_gu'], w['w_down'])


def kernel(x_prompt, x_sample, mem_prompt, cache_swa_k, cache_swa_v, state_ret, cache_mem_k, cache_mem_v,
           norm_mix, w_in, q_norm_a, k_norm_a, sinks, w_out, norm_cross, norm_mem, w_mq, w_mkv,
           q_norm_m, k_norm_m, w_mo, norm_ffn, w_gu, w_down):
    assert norm_mix.shape[0] == 1, "single-layer kernel"
    b, l, d = x_prompt.shape
    sb, st, _ = x_sample.shape
    half = (np.arange(LANES) // HEAD_DIM)[:, None] == (np.arange(LANES) // HEAD_DIM)[None, :]
    w = {
        'norm_mix': norm_mix, 'norm_cross': norm_cross, 'norm_mem': norm_mem, 'norm_ffn': norm_ffn,
        'w_out': w_out[0].astype(BF16), 'w_mq': w_mq[0].astype(BF16),
        'w_mkv': w_mkv[0].astype(BF16), 'w_mo': w_mo[0].astype(BF16), 'w_gu': w_gu[0].astype(BF16),
        'w_down': w_down[0].astype(BF16),
        'w_in_t': w_in[0].T.astype(BF16),
        'qg': jnp.tile(q_norm_a, (1, 2)), 'kg': jnp.tile(k_norm_a, (1, 2)),
        'qg_col': q_norm_a.reshape(HEAD_DIM, 1), 'kg_col': k_norm_a.reshape(HEAD_DIM, 1),
        'qgm': q_norm_m, 'kgm': k_norm_m, 'sinks': sinks[0],
        'bd': half.astype(BF16),
    }

    mk, mv = _mem_kv(mem_prompt.reshape(b * MEM_LEN, d), w, tile=MEM_KV_TILE)
    xp, kwin, vwin, ret_p = _prompt_mixer(x_prompt, w, tile=PROMPT_MIXER_TILE, sub=PROMPT_MIXER_SUB)
    xp = xp.reshape(b * l, d)
    mem_rows_p = (b, MEM_LEN * MEM_HEADS, MEM_HEAD_DIM)
    yp = _prompt_cross_ffn(xp, mk.reshape(mem_rows_p), mv.reshape(mem_rows_p), w, tile=PROMPT_FFN_TILE)
    yp = yp.reshape(b, l, d)
    ret_p = ret_p.reshape(1, b, RET_HEADS, RET_DIM, RET_DIM)

    nbuf = cache_swa_k.shape[2]
    assert nbuf == WINDOW
    to_feature_major = lambda c: jnp.transpose(c[0], (0, 2, 3, 1)).reshape(sb, SWA_KV_W, nbuf)
    from_feature_major = lambda c: jnp.transpose(c.reshape(sb, SWA_KV_HEADS, HEAD_DIM, nbuf), (0, 3, 1, 2))[None]
    xs, k_s, v_s, s_s = _sample_mixer(x_sample, to_feature_major(cache_swa_k), to_feature_major(cache_swa_v),
                                      jnp.transpose(state_ret[0], (1, 2, 3, 0)), w, tile=SAMPLE_TILE,
                                      nseq=SAMPLE_SEQS)
    mem_rows = (sb, MEM_LEN * MEM_HEADS, MEM_HEAD_DIM)
    os_ = _sample_cross_attn(xs, cache_mem_k.reshape(mem_rows), cache_mem_v.reshape(mem_rows), w,
                             nseq=SAMPLE_SEQS, tq=st)
    ys = _sample_cross_out_ffn(xs, os_, w, tile=SAMPLE_FFN_TILE).reshape(sb, st, d)

    kv_shape = (1, b, WINDOW, SWA_KV_HEADS, HEAD_DIM)
    mem_shape = (1, b, MEM_LEN, MEM_HEADS, MEM_HEAD_DIM)
    return (yp, ys, kwin.reshape(kv_shape), vwin.reshape(kv_shape), ret_p,
            mk.reshape(mem_shape), mv.reshape(mem_shape),
            from_feature_major(k_s), from_feature_major(v_s),
            jnp.transpose(s_s, (3, 0, 1, 2))[None])
```

```python
import functools

import jax
import jax.numpy as jnp
import numpy as np
from jax import lax
from jax.experimental import pallas as pl
from jax.experimental.pallas import tpu as pltpu

F32 = jnp.float32
BF16 = jnp.bfloat16

LANES = 128
SUBLANES = 8
D_MODEL = 1024
HEAD_DIM = 64
SWA_HEADS = 8
SWA_KV_HEADS = 2
WINDOW = 128
RET_HEADS = 8
RET_DIM = 64
RET_CHUNK = 128
RET_THETA = 10000.0
ROPE_THETA = 10000.0
MEM_LEN = 256
MEM_HEADS = 4
MEM_HEAD_DIM = 128
MEM_W = MEM_HEADS * MEM_HEAD_DIM
FFN_HIDDEN = 2816
RMS_EPS = 1e-6
NEG_INF = -1e30
LOG2E = 1.4426950408889634
PAST_LEN = 16384

SWA_Q_W = SWA_HEADS * HEAD_DIM
SWA_KV_W = SWA_KV_HEADS * HEAD_DIM
RET_W = RET_HEADS * RET_DIM
IN_COLS = SWA_Q_W + 2 * SWA_KV_W + 4 * RET_W
C_QA, C_KA, C_VA = 0, SWA_Q_W, SWA_Q_W + SWA_KV_W
C_QR = SWA_Q_W + 2 * SWA_KV_W
C_KR, C_VR, C_G = C_QR + RET_W, C_QR + 2 * RET_W, C_QR + 3 * RET_W
C_GATE_S = C_QR
SAMPLE_FEAT = C_QR + RET_W
N_QA_SLABS = SWA_Q_W // LANES
N_RET_SLABS = RET_W // LANES

V7X_VMEM_BYTES = 64 * 1024 * 1024
VMEM_LIMIT = V7X_VMEM_BYTES - 8 * 1024 * 1024
PROMPT_MIXER_TILE, PROMPT_MIXER_SUB = 1024, 512
PROMPT_FFN_TILE = 512
MEM_KV_TILE = 512
SAMPLE_TILE = 512
SAMPLE_FFN_TILE = 512
SAMPLE_SEQS = 8
STAGE_SKEW = 1


def _dot(a, b):
    return jnp.dot(a.astype(BF16), b.astype(BF16), preferred_element_type=F32)


def _dot_nt(a, b):
    return lax.dot_general(a.astype(BF16), b.astype(BF16), (((1,), (1,)), ((), ())),
                           preferred_element_type=F32)


def _rms(x):
    return x * lax.rsqrt(jnp.mean(x * x, axis=-1, keepdims=True) + RMS_EPS)


def _lane_consts():
    lane = lax.broadcasted_iota(jnp.int32, (1, LANES), 1)
    m_left = (lane < HEAD_DIM).astype(F32)
    m_right = 1.0 - m_left
    first_half = (lane % HEAD_DIM) < (HEAD_DIM // 2)
    return m_left, m_right, first_half


def _head_norm(y, bd):
    ss = jnp.dot((y * y).astype(BF16), bd, preferred_element_type=F32) * (1.0 / HEAD_DIM)
    return y * lax.rsqrt(ss + RMS_EPS)


def _rot_half(y, cos, sin_signed, first_half):
    swapped = jnp.where(first_half, pltpu.roll(y, LANES - HEAD_DIM // 2, 1), pltpu.roll(y, HEAD_DIM // 2, 1))
    return y * cos + swapped * sin_signed


def _dot_tn(a, b):
    return lax.dot_general(a.astype(BF16), b.astype(BF16), (((0,), (0,)), ((), ())),
                           preferred_element_type=F32)


def _head_norm_t(y):
    return y * lax.rsqrt(jnp.mean(y * y, axis=0, keepdims=True) + RMS_EPS)


def _rot_half_t(y, cos, sin):
    half = HEAD_DIM // 2
    y1, y2 = y[0:half], y[half:]
    return jnp.concatenate([y1 * cos - y2 * sin, y2 * cos + y1 * sin], axis=0)


def _rot_pairs_t(y, cos, sin_signed, even_row):
    n = y.shape[0]
    swapped = jnp.where(even_row, pltpu.roll(y, n - 1, 0), pltpu.roll(y, 1, 0))
    return y * cos + swapped * sin_signed


def _prompt_mixer_kernel(x_ref, ca_ref, sa_ref, cr_ref, sr_ref, nmix_ref, wint_ref, qg_ref, kg_ref,
                         sink_ref, wout_ref, dm_ref, qd_ref, kd_ref, cd_ref,
                         y_ref, kwin_ref, vwin_ref, sout_ref,
                         pk_ref, pv_ref, s_ref, *, tile, sub):
    t = pl.program_id(1)
    nblk = sub // WINDOW
    hd = HEAD_DIM
    group = SWA_HEADS // SWA_KV_HEADS

    @pl.when(t == 0)
    def _():
        pk_ref[...] = jnp.zeros_like(pk_ref)
        pv_ref[...] = jnp.zeros_like(pv_ref)
        s_ref[...] = jnp.zeros_like(s_ref)

    qg, kg = qg_ref[...], kg_ref[...]
    qs = qg * (hd ** -0.5 * LOG2E)
    head = lambda a, h: a[h * hd:(h + 1) * hd]
    chunk = lambda a, h, j: a[h * hd:(h + 1) * hd, j * RET_CHUNK:(j + 1) * RET_CHUNK]
    keys_of = lambda j: slice(j * WINDOW, (j + 2) * WINDOW)
    units = [(g, j) for j in range(nblk) for g in range(SWA_KV_HEADS)]
    key = lax.broadcasted_iota(jnp.int32, (2 * WINDOW, group * WINDOW), 0)
    qry = lax.broadcasted_iota(jnp.int32, (2 * WINDOW, group * WINDOW), 1) % WINDOW
    band = (key > qry) & (key <= qry + WINDOW)
    even_row = (lax.broadcasted_iota(jnp.int32, (RET_DIM, sub), 0) % 2) == 0
    ones_rows = jnp.ones((2 * SUBLANES, 2 * WINDOW), BF16)
    zeros = jnp.zeros((hd, RET_CHUNK), F32)
    carry = {'pk': pk_ref[...], 'pv': pv_ref[...], 'state': [s_ref[h] for h in range(RET_HEADS)]}

    def stages(idx):
        toks = slice(idx * sub, (idx + 1) * sub)
        v = {}

        def project():
            v['x'] = x_ref[0, toks]
            hb = (_rms(v['x']) * nmix_ref[...]).astype(BF16)
            proj = lambda lo, hi: lax.dot_general(wint_ref[lo:hi, :], hb, (((1,), (1,)), ((), ())),
                                                  preferred_element_type=F32)
            v['swa'], v['ret'] = proj(C_QA, C_QR), proj(C_QR, C_VR)
            v['vr'], v['gate'] = proj(C_VR, C_G), proj(C_G, IN_COLS)

        def swa_prepare():
            ca, sa = ca_ref[:, toks], sa_ref[:, toks]
            swa_t = v['swa']
            v['qa'] = [_rot_half_t(_head_norm_t(head(swa_t, h)) * qs, ca, sa) for h in range(SWA_HEADS)]
            ka = jnp.concatenate([_rot_half_t(_head_norm_t(head(swa_t, SWA_HEADS + g)) * kg, ca, sa)
                                  for g in range(SWA_KV_HEADS)], axis=0)
            va = swa_t[C_VA:C_QR]
            v['kfull'] = jnp.concatenate([carry['pk'], ka], axis=1).astype(BF16)
            v['vfull'] = jnp.concatenate([carry['pv'], va], axis=1).astype(BF16)
            carry['pk'], carry['pv'] = ka[:, sub - WINDOW:], va[:, sub - WINDOW:]

        def swa_scores():
            v['scores'] = []
            for g, j in units:
                blk = slice(j * WINDOW, (j + 1) * WINDOW)
                q4 = jnp.concatenate([v['qa'][group * g + u][:, blk] for u in range(group)], axis=1)
                v['scores'].append(_dot_tn(head(v['kfull'], g)[:, keys_of(j)], q4))

        def ret_rotate():
            cr, sr = cr_ref[:, toks], sr_ref[:, toks]
            rot = lambda a: jnp.concatenate([_rot_pairs_t(head(a, h), cr, sr, even_row)
                                             for h in range(RET_HEADS)], axis=0)
            v['qr'] = rot(v['ret'][0:RET_W])
            v['kr'] = rot(v['ret'][RET_W:]) * (RET_DIM ** -0.5)

        def ret_scores():
            qr, kr, vr = v['qr'], v['kr'], v['vr']
            v['inner'] = [[None] * nblk for _ in range(N_RET_SLABS)]
            for j in range(nblk):
                for pr in range(N_RET_SLABS):
                    q_bd = jnp.concatenate([jnp.concatenate([chunk(qr, 2 * pr, j), zeros], axis=1),
                                            jnp.concatenate([zeros, chunk(qr, 2 * pr + 1, j)], axis=1)], axis=0)
                    k2 = kr[pr * LANES:(pr + 1) * LANES, j * RET_CHUNK:(j + 1) * RET_CHUNK]
                    v['inner'][pr][j] = _dot_tn(k2, q_bd) * dm_ref[pr]
            v['incr'] = [[_dot_nt(chunk(vr, h, j), chunk(kr, h, j) * kd_ref[h]) for j in range(nblk)]
                         for h in range(RET_HEADS)]

        def swa_softmax():
            v['probs'], v['sink_terms'] = [], []
            for (g, j), s in zip(units, v['scores']):
                sink = sink_ref[g] * LOG2E
                valid = band & (key >= WINDOW * (1 - t)) if (idx == 0 and j == 0) else band
                s = jnp.where(valid, s, NEG_INF)
                m = jnp.maximum(jnp.max(s, axis=0, keepdims=True), sink)
                v['probs'].append(jnp.exp2(s - m).astype(BF16))
                v['sink_terms'].append(jnp.exp2(sink - m))

        def swa_values():
            v['oa'] = [[None] * nblk for _ in range(SWA_HEADS)]
            for (g, j), p, sink_term in zip(units, v['probs'], v['sink_terms']):
                v_aug = jnp.concatenate([head(v['vfull'], g)[:, keys_of(j)], ones_rows], axis=0)
                o_aug = jnp.dot(v_aug, p, preferred_element_type=F32)
                o = o_aug[0:hd] / (o_aug[hd:hd + 1] + sink_term)
                for u in range(group):
                    v['oa'][group * g + u][j] = o[:, u * WINDOW:(u + 1) * WINDOW]
            v['state'] = [[carry['state'][h]] for h in range(RET_HEADS)]
            for h in range(RET_HEADS):
                for j in range(nblk):
                    v['state'][h].append(v['state'][h][j] * cd_ref[h] + v['incr'][h][j])
                carry['state'][h] = v['state'][h][nblk]

        def ret_outputs():
            v['or'] = [[None] * nblk for _ in range(RET_HEADS)]
            for j in range(nblk):
                for h in range(RET_HEADS):
                    u = h % 2
                    lhs = jnp.concatenate([chunk(v['vr'], h, j), v['state'][h][j]], axis=1)
                    rhs = jnp.concatenate([v['inner'][h // 2][j][:, u * RET_CHUNK:(u + 1) * RET_CHUNK],
                                           chunk(v['qr'], h, j) * qd_ref[h]], axis=0)
                    v['or'][h][j] = _dot(lhs, rhs)

        def gate_mix():
            mix = [jnp.concatenate(blocks, axis=1) for blocks in v['oa']]
            for h in range(RET_HEADS):
                o_h = _head_norm_t(jnp.concatenate(v['or'][h], axis=1))
                g_h = head(v['gate'], h)
                mix.append(o_h * (g_h * jax.nn.sigmoid(g_h)))
            v['mix'] = jnp.concatenate(mix, axis=0).astype(BF16)

        def out_project():
            y_ref[0, toks] = v['x'] + lax.dot_general(v['mix'], wout_ref[...], (((0,), (0,)), ((), ())),
                                                      preferred_element_type=F32)

        return [project, swa_prepare, swa_scores, ret_rotate, ret_scores, swa_softmax, swa_values,
                ret_outputs, gate_mix, out_project]

    pipelines = [stages(i) for i in range(tile // sub)]
    nstage = len(pipelines[0])
    for slot in range(nstage + STAGE_SKEW * (len(pipelines) - 1)):
        for i, pipe in enumerate(pipelines):
            if 0 <= slot - STAGE_SKEW * i < nstage:
                pipe[slot - STAGE_SKEW * i]()

    pk_ref[...], pv_ref[...] = carry['pk'], carry['pv']
    kwin_ref[0], vwin_ref[0] = carry['pk'].T, carry['pv'].T
    for h in range(RET_HEADS):
        s_ref[h] = carry['state'][h]

    @pl.when(t == pl.num_programs(1) - 1)
    def _():
        for h in range(RET_HEADS):
            sout_ref[0, h] = carry['state'][h].T


def _sample_in_kernel(x_ref, ca_ref, sa_ref, nmix_ref, wint_ref, qg_ref, kg_ref, bd_ref, feat_ref):
    _, _, first_half = _lane_consts()
    ca, sa, bd = ca_ref[...], sa_ref[...], bd_ref[...]
    hb = (_rms(x_ref[...]) * nmix_ref[...]).astype(BF16)
    proj = lambda lo, hi: lax.dot_general(hb, wint_ref[lo:hi, :], (((1,), (1,)), ((), ())),
                                          preferred_element_type=F32)
    swa = proj(C_QA, C_QR)
    slabs = [_rot_half(_head_norm(swa[:, s * LANES:(s + 1) * LANES], bd) * qg_ref[...], ca, sa, first_half)
             for s in range(N_QA_SLABS)]
    slabs.append(_rot_half(_head_norm(swa[:, C_KA:C_VA], bd) * kg_ref[...], ca, sa, first_half))
    feat_ref[...] = jnp.concatenate(slabs + [swa[:, C_VA:C_QR], proj(C_G, IN_COLS)], axis=1)


def _sample_swa_kernel(feat_ref, ckt_ref, cvt_ref, sink_ref, o_ref, kout_ref, vout_ref,
                       kn_ref, vn_ref, *, nseq, ntok, unroll):
    @pl.when(pl.program_id(0) == 0)
    def _():
        kn_ref[...] = jnp.zeros_like(kn_ref)
        vn_ref[...] = jnp.zeros_like(vn_ref)

    m_left, m_right, _ = _lane_consts()
    tail = WINDOW - ntok
    nrow = SWA_HEADS * ntok
    row_tok = lax.broadcasted_iota(jnp.int32, (nrow, WINDOW), 0) % ntok
    col = lax.broadcasted_iota(jnp.int32, (nrow, WINDOW), 1)
    valid_cache = col > row_tok
    valid_new = (col >= tail) & (col - tail <= row_tok)
    in_tail = lax.broadcasted_iota(jnp.int32, (LANES, WINDOW), 1) >= tail
    sink = sink_ref[...]
    shift = lambda a: pltpu.roll(a, HEAD_DIM, 1)

    def body(step, carry):
        seqs = [step * unroll + u for u in range(unroll)]
        rows = [pl.ds(pl.multiple_of(i * ntok, ntok), ntok) for i in seqs]
        q_rows, caches = [], []
        for u, i in enumerate(seqs):
            slab = lambda c, u=u: feat_ref[rows[u], c:c + LANES]
            kn_ref[u, tail:WINDOW] = slab(C_KA)
            vn_ref[u, tail:WINDOW] = slab(C_VA)
            s0, s1, s2, s3 = [slab(C_QA + s * LANES) for s in range(N_QA_SLABS)]
            q_rows.append(jnp.concatenate(
                [s0 * m_left, shift(s0 * m_right), s1 * m_left, shift(s1 * m_right),
                 shift(s2 * m_left), s2 * m_right, shift(s3 * m_left), s3 * m_right], axis=0))
            caches.append((ckt_ref[i], cvt_ref[i], kn_ref[u], vn_ref[u]))
        scores = [(_dot(q, kt), _dot_nt(q, kn)) for q, (kt, _, kn, _) in zip(q_rows, caches)]
        probs = []
        for s_c, s_n in scores:
            s_c = jnp.where(valid_cache, s_c * (HEAD_DIM ** -0.5), NEG_INF)
            s_n = jnp.where(valid_new, s_n * (HEAD_DIM ** -0.5), NEG_INF)
            m = jnp.maximum(jnp.maximum(jnp.max(s_c, axis=-1, keepdims=True),
                                        jnp.max(s_n, axis=-1, keepdims=True)), sink)
            p_c, p_n = jnp.exp(s_c - m), jnp.exp(s_n - m)
            denom = (jnp.sum(p_c, axis=-1, keepdims=True) + jnp.sum(p_n, axis=-1, keepdims=True)
                     + jnp.exp(sink - m))
            probs.append((p_c, p_n, denom))
        for u, i in enumerate(seqs):
            kt, vt, kn, vn = caches[u]
            p_c, p_n, denom = probs[u]
            o = (_dot_nt(p_c, vt) + _dot(p_n, vn)) / denom
            n = ntok
            left = lambda h: o[h * n:(h + 1) * n] * m_left
            right = lambda h: o[h * n:(h + 1) * n] * m_right
            o_ref[rows[u], 0:LANES] = left(0) + shift(left(1))
            o_ref[rows[u], LANES:2 * LANES] = left(2) + shift(left(3))
            o_ref[rows[u], 2 * LANES:3 * LANES] = shift(right(4)) + right(5)
            o_ref[rows[u], 3 * LANES:4 * LANES] = shift(right(6)) + right(7)
            kout_ref[i] = jnp.where(in_tail, kn.T, pltpu.roll(kt, tail, 1))
            vout_ref[i] = jnp.where(in_tail, vn.T, pltpu.roll(vt, tail, 1))
        return carry

    lax.fori_loop(0, nseq // unroll, body, 0)


def _sample_ret_in_kernel(x_ref, cr_ref, sr_ref, nmix_ref, wint_ref, qkv_ref):
    hb = (_rms(x_ref[...]) * nmix_ref[...]).astype(BF16)
    qkv = lax.dot_general(wint_ref[C_QR:C_G, :], hb, (((1,), (1,)), ((), ())), preferred_element_type=F32)
    n = qkv.shape[1]
    even_row = (lax.broadcasted_iota(jnp.int32, (RET_DIM, n), 0) % 2) == 0
    cr, sr = cr_ref[...], sr_ref[...]
    for h in range(RET_HEADS):
        q_rows = slice(h * RET_DIM, (h + 1) * RET_DIM)
        k_rows = slice(RET_W + h * RET_DIM, RET_W + (h + 1) * RET_DIM)
        qkv_ref[q_rows] = _rot_pairs_t(qkv[q_rows], cr, sr, even_row)
        qkv_ref[k_rows] = _rot_pairs_t(qkv[k_rows], cr, sr, even_row) * (RET_DIM ** -0.5)
    qkv_ref[2 * RET_W:] = qkv[2 * RET_W:]


def _sample_ret_kernel(q_ref, k_ref, v_ref, s_ref, c_ref, o_ref, so_ref, ks_ref, *, ntok):
    hd, nseq = RET_DIM, LANES
    tok = lambda t: slice(t * nseq, (t + 1) * nseq)
    row_qd, row_kd, row_cd = ntok * ntok, ntok * ntok + ntok, ntok * ntok + 2 * ntok
    group = SUBLANES
    pair_outs = []
    for u in range(2):
        feats = slice(u * hd, (u + 1) * hd)
        const = lambda r: c_ref[u, r:r + 1, :]
        for t in range(ntok):
            ks_ref[t] = k_ref[feats, tok(t)] * const(row_kd + t)
        outs = []
        for t in range(ntok):
            q_t = q_ref[feats, tok(t)]
            acc = jnp.zeros((hd, nseq), F32)
            for k in range(t + 1):
                w_tk = jnp.sum(q_t * k_ref[feats, tok(k)], axis=0, keepdims=True) * const(t * ntok + k)
                acc = acc + w_tk * v_ref[feats, tok(k)]

            def cross(j, carry, t=t):
                d0 = pl.multiple_of(j * group, group)
                q_rows = q_ref[pl.ds(u * hd + d0, group), tok(t)]
                for i in range(group):
                    carry = carry + q_rows[i:i + 1, :] * s_ref[u, d0 + i]
                return carry
            carried = lax.fori_loop(0, hd // group, cross, jnp.zeros((hd, nseq), F32))
            outs.append(acc + carried * const(row_qd + t))

        def update(j, carry):
            d0 = pl.multiple_of(j * group, group)
            k_rows = [ks_ref[t, pl.ds(d0, group), :] for t in range(ntok)]
            for i in range(group):
                new = s_ref[u, d0 + i] * const(row_cd)
                for t in range(ntok):
                    new = new + k_rows[t][i:i + 1, :] * v_ref[feats, tok(t)]
                so_ref[u, d0 + i] = new
            return carry
        lax.fori_loop(0, hd // group, update, 0)
        pair_outs.append(outs)
    for t in range(ntok):
        o_ref[t] = jnp.concatenate([pair_outs[0][t], pair_outs[1][t]], axis=0).T


def _sample_out_kernel(x_ref, oa_ref, or_ref, feat_ref, bd_ref, wout_ref, y_ref):
    outs = [oa_ref[...]]
    for p in range(N_RET_SLABS):
        gate = feat_ref[:, C_GATE_S + p * LANES:C_GATE_S + (p + 1) * LANES]
        o_ret = or_ref[:, p * LANES:(p + 1) * LANES]
        outs.append(_head_norm(o_ret, bd_ref[...]) * (gate * jax.nn.sigmoid(gate)))
    mix = jnp.concatenate(outs, axis=1).astype(BF16)
    y_ref[...] = x_ref[...] + jnp.dot(mix, wout_ref[...], preferred_element_type=F32)


def _mem_kv_kernel(m_ref, nmem_ref, wkv_ref, kg_ref, k_ref, v_ref):
    hb = (_rms(m_ref[...]) * nmem_ref[...]).astype(BF16)
    kv = jnp.dot(hb, wkv_ref[...], preferred_element_type=F32)
    slots = m_ref.shape[0]
    for h in range(MEM_HEADS):
        rows = pl.ds(h, slots, stride=MEM_HEADS)
        k_ref[rows, :] = _rms(kv[:, h * LANES:(h + 1) * LANES]) * kg_ref[...]
        v_ref[rows, :] = kv[:, MEM_W + h * LANES:MEM_W + (h + 1) * LANES]


def _mem_queries(x, ncross, wq_ref, qg):
    hb = (_rms(x) * ncross).astype(BF16)
    q = jnp.dot(hb, wq_ref[...], preferred_element_type=F32)
    qs = qg * (MEM_HEAD_DIM ** -0.5 * LOG2E)
    return [_rms(q[:, h * LANES:(h + 1) * LANES]) * qs for h in range(MEM_HEADS)]


def _prompt_cross_ffn_kernel(x_ref, mk_ref, mv_ref, ncross_ref, wq_ref, qg_ref, wo_ref, nffn_ref, wgu_ref,
                             wdown_ref, y_ref, xa_ref):
    @pl.when(pl.program_id(0) == 0)
    def _():
        xa_ref[...] = jnp.zeros_like(xa_ref)

    xa = xa_ref[...]
    hb = (_rms(xa) * nffn_ref[...]).astype(BF16)
    x = x_ref[...]
    qn = _mem_queries(x, ncross_ref[...], wq_ref, qg_ref[...])
    bounds = (0, 6 * 2 * LANES, FFN_HIDDEN)

    def ffn_half(c):
        lo, hi = bounds[c], bounds[c + 1]
        g = jnp.dot(hb, wgu_ref[:, lo:hi], preferred_element_type=F32)
        u = jnp.dot(hb, wgu_ref[:, FFN_HIDDEN + lo:FFN_HIDDEN + hi], preferred_element_type=F32)
        act = (g * jax.nn.sigmoid(g) * u).astype(BF16)
        return jnp.dot(act, wdown_ref[lo:hi, :], preferred_element_type=F32)

    down = ffn_half(0)
    head = lambda ref, h: ref[0, pl.ds(h, MEM_LEN, stride=MEM_HEADS), :].astype(BF16)
    scores = [_dot_nt(head(mk_ref, h), qn[h]) for h in range(MEM_HEADS)]
    probs, denoms = [], []
    for s in scores:
        p = jnp.exp2(s - jnp.max(s, axis=0, keepdims=True))
        denoms.append(jnp.sum(p, axis=0, keepdims=True))
        probs.append(p.astype(BF16))
    down = down + ffn_half(1)
    o_t = jnp.concatenate([_dot_tn(head(mv_ref, h), probs[h]) / denoms[h] for h in range(MEM_HEADS)], axis=0)
    y_ref[...] = xa + down
    xa_ref[...] = x + lax.dot_general(o_t.astype(BF16), wo_ref[...], (((0,), (0,)), ((), ())),
                                      preferred_element_type=F32)


def _sample_cross_attn_kernel(x_ref, mk_ref, mv_ref, ncross_ref, wq_ref, qg_ref, o_ref, *, nseq, tq):
    qn = _mem_queries(x_ref[...], ncross_ref[...], wq_ref, qg_ref[...])
    nrow = MEM_HEADS * tq
    nmem = MEM_LEN * MEM_HEADS
    own = (lax.broadcasted_iota(jnp.int32, (nrow, nmem), 0) // tq
           == lax.broadcasted_iota(jnp.int32, (nrow, nmem), 1) % MEM_HEADS)
    scores = []
    for i in range(nseq):
        q_stack = jnp.concatenate([qn[h][i * tq:(i + 1) * tq] for h in range(MEM_HEADS)], axis=0)
        scores.append(_dot_nt(q_stack, mk_ref[i]))
    probs, denoms = [], []
    for s in scores:
        s = jnp.where(own, s, NEG_INF)
        p = jnp.exp2(s - jnp.max(s, axis=-1, keepdims=True))
        denoms.append(jnp.sum(p, axis=-1, keepdims=True))
        probs.append(p.astype(BF16))
    for i in range(nseq):
        o = jnp.dot(probs[i], mv_ref[i].astype(BF16), preferred_element_type=F32) / denoms[i]
        for h in range(MEM_HEADS):
            o_ref[i * tq:(i + 1) * tq, h * LANES:(h + 1) * LANES] = o[h * tq:(h + 1) * tq]


def _cross_out_ffn_kernel(x_ref, o_ref, wo_ref, nffn_ref, wgu_ref, wdown_ref, y_ref):
    x = x_ref[...] + jnp.dot(o_ref[...].astype(BF16), wo_ref[...], preferred_element_type=F32)
    hb = (_rms(x) * nffn_ref[...]).astype(BF16)
    g = jnp.dot(hb, wgu_ref[:, 0:FFN_HIDDEN], preferred_element_type=F32)
    u = jnp.dot(hb, wgu_ref[:, FFN_HIDDEN:], preferred_element_type=F32)
    act = (g * jax.nn.sigmoid(g) * u).astype(BF16)
    y_ref[...] = x + jnp.dot(act, wdown_ref[...], preferred_element_type=F32)


def _f32(*arrays):
    return [np.ascontiguousarray(a, dtype=np.float32) for a in arrays]


def _rope_angles(pos):
    half = HEAD_DIM // 2
    inv = 1.0 / (ROPE_THETA ** (np.arange(half, dtype=np.float64) / half))
    return pos.astype(np.float64)[:, None] * inv[None, :]


def _ret_angles(pos):
    inv = RET_THETA ** (-np.linspace(0.0, 1.0, RET_DIM // 2, dtype=np.float64))
    return pos.astype(np.float64)[:, None] * inv[None, :]


def _rope_tables(pos):
    ang = _rope_angles(pos)
    cos, sin = np.cos(ang), np.sin(ang)
    c64 = np.concatenate([cos, cos], axis=-1)
    s64 = np.concatenate([-sin, sin], axis=-1)
    return _f32(np.tile(c64, (1, 2)), np.tile(s64, (1, 2)))


def _retention_decays(c):
    log_g = np.log(1.0 - np.exp2(-5.0 - np.arange(RET_HEADS, dtype=np.float64)))
    idx = np.arange(c, dtype=np.float64)
    diff = idx[:, None] - idx[None, :]
    dmat = np.where(diff >= 0, np.exp(np.maximum(diff, 0.0)[None] * log_g[:, None, None]), 0.0)
    qd = np.exp((idx + 1.0)[None, :] * log_g[:, None])
    kd = np.exp((c - 1.0 - idx)[None, :] * log_g[:, None])
    cd = np.exp(c * log_g)
    return dmat, qd, kd, cd


def _sample_decay_rows(c):
    dmat, qd, kd, cd = _retention_decays(c)
    rows = np.concatenate([dmat.reshape(RET_HEADS, c * c), qd, kd, cd[:, None]], axis=1)
    rows = np.pad(rows, ((0, 0), (0, -rows.shape[1] % 8)))
    return _f32(np.broadcast_to(rows[:, :, None], rows.shape + (LANES,)))[0]


def _full(shape):
    nd = len(shape)
    return pl.BlockSpec(shape, lambda *_: (0,) * nd)


def _params(sem):
    return pltpu.CompilerParams(dimension_semantics=sem, vmem_limit_bytes=VMEM_LIMIT)


def _prompt_tables_t(pos):
    ang = _rope_angles(pos).T
    ang_r = _ret_angles(pos).T
    cos_r, sin_r = np.cos(ang_r), np.sin(ang_r)
    cr = np.repeat(cos_r, 2, axis=0)
    sr = np.stack([-sin_r, sin_r], axis=1).reshape(RET_DIM, pos.shape[0])
    return _f32(np.cos(ang), np.sin(ang), cr, sr)


def _decay_consts_t(c):
    dmat, qd, kd, cd = _retention_decays(c)
    dm = dmat.transpose(0, 2, 1).reshape(N_RET_SLABS, 2, c, c).transpose(0, 2, 1, 3).reshape(N_RET_SLABS, c, 2 * c)
    cd = np.broadcast_to(cd[:, None, None], (RET_HEADS, 1, RET_DIM))
    return _f32(dm, qd[:, None, :], kd[:, None, :], cd)


def _prompt_mixer(x, w, tile, sub):
    b, l, d = x.shape
    ca, sa, cr, sr = _prompt_tables_t(np.arange(l, dtype=np.int32))
    dm, qd, kd, cd = _decay_consts_t(RET_CHUNK)
    sink = jnp.repeat(w['sinks'].reshape(SWA_KV_HEADS, 1, -1), WINDOW, axis=-1)
    tab = lambda rows: pl.BlockSpec((rows, tile), lambda i, t: (0, t))
    xspec = pl.BlockSpec((1, tile, d), lambda i, t: (i, t, 0))
    win_spec = pl.BlockSpec((1, WINDOW, LANES), lambda i, t: (i, 0, 0))
    st_spec = pl.BlockSpec((1, RET_HEADS, RET_DIM, RET_DIM), lambda i, t: (i, 0, 0, 0))
    return pl.pallas_call(
        functools.partial(_prompt_mixer_kernel, tile=tile, sub=sub),
        grid=(b, l // tile),
        in_specs=[xspec, tab(HEAD_DIM // 2), tab(HEAD_DIM // 2), tab(RET_DIM), tab(RET_DIM), _full((1, d)),
                  _full((IN_COLS, d)), _full((HEAD_DIM, 1)), _full((HEAD_DIM, 1)), _full(sink.shape),
                  _full((d, d)), _full(dm.shape), _full(qd.shape), _full(kd.shape), _full(cd.shape)],
        out_specs=[xspec, win_spec, win_spec, st_spec],
        out_shape=[jax.ShapeDtypeStruct(x.shape, F32),
                   jax.ShapeDtypeStruct((b, WINDOW, LANES), F32),
                   jax.ShapeDtypeStruct((b, WINDOW, LANES), F32),
                   jax.ShapeDtypeStruct((b, RET_HEADS, RET_DIM, RET_DIM), F32)],
        scratch_shapes=[pltpu.VMEM((SWA_KV_W, WINDOW), F32), pltpu.VMEM((SWA_KV_W, WINDOW), F32),
                        pltpu.VMEM((RET_HEADS, RET_DIM, RET_DIM), F32)],
        compiler_params=_params(("arbitrary", "arbitrary")),
        name="prompt_mixer",
    )(x, ca, sa, cr, sr, w['norm_mix'], w['w_in_t'], w['qg_col'], w['kg_col'], sink, w['w_out'], dm, qd, kd,
      cd)


def _sample_mixer(x, cache_k, cache_v, state, w, tile, nseq):
    b, ntok, d = x.shape
    rows = b * ntok
    xf = x.reshape(rows, d)
    pos = np.tile(PAST_LEN + np.arange(ntok, dtype=np.int32), b)
    ca, sa = _rope_tables(pos)
    tab = pl.BlockSpec((tile, LANES), lambda i: (i, 0))
    row_spec = lambda width: pl.BlockSpec((tile, width), lambda i: (i, 0))
    feat = pl.pallas_call(
        _sample_in_kernel,
        grid=(rows // tile,),
        in_specs=[row_spec(d), tab, tab, _full((1, d)), _full((IN_COLS, d)), _full((1, LANES)),
                  _full((1, LANES)), _full((LANES, LANES))],
        out_specs=row_spec(SAMPLE_FEAT),
        out_shape=jax.ShapeDtypeStruct((rows, SAMPLE_FEAT), F32),
        compiler_params=_params(("arbitrary",)),
        name="sample_in",
    )(xf, ca, sa, w['norm_mix'], w['w_in_t'], w['qg'], w['kg'], w['bd'])

    sink = jnp.repeat(w['sinks'], ntok)[:, None]
    seq_rows = nseq * ntok
    unroll = nseq
    cache_spec = pl.BlockSpec((nseq, LANES, WINDOW), lambda i: (i, 0, 0))
    o_a, k_out, v_out = pl.pallas_call(
        functools.partial(_sample_swa_kernel, nseq=nseq, ntok=ntok, unroll=unroll),
        grid=(b // nseq,),
        in_specs=[pl.BlockSpec((seq_rows, SAMPLE_FEAT), lambda i: (i, 0)), cache_spec, cache_spec,
                  _full(sink.shape)],
        out_specs=[pl.BlockSpec((seq_rows, SWA_Q_W), lambda i: (i, 0)), cache_spec, cache_spec],
        out_shape=[jax.ShapeDtypeStruct((rows, SWA_Q_W), F32),
                   jax.ShapeDtypeStruct(cache_k.shape, F32),
                   jax.ShapeDtypeStruct(cache_v.shape, F32)],
        scratch_shapes=[pltpu.VMEM((unroll, WINDOW, LANES), F32), pltpu.VMEM((unroll, WINDOW, LANES), F32)],
        compiler_params=_params(("arbitrary",)),
        name="sample_swa",
    )(feat, cache_k, cache_v, sink)

    assert b == LANES, "the retention step puts one sequence per lane"
    xt = jnp.transpose(x, (1, 0, 2)).reshape(rows, d)
    pos_t = np.repeat(PAST_LEN + np.arange(ntok, dtype=np.int32), b)
    _, _, cr_t, sr_t = _prompt_tables_t(pos_t)
    half = rows // 2
    tab_t = pl.BlockSpec((RET_DIM, half), lambda i: (0, i))
    qkv = pl.pallas_call(
        _sample_ret_in_kernel,
        grid=(2,),
        in_specs=[pl.BlockSpec((half, d), lambda i: (i, 0)), tab_t, tab_t, _full((1, d)),
                  _full((IN_COLS, d))],
        out_specs=pl.BlockSpec((3 * RET_W, half), lambda i: (0, i)),
        out_shape=jax.ShapeDtypeStruct((3 * RET_W, rows), F32),
        compiler_params=_params(("arbitrary",)),
        name="sample_ret_in",
    )(xt, cr_t, sr_t, w['norm_mix'], w['w_in_t'])
    consts = _sample_decay_rows(ntok)
    pair_rows = lambda off: pl.BlockSpec((LANES, rows), lambda p: (off + p, 0))
    st_spec = pl.BlockSpec((2, RET_DIM, RET_DIM, b), lambda p: (p, 0, 0, 0))
    o_r, s_out = pl.pallas_call(
        functools.partial(_sample_ret_kernel, ntok=ntok),
        grid=(N_RET_SLABS,),
        in_specs=[pair_rows(0), pair_rows(N_RET_SLABS), pair_rows(2 * N_RET_SLABS), st_spec,
                  pl.BlockSpec((2,) + consts.shape[1:], lambda p: (p, 0, 0))],
        out_specs=[pl.BlockSpec((ntok, b, LANES), lambda p: (0, 0, p)), st_spec],
        out_shape=[jax.ShapeDtypeStruct((ntok, b, RET_W), F32), jax.ShapeDtypeStruct(state.shape, F32)],
        scratch_shapes=[pltpu.VMEM((ntok, RET_DIM, b), F32)],
        compiler_params=_params(("arbitrary",)),
        name="sample_ret",
    )(qkv, qkv, qkv, state, consts)
    o_r = jnp.transpose(o_r, (1, 0, 2)).reshape(rows, RET_W)

    y = pl.pallas_call(
        _sample_out_kernel,
        grid=(rows // tile,),
        in_specs=[row_spec(d), row_spec(SWA_Q_W), row_spec(RET_W), row_spec(SAMPLE_FEAT), _full((LANES, LANES)),
                  _full((d, d))],
        out_specs=row_spec(d),
        out_shape=jax.ShapeDtypeStruct((rows, d), F32),
        compiler_params=_params(("arbitrary",)),
        name="sample_out",
    )(xf, o_a, o_r, feat, w['bd'], w['w_out'])
    return y, k_out, v_out, s_out


def _mem_kv(mem, w, tile):
    rows, d = mem.shape
    row_spec = lambda width: pl.BlockSpec((tile, width), lambda i: (i, 0))
    return pl.pallas_call(
        _mem_kv_kernel,
        grid=(rows // tile,),
        in_specs=[row_spec(d), _full((1, d)), _full((d, 2 * MEM_W)), _full((1, LANES))],
        out_specs=[pl.BlockSpec((tile * MEM_HEADS, MEM_HEAD_DIM), lambda i: (i, 0))] * 2,
        out_shape=[jax.ShapeDtypeStruct((rows * MEM_HEADS, MEM_HEAD_DIM), F32)] * 2,
        compiler_params=_params(("arbitrary",)),
        name="mem_kv",
    )(mem, w['norm_mem'], w['w_mkv'], w['kgm'])


def _prompt_cross_ffn(x, mk, mv, w, tile):
    rows, d = x.shape
    n = rows // tile
    per_mem = n // mk.shape[0]
    cur = lambda i: jnp.minimum(i, n - 1)
    single = lambda shape: pl.BlockSpec(shape, lambda i: (0,) * len(shape), pipeline_mode=pl.Buffered(1))
    mem_spec = pl.BlockSpec((1, MEM_LEN * MEM_HEADS, MEM_HEAD_DIM), lambda i: (cur(i) // per_mem, 0, 0))
    return pl.pallas_call(
        _prompt_cross_ffn_kernel,
        grid=(n + 1,),
        in_specs=[pl.BlockSpec((tile, d), lambda i: (cur(i), 0)), mem_spec, mem_spec, _full((1, d)),
                  single((d, MEM_W)), _full((1, LANES)), single((MEM_W, d)), _full((1, d)),
                  single((d, 2 * FFN_HIDDEN)), single((FFN_HIDDEN, d))],
        out_specs=pl.BlockSpec((tile, d), lambda i: (jnp.maximum(i - 1, 0), 0)),
        out_shape=jax.ShapeDtypeStruct((rows, d), F32),
        scratch_shapes=[pltpu.VMEM((tile, d), F32)],
        compiler_params=_params(("arbitrary",)),
        name="prompt_cross_ffn",
    )(x, mk, mv, w['norm_cross'], w['w_mq'], w['qgm'], w['w_mo'], w['norm_ffn'], w['w_gu'], w['w_down'])


def _sample_cross_attn(x, mk, mv, w, nseq, tq):
    rows, d = x.shape
    blk = nseq * tq
    mem_spec = pl.BlockSpec((nseq, MEM_LEN * MEM_HEADS, MEM_HEAD_DIM), lambda i: (i, 0, 0))
    return pl.pallas_call(
        functools.partial(_sample_cross_attn_kernel, nseq=nseq, tq=tq),
        grid=(rows // blk,),
        in_specs=[pl.BlockSpec((blk, d), lambda i: (i, 0)), mem_spec, mem_spec, _full((1, d)),
                  _full((d, MEM_W)), _full((1, LANES))],
        out_specs=pl.BlockSpec((blk, MEM_W), lambda i: (i, 0)),
        out_shape=jax.ShapeDtypeStruct((rows, MEM_W), F32),
        compiler_params=_params(("arbitrary",)),
        name="sample_cross_attn",
    )(x, mk, mv, w['norm_cross'], w['w_mq'], w['qgm'])


def _sample_cross_out_ffn(x, o, w, tile):
    rows, d = x.shape
    row_spec = lambda width: pl.BlockSpec((tile, width), lambda i: (i, 0))
    single = lambda shape: pl.BlockSpec(shape, lambda i: (0,) * len(shape), pipeline_mode=pl.Buffered(1))
    return pl.pallas_call(
        _cross_out_ffn_kernel,
        grid=(rows // tile,),
        in_specs=[row_spec(d), row_spec(MEM_W), single((MEM_W, d)), _full((1, d)),
                  single((d, 2 * FFN_HIDDEN)), single((FFN_HIDDEN, d))],
        out_specs=row_spec(d),
        out_shape=jax.ShapeDtypeStruct((rows, d), F32),
        compiler_params=_params(("arbitrary",)),
        name="sample_ffn",
    )(x, o, w['w_mo'], w['norm_ffn'], w['w_gu'], w['w_down'])


def kernel(x_prompt, x_sample, mem_prompt, cache_swa_k, cache_swa_v, state_ret, cache_mem_k, cache_mem_v,
           norm_mix, w_in, q_norm_a, k_norm_a, sinks, w_out, norm_cross, norm_mem, w_mq, w_mkv,
           q_norm_m, k_norm_m, w_mo, norm_ffn, w_gu, w_down):
    assert norm_mix.shape[0] == 1, "single-layer kernel"
    b, l, d = x_prompt.shape
    sb, st, _ = x_sample.shape
    half = (np.arange(LANES) // HEAD_DIM)[:, None] == (np.arange(LANES) // HEAD_DIM)[None, :]
    w = {
        'norm_mix': norm_mix, 'norm_cross': norm_cross, 'norm_mem': norm_mem, 'norm_ffn': norm_ffn,
        'w_out': w_out[0].astype(BF16), 'w_mq': w_mq[0].astype(BF16),
        'w_mkv': w_mkv[0].astype(BF16), 'w_mo': w_mo[0].astype(BF16), 'w_gu': w_gu[0].astype(BF16),
        'w_down': w_down[0].astype(BF16),
        'w_in_t': w_in[0].T.astype(BF16),
        'qg': jnp.tile(q_norm_a, (1, 2)), 'kg': jnp.tile(k_norm_a, (1, 2)),
        'qg_col': q_norm_a.reshape(HEAD_DIM, 1), 'kg_col': k_norm_a.reshape(HEAD_DIM, 1),
        'qgm': q_norm_m, 'kgm': k_norm_m, 'sinks': sinks[0],
        'bd': half.astype(BF16),
    }

    mk, mv = _mem_kv(mem_prompt.reshape(b * MEM_LEN, d), w, tile=MEM_KV_TILE)
    xp, kwin, vwin, ret_p = _prompt_mixer(x_prompt, w, tile=PROMPT_MIXER_TILE, sub=PROMPT_MIXER_SUB)
    xp = xp.reshape(b * l, d)
    mem_rows_p = (b, MEM_LEN * MEM_HEADS, MEM_HEAD_DIM)
    yp = _prompt_cross_ffn(xp, mk.reshape(mem_rows_p), mv.reshape(mem_rows_p), w, tile=PROMPT_FFN_TILE)
    yp = yp.reshape(b, l, d)
    ret_p = ret_p.reshape(1, b, RET_HEADS, RET_DIM, RET_DIM)

    nbuf = cache_swa_k.shape[2]
    assert nbuf == WINDOW
    to_feature_major = lambda c: jnp.transpose(c[0], (0, 2, 3, 1)).reshape(sb, SWA_KV_W, nbuf)
    from_feature_major = lambda c: jnp.transpose(c.reshape(sb, SWA_KV_HEADS, HEAD_DIM, nbuf), (0, 3, 1, 2))[None]
    xs, k_s, v_s, s_s = _sample_mixer(x_sample, to_feature_major(cache_swa_k), to_feature_major(cache_swa_v),
                                      jnp.transpose(state_ret[0], (1, 2, 3, 0)), w, tile=SAMPLE_TILE,
                                      nseq=SAMPLE_SEQS)
    mem_rows = (sb, MEM_LEN * MEM_HEADS, MEM_HEAD_DIM)
    os_ = _sample_cross_attn(xs, cache_mem_k.reshape(mem_rows), cache_mem_v.reshape(mem_rows), w,
                             nseq=SAMPLE_SEQS, tq=st)
    ys = _sample_cross_out_ffn(xs, os_, w, tile=SAMPLE_FFN_TILE).reshape(sb, st, d)

    kv_shape = (1, b, WINDOW, SWA_KV_HEADS, HEAD_DIM)
    mem_shape = (1, b, MEM_LEN, MEM_HEADS, MEM_HEAD_DIM)
    return (yp, ys, kwin.reshape(kv_shape), vwin.reshape(kv_shape), ret_p,
            mk.reshape(mem_shape), mv.reshape(mem_shape),
            from_feature_major(k_s), from_feature_major(v_s),
            jnp.transpose(s_s, (3, 0, 1, 2))[None])
```
